```python
import math
import jax
import jax.numpy as jnp
from jax import lax
import numpy as np

D_MODEL = 1024
BATCH = 4
SEQ = 4096
DEPTH = 4
DEC_BATCH = 32
DEC_SEQ = 1
PAST_LEN = 8192
PAGE_SIZE = 128

N_MIXERS = 4
HEAD_DIM = 64
N_HEADS = D_MODEL // HEAD_DIM
ATTN_WIDTH = N_HEADS * HEAD_DIM
ROPE_THETA = 10000.0
RMS_EPS = 1e-6
LN_EPS = 1e-5
D_FF = 4 * D_MODEL

NSA_KV_HEADS = 4
NSA_GROUP = N_HEADS // NSA_KV_HEADS
NSA_KV_WIDTH = NSA_KV_HEADS * HEAD_DIM
NSA_N_KV_STREAMS = 6
NSA_N_CACHED = 4
NSA_N_BRANCHES = 3
NSA_IN_WIDTH = ATTN_WIDTH + NSA_N_KV_STREAMS * NSA_KV_WIDTH + N_HEADS * NSA_N_BRANCHES
CMP_STRIDE = 16
CMP_BLOCK = 2 * CMP_STRIDE
CMP_HIDDEN = 256
SEL_BLOCK = 64
SEL_TOPN = 16
WINDOW = 512
NSA_Q_CHUNK = 32
WIN_Q_BLOCK = 128

SCONV_WIDTH = 3

MOBA_BLOCK = 256
MOBA_TOPK = 3
MOBA_Q_CHUNK = 16

CONF_WIDTH = 31

N_NSA_LAYERS = (DEPTH + 3) // 4
N_SCONV_LAYERS = (DEPTH + 2) // 4
N_MOBA_LAYERS = (DEPTH + 1) // 4
N_CONF_LAYERS = DEPTH // 4

NEG_POS = -(2 ** 30)

kernel_name = 'hybrid_nsa_sconv_moba_conformer_step'


def rms_norm(x, g):
    xf = x.astype(jnp.float32)
    y = xf * lax.rsqrt(jnp.mean(xf * xf, axis=-1, keepdims=True) + RMS_EPS)
    return (y * g.astype(jnp.float32)).astype(x.dtype)


def layer_norm(x, g, b):
    xf = x.astype(jnp.float32)
    xc = xf - jnp.mean(xf, axis=-1, keepdims=True)
    y = xc * lax.rsqrt(jnp.mean(xc * xc, axis=-1, keepdims=True) + LN_EPS)
    return (y * g.astype(jnp.float32) + b.astype(jnp.float32)).astype(x.dtype)


def rope(x, pos):
    half = HEAD_DIM // 2
    inv_freq = ROPE_THETA ** (-jnp.arange(half, dtype=jnp.float32) / half)
    ang = pos.astype(jnp.float32)[:, None] * inv_freq[None, :]
    cos = jnp.cos(ang)[:, None, :]
    sin = jnp.sin(ang)[:, None, :]
    xf = x.astype(jnp.float32)
    x1, x2 = xf[..., :half], xf[..., half:]
    return jnp.concatenate([x1 * cos - x2 * sin, x2 * cos + x1 * sin], axis=-1).astype(x.dtype)


def masked_softmax(s, mask):
    s = jnp.where(mask, s.astype(jnp.float32), -jnp.inf)
    m = jnp.max(s, axis=-1, keepdims=True)
    m = jnp.where(jnp.isfinite(m), m, 0.0)
    p = jnp.exp(s - m)
    return p / jnp.maximum(jnp.sum(p, axis=-1, keepdims=True), 1e-30)


def causal_dwconv(padded, w):
    return lax.conv_general_dilated(
        padded, w[:, None, :].astype(padded.dtype), window_strides=(1,), padding='VALID',
        dimension_numbers=('NWC', 'WIO', 'NWC'), feature_group_count=padded.shape[-1])


def gather_pages(pool, page_table):
    rows = pool[page_table]
    n_seq, n_pages = page_table.shape
    return rows.reshape((n_seq, n_pages * PAGE_SIZE) + pool.shape[2:])


def compress_rows(rows, pe, w1, w2):
    B, L = rows.shape[:2]
    n_cmp = (L - CMP_BLOCK) // CMP_STRIDE + 1
    pieces = rows[:, :(n_cmp + 1) * CMP_STRIDE].reshape(B, n_cmp + 1, CMP_STRIDE, NSA_KV_HEADS, HEAD_DIM)
    blocks = jnp.concatenate([pieces[:, :-1], pieces[:, 1:]], axis=2) + pe[:, None, :]
    flat = blocks.transpose(0, 1, 3, 2, 4).reshape(B, n_cmp, NSA_KV_HEADS, CMP_BLOCK * HEAD_DIM)
    return jax.nn.silu(flat @ w1) @ w2


def nsa_mixer(h, pos0, past_rows, past_win, w_in, pe_k, w1_k, w2_k, pe_v, w1_v, w2_v, w_out):
    B, T, _ = h.shape
    pos = pos0 + jnp.arange(T, dtype=jnp.int32)
    scale = HEAD_DIM ** -0.5
    proj = h @ w_in
    kv_end = ATTN_WIDTH + NSA_N_KV_STREAMS * NSA_KV_WIDTH
    q = rope(proj[..., :ATTN_WIDTH].reshape(B, T, N_HEADS, HEAD_DIM), pos)
    q = q.reshape(B, T, NSA_KV_HEADS, NSA_GROUP, HEAD_DIM)
    kv = proj[..., ATTN_WIDTH:kv_end].reshape(B, T, NSA_N_KV_STREAMS, NSA_KV_HEADS, HEAD_DIM)
    gates = jax.nn.sigmoid(proj[..., kv_end:].astype(jnp.float32))
    gates = gates.reshape(B, T, NSA_KV_HEADS, NSA_GROUP, NSA_N_BRANCHES).astype(h.dtype)
    new_rows = jnp.stack([kv[:, :, 0], kv[:, :, 1], rope(kv[:, :, 2], pos), kv[:, :, 3]], axis=2)
    new_win = jnp.stack([rope(kv[:, :, 4], pos), kv[:, :, 5]], axis=2)
    rows = new_rows if past_rows is None else jnp.concatenate([past_rows, new_rows], axis=1)
    win = new_win if past_win is None else jnp.concatenate([past_win, new_win], axis=1)
    L = rows.shape[1]

    kc = compress_rows(rows[:, :, 0], pe_k, w1_k, w2_k)
    vc = compress_rows(rows[:, :, 1], pe_v, w1_v, w2_v)
    n_cmp = kc.shape[1]
    cmp_start = jnp.arange(n_cmp, dtype=jnp.int32) * CMP_STRIDE
    cmp_end = cmp_start + (CMP_BLOCK - 1)
    kc = rope(kc, cmp_end)

    n_sel = -(-L // SEL_BLOCK)
    sel_rows = jnp.pad(rows[:, :, 2:4], ((0, 0), (0, n_sel * SEL_BLOCK - L), (0, 0), (0, 0), (0, 0)))
    sel_rows = sel_rows.reshape(B, n_sel, SEL_BLOCK, 2, NSA_KV_HEADS, HEAD_DIM)
    blk = jnp.arange(n_sel, dtype=jnp.int32)
    sel_start = blk * SEL_BLOCK
    overlap = ((cmp_start[:, None] <= sel_start[None, :] + (SEL_BLOCK - 1))
               & (cmp_end[:, None] >= sel_start[None, :])).astype(jnp.float32)
    n_top = min(SEL_TOPN, n_sel)
    in_blk = jnp.arange(SEL_BLOCK, dtype=jnp.int32)
    b_idx = jnp.arange(B)[:, None, None, None]
    h_idx = jnp.arange(NSA_KV_HEADS)[None, :, None, None]
    c = math.gcd(NSA_Q_CHUNK, T)

    def cmp_sel_chunk(args):
        qc, pc = args
        s_c = jnp.einsum('bqhgd,bnhd->bhgqn', qc, kc) * scale
        p_c = masked_softmax(s_c, cmp_end[None, :] <= pc[:, None])
        o_c = jnp.einsum('bhgqn,bnhd->bqhgd', p_c.astype(vc.dtype), vc)
        imp = jnp.einsum('bhgqn,nj->bhqj', p_c, overlap)
        cur = pc // SEL_BLOCK
        forced = (blk[None, :] == 0) | (blk[None, :] == cur[:, None]) | (blk[None, :] == cur[:, None] - 1)
        imp = jnp.where(forced, jnp.inf, imp)
        imp = jnp.where(blk[None, :] <= cur[:, None], imp, -jnp.inf)
        top_s, top_i = lax.top_k(imp, n_top)
        kg = sel_rows[b_idx, top_i, :, 0, h_idx, :]
        vg = sel_rows[b_idx, top_i, :, 1, h_idx, :]
        s_s = jnp.einsum('bqhgd,bhqnkd->bhgqnk', qc, kg) * scale
        kpos = top_i[..., None] * SEL_BLOCK + in_blk
        ok = (top_s > -jnp.inf)[..., None] & (kpos <= pc[None, None, :, None, None])
        p_s = masked_softmax(s_s.reshape(B, NSA_KV_HEADS, NSA_GROUP, c, n_top * SEL_BLOCK),
                             ok.reshape(B, NSA_KV_HEADS, 1, c, n_top * SEL_BLOCK))
        o_s = jnp.einsum('bhgqnk,bhqnkd->bqhgd', p_s.reshape(s_s.shape).astype(vg.dtype), vg)
        return o_c, o_s

    q_chunks = jnp.moveaxis(q.reshape(B, T // c, c, NSA_KV_HEADS, NSA_GROUP, HEAD_DIM), 1, 0)
    o_c, o_s = lax.map(cmp_sel_chunk, (q_chunks, pos.reshape(T // c, c)))
    o_c = jnp.moveaxis(o_c, 0, 1).reshape(B, T, NSA_KV_HEADS, NSA_GROUP, HEAD_DIM)
    o_s = jnp.moveaxis(o_s, 0, 1).reshape(B, T, NSA_KV_HEADS, NSA_GROUP, HEAD_DIM)

    Lw = win.shape[1]
    wb = math.gcd(WIN_Q_BLOCK, T)
    band = wb + WINDOW
    win_pad = jnp.pad(win, ((0, 0), (WINDOW, 0), (0, 0), (0, 0), (0, 0)))
    pos_w = jnp.concatenate([jnp.full((WINDOW,), NEG_POS, jnp.int32),
                             pos0 + T - Lw + jnp.arange(Lw, dtype=jnp.int32)])
    starts = (Lw - T) + jnp.arange(T // wb, dtype=jnp.int32) * wb

    def win_block(args):
        qb, pb, st = args
        kvb = lax.dynamic_slice_in_dim(win_pad, st, band, axis=1)
        kp = lax.dynamic_slice_in_dim(pos_w, st, band, axis=0)
        s = jnp.einsum('bqhgd,bkhd->bhgqk', qb, kvb[:, :, 0]) * scale
        diff = pb[:, None] - kp[None, :]
        p = masked_softmax(s, (diff >= 0) & (diff <= WINDOW))
        return jnp.einsum('bhgqk,bkhd->bqhgd', p.astype(kvb.dtype), kvb[:, :, 1])

    q_blocks = jnp.moveaxis(q.reshape(B, T // wb, wb, NSA_KV_HEADS, NSA_GROUP, HEAD_DIM), 1, 0)
    o_w = lax.map(win_block, (q_blocks, pos.reshape(T // wb, wb), starts))
    o_w = jnp.moveaxis(o_w, 0, 1).reshape(B, T, NSA_KV_HEADS, NSA_GROUP, HEAD_DIM)

    o = gates[..., 0:1] * o_c + gates[..., 1:2] * o_s + gates[..., 2:3] * o_w
    keep = min(WINDOW, T) if past_win is None else past_win.shape[1]
    return o.reshape(B, T, ATTN_WIDTH) @ w_out, new_rows, win[:, Lw - keep:]


def sconv_mixer(h, past, w_in, w_conv, w_out):
    B, T, D = h.shape
    proj = h @ w_in
    b_gate, c_gate, u = proj[..., :D], proj[..., D:2 * D], proj[..., 2 * D:]
    pre = c_gate * u
    if past is None:
        past = jnp.zeros((B, SCONV_WIDTH - 1, D), pre.dtype)
    padded = jnp.concatenate([past, pre], axis=1)
    y = causal_dwconv(padded, w_conv)
    return (b_gate * y) @ w_out, padded[:, T:]


def moba_mixer(h, pos0, past_rows, w_qkv, w_out):
    B, T, _ = h.shape
    pos = pos0 + jnp.arange(T, dtype=jnp.int32)
    scale = HEAD_DIM ** -0.5
    proj = (h @ w_qkv).reshape(B, T, 3, N_HEADS, HEAD_DIM)
    q = rope(proj[:, :, 0], pos)
    new_rows = jnp.stack([rope(proj[:, :, 1], pos), proj[:, :, 2]], axis=2)
    pieces = [new_rows] if past_rows is None else [past_rows, new_rows]
    L = sum(p.shape[1] for p in pieces)
    nb = -(-L // MOBA_BLOCK)
    pieces.append(jnp.zeros((B, nb * MOBA_BLOCK - L, 2, N_HEADS, HEAD_DIM), new_rows.dtype))
    rows_blk = jnp.concatenate(pieces, axis=1).reshape(B, nb, MOBA_BLOCK, 2, N_HEADS, HEAD_DIM)
    k_mean = jnp.mean(rows_blk[:, :, :, 0].astype(jnp.float32), axis=2)
    n_top = min(MOBA_TOPK, nb)
    blk = jnp.arange(nb, dtype=jnp.int32)
    in_blk = jnp.arange(MOBA_BLOCK, dtype=jnp.int32)
    b_idx = jnp.arange(B)[:, None, None, None]
    h_idx = jnp.arange(N_HEADS)[None, :, None, None]
    c = math.gcd(MOBA_Q_CHUNK, T)

    def chunk(args):
        qc, pc = args
        gate = jnp.einsum('bqhd,bnhd->bhqn', qc.astype(jnp.float32), k_mean)
        cur = pc // MOBA_BLOCK
        gate = jnp.where(blk[None, :] < cur[:, None], gate, -jnp.inf)
        top_s, top_i = lax.top_k(gate, n_top)
        own = jnp.broadcast_to(cur[None, None, :, None], (B, N_HEADS, c, 1))
        idx = jnp.concatenate([top_i, own], axis=-1)
        ok = jnp.concatenate([top_s > -jnp.inf, jnp.ones((B, N_HEADS, c, 1), bool)], axis=-1)
        kg = rows_blk[b_idx, idx, :, 0, h_idx, :]
        vg = rows_blk[b_idx, idx, :, 1, h_idx, :]
        s = jnp.einsum('bqhd,bhqnkd->bhqnk', qc, kg) * scale
        kpos = idx[..., None] * MOBA_BLOCK + in_blk
        mask = ok[..., None] & (kpos <= pc[None, None, :, None, None])
        p = masked_softmax(s.reshape(B, N_HEADS, c, -1), mask.reshape(B, N_HEADS, c, -1))
        return jnp.einsum('bhqnk,bhqnkd->bqhd', p.reshape(s.shape).astype(vg.dtype), vg)

    q_chunks = jnp.moveaxis(q.reshape(B, T // c, c, N_HEADS, HEAD_DIM), 1, 0)
    o = lax.map(chunk, (q_chunks, pos.reshape(T // c, c)))
    o = jnp.moveaxis(o, 0, 1).reshape(B, T, ATTN_WIDTH)
    return o @ w_out, new_rows


def conformer_mixer(h, past, w_pw1, w_dw, b_dw, ln_g, ln_b, w_pw2):
    B, T, D = h.shape
    a = h @ w_pw1
    u = a[..., :D] * jax.nn.sigmoid(a[..., D:])
    if past is None:
        past = jnp.zeros((B, CONF_WIDTH - 1, D), u.dtype)
    padded = jnp.concatenate([past, u], axis=1)
    y = causal_dwconv(padded, w_dw) + b_dw
    y = jax.nn.silu(layer_norm(y, ln_g, ln_b))
    return y @ w_pw2, padded[:, T:]


def sqrelu_mlp(h, w_up, w_down):
    return jnp.square(jax.nn.relu(h @ w_up)) @ w_down


def setup_inputs(seed: int = 0) -> dict:
    key = jax.random.key(seed)
    keys = iter(jax.random.split(key, 48))

    def nrm(shape, scale):
        return jax.random.normal(next(keys), shape, jnp.float32) * scale

    def gain(shape):
        return 1.0 + nrm(shape, 0.02)

    D = D_MODEL
    n_pages = PAST_LEN // PAGE_SIZE
    n_used = DEC_BATCH * n_pages
    n_phys = n_used + max(1, n_used // 4)
    wbuf = min(WINDOW, PAST_LEN)
    return {
        'x_prompt': nrm((BATCH, SEQ, D), 1.0),
        'x_sample': nrm((DEC_BATCH, DEC_SEQ, D), 1.0),
        'cache_nsa_kv': nrm((N_NSA_LAYERS, n_phys, PAGE_SIZE, NSA_N_CACHED, NSA_KV_HEADS, HEAD_DIM), 1.0),
        'cache_nsa_win': nrm((N_NSA_LAYERS, DEC_BATCH, wbuf, 2, NSA_KV_HEADS, HEAD_DIM), 1.0),
        'state_sconv': nrm((N_SCONV_LAYERS, DEC_BATCH, SCONV_WIDTH - 1, D), 1.0),
        'cache_moba_kv': nrm((N_MOBA_LAYERS, n_phys, PAGE_SIZE, 2, N_HEADS, HEAD_DIM), 1.0),
        'state_conformer': nrm((N_CONF_LAYERS, DEC_BATCH, CONF_WIDTH - 1, D), 1.0),
        'page_table': jax.random.permutation(next(keys), n_phys)[:n_used].reshape(DEC_BATCH, n_pages).astype(jnp.int32),
        'norm_mix': gain((DEPTH, D)),
        'norm_ffn': gain((DEPTH, D)),
        'norm_final': gain((D,)),
        'ffn_w_up': nrm((DEPTH, D, D_FF), D ** -0.5),
        'ffn_w_down': nrm((DEPTH, D_FF, D), D_FF ** -0.5),
        'nsa_w_in': nrm((N_NSA_LAYERS, D, NSA_IN_WIDTH), D ** -0.5),
        'nsa_pe_k': nrm((N_NSA_LAYERS, CMP_BLOCK, HEAD_DIM), 0.1),
        'nsa_w1_k': nrm((N_NSA_LAYERS, CMP_BLOCK * HEAD_DIM, CMP_HIDDEN), (CMP_BLOCK * HEAD_DIM) ** -0.5),
        'nsa_w2_k': nrm((N_NSA_LAYERS, CMP_HIDDEN, HEAD_DIM), 2.0 * CMP_HIDDEN ** -0.5),
        'nsa_pe_v': nrm((N_NSA_LAYERS, CMP_BLOCK, HEAD_DIM), 0.1),
        'nsa_w1_v': nrm((N_NSA_LAYERS, CMP_BLOCK * HEAD_DIM, CMP_HIDDEN), (CMP_BLOCK * HEAD_DIM) ** -0.5),
        'nsa_w2_v': nrm((N_NSA_LAYERS, CMP_HIDDEN, HEAD_DIM), 2.0 * CMP_HIDDEN ** -0.5),
        'nsa_w_out': nrm((N_NSA_LAYERS, ATTN_WIDTH, D), ATTN_WIDTH ** -0.5),
        'sconv_w_in': nrm((N_SCONV_LAYERS, D, 3 * D), D ** -0.5),
        'sconv_w_conv': nrm((N_SCONV_LAYERS, SCONV_WIDTH, D), SCONV_WIDTH ** -0.5),
        'sconv_w_out': nrm((N_SCONV_LAYERS, D, D), D ** -0.5),
        'moba_w_qkv': nrm((N_MOBA_LAYERS, D, 3 * ATTN_WIDTH), D ** -0.5),
        'moba_w_out': nrm((N_MOBA_LAYERS, ATTN_WIDTH, D), ATTN_WIDTH ** -0.5),
        'conf_w_pw1': nrm((N_CONF_LAYERS, D, 2 * D), D ** -0.5),
        'conf_w_dw': nrm((N_CONF_LAYERS, CONF_WIDTH, D), CONF_WIDTH ** -0.5),
        'conf_b_dw': nrm((N_CONF_LAYERS, D), 0.02),
        'conf_ln_g': gain((N_CONF_LAYERS, D)),
        'conf_ln_b': nrm((N_CONF_LAYERS, D), 0.02),
        'conf_w_pw2': nrm((N_CONF_LAYERS, D, D), D ** -0.5),
    }


def reference(x_prompt, x_sample, cache_nsa_kv, cache_nsa_win, state_sconv, cache_moba_kv, state_conformer,
              page_table, norm_mix, norm_ffn, norm_final, ffn_w_up, ffn_w_down,
              nsa_w_in, nsa_pe_k, nsa_w1_k, nsa_w2_k, nsa_pe_v, nsa_w1_v, nsa_w2_v, nsa_w_out,
              sconv_w_in, sconv_w_conv, sconv_w_out, moba_w_qkv, moba_w_out,
              conf_w_pw1, conf_w_dw, conf_b_dw, conf_ln_g, conf_ln_b, conf_w_pw2):
    past_len = page_table.shape[1] * PAGE_SIZE
    hp, hs = x_prompt, x_sample
    nsa_kv_p, nsa_kv_s, nsa_win_p, nsa_win_s = [], [], [], []
    sconv_p, sconv_s, moba_p, moba_s, conf_p, conf_s = [], [], [], [], [], []
    for i in range(DEPTH):
        kind, j = i % N_MIXERS, i // N_MIXERS
        a_p = rms_norm(hp, norm_mix[i])
        a_s = rms_norm(hs, norm_mix[i])
        if kind == 0:
            w = (nsa_w_in[j], nsa_pe_k[j], nsa_w1_k[j], nsa_w2_k[j], nsa_pe_v[j], nsa_w1_v[j], nsa_w2_v[j], nsa_w_out[j])
            o_p, r_p, win_p = nsa_mixer(a_p, 0, None, None, *w)
            o_s, r_s, win_s = nsa_mixer(a_s, past_len, gather_pages(cache_nsa_kv[j], page_table), cache_nsa_win[j], *w)
            nsa_kv_p.append(r_p)
            nsa_kv_s.append(r_s)
            nsa_win_p.append(win_p)
            nsa_win_s.append(win_s)
        elif kind == 1:
            w = (sconv_w_in[j], sconv_w_conv[j], sconv_w_out[j])
            o_p, st_p = sconv_mixer(a_p, None, *w)
            o_s, st_s = sconv_mixer(a_s, state_sconv[j], *w)
            sconv_p.append(st_p)
            sconv_s.append(st_s)
        elif kind == 2:
            w = (moba_w_qkv[j], moba_w_out[j])
            o_p, r_p = moba_mixer(a_p, 0, None, *w)
            o_s, r_s = moba_mixer(a_s, past_len, gather_pages(cache_moba_kv[j], page_table), *w)
            moba_p.append(r_p)
            moba_s.append(r_s)
        else:
            w = (conf_w_pw1[j], conf_w_dw[j], conf_b_dw[j], conf_ln_g[j], conf_ln_b[j], conf_w_pw2[j])
            o_p, st_p = conformer_mixer(a_p, None, *w)
            o_s, st_s = conformer_mixer(a_s, state_conformer[j], *w)
            conf_p.append(st_p)
            conf_s.append(st_s)
        hp = hp + o_p
        hs = hs + o_s
        hp = hp + sqrelu_mlp(rms_norm(hp, norm_ffn[i]), ffn_w_up[i], ffn_w_down[i])
        hs = hs + sqrelu_mlp(rms_norm(hs, norm_ffn[i]), ffn_w_up[i], ffn_w_down[i])
    y_prompt = rms_norm(hp, norm_final)
    y_sample = rms_norm(hs, norm_final)
    return (y_prompt, y_sample,
            jnp.stack(nsa_kv_p), jnp.stack(nsa_kv_s), jnp.stack(nsa_win_p), jnp.stack(nsa_win_s),
            jnp.stack(sconv_p), jnp.stack(sconv_s), jnp.stack(moba_p), jnp.stack(moba_s),
            jnp.stack(conf_p), jnp.stack(conf_s))
```

```python
import functools

import jax
import jax.numpy as jnp
from jax import lax
from jax.experimental import pallas as pl
from jax.experimental.pallas import tpu as pltpu

F32 = jnp.float32
BF16 = jnp.bfloat16

HEAD_DIM = 64
ROPE_THETA = 10000.0
RMS_EPS = 1e-6
LN_EPS = 1e-5
NSA_KV_HEADS = 4
NSA_GROUP = 4
CMP_STRIDE = 16
CMP_BLOCK = 32
SEL_BLOCK = 64
SEL_TOPN = 16
WINDOW = 512
MOBA_BLOCK = 256
MOBA_TOPK = 3
PAGE_SIZE = 128
SCALE = HEAD_DIM ** -0.5

LANES = 128
NEG = -1e30
VMEM_LIMIT = 56 * 1024 * 1024


def _params(*sem):
    return pltpu.CompilerParams(dimension_semantics=sem, vmem_limit_bytes=VMEM_LIMIT)


def _dot(a, b):
    return jnp.dot(a, b, preferred_element_type=F32)


def _dot_nt(a, b):
    return lax.dot_general(a, b, (((1,), (1,)), ((), ())), preferred_element_type=F32)


def _split3(x):
    hi = x.astype(BF16)
    r = x - hi.astype(F32)
    mid = r.astype(BF16)
    lo = (r - mid.astype(F32)).astype(BF16)
    return hi, mid, lo


def _dot3(x, m):
    hi, mid, lo = _split3(x)
    return _dot(hi, m) + _dot(mid, m) + _dot(lo, m)


def _dot3_rhs(m, x):
    hi, mid, lo = _split3(x)
    return _dot(m, hi) + _dot(m, mid) + _dot(m, lo)


def _rms(x, g):
    return x * lax.rsqrt(jnp.mean(x * x, axis=-1, keepdims=True) + RMS_EPS) * g


def _sigmoid(x):
    return 1.0 / (1.0 + jnp.exp(-x))


def _rope(x, cos, sin):
    w = x.shape[-1]
    lane = lax.broadcasted_iota(jnp.int32, x.shape, 1)
    first = (lane % HEAD_DIM) < (HEAD_DIM // 2)
    rot = jnp.where(first, pltpu.roll(x, w - HEAD_DIM // 2, 1), pltpu.roll(x, HEAD_DIM // 2, 1))
    reps = w // LANES
    if reps > 1:
        cos = jnp.concatenate([cos] * reps, axis=1)
        sin = jnp.concatenate([sin] * reps, axis=1)
    return x * cos + rot * sin


def _masked_softmax(s, mask):
    s = jnp.where(mask, s, -jnp.inf)
    m = jnp.max(s, axis=-1, keepdims=True)
    m = jnp.where(m > -jnp.inf, m, 0.0)
    p = jnp.exp(s - m)
    return p / jnp.maximum(jnp.sum(p, axis=-1, keepdims=True), 1e-30)


def _topk_mask(score, k):
    lane = lax.broadcasted_iota(jnp.int32, score.shape, 1).astype(F32)
    sel = jnp.zeros(score.shape, F32)
    for _ in range(k):
        m = jnp.max(score, axis=-1, keepdims=True)
        idx = jnp.min(jnp.where(score == m, lane, 1e9), axis=-1, keepdims=True)
        hit = lane == idx
        sel = jnp.where(hit & (m > -jnp.inf), 1.0, sel)
        score = jnp.where(hit, -jnp.inf, score)
    return sel


def _online_update(carry, s, ok, v):
    m, l, acc = carry
    sm = jnp.where(ok, s, NEG)
    m_new = jnp.maximum(m, jnp.max(sm, axis=-1, keepdims=True))
    alpha = jnp.exp(m - m_new)
    p = jnp.where(ok, jnp.exp(sm - m_new), 0.0)
    l = alpha * l + jnp.sum(p, axis=-1, keepdims=True)
    acc = alpha * acc + _dot(p.astype(BF16), v)
    return m_new, l, acc


def _online_init(rows, width):
    return (jnp.full((rows, 1), NEG, F32), jnp.zeros((rows, 1), F32), jnp.zeros((rows, width), F32))


def _ffn_kernel(x_ref, g_ref, wu_ref, wd_ref, gf_ref, o_ref, *, chunk, final):
    x = x_ref[...]
    xn = _rms(x, g_ref[...]).astype(BF16)
    acc = x
    for c in range(0, wu_ref.shape[1], chunk):
        u = _dot(xn, wu_ref[:, c:c + chunk])
        a = jnp.square(jnp.maximum(u, 0.0)).astype(BF16)
        acc = acc + _dot(a, wd_ref[c:c + chunk, :])
    if final:
        acc = _rms(acc, gf_ref[...])
    o_ref[...] = acc


def ffn(x, g, wu, wd, gf, *, tm, final):
    m, d = x.shape
    dff = wu.shape[1]
    row = lambda i: (i, 0)
    const = lambda i: (0, 0)
    return pl.pallas_call(
        functools.partial(_ffn_kernel, chunk=512, final=final),
        grid=(m // tm,),
        in_specs=[pl.BlockSpec((tm, d), row), pl.BlockSpec((1, d), const),
                  pl.BlockSpec((d, dff), const, pipeline_mode=pl.Buffered(1)),
                  pl.BlockSpec((dff, d), const, pipeline_mode=pl.Buffered(1)),
                  pl.BlockSpec((1, d), const)],
        out_specs=pl.BlockSpec((tm, d), row),
        out_shape=jax.ShapeDtypeStruct((m, d), F32),
        compiler_params=_params("parallel"),
        name="ffn",
    )(x, g, wu, wd, gf)


def _mm_res_kernel(x_ref, w_ref, r_ref, o_ref):
    o_ref[...] = r_ref[...] + _dot(x_ref[...], w_ref[...])


def mm_res(x, w, res, *, tm):
    m, k = x.shape
    n = w.shape[1]
    row = lambda i: (i, 0)
    return pl.pallas_call(
        _mm_res_kernel,
        grid=(m // tm,),
        in_specs=[pl.BlockSpec((tm, k), row), pl.BlockSpec((k, n), lambda i: (0, 0)),
                  pl.BlockSpec((tm, n), row)],
        out_specs=pl.BlockSpec((tm, n), row),
        out_shape=jax.ShapeDtypeStruct((m, n), F32),
        compiler_params=_params("parallel"),
        name="mm_res",
    )(x, w, res)


def _nsa_in_kernel(x_ref, g_ref, w_ref, cos_ref, sin_ref, q_ref, rows_ref, kva_ref, win_ref, gate_ref):
    xn = _rms(x_ref[...], g_ref[...]).astype(BF16)
    cos = cos_ref[...]
    sin = sin_ref[...]
    q = _rope(_dot(xn, w_ref[:, 0:1024]), cos, sin) * SCALE
    q_ref[...] = q.astype(BF16)
    kv = _dot(xn, w_ref[:, 1024:2048])
    ks = _rope(kv[:, 512:768], cos, sin)
    rows_ref[:, 0:512] = kv[:, 0:512]
    rows_ref[:, 512:768] = ks
    rows_ref[:, 768:1024] = kv[:, 768:1024]
    wkv = _dot(xn, w_ref[:, 2048:2560])
    kw = _rope(wkv[:, 0:256], cos, sin)
    win_ref[:, 0:256] = kw
    win_ref[:, 256:512] = wkv[:, 256:512]
    kva_ref[:, 0:256] = ks.astype(BF16)
    kva_ref[:, 256:512] = kv[:, 768:1024].astype(BF16)
    kva_ref[:, 512:768] = kw.astype(BF16)
    kva_ref[:, 768:1024] = wkv[:, 256:512].astype(BF16)
    gate_ref[...] = _sigmoid(_dot(xn, w_ref[:, 2560:2688]))


def nsa_in(x, g, w, cos, sin, *, tm):
    m, d = x.shape
    nw = w.shape[1]
    nt = cos.shape[0] // tm
    row = lambda i: (i, 0)
    tab = lambda i: (i % nt, 0)
    const = lambda i: (0, 0)
    outs = [(1024, BF16), (1024, F32), (1024, BF16), (512, F32), (LANES, F32)]
    return pl.pallas_call(
        _nsa_in_kernel,
        grid=(m // tm,),
        in_specs=[pl.BlockSpec((tm, d), row), pl.BlockSpec((1, d), const),
                  pl.BlockSpec((d, nw), const, pipeline_mode=pl.Buffered(1)),
                  pl.BlockSpec((tm, LANES), tab), pl.BlockSpec((tm, LANES), tab)],
        out_specs=[pl.BlockSpec((tm, n), row) for n, _ in outs],
        out_shape=[jax.ShapeDtypeStruct((m, n), dt) for n, dt in outs],
        compiler_params=_params("parallel"),
        name="nsa_in",
    )(x, g, w, cos, sin)


def _compress_kernel(pt_ref, *refs, pps):
    del pt_ref
    pages = refs[:pps + 1]
    (pelo_k, pehi_k, w1lo_k, w1hi_k, w2_k, pelo_v, pehi_v, w1lo_v, w1hi_v, w2_v,
     cos_ref, sin_ref, kc_ref, vc_ref, xs_ref) = refs[pps + 1:]
    npc = pps * 8
    rows = (pps + 1) * 8
    low = lax.broadcasted_iota(jnp.int32, (rows, LANES), 1) < HEAD_DIM
    for k, pg in enumerate(pages):
        for lb in range(4):
            xs_ref[lb, k * PAGE_SIZE:(k + 1) * PAGE_SIZE, :] = pg[:, lb * LANES:(lb + 1) * LANES]
    streams = ((pelo_k, pehi_k, w1lo_k, w1hi_k, w2_k, kc_ref), (pelo_v, pehi_v, w1lo_v, w1hi_v, w2_v, vc_ref))
    for s, (pelo, pehi, w1lo, w1hi, w2, out_ref) in enumerate(streams):
        heads = [[] for _ in range(NSA_KV_HEADS)]
        for lb in range(2):
            for q in range(CMP_STRIDE // 2):
                a = xs_ref[2 * s + lb, pl.ds(2 * q, rows, stride=CMP_STRIDE), :]
                b = xs_ref[2 * s + lb, pl.ds(2 * q + 1, rows, stride=CMP_STRIDE), :]
                heads[2 * lb].append(jnp.where(low, a, pltpu.roll(b, HEAD_DIM, 1)))
                heads[2 * lb + 1].append(jnp.where(low, pltpu.roll(a, HEAD_DIM, 1), b))
        x = jnp.concatenate([jnp.concatenate(hh, axis=1) for hh in heads], axis=0)
        first = _dot((x + pelo[...]).astype(BF16), w1lo[...])
        second = _dot((x + pehi[...]).astype(BF16), w1hi[...])
        outs = []
        for h in range(NSA_KV_HEADS):
            pre = first[h * rows:h * rows + npc] + second[h * rows + 1:h * rows + 1 + npc]
            hid = pre * _sigmoid(pre)
            outs.append(_dot(hid.astype(BF16), w2[...]))
        res = jnp.concatenate(outs, axis=1)
        if s == 0:
            res = _rope(res, cos_ref[...], sin_ref[...])
        out_ref[...] = res.astype(BF16)


def nsa_compress(rows3d, pt_flat, n_seq, n_pages, wk, wv, cos_c, sin_c):
    pps = min(16, n_pages)
    steps = n_pages // pps
    npc = pps * 8

    def page_map(k):
        return lambda b, s, pt: (pt[b * n_pages + jnp.minimum(s * pps + k, n_pages - 1)], 0, 0)

    const2 = lambda b, s, pt: (0, 0)
    wspecs = []
    for _ in range(2):
        wspecs += [pl.BlockSpec((1, 1024), const2), pl.BlockSpec((1, 1024), const2),
                   pl.BlockSpec((1024, 256), const2), pl.BlockSpec((1024, 256), const2),
                   pl.BlockSpec((256, HEAD_DIM), const2)]
    grid_spec = pltpu.PrefetchScalarGridSpec(
        num_scalar_prefetch=1,
        grid=(n_seq, steps),
        in_specs=[pl.BlockSpec((None, PAGE_SIZE, 512), page_map(k)) for k in range(pps + 1)] + wspecs
        + [pl.BlockSpec((npc, LANES), lambda b, s, pt: (s, 0))] * 2,
        out_specs=[pl.BlockSpec((None, npc, 256), lambda b, s, pt: (b, s, 0))] * 2,
        scratch_shapes=[pltpu.VMEM((4, (pps + 1) * PAGE_SIZE, LANES), F32)],
    )
    return pl.pallas_call(
        functools.partial(_compress_kernel, pps=pps),
        grid_spec=grid_spec,
        out_shape=[jax.ShapeDtypeStruct((n_seq, n_pages * 8, 256), BF16)] * 2,
        compiler_params=_params("parallel", "arbitrary"),
        name="nsa_compress",
    )(pt_flat, *([rows3d] * (pps + 1)), *wk, *wv, cos_c, sin_c)


def _nsa_attn_kernel(q_ref, gate_ref, kc_ref, vc_ref, kva_ref, ovl_ref, e_ref, o_ref, *, tq, tks, n_top):
    i = pl.program_id(1)
    c0 = i * tq
    r4 = NSA_GROUP * tq
    pos4 = c0 + lax.broadcasted_iota(jnp.int32, (r4, 1), 0) % tq
    pos1 = c0 + lax.broadcasted_iota(jnp.int32, (tq, 1), 0)
    blk = lax.broadcasted_iota(jnp.int32, (tq, LANES), 1)
    cur = pos1 // SEL_BLOCK
    gt = gate_ref[...]
    outs = []
    for h in range(NSA_KV_HEADS):
        hs = slice(h * HEAD_DIM, (h + 1) * HEAD_DIM)
        qs = jnp.concatenate(
            [q_ref[:, (NSA_GROUP * h + g) * HEAD_DIM:(NSA_GROUP * h + g + 1) * HEAD_DIM] for g in range(NSA_GROUP)],
            axis=0)

        s = _dot_nt(qs, kc_ref[:, hs])
        cend = lax.broadcasted_iota(jnp.int32, s.shape, 1) * CMP_STRIDE + (CMP_BLOCK - 1)
        p = _masked_softmax(s, cend <= pos4)
        o_c = _dot(p.astype(BF16), vc_ref[:, hs])
        psum = p[0:tq] + p[tq:2 * tq] + p[2 * tq:3 * tq] + p[3 * tq:4 * tq]
        imp = _dot3(psum, ovl_ref[...])
        forced = (blk == 0) | (blk == cur) | (blk == cur - 1)
        imp = jnp.where(forced, jnp.inf, imp)
        imp = jnp.where(blk <= cur, imp, -jnp.inf)
        selb = _topk_mask(imp, n_top).astype(BF16)

        def sel_body(t, carry):
            k0 = pl.multiple_of(t * tks, tks)
            k = kva_ref[pl.ds(k0, tks), h * HEAD_DIM:(h + 1) * HEAD_DIM]
            v = kva_ref[pl.ds(k0, tks), 256 + h * HEAD_DIM:256 + (h + 1) * HEAD_DIM]
            chosen = _dot(selb, e_ref[t])
            kpos = k0 + lax.broadcasted_iota(jnp.int32, (tq, tks), 1)
            allow = jnp.where(kpos <= pos1, chosen, 0.0)
            ok = jnp.concatenate([allow] * NSA_GROUP, axis=0) > 0.5
            return _online_update(carry, _dot_nt(qs, k), ok, v)

        _, l, acc = lax.fori_loop(0, (c0 + tq - 1) // tks + 1, sel_body, _online_init(r4, HEAD_DIM))
        o_s = acc / jnp.maximum(l, 1e-30)

        def win_body(w, carry):
            k0 = pl.multiple_of(w * LANES, LANES)
            k = kva_ref[pl.ds(k0, LANES), 512 + h * HEAD_DIM:512 + (h + 1) * HEAD_DIM]
            v = kva_ref[pl.ds(k0, LANES), 768 + h * HEAD_DIM:768 + (h + 1) * HEAD_DIM]
            diff = pos4 - (k0 + lax.broadcasted_iota(jnp.int32, (r4, LANES), 1))
            ok = (diff >= 0) & (diff <= WINDOW)
            return _online_update(carry, _dot_nt(qs, k), ok, v)

        w_lo = jnp.maximum(c0 - WINDOW, 0) // LANES
        _, l, acc = lax.fori_loop(w_lo, (c0 + tq) // LANES, win_body, _online_init(r4, HEAD_DIM))
        o_w = acc / jnp.maximum(l, 1e-30)

        for g in range(NSA_GROUP):
            j = (h * NSA_GROUP + g) * 3
            rs = slice(g * tq, (g + 1) * tq)
            outs.append(gt[:, j:j + 1] * o_c[rs] + gt[:, j + 1:j + 2] * o_s[rs] + gt[:, j + 2:j + 3] * o_w[rs])
    o_ref[...] = jnp.concatenate(outs, axis=1).astype(BF16)


def nsa_attn(q, gates, kc, vc, kva, ovl, emat, *, n_b, t, tq, tks):
    nq = t // tq
    npiece = kc.shape[1]
    n_top = min(SEL_TOPN, t // SEL_BLOCK)
    row = lambda b, i: (b * nq + i, 0)
    per_b = lambda b, i: (b, 0, 0)
    return pl.pallas_call(
        functools.partial(_nsa_attn_kernel, tq=tq, tks=tks, n_top=n_top),
        grid=(n_b, nq),
        in_specs=[pl.BlockSpec((tq, 1024), row), pl.BlockSpec((tq, LANES), row),
                  pl.BlockSpec((None, npiece, 256), per_b), pl.BlockSpec((None, npiece, 256), per_b),
                  pl.BlockSpec((None, t, 1024), per_b),
                  pl.BlockSpec(ovl.shape, lambda b, i: (0, 0)),
                  pl.BlockSpec(emat.shape, lambda b, i: (0, 0, 0))],
        out_specs=pl.BlockSpec((tq, 1024), row),
        out_shape=jax.ShapeDtypeStruct((n_b * t, 1024), BF16),
        compiler_params=_params("parallel", "arbitrary"),
        name="nsa_attn",
    )(q, gates, kc, vc, kva.reshape(n_b, t, 1024), ovl, emat)


def _nsa_s_sel_kernel(q_ref, kc_ref, vc_ref, ovl_ref, gm_ref, oc_ref, idx_ref, *, pos, n_pick):
    s = _dot_nt(q_ref[...], kc_ref[...])
    cend = lax.broadcasted_iota(jnp.int32, s.shape, 1) * CMP_STRIDE + (CMP_BLOCK - 1)
    p = _masked_softmax(s, cend <= pos)
    oc_ref[...] = _dot(p.astype(BF16), vc_ref[...])
    psum = _dot3_rhs(gm_ref[...], p)
    imp = _dot3(psum, ovl_ref[...])
    blk = lax.broadcasted_iota(jnp.int32, imp.shape, 1)
    cur = pos // SEL_BLOCK
    imp = jnp.where((blk == 0) | (blk == cur - 1), jnp.inf, imp)
    imp = jnp.where(blk < cur, imp, -jnp.inf)
    lane = blk.astype(F32)
    slot = lax.broadcasted_iota(jnp.int32, (8, LANES), 1)
    picks = jnp.zeros((8, LANES), F32)
    for r in range(n_pick):
        m = jnp.max(imp, axis=-1, keepdims=True)
        idx = jnp.min(jnp.where(imp == m, lane, 1e9), axis=-1, keepdims=True)
        imp = jnp.where(lane == idx, -jnp.inf, imp)
        picks = jnp.where(slot == r, idx, picks)
    idx_ref[...] = picks.astype(jnp.int32)


def nsa_s_sel(qbd, kc, vc, ovl, gm, *, pos, n_pick):
    n_b, _, npiece = kc.shape[0], None, kc.shape[1]
    per_b = lambda b: (b, 0, 0)
    return pl.pallas_call(
        functools.partial(_nsa_s_sel_kernel, pos=pos, n_pick=n_pick),
        grid=(n_b,),
        in_specs=[pl.BlockSpec((None, 16, 256), per_b), pl.BlockSpec((None, npiece, 256), per_b),
                  pl.BlockSpec((None, npiece, 256), per_b),
                  pl.BlockSpec(ovl.shape, lambda b: (0, 0)), pl.BlockSpec(gm.shape, lambda b: (0, 0))],
        out_specs=[pl.BlockSpec((None, 16, 256), per_b), pl.BlockSpec((None, 8, LANES), per_b)],
        out_shape=[jax.ShapeDtypeStruct((n_b, 16, 256), F32), jax.ShapeDtypeStruct((n_b, 8, LANES), jnp.int32)],
        compiler_params=_params("parallel"),
        name="nsa_s_sel",
    )(qbd, kc, vc, ovl, gm)


def _attend_with_new(q, k, v, k_new, v_new):
    s = _dot_nt(q, k)
    qf = q.astype(F32)
    s_new = jnp.sum(qf * k_new.astype(F32), axis=-1, keepdims=True)
    m = jnp.maximum(jnp.max(s, axis=-1, keepdims=True), s_new)
    p = jnp.exp(s - m)
    p_new = jnp.exp(s_new - m)
    l = jnp.sum(p, axis=-1, keepdims=True) + p_new
    return (_dot(p.astype(BF16), v) + p_new * v_new.astype(F32)) / l


def _nsa_s_attn_kernel(pt_ref, ix_ref, q_ref, *refs, n_pick):
    del pt_ref, ix_ref
    kblk = refs[:n_pick]
    vblk = refs[n_pick:2 * n_pick]
    ks_new, vs_new, kw_new, vw_new, kw_ref, vw_ref, os_ref, ow_ref = refs[2 * n_pick:]
    q = q_ref[...]
    k = jnp.concatenate([r[...] for r in kblk], axis=0).astype(BF16)
    v = jnp.concatenate([r[...] for r in vblk], axis=0).astype(BF16)
    os_ref[...] = _attend_with_new(q, k, v, ks_new[...], vs_new[...])
    ow_ref[...] = _attend_with_new(q, kw_ref[...].astype(BF16), vw_ref[...].astype(BF16), kw_new[...], vw_new[...])


def nsa_s_attn(pt_flat, ix_flat, qpad, cache3d, kva_s, wincache, *, n_pages, n_pick):
    n_b = qpad.shape[0]
    nwin = wincache.shape[1]

    def kv_map(r, lane0):
        def f(b, h, pt, ix):
            j = ix[(b * NSA_KV_HEADS + h) * n_pick + r]
            return (pt[b * n_pages + j // 2], j % 2, lane0 + h // 2)
        return f

    new_map = lambda lane0: (lambda b, h, pt, ix: (b, 0, lane0 + h // 2))
    out_spec = pl.BlockSpec((None, None, 8, LANES), lambda b, h, pt, ix: (b, h, 0, 0))
    grid_spec = pltpu.PrefetchScalarGridSpec(
        num_scalar_prefetch=2,
        grid=(n_b, NSA_KV_HEADS),
        in_specs=[pl.BlockSpec((None, None, 8, LANES), lambda b, h, pt, ix: (b, h, 0, 0))]
        + [pl.BlockSpec((None, SEL_BLOCK, LANES), kv_map(r, 4)) for r in range(n_pick)]
        + [pl.BlockSpec((None, SEL_BLOCK, LANES), kv_map(r, 6)) for r in range(n_pick)]
        + [pl.BlockSpec((None, 1, LANES), new_map(l0)) for l0 in (0, 2, 4, 6)]
        + [pl.BlockSpec((None, nwin, LANES), new_map(0)), pl.BlockSpec((None, nwin, LANES), new_map(2))],
        out_specs=[out_spec, out_spec],
    )
    return pl.pallas_call(
        functools.partial(_nsa_s_attn_kernel, n_pick=n_pick),
        grid_spec=grid_spec,
        out_shape=[jax.ShapeDtypeStruct((n_b, NSA_KV_HEADS, 8, LANES), F32)] * 2,
        compiler_params=_params("parallel", "arbitrary"),
        name="nsa_s_attn",
    )(pt_flat, ix_flat, qpad, *([cache3d] * (2 * n_pick)), *([kva_s] * 4), wincache, wincache)


def _nsa_out_s_kernel(oc_ref, os_ref, ow_ref, g0_ref, g1_ref, g2_ref, w_ref, r_ref, o_ref):
    o = g0_ref[...] * oc_ref[...] + g1_ref[...] * os_ref[...] + g2_ref[...] * ow_ref[...]
    o_ref[...] = r_ref[...] + _dot(o.astype(BF16), w_ref[...])


def nsa_out_s(oc, osel, ow, g0, g1, g2, w, res):
    m, d = res.shape
    full = pl.BlockSpec((m, d), lambda i: (0, 0))
    return pl.pallas_call(
        _nsa_out_s_kernel,
        grid=(1,),
        in_specs=[full] * 6 + [pl.BlockSpec(w.shape, lambda i: (0, 0)), full],
        out_specs=full,
        out_shape=jax.ShapeDtypeStruct((m, d), F32),
        compiler_params=_params("arbitrary"),
        name="nsa_out_s",
    )(oc, osel, ow, g0, g1, g2, w, res)


def _sconv_kernel(x_ref, g_ref, win_ref, wc_ref, wout_ref, o_ref, st_ref, carry_ref):
    d = x_ref.shape[1]
    tm = x_ref.shape[0]

    @pl.when(pl.program_id(1) == 0)
    def _():
        carry_ref[...] = jnp.zeros(carry_ref.shape, F32)

    x = x_ref[...]
    xn = _rms(x, g_ref[...]).astype(BF16)
    b_gate = _dot(xn, win_ref[:, 0:d])
    pre = _dot(xn, win_ref[:, d:2 * d]) * _dot(xn, win_ref[:, 2 * d:3 * d])
    row = lax.broadcasted_iota(jnp.int32, (tm, d), 0)
    back1 = jnp.where(row == 0, carry_ref[7:8, :], pltpu.roll(pre, 1, 0))
    back2 = jnp.where(row == 0, carry_ref[6:7, :], jnp.where(row == 1, carry_ref[7:8, :], pltpu.roll(pre, 2, 0)))
    y = back2 * wc_ref[0:1, :] + back1 * wc_ref[1:2, :] + pre * wc_ref[2:3, :]
    tail = pre[tm - 8:tm]
    carry_ref[...] = tail
    st_ref[...] = tail
    o_ref[...] = x + _dot((b_gate * y).astype(BF16), wout_ref[...])


def sconv_prompt(x, g, w_in, w_conv, w_out, *, n_b, t, tm):
    d = x.shape[1]
    nt = t // tm
    row = lambda b, i: (b * nt + i, 0)
    const = lambda b, i: (0, 0)
    return pl.pallas_call(
        _sconv_kernel,
        grid=(n_b, nt),
        in_specs=[pl.BlockSpec((tm, d), row), pl.BlockSpec((1, d), const),
                  pl.BlockSpec((d, 3 * d), const, pipeline_mode=pl.Buffered(1)),
                  pl.BlockSpec(w_conv.shape, const),
                  pl.BlockSpec((d, d), const, pipeline_mode=pl.Buffered(1))],
        out_specs=[pl.BlockSpec((tm, d), row), pl.BlockSpec((None, 8, d), lambda b, i: (b, 0, 0))],
        out_shape=[jax.ShapeDtypeStruct((n_b * t, d), F32), jax.ShapeDtypeStruct((n_b, 8, d), F32)],
        scratch_shapes=[pltpu.VMEM((8, d), F32)],
        compiler_params=_params("parallel", "arbitrary"),
        name="sconv_prompt",
    )(x, g, w_in, w_conv, w_out)


def _sconv_s_kernel(x_ref, g_ref, win_ref, wc_ref, wout_ref, p0_ref, p1_ref, o_ref, pre_ref):
    d = x_ref.shape[1]
    x = x_ref[...]
    xn = _rms(x, g_ref[...]).astype(BF16)
    b_gate = _dot(xn, win_ref[:, 0:d])
    pre = _dot(xn, win_ref[:, d:2 * d]) * _dot(xn, win_ref[:, 2 * d:3 * d])
    y = p0_ref[...] * wc_ref[0:1, :] + p1_ref[...] * wc_ref[1:2, :] + pre * wc_ref[2:3, :]
    pre_ref[...] = pre
    o_ref[...] = x + _dot((b_gate * y).astype(BF16), wout_ref[...])


def sconv_sample(x, g, w_in, w_conv, w_out, past0, past1):
    m, d = x.shape
    full = lambda a: pl.BlockSpec(a.shape, lambda i: (0,) * a.ndim)
    args = (x, g, w_in, w_conv, w_out, past0, past1)
    return pl.pallas_call(
        _sconv_s_kernel,
        grid=(1,),
        in_specs=[full(a) for a in args],
        out_specs=[pl.BlockSpec((m, d), lambda i: (0, 0))] * 2,
        out_shape=[jax.ShapeDtypeStruct((m, d), F32)] * 2,
        compiler_params=_params("arbitrary"),
        name="sconv_sample",
    )(*args)


def _moba_in_kernel(x_ref, g_ref, w_ref, cos_ref, sin_ref, q_ref, rows_ref, kb_ref, vb_ref, km_ref):
    d = x_ref.shape[1]
    xn = _rms(x_ref[...], g_ref[...]).astype(BF16)
    cos = cos_ref[...]
    sin = sin_ref[...]
    q_ref[...] = (_rope(_dot(xn, w_ref[:, 0:d]), cos, sin) * SCALE).astype(BF16)
    k = _rope(_dot(xn, w_ref[:, d:2 * d]), cos, sin)
    v = _dot(xn, w_ref[:, 2 * d:3 * d])
    rows_ref[:, 0:d] = k
    rows_ref[:, d:2 * d] = v
    kb_ref[...] = k.astype(BF16)
    vb_ref[...] = v.astype(BF16)
    km_ref[...] = jnp.sum(k, axis=0, keepdims=True) * (1.0 / MOBA_BLOCK)


def moba_in(x, g, w, cos, sin, *, tm):
    m, d = x.shape
    nt = cos.shape[0] // tm
    row = lambda i: (i, 0)
    tab = lambda i: (i % nt, 0)
    const = lambda i: (0, 0)
    return pl.pallas_call(
        _moba_in_kernel,
        grid=(m // tm,),
        in_specs=[pl.BlockSpec((tm, d), row), pl.BlockSpec((1, d), const),
                  pl.BlockSpec((d, 3 * d), const, pipeline_mode=pl.Buffered(1)),
                  pl.BlockSpec((tm, LANES), tab), pl.BlockSpec((tm, LANES), tab)],
        out_specs=[pl.BlockSpec((tm, d), row), pl.BlockSpec((tm, 2 * d), row), pl.BlockSpec((tm, d), row),
                   pl.BlockSpec((tm, d), row), pl.BlockSpec((None, 1, d), lambda i: (i, 0, 0))],
        out_shape=[jax.ShapeDtypeStruct((m, d), BF16), jax.ShapeDtypeStruct((m, 2 * d), F32),
                   jax.ShapeDtypeStruct((m, d), BF16), jax.ShapeDtypeStruct((m, d), BF16),
                   jax.ShapeDtypeStruct((m // tm, 1, d), F32)],
        compiler_params=_params("parallel"),
        name="moba_in",
    )(x, g, w, cos, sin)


def _moba_attn_kernel(q_ref, k_ref, v_ref, km_ref, o_ref, *, tq, n_top):
    i = pl.program_id(2)
    c0 = i * tq
    pos = c0 + lax.broadcasted_iota(jnp.int32, (tq, 1), 0)
    cur = pos // MOBA_BLOCK
    blk = lax.broadcasted_iota(jnp.int32, (tq, LANES), 1)
    outs = []
    for hh in range(2):
        hs = slice(hh * HEAD_DIM, (hh + 1) * HEAD_DIM)
        q = q_ref[:, hs]
        m1, m2, m3 = _split3(km_ref[:, hs])
        gate = _dot_nt(q, m1) + _dot_nt(q, m2) + _dot_nt(q, m3)
        gate = jnp.where(blk < cur, gate, -jnp.inf)
        allow = jnp.where(blk == cur, 1.0, _topk_mask(gate, n_top))

        def body(t, carry):
            k0 = pl.multiple_of(t * MOBA_BLOCK, MOBA_BLOCK)
            k = k_ref[pl.ds(k0, MOBA_BLOCK), hh * HEAD_DIM:(hh + 1) * HEAD_DIM]
            v = v_ref[pl.ds(k0, MOBA_BLOCK), hh * HEAD_DIM:(hh + 1) * HEAD_DIM]
            chosen = jnp.sum(jnp.where(blk == t, allow, 0.0), axis=-1, keepdims=True)
            kpos = k0 + lax.broadcasted_iota(jnp.int32, (tq, MOBA_BLOCK), 1)
            ok = (chosen > 0.5) & (kpos <= pos)
            return _online_update(carry, _dot_nt(q, k), ok, v)

        _, l, acc = lax.fori_loop(0, (c0 + tq - 1) // MOBA_BLOCK + 1, body, _online_init(tq, HEAD_DIM))
        outs.append(acc / jnp.maximum(l, 1e-30))
    o_ref[...] = jnp.concatenate(outs, axis=1).astype(BF16)


def moba_attn(q, kb, vb, kmean, *, n_b, t, tq):
    nq = t // tq
    d = q.shape[1]
    n_top = min(MOBA_TOPK, t // MOBA_BLOCK)
    qmap = lambda b, hp, i: (b * nq + i, hp)
    kvmap = lambda b, hp, i: (b, 0, hp)
    return pl.pallas_call(
        functools.partial(_moba_attn_kernel, tq=tq, n_top=n_top),
        grid=(n_b, d // LANES, nq),
        in_specs=[pl.BlockSpec((tq, LANES), qmap), pl.BlockSpec((None, t, LANES), kvmap),
                  pl.BlockSpec((None, t, LANES), kvmap), pl.BlockSpec((None, LANES, LANES), kvmap)],
        out_specs=pl.BlockSpec((tq, LANES), qmap),
        out_shape=jax.ShapeDtypeStruct((n_b * t, d), BF16),
        compiler_params=_params("parallel", "parallel", "arbitrary"),
        name="moba_attn",
    )(q, kb.reshape(n_b, t, d), vb.reshape(n_b, t, d), kmean)


def _kmean_kernel(pt_ref, *refs, pps):
    del pt_ref
    pages, out_ref = refs[:pps], refs[pps]
    per_blk = MOBA_BLOCK // PAGE_SIZE
    rows = []
    for j in range(pps // per_blk):
        s = jnp.sum(pages[per_blk * j][...], axis=0, keepdims=True)
        for e in range(1, per_blk):
            s = s + jnp.sum(pages[per_blk * j + e][...], axis=0, keepdims=True)
        rows.append(s * (1.0 / MOBA_BLOCK))
    out_ref[...] = jnp.concatenate(rows, axis=0)


def moba_kmean(cache3d, pt_flat, *, n_b, n_pages, d):
    pps = min(16, n_pages)
    per_blk = MOBA_BLOCK // PAGE_SIZE
    page_map = lambda k: (lambda b, s, pt: (pt[b * n_pages + s * pps + k], 0, 0))
    grid_spec = pltpu.PrefetchScalarGridSpec(
        num_scalar_prefetch=1,
        grid=(n_b, n_pages // pps),
        in_specs=[pl.BlockSpec((None, PAGE_SIZE, d), page_map(k)) for k in range(pps)],
        out_specs=pl.BlockSpec((None, pps // per_blk, d), lambda b, s, pt: (b, s, 0)),
    )
    return pl.pallas_call(
        functools.partial(_kmean_kernel, pps=pps),
        grid_spec=grid_spec,
        out_shape=jax.ShapeDtypeStruct((n_b, n_pages // per_blk, d), F32),
        compiler_params=_params("parallel", "arbitrary"),
        name="moba_kmean",
    )(pt_flat, *([cache3d] * pps))


def _moba_s_gate_kernel(q_ref, km_ref, seg_ref, idx_ref, *, n_top):
    prod = km_ref[...] * q_ref[...].astype(F32)
    gate = _dot3(prod, seg_ref[...])
    row = lax.broadcasted_iota(jnp.int32, gate.shape, 0).astype(F32)
    slot = lax.broadcasted_iota(jnp.int32, (8, LANES), 0)
    picks = jnp.zeros((8, LANES), F32)
    for r in range(n_top):
        m = jnp.max(gate, axis=0, keepdims=True)
        idx = jnp.min(jnp.where(gate == m, row, 1e9), axis=0, keepdims=True)
        gate = jnp.where(row == idx, -jnp.inf, gate)
        picks = jnp.where(slot == r, idx, picks)
    idx_ref[...] = picks.astype(jnp.int32)


def moba_s_gate(q3, kmean, seg, *, n_top):
    n_b, nblk, d = kmean.shape
    per_b = lambda b: (b, 0, 0)
    return pl.pallas_call(
        functools.partial(_moba_s_gate_kernel, n_top=n_top),
        grid=(n_b,),
        in_specs=[pl.BlockSpec((None, 1, d), per_b), pl.BlockSpec((None, nblk, d), per_b),
                  pl.BlockSpec(seg.shape, lambda b: (0, 0))],
        out_specs=pl.BlockSpec((None, 8, LANES), per_b),
        out_shape=jax.ShapeDtypeStruct((n_b, 8, LANES), jnp.int32),
        compiler_params=_params("parallel"),
        name="moba_s_gate",
    )(q3, kmean, seg)


def _moba_s_attn_kernel(pt_ref, ix_ref, q_ref, *refs, n_blocks):
    del pt_ref, ix_ref
    kblk = refs[:n_blocks]
    vblk = refs[n_blocks:2 * n_blocks]
    k_new, v_new, o_ref = refs[2 * n_blocks:]
    k = jnp.concatenate([r[...] for r in kblk], axis=0).astype(BF16)
    v = jnp.concatenate([r[...] for r in vblk], axis=0).astype(BF16)
    o_ref[...] = _attend_with_new(q_ref[...], k, v, k_new[...], v_new[...])


def moba_s_attn(pt_flat, ix_flat, qpad, cache3d, kb_s, vb_s, *, n_pages, n_top, n_heads):
    n_b = qpad.shape[0]
    per_blk = MOBA_BLOCK // PAGE_SIZE
    n_blocks = n_top * per_blk
    v_lane0 = n_heads * HEAD_DIM // LANES

    def kv_map(r, lane0):
        def f(b, h, pt, ix):
            j = ix[(b * n_heads + h) * n_top + r // per_blk]
            return (pt[b * n_pages + j * per_blk + r % per_blk], 0, lane0 + h // 2)
        return f

    new_map = lambda b, h, pt, ix: (b, 0, h // 2)
    qo_spec = pl.BlockSpec((None, None, 8, LANES), lambda b, h, pt, ix: (b, h, 0, 0))
    grid_spec = pltpu.PrefetchScalarGridSpec(
        num_scalar_prefetch=2,
        grid=(n_b, n_heads),
        in_specs=[qo_spec]
        + [pl.BlockSpec((None, PAGE_SIZE, LANES), kv_map(r, 0)) for r in range(n_blocks)]
        + [pl.BlockSpec((None, PAGE_SIZE, LANES), kv_map(r, v_lane0)) for r in range(n_blocks)]
        + [pl.BlockSpec((None, 1, LANES), new_map)] * 2,
        out_specs=qo_spec,
    )
    return pl.pallas_call(
        functools.partial(_moba_s_attn_kernel, n_blocks=n_blocks),
        grid_spec=grid_spec,
        out_shape=jax.ShapeDtypeStruct((n_b, n_heads, 8, LANES), F32),
        compiler_params=_params("parallel", "arbitrary"),
        name="moba_s_attn",
    )(pt_flat, ix_flat, qpad, *([cache3d] * (2 * n_blocks)), kb_s, vb_s)


def _layer_norm_silu(y, g, b):
    yc = y - jnp.mean(y, axis=-1, keepdims=True)
    yn = yc * lax.rsqrt(jnp.mean(yc * yc, axis=-1, keepdims=True) + LN_EPS) * g + b
    return yn * _sigmoid(yn)


def _conf_kernel(x_ref, g_ref, w1_ref, wdw_ref, bdw_ref, lg_ref, lb_ref, w2_ref, o_ref, st_ref, ubuf_ref, *, hist):
    tm, d = x_ref.shape
    width = wdw_ref.shape[0]

    @pl.when(pl.program_id(1) == 0)
    def _():
        ubuf_ref[0:hist, :] = jnp.zeros((hist, d), F32)

    x = x_ref[...]
    xn = _rms(x, g_ref[...]).astype(BF16)
    u = _dot(xn, w1_ref[:, 0:d]) * _sigmoid(_dot(xn, w1_ref[:, d:2 * d]))
    ubuf_ref[hist:hist + tm, :] = u
    base = hist - (width - 1)
    y = bdw_ref[...] + ubuf_ref[base:base + tm, :] * wdw_ref[0:1, :]
    for k in range(1, width):
        y = y + ubuf_ref[base + k:base + k + tm, :] * wdw_ref[k:k + 1, :]
    z = _layer_norm_silu(y, lg_ref[...], lb_ref[...])
    o_ref[...] = x + _dot(z.astype(BF16), w2_ref[...])
    tail = ubuf_ref[tm:tm + hist, :]
    st_ref[...] = tail
    ubuf_ref[0:hist, :] = tail


def conf_prompt(x, g, w1, wdw, bdw, lg, lb, w2, *, n_b, t, tm):
    d = x.shape[1]
    nt = t // tm
    hist = 32
    row = lambda b, i: (b * nt + i, 0)
    const = lambda b, i: (0, 0)
    return pl.pallas_call(
        functools.partial(_conf_kernel, hist=hist),
        grid=(n_b, nt),
        in_specs=[pl.BlockSpec((tm, d), row), pl.BlockSpec((1, d), const),
                  pl.BlockSpec((d, 2 * d), const, pipeline_mode=pl.Buffered(1)),
                  pl.BlockSpec(wdw.shape, const), pl.BlockSpec((1, d), const), pl.BlockSpec((1, d), const),
                  pl.BlockSpec((1, d), const), pl.BlockSpec((d, d), const, pipeline_mode=pl.Buffered(1))],
        out_specs=[pl.BlockSpec((tm, d), row), pl.BlockSpec((None, hist, d), lambda b, i: (b, 0, 0))],
        out_shape=[jax.ShapeDtypeStruct((n_b * t, d), F32), jax.ShapeDtypeStruct((n_b, hist, d), F32)],
        scratch_shapes=[pltpu.VMEM((hist + tm, d), F32)],
        compiler_params=_params("parallel", "arbitrary"),
        name="conf_prompt",
    )(x, g, w1, wdw, bdw, lg, lb, w2)


def _conf_s_kernel(x_ref, g_ref, w1_ref, wdw_ref, bdw_ref, lg_ref, lb_ref, w2_ref, past_ref, o_ref, u_ref):
    d = x_ref.shape[1]
    width = wdw_ref.shape[0]
    x = x_ref[...]
    xn = _rms(x, g_ref[...]).astype(BF16)
    u = _dot(xn, w1_ref[:, 0:d]) * _sigmoid(_dot(xn, w1_ref[:, d:2 * d]))
    y = bdw_ref[...] + past_ref[0] * wdw_ref[0:1, :]
    for k in range(1, width - 1):
        y = y + past_ref[k] * wdw_ref[k:k + 1, :]
    y = y + u * wdw_ref[width - 1:width, :]
    z = _layer_norm_silu(y, lg_ref[...], lb_ref[...])
    u_ref[...] = u
    o_ref[...] = x + _dot(z.astype(BF16), w2_ref[...])


def conf_sample(x, g, w1, wdw, bdw, lg, lb, w2, past_t):
    m, d = x.shape
    full = lambda a: pl.BlockSpec(a.shape, lambda i: (0,) * a.ndim)
    args = (x, g, w1, wdw, bdw, lg, lb, w2, past_t)
    return pl.pallas_call(
        _conf_s_kernel,
        grid=(1,),
        in_specs=[full(a) for a in args],
        out_specs=[pl.BlockSpec((m, d), lambda i: (0, 0))] * 2,
        out_shape=[jax.ShapeDtypeStruct((m, d), F32)] * 2,
        compiler_params=_params("arbitrary"),
        name="conf_sample",
    )(*args)


def _rope_tables(pos):
    half = HEAD_DIM // 2
    inv_freq = ROPE_THETA ** (-jnp.arange(half, dtype=F32) / half)
    ang = pos.astype(F32)[:, None] * inv_freq[None, :]
    cos = jnp.cos(ang)
    sin = jnp.sin(ang)
    cos = jnp.concatenate([cos, cos], axis=-1)
    sin = jnp.concatenate([-sin, sin], axis=-1)
    return jnp.tile(cos, (1, LANES // HEAD_DIM)), jnp.tile(sin, (1, LANES // HEAD_DIM))


def _overlap(n_cmp_rows, n_cols):
    i = jnp.arange(n_cmp_rows, dtype=jnp.int32)[:, None]
    j = jnp.arange(n_cols, dtype=jnp.int32)[None, :]
    start = i * CMP_STRIDE
    hit = (start <= j * SEL_BLOCK + (SEL_BLOCK - 1)) & (start + (CMP_BLOCK - 1) >= j * SEL_BLOCK)
    return hit.astype(BF16)


def _pad_heads_to_lane_pairs(x, rows):
    n, nh, r, hd = x.shape
    z = jnp.zeros_like(x)
    even = jnp.concatenate([x, z], axis=-1)
    odd = jnp.concatenate([z, x], axis=-1)
    is_even = (jnp.arange(nh) % 2 == 0)[None, :, None, None]
    out = jnp.where(is_even, even, odd)
    return jnp.pad(out, ((0, 0), (0, 0), (0, rows - r), (0, 0)))


def _take_lane_half(x, r):
    nh = x.shape[1]
    is_even = (jnp.arange(nh) % 2 == 0)[None, :, None, None]
    return jnp.where(is_even, x[:, :, :r, :HEAD_DIM], x[:, :, :r, HEAD_DIM:])


def _nsa_layer(hp, hs, g, cache_kv, cache_win, pt_flat, n_pages, w, *, n_b, t, n_s):
    w_in, pe_k, w1_k, w2_k, pe_v, w1_v, w2_v, w_out = w
    d = hp.shape[1]
    past_len = n_pages * PAGE_SIZE
    w_in_p = jnp.pad(w_in, ((0, 0), (0, 2688 - w_in.shape[1]))).astype(BF16)
    w_out_b = w_out.astype(BF16)
    half = CMP_BLOCK * HEAD_DIM // 2

    def cmp_weights(pe, w1, w2):
        return (pe[:CMP_STRIDE].reshape(1, half), pe[CMP_STRIDE:].reshape(1, half),
                w1[:half].astype(BF16), w1[half:].astype(BF16), w2.astype(BF16))

    wk = cmp_weights(pe_k, w1_k, w2_k)
    wv = cmp_weights(pe_v, w1_v, w2_v)

    cos_p, sin_p = _rope_tables(jnp.arange(t, dtype=jnp.int32))
    q, rows, kva, win, gates = nsa_in(hp, g, w_in_p, cos_p, sin_p, tm=512)
    npg_p = t // PAGE_SIZE
    cend_p = jnp.arange(npg_p * 8, dtype=jnp.int32) * CMP_STRIDE + (CMP_BLOCK - 1)
    kc, vc = nsa_compress(rows.reshape(n_b * npg_p, PAGE_SIZE, 1024), jnp.arange(n_b * npg_p, dtype=jnp.int32),
                          n_b, npg_p, wk, wv, *_rope_tables(cend_p))
    tks = 256
    tile = jnp.arange(t // tks, dtype=jnp.int32)[:, None, None]
    blk = jnp.arange(LANES, dtype=jnp.int32)[None, :, None]
    key = jnp.arange(tks, dtype=jnp.int32)[None, None, :]
    emat = (blk == (tile * tks + key) // SEL_BLOCK).astype(BF16)
    o = nsa_attn(q, gates, kc, vc, kva, _overlap(npg_p * 8, LANES), emat, n_b=n_b, t=t, tq=128, tks=tks)
    hp = mm_res(o, w_out_b, hp, tm=512)
    kv_p = rows.reshape(n_b, t, 4, NSA_KV_HEADS, HEAD_DIM)
    keep = min(WINDOW, t)
    win_p = win.reshape(n_b, t, 2, NSA_KV_HEADS, HEAD_DIM)[:, t - keep:]

    cos_s, sin_s = _rope_tables(jnp.full((n_s,), past_len, jnp.int32))
    q_s, rows_s, kva_s, win_s, gates_s = nsa_in(hs, g, w_in_p, cos_s, sin_s, tm=n_s)
    cend_s = jnp.arange(n_pages * 8, dtype=jnp.int32) * CMP_STRIDE + (CMP_BLOCK - 1)
    cache3d = cache_kv.reshape(cache_kv.shape[0], PAGE_SIZE, 1024)
    kc_s, vc_s = nsa_compress(cache3d, pt_flat, n_s, n_pages, wk, wv, *_rope_tables(cend_s))
    n_sel = -(-(past_len + 1) // SEL_BLOCK)
    n_pick = min(SEL_TOPN, n_sel) - 1
    q4 = q_s.reshape(n_s, NSA_KV_HEADS, NSA_GROUP, HEAD_DIM)
    eye = jnp.eye(NSA_KV_HEADS, dtype=bool)[None, :, None, :, None]
    qbd = jnp.where(eye, q4[:, :, :, None, :], jnp.zeros((), BF16)).reshape(n_s, 16, 256)
    gm = (jnp.arange(8)[:, None] == jnp.arange(16)[None, :] // NSA_GROUP).astype(BF16)
    n_blk_pad = -(-n_sel // LANES) * LANES
    oc16, idx = nsa_s_sel(qbd, kc_s, vc_s, _overlap(n_pages * 8, n_blk_pad), gm, pos=past_len, n_pick=n_pick)
    oc5 = oc16.reshape(n_s, NSA_KV_HEADS, NSA_GROUP, NSA_KV_HEADS, HEAD_DIM)
    o_c = jnp.sum(jnp.where(eye, oc5, 0.0), axis=3).reshape(n_s, d)
    ix_flat = idx[:, :NSA_KV_HEADS, :n_pick].reshape(-1)
    qpad = _pad_heads_to_lane_pairs(q4, 8)
    wincache = cache_win.reshape(n_s, cache_win.shape[1], 512)
    os_p, ow_p = nsa_s_attn(pt_flat, ix_flat, qpad, cache3d, kva_s.reshape(n_s, 1, 1024), wincache,
                            n_pages=n_pages, n_pick=n_pick)
    o_s = _take_lane_half(os_p, NSA_GROUP).reshape(n_s, d)
    o_w = _take_lane_half(ow_p, NSA_GROUP).reshape(n_s, d)
    g3 = jnp.repeat(gates_s[:, :48].reshape(n_s, 16, 3), HEAD_DIM, axis=1)
    hs = nsa_out_s(o_c, o_s, o_w, g3[:, :, 0], g3[:, :, 1], g3[:, :, 2], w_out_b, hs)
    kv_s = rows_s.reshape(n_s, 1, 4, NSA_KV_HEADS, HEAD_DIM)
    win_new = win_s.reshape(n_s, 1, 2, NSA_KV_HEADS, HEAD_DIM)
    win_all = jnp.concatenate([cache_win, win_new], axis=1)
    win_s_out = win_all[:, win_all.shape[1] - cache_win.shape[1]:]
    return hp, hs, kv_p, kv_s, win_p, win_s_out


def _sconv_layer(hp, hs, g, state, w, *, n_b, t):
    w_in, w_conv, w_out = w
    w_in_b = w_in.astype(BF16)
    w_out_b = w_out.astype(BF16)
    hp, st = sconv_prompt(hp, g, w_in_b, w_conv, w_out_b, n_b=n_b, t=t, tm=256)
    st_p = st[:, 8 - (w_conv.shape[0] - 1):]
    hs, pre = sconv_sample(hs, g, w_in_b, w_conv, w_out_b, state[:, 0], state[:, 1])
    st_s = jnp.concatenate([state[:, 1:], pre[:, None, :]], axis=1)
    return hp, hs, st_p, st_s


def _moba_layer(hp, hs, g, cache_kv, pt_flat, n_pages, w, *, n_b, t, n_s):
    w_qkv, w_out = w
    d = hp.shape[1]
    n_heads = d // HEAD_DIM
    past_len = n_pages * PAGE_SIZE
    w_qkv_b = w_qkv.astype(BF16)
    w_out_b = w_out.astype(BF16)

    cos_p, sin_p = _rope_tables(jnp.arange(t, dtype=jnp.int32))
    q, rows, kb, vb, km = moba_in(hp, g, w_qkv_b, cos_p, sin_p, tm=MOBA_BLOCK)
    nblk = t // MOBA_BLOCK
    kmean = jnp.pad(km.reshape(n_b, nblk, d), ((0, 0), (0, LANES - nblk), (0, 0)))
    o = moba_attn(q, kb, vb, kmean, n_b=n_b, t=t, tq=512)
    hp = mm_res(o, w_out_b, hp, tm=512)
    kv_p = rows.reshape(n_b, t, 2, n_heads, HEAD_DIM)

    cos_s, sin_s = _rope_tables(jnp.full((n_s,), past_len, jnp.int32))
    q_s, rows_s, kb_s, vb_s, _ = moba_in(hs, g, w_qkv_b, cos_s, sin_s, tm=n_s)
    cache3d = cache_kv.reshape(cache_kv.shape[0], PAGE_SIZE, 2 * d)
    kmean_s = moba_kmean(cache3d, pt_flat, n_b=n_s, n_pages=n_pages, d=d)
    n_top = min(MOBA_TOPK, -(-(past_len + 1) // MOBA_BLOCK))
    seg = (jnp.arange(d)[:, None] // HEAD_DIM == jnp.arange(LANES)[None, :]).astype(BF16)
    idx = moba_s_gate(q_s.reshape(n_s, 1, d), kmean_s, seg, n_top=n_top)
    ix_flat = jnp.transpose(idx[:, :n_top, :n_heads], (0, 2, 1)).reshape(-1)
    qpad = _pad_heads_to_lane_pairs(q_s.reshape(n_s, n_heads, 1, HEAD_DIM), 8)
    o_p = moba_s_attn(pt_flat, ix_flat, qpad, cache3d, kb_s.reshape(n_s, 1, d), vb_s.reshape(n_s, 1, d),
                      n_pages=n_pages, n_top=n_top, n_heads=n_heads)
    o_s = _take_lane_half(o_p, 1).reshape(n_s, d).astype(BF16)
    hs = mm_res(o_s, w_out_b, hs, tm=n_s)
    kv_s = rows_s.reshape(n_s, 1, 2, n_heads, HEAD_DIM)
    return hp, hs, kv_p, kv_s


def _conf_layer(hp, hs, g, state, w, *, n_b, t):
    w_pw1, w_dw, b_dw, ln_g, ln_b, w_pw2 = w
    d = hp.shape[1]
    r = lambda a: a.reshape(1, d)
    args = (w_pw1.astype(BF16), w_dw, r(b_dw), r(ln_g), r(ln_b), w_pw2.astype(BF16))
    hp, st = conf_prompt(hp, g, *args, n_b=n_b, t=t, tm=256)
    st_p = st[:, st.shape[1] - (w_dw.shape[0] - 1):]
    hs, u = conf_sample(hs, g, *args, jnp.transpose(state, (1, 0, 2)))
    st_s = jnp.concatenate([state[:, 1:], u[:, None, :]], axis=1)
    return hp, hs, st_p, st_s


def kernel(x_prompt, x_sample, cache_nsa_kv, cache_nsa_win, state_sconv, cache_moba_kv, state_conformer,
           page_table, norm_mix, norm_ffn, norm_final, ffn_w_up, ffn_w_down,
           nsa_w_in, nsa_pe_k, nsa_w1_k, nsa_w2_k, nsa_pe_v, nsa_w1_v, nsa_w2_v, nsa_w_out,
           sconv_w_in, sconv_w_conv, sconv_w_out, moba_w_qkv, moba_w_out,
           conf_w_pw1, conf_w_dw, conf_b_dw, conf_ln_g, conf_ln_b, conf_w_pw2):
    n_b, t, d = x_prompt.shape
    n_s = x_sample.shape[0]
    depth = norm_mix.shape[0]
    n_pages = page_table.shape[1]
    pt_flat = page_table.reshape(-1).astype(jnp.int32)
    hp = x_prompt.reshape(n_b * t, d)
    hs = x_sample.reshape(n_s, d)
    outs = {k: [] for k in ("nsa_kv_p", "nsa_kv_s", "nsa_win_p", "nsa_win_s", "sconv_p", "sconv_s",
                            "moba_p", "moba_s", "conf_p", "conf_s")}
    for i in range(depth):
        kind, j = i % 4, i // 4
        g = norm_mix[i].reshape(1, d)
        if kind == 0:
            w = (nsa_w_in[j], nsa_pe_k[j], nsa_w1_k[j], nsa_w2_k[j], nsa_pe_v[j], nsa_w1_v[j], nsa_w2_v[j],
                 nsa_w_out[j])
            hp, hs, kv_p, kv_s, win_p, win_s = _nsa_layer(hp, hs, g, cache_nsa_kv[j], cache_nsa_win[j], pt_flat,
                                                          n_pages, w, n_b=n_b, t=t, n_s=n_s)
            outs["nsa_kv_p"].append(kv_p)
            outs["nsa_kv_s"].append(kv_s)
            outs["nsa_win_p"].append(win_p)
            outs["nsa_win_s"].append(win_s)
        elif kind == 1:
            hp, hs, st_p, st_s = _sconv_layer(hp, hs, g, state_sconv[j],
                                              (sconv_w_in[j], sconv_w_conv[j], sconv_w_out[j]), n_b=n_b, t=t)
            outs["sconv_p"].append(st_p)
            outs["sconv_s"].append(st_s)
        elif kind == 2:
            hp, hs, kv_p, kv_s = _moba_layer(hp, hs, g, cache_moba_kv[j], pt_flat, n_pages,
                                             (moba_w_qkv[j], moba_w_out[j]), n_b=n_b, t=t, n_s=n_s)
            outs["moba_p"].append(kv_p)
            outs["moba_s"].append(kv_s)
        else:
            w = (conf_w_pw1[j], conf_w_dw[j], conf_b_dw[j], conf_ln_g[j], conf_ln_b[j], conf_w_pw2[j])
            hp, hs, st_p, st_s = _conf_layer(hp, hs, g, state_conformer[j], w, n_b=n_b, t=t)
            outs["conf_p"].append(st_p)
            outs["conf_s"].append(st_s)
        gf = norm_ffn[i].reshape(1, d)
        wu = ffn_w_up[i].astype(BF16)
        wd = ffn_w_down[i].astype(BF16)
        final = i == depth - 1
        gfin = norm_final.reshape(1, d)
        hp = ffn(hp, gf, wu, wd, gfin, tm=512, final=final)
        hs = ffn(hs, gf, wu, wd, gfin, tm=n_s, final=final)
    return (hp.reshape(n_b, t, d), hs.reshape(n_s, 1, d),
            jnp.stack(outs["nsa_kv_p"]), jnp.stack(outs["nsa_kv_s"]),
            jnp.stack(outs["nsa_win_p"]), jnp.stack(outs["nsa_win_s"]),
            jnp.stack(outs["sconv_p"]), jnp.stack(outs["sconv_s"]),
            jnp.stack(outs["moba_p"]), jnp.stack(outs["moba_s"]),
            jnp.stack(outs["conf_p"]), jnp.stack(outs["conf_s"]))
```

```python
import functools

import jax
import jax.numpy as jnp
from jax import lax
from jax.experimental import pallas as pl
from jax.experimental.pallas import tpu as pltpu

F32 = jnp.float32
BF16 = jnp.bfloat16

HEAD_DIM = 64
ROPE_THETA = 10000.0
RMS_EPS = 1e-6
LN_EPS = 1e-5
NSA_KV_HEADS = 4
NSA_GROUP = 4
CMP_STRIDE = 16
CMP_BLOCK = 32
SEL_BLOCK = 64
SEL_TOPN = 16
WINDOW = 512
MOBA_BLOCK = 256
MOBA_TOPK = 3
PAGE_SIZE = 128
SCALE = HEAD_DIM ** -0.5

LANES = 128
NEG = -1e30
VMEM_LIMIT = 56 * 1024 * 1024


def _params(*sem):
    return pltpu.CompilerParams(dimension_semantics=sem, vmem_limit_bytes=VMEM_LIMIT)


def _dot(a, b):
    return jnp.dot(a, b, preferred_element_type=F32)


def _dot_nt(a, b):
    return lax.dot_general(a, b, (((1,), (1,)), ((), ())), preferred_element_type=F32)


def _split3(x):
    hi = x.astype(BF16)
    r = x - hi.astype(F32)
    mid = r.astype(BF16)
    lo = (r - mid.astype(F32)).astype(BF16)
    return hi, mid, lo


def _dot3(x, m):
    hi, mid, lo = _split3(x)
    return _dot(hi, m) + _dot(mid, m) + _dot(lo, m)


def _dot3_rhs(m, x):
    hi, mid, lo = _split3(x)
    return _dot(m, hi) + _dot(m, mid) + _dot(m, lo)


def _rms(x, g):
    return x * lax.rsqrt(jnp.mean(x * x, axis=-1, keepdims=True) + RMS_EPS) * g


def _sigmoid(x):
    return 1.0 / (1.0 + jnp.exp(-x))


def _rope(x, cos, sin):
    w = x.shape[-1]
    lane = lax.broadcasted_iota(jnp.int32, x.shape, 1)
    first = (lane % HEAD_DIM) < (HEAD_DIM // 2)
    rot = jnp.where(first, pltpu.roll(x, w - HEAD_DIM // 2, 1), pltpu.roll(x, HEAD_DIM // 2, 1))
    reps = w // LANES
    if reps > 1:
        cos = jnp.concatenate([cos] * reps, axis=1)
        sin = jnp.concatenate([sin] * reps, axis=1)
    return x * cos + rot * sin


def _masked_softmax(s, mask):
    s = jnp.where(mask, s, -jnp.inf)
    m = jnp.max(s, axis=-1, keepdims=True)
    m = jnp.where(m > -jnp.inf, m, 0.0)
    p = jnp.exp(s - m)
    return p / jnp.maximum(jnp.sum(p, axis=-1, keepdims=True), 1e-30)


def _topk_mask(score, k):
    lane = lax.broadcasted_iota(jnp.int32, score.shape, 1).astype(F32)
    sel = jnp.zeros(score.shape, F32)
    for _ in range(k):
        m = jnp.max(score, axis=-1, keepdims=True)
        idx = jnp.min(jnp.where(score == m, lane, 1e9), axis=-1, keepdims=True)
        hit = lane == idx
        sel = jnp.where(hit & (m > -jnp.inf), 1.0, sel)
        score = jnp.where(hit, -jnp.inf, score)
    return sel


def _online_update(carry, s, ok, v):
    m, l, acc = carry
    sm = jnp.where(ok, s, NEG)
    m_new = jnp.maximum(m, jnp.max(sm, axis=-1, keepdims=True))
    alpha = jnp.exp(m - m_new)
    p = jnp.where(ok, jnp.exp(sm - m_new), 0.0)
    l = alpha * l + jnp.sum(p, axis=-1, keepdims=True)
    acc = alpha * acc + _dot(p.astype(BF16), v)
    return m_new, l, acc


def _online_init(rows, width):
    return (jnp.full((rows, 1), NEG, F32), jnp.zeros((rows, 1), F32), jnp.zeros((rows, width), F32))


def _rope_t(x, cos, sin):
    r = x.shape[0]
    row = lax.broadcasted_iota(jnp.int32, x.shape, 0)
    first = (row % HEAD_DIM) < (HEAD_DIM // 2)
    rot = jnp.where(first, pltpu.roll(x, r - HEAD_DIM // 2, 0), pltpu.roll(x, HEAD_DIM // 2, 0))
    reps = r // HEAD_DIM
    if reps > 1:
        cos = jnp.concatenate([cos] * reps, axis=0)
        sin = jnp.concatenate([sin] * reps, axis=0)
    return x * cos + rot * sin


def _topk_mask_t(score, k):
    row = lax.broadcasted_iota(jnp.int32, score.shape, 0).astype(F32)
    sel = jnp.zeros(score.shape, F32)
    for _ in range(k):
        m = jnp.max(score, axis=0, keepdims=True)
        idx = jnp.min(jnp.where(score == m, row, 1e9), axis=0, keepdims=True)
        hit = row == idx
        sel = jnp.where(hit & (m > -jnp.inf), 1.0, sel)
        score = jnp.where(hit, -jnp.inf, score)
    return sel


def _online_init_t(cols):
    return (jnp.full((1, cols), NEG, F32), jnp.zeros((1, cols), F32), jnp.zeros((HEAD_DIM, cols), F32))


def _online_update_t(carry, s, bias, v_t):
    m, l, acc = carry
    sm = s + bias
    m_new = jnp.maximum(m, jnp.max(sm, axis=0, keepdims=True))
    alpha = jnp.exp(m - m_new)
    p = jnp.exp(sm - m_new)
    l = alpha * l + jnp.sum(p, axis=0, keepdims=True)
    acc = alpha * acc + _dot(v_t, p.astype(BF16))
    return m_new, l, acc


def _pad_pair(q, odd):
    z = jnp.zeros_like(q)
    return jnp.concatenate([z, q] if odd else [q, z], axis=0)


def _ffn_kernel(x_ref, g_ref, wu_ref, wd_ref, gf_ref, o_ref, *, chunk, final):
    x = x_ref[...]
    xn = _rms(x, g_ref[...]).astype(BF16)
    acc = x
    for c in range(0, wu_ref.shape[1], chunk):
        u = _dot(xn, wu_ref[:, c:c + chunk])
        a = jnp.square(jnp.maximum(u, 0.0)).astype(BF16)
        acc = acc + _dot(a, wd_ref[c:c + chunk, :])
    if final:
        acc = _rms(acc, gf_ref[...])
    o_ref[...] = acc


def ffn(x, g, wu, wd, gf, *, tm, final):
    m, d = x.shape
    dff = wu.shape[1]
    row = lambda i: (i, 0)
    const = lambda i: (0, 0)
    return pl.pallas_call(
        functools.partial(_ffn_kernel, chunk=512, final=final),
        grid=(m // tm,),
        in_specs=[pl.BlockSpec((tm, d), row), pl.BlockSpec((1, d), const),
                  pl.BlockSpec((d, dff), const, pipeline_mode=pl.Buffered(1)),
                  pl.BlockSpec((dff, d), const, pipeline_mode=pl.Buffered(1)),
                  pl.BlockSpec((1, d), const)],
        out_specs=pl.BlockSpec((tm, d), row),
        out_shape=jax.ShapeDtypeStruct((m, d), F32),
        compiler_params=_params("parallel"),
        name="ffn",
    )(x, g, wu, wd, gf)


def _mm_res_kernel(x_ref, w_ref, r_ref, o_ref):
    o_ref[...] = r_ref[...] + _dot(x_ref[...], w_ref[...])


def mm_res(x, w, res, *, tm):
    m, k = x.shape
    n = w.shape[1]
    row = lambda i: (i, 0)
    return pl.pallas_call(
        _mm_res_kernel,
        grid=(m // tm,),
        in_specs=[pl.BlockSpec((tm, k), row), pl.BlockSpec((k, n), lambda i: (0, 0)),
                  pl.BlockSpec((tm, n), row)],
        out_specs=pl.BlockSpec((tm, n), row),
        out_shape=jax.ShapeDtypeStruct((m, n), F32),
        compiler_params=_params("parallel"),
        name="mm_res",
    )(x, w, res)


def _mm_res_t_kernel(xt_ref, wt_ref, r_ref, o_ref):
    o_ref[...] = r_ref[...] + _dot(wt_ref[...], xt_ref[...]).T


def mm_res_t(xt, wt, res, *, tm):
    k, m = xt.shape
    n = wt.shape[0]
    row = lambda i: (i, 0)
    return pl.pallas_call(
        _mm_res_t_kernel,
        grid=(m // tm,),
        in_specs=[pl.BlockSpec((k, tm), lambda i: (0, i)), pl.BlockSpec((n, k), lambda i: (0, 0)),
                  pl.BlockSpec((tm, n), row)],
        out_specs=pl.BlockSpec((tm, n), row),
        out_shape=jax.ShapeDtypeStruct((m, n), F32),
        compiler_params=_params("parallel"),
        name="mm_res_t",
    )(xt, wt, res)


def _nsa_in_kernel(x_ref, g_ref, w_ref, cos_ref, sin_ref, q_ref, rows_ref, kva_ref, win_ref, gate_ref):
    xn = _rms(x_ref[...], g_ref[...]).astype(BF16)
    cos = cos_ref[...]
    sin = sin_ref[...]
    q = _rope(_dot(xn, w_ref[:, 0:1024]), cos, sin) * SCALE
    q_ref[...] = q.astype(BF16)
    kv = _dot(xn, w_ref[:, 1024:2048])
    ks = _rope(kv[:, 512:768], cos, sin)
    rows_ref[:, 0:512] = kv[:, 0:512]
    rows_ref[:, 512:768] = ks
    rows_ref[:, 768:1024] = kv[:, 768:1024]
    wkv = _dot(xn, w_ref[:, 2048:2560])
    kw = _rope(wkv[:, 0:256], cos, sin)
    win_ref[:, 0:256] = kw
    win_ref[:, 256:512] = wkv[:, 256:512]
    kva_ref[:, 0:256] = ks.astype(BF16)
    kva_ref[:, 256:512] = kv[:, 768:1024].astype(BF16)
    kva_ref[:, 512:768] = kw.astype(BF16)
    kva_ref[:, 768:1024] = wkv[:, 256:512].astype(BF16)
    gate_ref[...] = _sigmoid(_dot(xn, w_ref[:, 2560:2688]))


def nsa_in(x, g, w, cos, sin, *, tm):
    m, d = x.shape
    nw = w.shape[1]
    nt = cos.shape[0] // tm
    row = lambda i: (i, 0)
    tab = lambda i: (i % nt, 0)
    const = lambda i: (0, 0)
    outs = [(1024, BF16), (1024, F32), (1024, BF16), (512, F32), (LANES, F32)]
    return pl.pallas_call(
        _nsa_in_kernel,
        grid=(m // tm,),
        in_specs=[pl.BlockSpec((tm, d), row), pl.BlockSpec((1, d), const),
                  pl.BlockSpec((d, nw), const, pipeline_mode=pl.Buffered(1)),
                  pl.BlockSpec((tm, LANES), tab), pl.BlockSpec((tm, LANES), tab)],
        out_specs=[pl.BlockSpec((tm, n), row) for n, _ in outs],
        out_shape=[jax.ShapeDtypeStruct((m, n), dt) for n, dt in outs],
        compiler_params=_params("parallel"),
        name="nsa_in",
    )(x, g, w, cos, sin)


def _nsa_in_t_kernel(x_ref, g_ref, w_ref, wqt_ref, wvt_ref, wgt_ref, cos_ref, sin_ref, cost_ref, sint_ref,
                     qt_ref, rows_ref, kk_ref, vt_ref, win_ref, gt_ref):
    xn = _rms(x_ref[...], g_ref[...]).astype(BF16)
    cos = cos_ref[...]
    sin = sin_ref[...]
    qt = _rope_t(_dot_nt(wqt_ref[...], xn), cost_ref[...], sint_ref[...]) * SCALE
    qt_ref[...] = qt.astype(BF16)
    kv = _dot(xn, w_ref[:, 0:1024])
    ks = _rope(kv[:, 512:768], cos, sin)
    rows_ref[:, 0:512] = kv[:, 0:512]
    rows_ref[:, 512:768] = ks
    rows_ref[:, 768:1024] = kv[:, 768:1024]
    wkv = _dot(xn, w_ref[:, 1024:1536])
    kw = _rope(wkv[:, 0:256], cos, sin)
    win_ref[:, 0:256] = kw
    win_ref[:, 256:512] = wkv[:, 256:512]
    kk_ref[:, 0:256] = ks.astype(BF16)
    kk_ref[:, 256:512] = kw.astype(BF16)
    vt_ref[...] = _dot_nt(wvt_ref[...], xn).astype(BF16)
    gt_ref[...] = _sigmoid(_dot_nt(wgt_ref[...], xn))


def nsa_in_t(x, g, w_kv, wq_t, wv_t, wg_t, cos, sin, cos_t, sin_t, *, tm):
    m, d = x.shape
    nt = cos.shape[0] // tm
    row = lambda i: (i, 0)
    col = lambda i: (0, i)
    const = lambda i: (0, 0)
    one = pl.Buffered(1)
    return pl.pallas_call(
        _nsa_in_t_kernel,
        grid=(m // tm,),
        in_specs=[pl.BlockSpec((tm, d), row), pl.BlockSpec((1, d), const),
                  pl.BlockSpec(w_kv.shape, const, pipeline_mode=one), pl.BlockSpec(wq_t.shape, const, pipeline_mode=one),
                  pl.BlockSpec(wv_t.shape, const, pipeline_mode=one), pl.BlockSpec(wg_t.shape, const, pipeline_mode=one),
                  pl.BlockSpec((tm, LANES), lambda i: (i % nt, 0)), pl.BlockSpec((tm, LANES), lambda i: (i % nt, 0)),
                  pl.BlockSpec((HEAD_DIM, tm), lambda i: (0, i % nt)), pl.BlockSpec((HEAD_DIM, tm), lambda i: (0, i % nt))],
        out_specs=[pl.BlockSpec((1024, tm), col), pl.BlockSpec((tm, 1024), row), pl.BlockSpec((tm, 512), row),
                   pl.BlockSpec((None, 512, tm), lambda i: (i, 0, 0)), pl.BlockSpec((tm, 512), row),
                   pl.BlockSpec((LANES, tm), col)],
        out_shape=[jax.ShapeDtypeStruct((1024, m), BF16), jax.ShapeDtypeStruct((m, 1024), F32),
                   jax.ShapeDtypeStruct((m, 512), BF16), jax.ShapeDtypeStruct((m // tm, 512, tm), BF16),
                   jax.ShapeDtypeStruct((m, 512), F32), jax.ShapeDtypeStruct((LANES, m), F32)],
        compiler_params=_params("parallel"),
        name="nsa_in_t",
    )(x, g, w_kv, wq_t, wv_t, wg_t, cos, sin, cos_t, sin_t)


def _compress_kernel(pt_ref, *refs, pps):
    del pt_ref
    pages = refs[:pps + 1]
    (pelo_k, pehi_k, w1lo_k, w1hi_k, w2_k, pelo_v, pehi_v, w1lo_v, w1hi_v, w2_v, w2t_v,
     cos_ref, sin_ref, kc_ref, vc_ref, vct_ref, xs_ref) = refs[pps + 1:]
    npc = pps * 8
    rows = (pps + 1) * 8
    low = lax.broadcasted_iota(jnp.int32, (rows, LANES), 1) < HEAD_DIM
    for k, pg in enumerate(pages):
        for lb in range(4):
            xs_ref[lb, k * PAGE_SIZE:(k + 1) * PAGE_SIZE, :] = pg[:, lb * LANES:(lb + 1) * LANES]
    streams = ((pelo_k, pehi_k, w1lo_k, w1hi_k, w2_k, kc_ref), (pelo_v, pehi_v, w1lo_v, w1hi_v, w2_v, vc_ref))
    for s, (pelo, pehi, w1lo, w1hi, w2, out_ref) in enumerate(streams):
        heads = [[] for _ in range(NSA_KV_HEADS)]
        for lb in range(2):
            for q in range(CMP_STRIDE // 2):
                a = xs_ref[2 * s + lb, pl.ds(2 * q, rows, stride=CMP_STRIDE), :]
                b = xs_ref[2 * s + lb, pl.ds(2 * q + 1, rows, stride=CMP_STRIDE), :]
                heads[2 * lb].append(jnp.where(low, a, pltpu.roll(b, HEAD_DIM, 1)))
                heads[2 * lb + 1].append(jnp.where(low, pltpu.roll(a, HEAD_DIM, 1), b))
        x = jnp.concatenate([jnp.concatenate(hh, axis=1) for hh in heads], axis=0)
        first = _dot((x + pelo[...]).astype(BF16), w1lo[...])
        second = _dot((x + pehi[...]).astype(BF16), w1hi[...])
        outs = []
        outs_t = []
        for h in range(NSA_KV_HEADS):
            pre = first[h * rows:h * rows + npc] + second[h * rows + 1:h * rows + 1 + npc]
            hid = (pre * _sigmoid(pre)).astype(BF16)
            outs.append(_dot(hid, w2[...]))
            if s == 1:
                outs_t.append(_dot_nt(w2t_v[...], hid))
        res = jnp.concatenate(outs, axis=1)
        if s == 0:
            res = _rope(res, cos_ref[...], sin_ref[...])
        else:
            vct_ref[...] = jnp.concatenate(outs_t, axis=0).astype(BF16)
        out_ref[...] = res.astype(BF16)


def nsa_compress(rows3d, pt_flat, n_seq, n_pages, wk, wv, cos_c, sin_c):
    pps = min(16, n_pages)
    steps = n_pages // pps
    npc = pps * 8

    def page_map(k):
        return lambda b, s, pt: (pt[b * n_pages + jnp.minimum(s * pps + k, n_pages - 1)], 0, 0)

    const2 = lambda b, s, pt: (0, 0)
    wspecs = []
    for _ in range(2):
        wspecs += [pl.BlockSpec((1, 1024), const2), pl.BlockSpec((1, 1024), const2),
                   pl.BlockSpec((1024, 256), const2), pl.BlockSpec((1024, 256), const2),
                   pl.BlockSpec((256, HEAD_DIM), const2)]
    wspecs.append(pl.BlockSpec((HEAD_DIM, 256), const2))
    grid_spec = pltpu.PrefetchScalarGridSpec(
        num_scalar_prefetch=1,
        grid=(n_seq, steps),
        in_specs=[pl.BlockSpec((None, PAGE_SIZE, 512), page_map(k)) for k in range(pps + 1)] + wspecs
        + [pl.BlockSpec((npc, LANES), lambda b, s, pt: (s, 0))] * 2,
        out_specs=[pl.BlockSpec((None, npc, 256), lambda b, s, pt: (b, s, 0))] * 2
        + [pl.BlockSpec((None, 256, npc), lambda b, s, pt: (b, 0, s))],
        scratch_shapes=[pltpu.VMEM((4, (pps + 1) * PAGE_SIZE, LANES), F32)],
    )
    return pl.pallas_call(
        functools.partial(_compress_kernel, pps=pps),
        grid_spec=grid_spec,
        out_shape=[jax.ShapeDtypeStruct((n_seq, n_pages * 8, 256), BF16)] * 2
        + [jax.ShapeDtypeStruct((n_seq, 256, n_pages * 8), BF16)],
        compiler_params=_params("parallel", "arbitrary"),
        name="nsa_compress",
    )(pt_flat, *([rows3d] * (pps + 1)), *wk, *wv, cos_c, sin_c)


def _nsa_attn_kernel(qt_ref, gt_ref, kc_ref, vct_ref, kk_ref, vt_ref, ovlt_ref, emt_ref, ot_ref, *,
                     tq, tks, n_top, nblk):
    i = pl.program_id(1)
    c0 = i * tq
    n4 = NSA_GROUP * tq
    pos1 = c0 + lax.broadcasted_iota(jnp.int32, (1, tq), 1)
    pos4 = c0 + lax.broadcasted_iota(jnp.int32, (1, n4), 1) % tq
    blk = lax.broadcasted_iota(jnp.int32, (nblk, tq), 0)
    cur = pos1 // SEL_BLOCK
    heads = range(NSA_KV_HEADS)
    pair = [slice((h // 2) * LANES, (h // 2 + 1) * LANES) for h in heads]
    vrow = [slice(h * HEAD_DIM, (h + 1) * HEAD_DIM) for h in heads]
    qpad = []
    for h in heads:
        q4 = jnp.concatenate(
            [qt_ref[(NSA_GROUP * h + g) * HEAD_DIM:(NSA_GROUP * h + g + 1) * HEAD_DIM, :] for g in range(NSA_GROUP)],
            axis=1)
        qpad.append(_pad_pair(q4, h % 2 == 1))

    o_c, selb = [], []
    for h in heads:
        s = _dot(kc_ref[:, pair[h]], qpad[h])
        cend = lax.broadcasted_iota(jnp.int32, s.shape, 0) * CMP_STRIDE + (CMP_BLOCK - 1)
        s = jnp.where(cend <= pos4, s, -jnp.inf)
        m = jnp.max(s, axis=0, keepdims=True)
        p = jnp.exp(s - jnp.where(m > -jnp.inf, m, 0.0))
        p = p * (1.0 / jnp.maximum(jnp.sum(p, axis=0, keepdims=True), 1e-30))
        o_c.append(_dot(vct_ref[vrow[h], :], p.astype(BF16)))
        psum = p[:, 0:tq] + p[:, tq:2 * tq] + p[:, 2 * tq:3 * tq] + p[:, 3 * tq:4 * tq]
        imp = _dot3_rhs(ovlt_ref[...], psum)
        forced = (blk == 0) | (blk == cur) | (blk == cur - 1)
        imp = jnp.where(forced, jnp.inf, imp)
        imp = jnp.where(blk <= cur, imp, -jnp.inf)
        selb.append(_topk_mask_t(imp, n_top).astype(BF16))

    def finish(carry):
        return [acc * (1.0 / jnp.maximum(l, 1e-30)) for _, l, acc in carry]

    init = tuple(_online_init_t(n4) for _ in heads)
    t_hi = (c0 + tq - 1) // tks + 1

    def sel_body(t, carry):
        k0 = pl.multiple_of(t * tks, tks)
        kpos = k0 + lax.broadcasted_iota(jnp.int32, (tks, tq), 0)
        new = []
        for h in heads:
            chosen = _dot(emt_ref[t], selb[h])
            bias = jnp.where((chosen > 0.5) & (kpos <= pos1), 0.0, NEG)
            bias = jnp.concatenate([bias] * NSA_GROUP, axis=1)
            s = _dot(kk_ref[pl.ds(k0, tks), pair[h]], qpad[h])
            new.append(_online_update_t(carry[h], s, bias, vt_ref[t, vrow[h], :]))
        return tuple(new)

    o_s = finish(lax.fori_loop(0, t_hi, sel_body, init))

    def win_body(t, carry):
        k0 = pl.multiple_of(t * tks, tks)
        diff = pos4 - (k0 + lax.broadcasted_iota(jnp.int32, (tks, n4), 0))
        bias = jnp.where((diff >= 0) & (diff <= WINDOW), 0.0, NEG)
        new = []
        for h in heads:
            s = _dot(kk_ref[pl.ds(k0, tks), 256 + pair[h].start:256 + pair[h].stop], qpad[h])
            new.append(_online_update_t(carry[h], s, bias, vt_ref[t, 256 + vrow[h].start:256 + vrow[h].stop, :]))
        return tuple(new)

    o_w = finish(lax.fori_loop(jnp.maximum(c0 - WINDOW, 0) // tks, t_hi, win_body, init))

    for h in heads:
        for g in range(NSA_GROUP):
            j = (h * NSA_GROUP + g) * 3
            cs = slice(g * tq, (g + 1) * tq)
            o = (gt_ref[j:j + 1, :] * o_c[h][:, cs] + gt_ref[j + 1:j + 2, :] * o_s[h][:, cs]
                 + gt_ref[j + 2:j + 3, :] * o_w[h][:, cs])
            ot_ref[(NSA_GROUP * h + g) * HEAD_DIM:(NSA_GROUP * h + g + 1) * HEAD_DIM, :] = o.astype(BF16)


def nsa_attn(qt, gt, kc, vct, kk, vt, ovlt, emt, *, n_b, t, tq):
    nq = t // tq
    tks = vt.shape[2]
    npiece = kc.shape[1]
    nblk = ovlt.shape[0]
    n_top = min(SEL_TOPN, t // SEL_BLOCK)
    col = lambda b, i: (0, b * nq + i)
    per_b = lambda b, i: (b, 0, 0)
    return pl.pallas_call(
        functools.partial(_nsa_attn_kernel, tq=tq, tks=tks, n_top=n_top, nblk=nblk),
        grid=(n_b, nq),
        in_specs=[pl.BlockSpec((1024, tq), col), pl.BlockSpec((LANES, tq), col),
                  pl.BlockSpec((None, npiece, 256), per_b), pl.BlockSpec((None, 256, npiece), per_b),
                  pl.BlockSpec((None, t, 512), per_b), pl.BlockSpec((t // tks, 512, tks), per_b),
                  pl.BlockSpec(ovlt.shape, lambda b, i: (0, 0)),
                  pl.BlockSpec(emt.shape, lambda b, i: (0, 0, 0))],
        out_specs=pl.BlockSpec((1024, tq), col),
        out_shape=jax.ShapeDtypeStruct((1024, n_b * t), BF16),
        compiler_params=_params("parallel", "arbitrary"),
        name="nsa_attn",
    )(qt, gt, kc, vct, kk.reshape(n_b, t, 512), vt, ovlt, emt)


def _nsa_s_sel_kernel(q_ref, kc_ref, vc_ref, ovl_ref, gm_ref, oc_ref, idx_ref, *, pos, n_pick):
    s = _dot_nt(q_ref[...], kc_ref[...])
    cend = lax.broadcasted_iota(jnp.int32, s.shape, 1) * CMP_STRIDE + (CMP_BLOCK - 1)
    p = _masked_softmax(s, cend <= pos)
    oc_ref[...] = _dot(p.astype(BF16), vc_ref[...])
    psum = _dot3_rhs(gm_ref[...], p)
    imp = _dot3(psum, ovl_ref[...])
    blk = lax.broadcasted_iota(jnp.int32, imp.shape, 1)
    cur = pos // SEL_BLOCK
    imp = jnp.where((blk == 0) | (blk == cur - 1), jnp.inf, imp)
    imp = jnp.where(blk < cur, imp, -jnp.inf)
    lane = blk.astype(F32)
    slot = lax.broadcasted_iota(jnp.int32, (8, LANES), 1)
    picks = jnp.zeros((8, LANES), F32)
    for r in range(n_pick):
        m = jnp.max(imp, axis=-1, keepdims=True)
        idx = jnp.min(jnp.where(imp == m, lane, 1e9), axis=-1, keepdims=True)
        imp = jnp.where(lane == idx, -jnp.inf, imp)
        picks = jnp.where(slot == r, idx, picks)
    idx_ref[...] = picks.astype(jnp.int32)


def nsa_s_sel(qbd, kc, vc, ovl, gm, *, pos, n_pick):
    n_b, _, npiece = kc.shape[0], None, kc.shape[1]
    per_b = lambda b: (b, 0, 0)
    return pl.pallas_call(
        functools.partial(_nsa_s_sel_kernel, pos=pos, n_pick=n_pick),
        grid=(n_b,),
        in_specs=[pl.BlockSpec((None, 16, 256), per_b), pl.BlockSpec((None, npiece, 256), per_b),
                  pl.BlockSpec((None, npiece, 256), per_b),
                  pl.BlockSpec(ovl.shape, lambda b: (0, 0)), pl.BlockSpec(gm.shape, lambda b: (0, 0))],
        out_specs=[pl.BlockSpec((None, 16, 256), per_b), pl.BlockSpec((None, 8, LANES), per_b)],
        out_shape=[jax.ShapeDtypeStruct((n_b, 16, 256), F32), jax.ShapeDtypeStruct((n_b, 8, LANES), jnp.int32)],
        compiler_params=_params("parallel"),
        name="nsa_s_sel",
    )(qbd, kc, vc, ovl, gm)


def _attend_with_new(q, k, v, k_new, v_new):
    s = _dot_nt(q, k)
    qf = q.astype(F32)
    s_new = jnp.sum(qf * k_new.astype(F32), axis=-1, keepdims=True)
    m = jnp.maximum(jnp.max(s, axis=-1, keepdims=True), s_new)
    p = jnp.exp(s - m)
    p_new = jnp.exp(s_new - m)
    l = jnp.sum(p, axis=-1, keepdims=True) + p_new
    return (_dot(p.astype(BF16), v) + p_new * v_new.astype(F32)) / l


def _nsa_s_attn_kernel(pt_ref, ix_ref, q_ref, *refs, n_pick):
    del pt_ref, ix_ref
    kblk = refs[:n_pick]
    vblk = refs[n_pick:2 * n_pick]
    ks_new, vs_new, kw_new, vw_new, kw_ref, vw_ref, os_ref, ow_ref = refs[2 * n_pick:]
    q = q_ref[...]
    k = jnp.concatenate([r[...] for r in kblk], axis=0).astype(BF16)
    v = jnp.concatenate([r[...] for r in vblk], axis=0).astype(BF16)
    os_ref[...] = _attend_with_new(q, k, v, ks_new[...], vs_new[...])
    ow_ref[...] = _attend_with_new(q, kw_ref[...].astype(BF16), vw_ref[...].astype(BF16), kw_new[...], vw_new[...])


def nsa_s_attn(pt_flat, ix_flat, qpad, cache3d, kva_s, wincache, *, n_pages, n_pick):
    n_b = qpad.shape[0]
    nwin = wincache.shape[1]

    def kv_map(r, lane0):
        def f(b, h, pt, ix):
            j = ix[(b * NSA_KV_HEADS + h) * n_pick + r]
            return (pt[b * n_pages + j // 2], j % 2, lane0 + h // 2)
        return f

    new_map = lambda lane0: (lambda b, h, pt, ix: (b, 0, lane0 + h // 2))
    out_spec = pl.BlockSpec((None, None, 8, LANES), lambda b, h, pt, ix: (b, h, 0, 0))
    grid_spec = pltpu.PrefetchScalarGridSpec(
        num_scalar_prefetch=2,
        grid=(n_b, NSA_KV_HEADS),
        in_specs=[pl.BlockSpec((None, None, 8, LANES), lambda b, h, pt, ix: (b, h, 0, 0))]
        + [pl.BlockSpec((None, SEL_BLOCK, LANES), kv_map(r, 4)) for r in range(n_pick)]
        + [pl.BlockSpec((None, SEL_BLOCK, LANES), kv_map(r, 6)) for r in range(n_pick)]
        + [pl.BlockSpec((None, 1, LANES), new_map(l0)) for l0 in (0, 2, 4, 6)]
        + [pl.BlockSpec((None, nwin, LANES), new_map(0)), pl.BlockSpec((None, nwin, LANES), new_map(2))],
        out_specs=[out_spec, out_spec],
    )
    return pl.pallas_call(
        functools.partial(_nsa_s_attn_kernel, n_pick=n_pick),
        grid_spec=grid_spec,
        out_shape=[jax.ShapeDtypeStruct((n_b, NSA_KV_HEADS, 8, LANES), F32)] * 2,
        compiler_params=_params("parallel", "arbitrary"),
        name="nsa_s_attn",
    )(pt_flat, ix_flat, qpad, *([cache3d] * (2 * n_pick)), *([kva_s] * 4), wincache, wincache)


def _nsa_out_s_kernel(oc_ref, os_ref, ow_ref, g0_ref, g1_ref, g2_ref, w_ref, r_ref, o_ref):
    o = g0_ref[...] * oc_ref[...] + g1_ref[...] * os_ref[...] + g2_ref[...] * ow_ref[...]
    o_ref[...] = r_ref[...] + _dot(o.astype(BF16), w_ref[...])


def nsa_out_s(oc, osel, ow, g0, g1, g2, w, res):
    m, d = res.shape
    full = pl.BlockSpec((m, d), lambda i: (0, 0))
    return pl.pallas_call(
        _nsa_out_s_kernel,
        grid=(1,),
        in_specs=[full] * 6 + [pl.BlockSpec(w.shape, lambda i: (0, 0)), full],
        out_specs=full,
        out_shape=jax.ShapeDtypeStruct((m, d), F32),
        compiler_params=_params("arbitrary"),
        name="nsa_out_s",
    )(oc, osel, ow, g0, g1, g2, w, res)


def _sconv_kernel(x_ref, g_ref, win_ref, wc_ref, wout_ref, o_ref, st_ref, carry_ref):
    d = x_ref.shape[1]
    tm = x_ref.shape[0]

    @pl.when(pl.program_id(1) == 0)
    def _():
        carry_ref[...] = jnp.zeros(carry_ref.shape, F32)

    x = x_ref[...]
    xn = _rms(x, g_ref[...]).astype(BF16)
    b_gate = _dot(xn, win_ref[:, 0:d])
    pre = _dot(xn, win_ref[:, d:2 * d]) * _dot(xn, win_ref[:, 2 * d:3 * d])
    row = lax.broadcasted_iota(jnp.int32, (tm, d), 0)
    back1 = jnp.where(row == 0, carry_ref[7:8, :], pltpu.roll(pre, 1, 0))
    back2 = jnp.where(row == 0, carry_ref[6:7, :], jnp.where(row == 1, carry_ref[7:8, :], pltpu.roll(pre, 2, 0)))
    y = back2 * wc_ref[0:1, :] + back1 * wc_ref[1:2, :] + pre * wc_ref[2:3, :]
    tail = pre[tm - 8:tm]
    carry_ref[...] = tail
    st_ref[...] = tail
    o_ref[...] = x + _dot((b_gate * y).astype(BF16), wout_ref[...])


def sconv_prompt(x, g, w_in, w_conv, w_out, *, n_b, t, tm):
    d = x.shape[1]
    nt = t // tm
    row = lambda b, i: (b * nt + i, 0)
    const = lambda b, i: (0, 0)
    return pl.pallas_call(
        _sconv_kernel,
        grid=(n_b, nt),
        in_specs=[pl.BlockSpec((tm, d), row), pl.BlockSpec((1, d), const),
                  pl.BlockSpec((d, 3 * d), const, pipeline_mode=pl.Buffered(1)),
                  pl.BlockSpec(w_conv.shape, const),
                  pl.BlockSpec((d, d), const, pipeline_mode=pl.Buffered(1))],
        out_specs=[pl.BlockSpec((tm, d), row), pl.BlockSpec((None, 8, d), lambda b, i: (b, 0, 0))],
        out_shape=[jax.ShapeDtypeStruct((n_b * t, d), F32), jax.ShapeDtypeStruct((n_b, 8, d), F32)],
        scratch_shapes=[pltpu.VMEM((8, d), F32)],
        compiler_params=_params("parallel", "arbitrary"),
        name="sconv_prompt",
    )(x, g, w_in, w_conv, w_out)


def _sconv_s_kernel(x_ref, g_ref, win_ref, wc_ref, wout_ref, p0_ref, p1_ref, o_ref, pre_ref):
    d = x_ref.shape[1]
    x = x_ref[...]
    xn = _rms(x, g_ref[...]).astype(BF16)
    b_gate = _dot(xn, win_ref[:, 0:d])
    pre = _dot(xn, win_ref[:, d:2 * d]) * _dot(xn, win_ref[:, 2 * d:3 * d])
    y = p0_ref[...] * wc_ref[0:1, :] + p1_ref[...] * wc_ref[1:2, :] + pre * wc_ref[2:3, :]
    pre_ref[...] = pre
    o_ref[...] = x + _dot((b_gate * y).astype(BF16), wout_ref[...])


def sconv_sample(x, g, w_in, w_conv, w_out, past0, past1):
    m, d = x.shape
    full = lambda a: pl.BlockSpec(a.shape, lambda i: (0,) * a.ndim)
    args = (x, g, w_in, w_conv, w_out, past0, past1)
    return pl.pallas_call(
        _sconv_s_kernel,
        grid=(1,),
        in_specs=[full(a) for a in args],
        out_specs=[pl.BlockSpec((m, d), lambda i: (0, 0))] * 2,
        out_shape=[jax.ShapeDtypeStruct((m, d), F32)] * 2,
        compiler_params=_params("arbitrary"),
        name="sconv_sample",
    )(*args)


def _moba_in_kernel(x_ref, g_ref, w_ref, cos_ref, sin_ref, q_ref, rows_ref, kb_ref, vb_ref, km_ref):
    d = x_ref.shape[1]
    xn = _rms(x_ref[...], g_ref[...]).astype(BF16)
    cos = cos_ref[...]
    sin = sin_ref[...]
    q_ref[...] = (_rope(_dot(xn, w_ref[:, 0:d]), cos, sin) * SCALE).astype(BF16)
    k = _rope(_dot(xn, w_ref[:, d:2 * d]), cos, sin)
    v = _dot(xn, w_ref[:, 2 * d:3 * d])
    rows_ref[:, 0:d] = k
    rows_ref[:, d:2 * d] = v
    kb_ref[...] = k.astype(BF16)
    vb_ref[...] = v.astype(BF16)
    km_ref[...] = jnp.sum(k, axis=0, keepdims=True) * (1.0 / MOBA_BLOCK)


def moba_in(x, g, w, cos, sin, *, tm):
    m, d = x.shape
    nt = cos.shape[0] // tm
    row = lambda i: (i, 0)
    tab = lambda i: (i % nt, 0)
    const = lambda i: (0, 0)
    return pl.pallas_call(
        _moba_in_kernel,
        grid=(m // tm,),
        in_specs=[pl.BlockSpec((tm, d), row), pl.BlockSpec((1, d), const),
                  pl.BlockSpec((d, 3 * d), const, pipeline_mode=pl.Buffered(1)),
                  pl.BlockSpec((tm, LANES), tab), pl.BlockSpec((tm, LANES), tab)],
        out_specs=[pl.BlockSpec((tm, d), row), pl.BlockSpec((tm, 2 * d), row), pl.BlockSpec((tm, d), row),
                   pl.BlockSpec((tm, d), row), pl.BlockSpec((None, 1, d), lambda i: (i, 0, 0))],
        out_shape=[jax.ShapeDtypeStruct((m, d), BF16), jax.ShapeDtypeStruct((m, 2 * d), F32),
                   jax.ShapeDtypeStruct((m, d), BF16), jax.ShapeDtypeStruct((m, d), BF16),
                   jax.ShapeDtypeStruct((m // tm, 1, d), F32)],
        compiler_params=_params("parallel"),
        name="moba_in",
    )(x, g, w, cos, sin)


def _moba_in_t_kernel(x_ref, g_ref, w_ref, wqt_ref, wvt_ref, cos_ref, sin_ref, cost_ref, sint_ref,
                      qt_ref, rows_ref, kb_ref, vt_ref, km_ref):
    d = x_ref.shape[1]
    xn = _rms(x_ref[...], g_ref[...]).astype(BF16)
    qt = _rope_t(_dot_nt(wqt_ref[...], xn), cost_ref[...], sint_ref[...]) * SCALE
    qt_ref[...] = qt.astype(BF16)
    k = _rope(_dot(xn, w_ref[:, 0:d]), cos_ref[...], sin_ref[...])
    rows_ref[:, 0:d] = k
    rows_ref[:, d:2 * d] = _dot(xn, w_ref[:, d:2 * d])
    kb_ref[...] = k.astype(BF16)
    vt_ref[...] = _dot_nt(wvt_ref[...], xn).astype(BF16)
    km_ref[...] = jnp.sum(k, axis=0, keepdims=True) * (1.0 / MOBA_BLOCK)


def moba_in_t(x, g, w_kv, wq_t, wv_t, cos, sin, cos_t, sin_t):
    m, d = x.shape
    tm = MOBA_BLOCK
    nt = cos.shape[0] // tm
    row = lambda i: (i, 0)
    const = lambda i: (0, 0)
    one = pl.Buffered(1)
    return pl.pallas_call(
        _moba_in_t_kernel,
        grid=(m // tm,),
        in_specs=[pl.BlockSpec((tm, d), row), pl.BlockSpec((1, d), const),
                  pl.BlockSpec(w_kv.shape, const, pipeline_mode=one), pl.BlockSpec(wq_t.shape, const, pipeline_mode=one),
                  pl.BlockSpec(wv_t.shape, const, pipeline_mode=one),
                  pl.BlockSpec((tm, LANES), lambda i: (i % nt, 0)), pl.BlockSpec((tm, LANES), lambda i: (i % nt, 0)),
                  pl.BlockSpec((HEAD_DIM, tm), lambda i: (0, i % nt)), pl.BlockSpec((HEAD_DIM, tm), lambda i: (0, i % nt))],
        out_specs=[pl.BlockSpec((d, tm), lambda i: (0, i)), pl.BlockSpec((tm, 2 * d), row), pl.BlockSpec((tm, d), row),
                   pl.BlockSpec((None, d, tm), lambda i: (i, 0, 0)), pl.BlockSpec((None, 1, d), lambda i: (i, 0, 0))],
        out_shape=[jax.ShapeDtypeStruct((d, m), BF16), jax.ShapeDtypeStruct((m, 2 * d), F32),
                   jax.ShapeDtypeStruct((m, d), BF16), jax.ShapeDtypeStruct((m // tm, d, tm), BF16),
                   jax.ShapeDtypeStruct((m // tm, 1, d), F32)],
        compiler_params=_params("parallel"),
        name="moba_in_t",
    )(x, g, w_kv, wq_t, wv_t, cos, sin, cos_t, sin_t)


def _moba_attn_kernel(qt_ref, k_ref, vt_ref, km_ref, ot_ref, *, tq, n_top):
    i = pl.program_id(2)
    c0 = i * tq
    nblk = km_ref.shape[0]
    pos = c0 + lax.broadcasted_iota(jnp.int32, (1, tq), 1)
    cur = pos // MOBA_BLOCK
    blk = lax.broadcasted_iota(jnp.int32, (nblk, tq), 0)
    m1, m2, m3 = _split3(km_ref[...])
    vrow = [slice(hh * HEAD_DIM, (hh + 1) * HEAD_DIM) for hh in range(2)]
    qpad, allow = [], []
    for hh in range(2):
        qp = _pad_pair(qt_ref[vrow[hh], :], hh == 1)
        gate = _dot(m1, qp) + _dot(m2, qp) + _dot(m3, qp)
        gate = jnp.where(blk < cur, gate, -jnp.inf)
        qpad.append(qp)
        allow.append(jnp.where(blk == cur, 1.0, _topk_mask_t(gate, n_top)))

    def body(t, carry):
        k0 = pl.multiple_of(t * MOBA_BLOCK, MOBA_BLOCK)
        kpos = k0 + lax.broadcasted_iota(jnp.int32, (MOBA_BLOCK, tq), 0)
        k = k_ref[pl.ds(k0, MOBA_BLOCK), :]
        new = []
        for hh in range(2):
            chosen = jnp.sum(jnp.where(blk == t, allow[hh], 0.0), axis=0, keepdims=True)
            bias = jnp.where((chosen > 0.5) & (kpos <= pos), 0.0, NEG)
            new.append(_online_update_t(carry[hh], _dot(k, qpad[hh]), bias, vt_ref[t, vrow[hh], :]))
        return tuple(new)

    res = lax.fori_loop(0, (c0 + tq - 1) // MOBA_BLOCK + 1, body, (_online_init_t(tq), _online_init_t(tq)))
    for hh in range(2):
        _, l, acc = res[hh]
        ot_ref[vrow[hh], :] = (acc * (1.0 / jnp.maximum(l, 1e-30))).astype(BF16)


def moba_attn(qt, kb, vt, kmean, *, n_b, t, tq):
    nq = t // tq
    d = qt.shape[0]
    nblk = kmean.shape[1]
    n_top = min(MOBA_TOPK, t // MOBA_BLOCK)
    qmap = lambda b, hp, i: (hp, b * nq + i)
    kvmap = lambda b, hp, i: (b, 0, hp)
    return pl.pallas_call(
        functools.partial(_moba_attn_kernel, tq=tq, n_top=n_top),
        grid=(n_b, d // LANES, nq),
        in_specs=[pl.BlockSpec((LANES, tq), qmap), pl.BlockSpec((None, t, LANES), kvmap),
                  pl.BlockSpec((t // MOBA_BLOCK, LANES, MOBA_BLOCK), lambda b, hp, i: (b, hp, 0)),
                  pl.BlockSpec((None, nblk, LANES), kvmap)],
        out_specs=pl.BlockSpec((LANES, tq), qmap),
        out_shape=jax.ShapeDtypeStruct((d, n_b * t), BF16),
        compiler_params=_params("parallel", "parallel", "arbitrary"),
        name="moba_attn",
    )(qt, kb.reshape(n_b, t, d), vt, kmean)


def _kmean_kernel(pt_ref, *refs, pps):
    del pt_ref
    pages, out_ref = refs[:pps], refs[pps]
    per_blk = MOBA_BLOCK // PAGE_SIZE
    rows = []
    for j in range(pps // per_blk):
        s = jnp.sum(pages[per_blk * j][...], axis=0, keepdims=True)
        for e in range(1, per_blk):
            s = s + jnp.sum(pages[per_blk * j + e][...], axis=0, keepdims=True)
        rows.append(s * (1.0 / MOBA_BLOCK))
    out_ref[...] = jnp.concatenate(rows, axis=0)


def moba_kmean(cache3d, pt_flat, *, n_b, n_pages, d):
    pps = min(16, n_pages)
    per_blk = MOBA_BLOCK // PAGE_SIZE
    page_map = lambda k: (lambda b, s, pt: (pt[b * n_pages + s * pps + k], 0, 0))
    grid_spec = pltpu.PrefetchScalarGridSpec(
        num_scalar_prefetch=1,
        grid=(n_b, n_pages // pps),
        in_specs=[pl.BlockSpec((None, PAGE_SIZE, d), page_map(k)) for k in range(pps)],
        out_specs=pl.BlockSpec((None, pps // per_blk, d), lambda b, s, pt: (b, s, 0)),
    )
    return pl.pallas_call(
        functools.partial(_kmean_kernel, pps=pps),
        grid_spec=grid_spec,
        out_shape=jax.ShapeDtypeStruct((n_b, n_pages // per_blk, d), F32),
        compiler_params=_params("parallel", "arbitrary"),
        name="moba_kmean",
    )(pt_flat, *([cache3d] * pps))


def _moba_s_gate_kernel(q_ref, km_ref, seg_ref, idx_ref, *, n_top):
    prod = km_ref[...] * q_ref[...].astype(F32)
    gate = _dot3(prod, seg_ref[...])
    row = lax.broadcasted_iota(jnp.int32, gate.shape, 0).astype(F32)
    slot = lax.broadcasted_iota(jnp.int32, (8, LANES), 0)
    picks = jnp.zeros((8, LANES), F32)
    for r in range(n_top):
        m = jnp.max(gate, axis=0, keepdims=True)
        idx = jnp.min(jnp.where(gate == m, row, 1e9), axis=0, keepdims=True)
        gate = jnp.where(row == idx, -jnp.inf, gate)
        picks = jnp.where(slot == r, idx, picks)
    idx_ref[...] = picks.astype(jnp.int32)


def moba_s_gate(q3, kmean, seg, *, n_top):
    n_b, nblk, d = kmean.shape
    per_b = lambda b: (b, 0, 0)
    return pl.pallas_call(
        functools.partial(_moba_s_gate_kernel, n_top=n_top),
        grid=(n_b,),
        in_specs=[pl.BlockSpec((None, 1, d), per_b), pl.BlockSpec((None, nblk, d), per_b),
                  pl.BlockSpec(seg.shape, lambda b: (0, 0))],
        out_specs=pl.BlockSpec((None, 8, LANES), per_b),
        out_shape=jax.ShapeDtypeStruct((n_b, 8, LANES), jnp.int32),
        compiler_params=_params("parallel"),
        name="moba_s_gate",
    )(q3, kmean, seg)


def _moba_s_attn_kernel(pt_ref, ix_ref, q_ref, *refs, n_blocks):
    del pt_ref, ix_ref
    kblk = refs[:n_blocks]
    vblk = refs[n_blocks:2 * n_blocks]
    k_new, v_new, o_ref = refs[2 * n_blocks:]
    k = jnp.concatenate([r[...] for r in kblk], axis=0).astype(BF16)
    v = jnp.concatenate([r[...] for r in vblk], axis=0).astype(BF16)
    o_ref[...] = _attend_with_new(q_ref[...], k, v, k_new[...], v_new[...])


def moba_s_attn(pt_flat, ix_flat, qpad, cache3d, kb_s, vb_s, *, n_pages, n_top, n_heads):
    n_b = qpad.shape[0]
    per_blk = MOBA_BLOCK // PAGE_SIZE
    n_blocks = n_top * per_blk
    v_lane0 = n_heads * HEAD_DIM // LANES

    def kv_map(r, lane0):
        def f(b, h, pt, ix):
            j = ix[(b * n_heads + h) * n_top + r // per_blk]
            return (pt[b * n_pages + j * per_blk + r % per_blk], 0, lane0 + h // 2)
        return f

    new_map = lambda b, h, pt, ix: (b, 0, h // 2)
    qo_spec = pl.BlockSpec((None, None, 8, LANES), lambda b, h, pt, ix: (b, h, 0, 0))
    grid_spec = pltpu.PrefetchScalarGridSpec(
        num_scalar_prefetch=2,
        grid=(n_b, n_heads),
        in_specs=[qo_spec]
        + [pl.BlockSpec((None, PAGE_SIZE, LANES), kv_map(r, 0)) for r in range(n_blocks)]
        + [pl.BlockSpec((None, PAGE_SIZE, LANES), kv_map(r, v_lane0)) for r in range(n_blocks)]
        + [pl.BlockSpec((None, 1, LANES), new_map)] * 2,
        out_specs=qo_spec,
    )
    return pl.pallas_call(
        functools.partial(_moba_s_attn_kernel, n_blocks=n_blocks),
        grid_spec=grid_spec,
        out_shape=jax.ShapeDtypeStruct((n_b, n_heads, 8, LANES), F32),
        compiler_params=_params("parallel", "arbitrary"),
        name="moba_s_attn",
    )(pt_flat, ix_flat, qpad, *([cache3d] * (2 * n_blocks)), kb_s, vb_s)


def _layer_norm_silu(y, g, b):
    yc = y - jnp.mean(y, axis=-1, keepdims=True)
    yn = yc * lax.rsqrt(jnp.mean(yc * yc, axis=-1, keepdims=True) + LN_EPS) * g + b
    return yn * _sigmoid(yn)


def _conf_kernel(x_ref, g_ref, w1_ref, wdw_ref, bdw_ref, lg_ref, lb_ref, w2_ref, o_ref, st_ref, ubuf_ref, *, hist):
    tm, d = x_ref.shape
    width = wdw_ref.shape[0]

    @pl.when(pl.program_id(1) == 0)
    def _():
        ubuf_ref[0:hist, :] = jnp.zeros((hist, d), F32)

    x = x_ref[...]
    xn = _rms(x, g_ref[...]).astype(BF16)
    u = _dot(xn, w1_ref[:, 0:d]) * _sigmoid(_dot(xn, w1_ref[:, d:2 * d]))
    ubuf_ref[hist:hist + tm, :] = u
    base = hist - (width - 1)
    y = bdw_ref[...] + ubuf_ref[base:base + tm, :] * wdw_ref[0:1, :]
    for k in range(1, width):
        y = y + ubuf_ref[base + k:base + k + tm, :] * wdw_ref[k:k + 1, :]
    z = _layer_norm_silu(y, lg_ref[...], lb_ref[...])
    o_ref[...] = x + _dot(z.astype(BF16), w2_ref[...])
    tail = ubuf_ref[tm:tm + hist, :]
    st_ref[...] = tail
    ubuf_ref[0:hist, :] = tail


def conf_prompt(x, g, w1, wdw, bdw, lg, lb, w2, *, n_b, t, tm):
    d = x.shape[1]
    nt = t // tm
    hist = 32
    row = lambda b, i: (b * nt + i, 0)
    const = lambda b, i: (0, 0)
    return pl.pallas_call(
        functools.partial(_conf_kernel, hist=hist),
        grid=(n_b, nt),
        in_specs=[pl.BlockSpec((tm, d), row), pl.BlockSpec((1, d), const),
                  pl.BlockSpec((d, 2 * d), const, pipeline_mode=pl.Buffered(1)),
                  pl.BlockSpec(wdw.shape, const), pl.BlockSpec((1, d), const), pl.BlockSpec((1, d), const),
                  pl.BlockSpec((1, d), const), pl.BlockSpec((d, d), const, pipeline_mode=pl.Buffered(1))],
        out_specs=[pl.BlockSpec((tm, d), row), pl.BlockSpec((None, hist, d), lambda b, i: (b, 0, 0))],
        out_shape=[jax.ShapeDtypeStruct((n_b * t, d), F32), jax.ShapeDtypeStruct((n_b, hist, d), F32)],
        scratch_shapes=[pltpu.VMEM((hist + tm, d), F32)],
        compiler_params=_params("parallel", "arbitrary"),
        name="conf_prompt",
    )(x, g, w1, wdw, bdw, lg, lb, w2)


def _conf_s_kernel(x_ref, g_ref, w1_ref, wdw_ref, bdw_ref, lg_ref, lb_ref, w2_ref, past_ref, o_ref, u_ref):
    d = x_ref.shape[1]
    width = wdw_ref.shape[0]
    x = x_ref[...]
    xn = _rms(x, g_ref[...]).astype(BF16)
    u = _dot(xn, w1_ref[:, 0:d]) * _sigmoid(_dot(xn, w1_ref[:, d:2 * d]))
    y = bdw_ref[...] + past_ref[0] * wdw_ref[0:1, :]
    for k in range(1, width - 1):
        y = y + past_ref[k] * wdw_ref[k:k + 1, :]
    y = y + u * wdw_ref[width - 1:width, :]
    z = _layer_norm_silu(y, lg_ref[...], lb_ref[...])
    u_ref[...] = u
    o_ref[...] = x + _dot(z.astype(BF16), w2_ref[...])


def conf_sample(x, g, w1, wdw, bdw, lg, lb, w2, past_t):
    m, d = x.shape
    full = lambda a: pl.BlockSpec(a.shape, lambda i: (0,) * a.ndim)
    args = (x, g, w1, wdw, bdw, lg, lb, w2, past_t)
    return pl.pallas_call(
        _conf_s_kernel,
        grid=(1,),
        in_specs=[full(a) for a in args],
        out_specs=[pl.BlockSpec((m, d), lambda i: (0, 0))] * 2,
        out_shape=[jax.ShapeDtypeStruct((m, d), F32)] * 2,
        compiler_params=_params("arbitrary"),
        name="conf_sample",
    )(*args)


def _rope_tables(pos):
    half = HEAD_DIM // 2
    inv_freq = ROPE_THETA ** (-jnp.arange(half, dtype=F32) / half)
    ang = pos.astype(F32)[:, None] * inv_freq[None, :]
    cos = jnp.cos(ang)
    sin = jnp.sin(ang)
    cos = jnp.concatenate([cos, cos], axis=-1)
    sin = jnp.concatenate([-sin, sin], axis=-1)
    return jnp.tile(cos, (1, LANES // HEAD_DIM)), jnp.tile(sin, (1, LANES // HEAD_DIM))


def _overlap(n_cmp_rows, n_cols):
    i = jnp.arange(n_cmp_rows, dtype=jnp.int32)[:, None]
    j = jnp.arange(n_cols, dtype=jnp.int32)[None, :]
    start = i * CMP_STRIDE
    hit = (start <= j * SEL_BLOCK + (SEL_BLOCK - 1)) & (start + (CMP_BLOCK - 1) >= j * SEL_BLOCK)
    return hit.astype(BF16)


def _pad_heads_to_lane_pairs(x, rows):
    n, nh, r, hd = x.shape
    z = jnp.zeros_like(x)
    even = jnp.concatenate([x, z], axis=-1)
    odd = jnp.concatenate([z, x], axis=-1)
    is_even = (jnp.arange(nh) % 2 == 0)[None, :, None, None]
    out = jnp.where(is_even, even, odd)
    return jnp.pad(out, ((0, 0), (0, 0), (0, rows - r), (0, 0)))


def _take_lane_half(x, r):
    nh = x.shape[1]
    is_even = (jnp.arange(nh) % 2 == 0)[None, :, None, None]
    return jnp.where(is_even, x[:, :, :r, :HEAD_DIM], x[:, :, :r, HEAD_DIM:])


def _nsa_layer(hp, hs, g, cache_kv, cache_win, pt_flat, n_pages, w, *, n_b, t, n_s):
    w_in, pe_k, w1_k, w2_k, pe_v, w1_v, w2_v, w_out = w
    d = hp.shape[1]
    past_len = n_pages * PAGE_SIZE
    w_in_p = jnp.pad(w_in, ((0, 0), (0, 2688 - w_in.shape[1]))).astype(BF16)
    w_out_b = w_out.astype(BF16)
    half = CMP_BLOCK * HEAD_DIM // 2

    def cmp_weights(pe, w1, w2):
        return (pe[:CMP_STRIDE].reshape(1, half), pe[CMP_STRIDE:].reshape(1, half),
                w1[:half].astype(BF16), w1[half:].astype(BF16), w2.astype(BF16))

    wk = cmp_weights(pe_k, w1_k, w2_k)
    wv = cmp_weights(pe_v, w1_v, w2_v) + (w2_v.T.astype(BF16),)

    cos_p, sin_p = _rope_tables(jnp.arange(t, dtype=jnp.int32))
    cos_pt, sin_pt = cos_p[:, :HEAD_DIM].T, sin_p[:, :HEAD_DIM].T
    kv0, kv1, n_gate = 1024, 2560, NSA_KV_HEADS * NSA_GROUP * 3
    wq_t = w_in[:, :kv0].T.astype(BF16)
    wv_t = jnp.concatenate([w_in[:, kv0 + 768:kv0 + 1024], w_in[:, kv0 + 1280:kv1]], axis=1).T.astype(BF16)
    wg_t = jnp.pad(w_in[:, kv1:kv1 + n_gate].T, ((0, LANES - n_gate), (0, 0))).astype(BF16)
    tks = 256
    qt, rows, kk, vt, win, gt = nsa_in_t(hp, g, w_in[:, kv0:kv1].astype(BF16), wq_t, wv_t, wg_t,
                                         cos_p, sin_p, cos_pt, sin_pt, tm=tks)
    npg_p = t // PAGE_SIZE
    cend_p = jnp.arange(npg_p * 8, dtype=jnp.int32) * CMP_STRIDE + (CMP_BLOCK - 1)
    kc, _, vct = nsa_compress(rows.reshape(n_b * npg_p, PAGE_SIZE, 1024), jnp.arange(n_b * npg_p, dtype=jnp.int32),
                              n_b, npg_p, wk, wv, *_rope_tables(cend_p))
    nblk = -(-(t // SEL_BLOCK) // 16) * 16
    tile = jnp.arange(t // tks, dtype=jnp.int32)[:, None, None]
    key = jnp.arange(tks, dtype=jnp.int32)[None, :, None]
    blk = jnp.arange(nblk, dtype=jnp.int32)[None, None, :]
    emt = (blk == (tile * tks + key) // SEL_BLOCK).astype(BF16)
    ot = nsa_attn(qt, gt, kc, vct, kk, vt, _overlap(npg_p * 8, nblk).T, emt, n_b=n_b, t=t, tq=128)
    hp = mm_res_t(ot, w_out.T.astype(BF16), hp, tm=512)
    kv_p = rows.reshape(n_b, t, 4, NSA_KV_HEADS, HEAD_DIM)
    keep = min(WINDOW, t)
    win_p = win.reshape(n_b, t, 2, NSA_KV_HEADS, HEAD_DIM)[:, t - keep:]

    cos_s, sin_s = _rope_tables(jnp.full((n_s,), past_len, jnp.int32))
    q_s, rows_s, kva_s, win_s, gates_s = nsa_in(hs, g, w_in_p, cos_s, sin_s, tm=n_s)
    cend_s = jnp.arange(n_pages * 8, dtype=jnp.int32) * CMP_STRIDE + (CMP_BLOCK - 1)
    cache3d = cache_kv.reshape(cache_kv.shape[0], PAGE_SIZE, 1024)
    kc_s, vc_s, _ = nsa_compress(cache3d, pt_flat, n_s, n_pages, wk, wv, *_rope_tables(cend_s))
    n_sel = -(-(past_len + 1) // SEL_BLOCK)
    n_pick = min(SEL_TOPN, n_sel) - 1
    q4 = q_s.reshape(n_s, NSA_KV_HEADS, NSA_GROUP, HEAD_DIM)
    eye = jnp.eye(NSA_KV_HEADS, dtype=bool)[None, :, None, :, None]
    qbd = jnp.where(eye, q4[:, :, :, None, :], jnp.zeros((), BF16)).reshape(n_s, 16, 256)
    gm = (jnp.arange(8)[:, None] == jnp.arange(16)[None, :] // NSA_GROUP).astype(BF16)
    n_blk_pad = -(-n_sel // LANES) * LANES
    oc16, idx = nsa_s_sel(qbd, kc_s, vc_s, _overlap(n_pages * 8, n_blk_pad), gm, pos=past_len, n_pick=n_pick)
    oc5 = oc16.reshape(n_s, NSA_KV_HEADS, NSA_GROUP, NSA_KV_HEADS, HEAD_DIM)
    o_c = jnp.sum(jnp.where(eye, oc5, 0.0), axis=3).reshape(n_s, d)
    ix_flat = idx[:, :NSA_KV_HEADS, :n_pick].reshape(-1)
    qpad = _pad_heads_to_lane_pairs(q4, 8)
    wincache = cache_win.reshape(n_s, cache_win.shape[1], 512)
    os_p, ow_p = nsa_s_attn(pt_flat, ix_flat, qpad, cache3d, kva_s.reshape(n_s, 1, 1024), wincache,
                            n_pages=n_pages, n_pick=n_pick)
    o_s = _take_lane_half(os_p, NSA_GROUP).reshape(n_s, d)
    o_w = _take_lane_half(ow_p, NSA_GROUP).reshape(n_s, d)
    g3 = jnp.repeat(gates_s[:, :48].reshape(n_s, 16, 3), HEAD_DIM, axis=1)
    hs = nsa_out_s(o_c, o_s, o_w, g3[:, :, 0], g3[:, :, 1], g3[:, :, 2], w_out_b, hs)
    kv_s = rows_s.reshape(n_s, 1, 4, NSA_KV_HEADS, HEAD_DIM)
    win_new = win_s.reshape(n_s, 1, 2, NSA_KV_HEADS, HEAD_DIM)
    win_all = jnp.concatenate([cache_win, win_new], axis=1)
    win_s_out = win_all[:, win_all.shape[1] - cache_win.shape[1]:]
    return hp, hs, kv_p, kv_s, win_p, win_s_out


def _sconv_layer(hp, hs, g, state, w, *, n_b, t):
    w_in, w_conv, w_out = w
    w_in_b = w_in.astype(BF16)
    w_out_b = w_out.astype(BF16)
    hp, st = sconv_prompt(hp, g, w_in_b, w_conv, w_out_b, n_b=n_b, t=t, tm=256)
    st_p = st[:, 8 - (w_conv.shape[0] - 1):]
    hs, pre = sconv_sample(hs, g, w_in_b, w_conv, w_out_b, state[:, 0], state[:, 1])
    st_s = jnp.concatenate([state[:, 1:], pre[:, None, :]], axis=1)
    return hp, hs, st_p, st_s


def _moba_layer(hp, hs, g, cache_kv, pt_flat, n_pages, w, *, n_b, t, n_s):
    w_qkv, w_out = w
    d = hp.shape[1]
    n_heads = d // HEAD_DIM
    past_len = n_pages * PAGE_SIZE
    w_qkv_b = w_qkv.astype(BF16)
    w_out_b = w_out.astype(BF16)

    cos_p, sin_p = _rope_tables(jnp.arange(t, dtype=jnp.int32))
    cos_pt, sin_pt = cos_p[:, :HEAD_DIM].T, sin_p[:, :HEAD_DIM].T
    qt, rows, kb, vt, km = moba_in_t(hp, g, w_qkv_b[:, d:], w_qkv_b[:, :d].T, w_qkv_b[:, 2 * d:].T,
                                     cos_p, sin_p, cos_pt, sin_pt)
    nblk = t // MOBA_BLOCK
    kmean = jnp.pad(km.reshape(n_b, nblk, d), ((0, 0), (0, -(-nblk // 16) * 16 - nblk), (0, 0)))
    ot = moba_attn(qt, kb, vt, kmean, n_b=n_b, t=t, tq=512)
    hp = mm_res_t(ot, w_out_b.T, hp, tm=512)
    kv_p = rows.reshape(n_b, t, 2, n_heads, HEAD_DIM)

    cos_s, sin_s = _rope_tables(jnp.full((n_s,), past_len, jnp.int32))
    q_s, rows_s, kb_s, vb_s, _ = moba_in(hs, g, w_qkv_b, cos_s, sin_s, tm=n_s)
    cache3d = cache_kv.reshape(cache_kv.shape[0], PAGE_SIZE, 2 * d)
    kmean_s = moba_kmean(cache3d, pt_flat, n_b=n_s, n_pages=n_pages, d=d)
    n_top = min(MOBA_TOPK, -(-(past_len + 1) // MOBA_BLOCK))
    seg = (jnp.arange(d)[:, None] // HEAD_DIM == jnp.arange(LANES)[None, :]).astype(BF16)
    idx = moba_s_gate(q_s.reshape(n_s, 1, d), kmean_s, seg, n_top=n_top)
    ix_flat = jnp.transpose(idx[:, :n_top, :n_heads], (0, 2, 1)).reshape(-1)
    qpad = _pad_heads_to_lane_pairs(q_s.reshape(n_s, n_heads, 1, HEAD_DIM), 8)
    o_p = moba_s_attn(pt_flat, ix_flat, qpad, cache3d, kb_s.reshape(n_s, 1, d), vb_s.reshape(n_s, 1, d),
                      n_pages=n_pages, n_top=n_top, n_heads=n_heads)
    o_s = _take_lane_half(o_p, 1).reshape(n_s, d).astype(BF16)
    hs = mm_res(o_s, w_out_b, hs, tm=n_s)
    kv_s = rows_s.reshape(n_s, 1, 2, n_heads, HEAD_DIM)
    return hp, hs, kv_p, kv_s


def _conf_layer(hp, hs, g, state, w, *, n_b, t):
    w_pw1, w_dw, b_dw, ln_g, ln_b, w_pw2 = w
    d = hp.shape[1]
    r = lambda a: a.reshape(1, d)
    args = (w_pw1.astype(BF16), w_dw, r(b_dw), r(ln_g), r(ln_b), w_pw2.astype(BF16))
    hp, st = conf_prompt(hp, g, *args, n_b=n_b, t=t, tm=256)
    st_p = st[:, st.shape[1] - (w_dw.shape[0] - 1):]
    hs, u = conf_sample(hs, g, *args, jnp.transpose(state, (1, 0, 2)))
    st_s = jnp.concatenate([state[:, 1:], u[:, None, :]], axis=1)
    return hp, hs, st_p, st_s


def kernel(x_prompt, x_sample, cache_nsa_kv, cache_nsa_win, state_sconv, cache_moba_kv, state_conformer,
           page_table, norm_mix, norm_ffn, norm_final, ffn_w_up, ffn_w_down,
           nsa_w_in, nsa_pe_k, nsa_w1_k, nsa_w2_k, nsa_pe_v, nsa_w1_v, nsa_w2_v, nsa_w_out,
           sconv_w_in, sconv_w_conv, sconv_w_out, moba_w_qkv, moba_w_out,
           conf_w_pw1, conf_w_dw, conf_b_dw, conf_ln_g, conf_ln_b, conf_w_pw2):
    n_b, t, d = x_prompt.shape
    n_s = x_sample.shape[0]
    depth = norm_mix.shape[0]
    n_pages = page_table.shape[1]
    pt_flat = page_table.reshape(-1).astype(jnp.int32)
    hp = x_prompt.reshape(n_b * t, d)
    hs = x_sample.reshape(n_s, d)
    outs = {k: [] for k in ("nsa_kv_p", "nsa_kv_s", "nsa_win_p", "nsa_win_s", "sconv_p", "sconv_s",
                            "moba_p", "moba_s", "conf_p", "conf_s")}
    for i in range(depth):
        kind, j = i % 4, i // 4
        g = norm_mix[i].reshape(1, d)
        if kind == 0:
            w = (nsa_w_in[j], nsa_pe_k[j], nsa_w1_k[j], nsa_w2_k[j], nsa_pe_v[j], nsa_w1_v[j], nsa_w2_v[j],
                 nsa_w_out[j])
            hp, hs, kv_p, kv_s, win_p, win_s = _nsa_layer(hp, hs, g, cache_nsa_kv[j], cache_nsa_win[j], pt_flat,
                                                          n_pages, w, n_b=n_b, t=t, n_s=n_s)
            outs["nsa_kv_p"].append(kv_p)
            outs["nsa_kv_s"].append(kv_s)
            outs["nsa_win_p"].append(win_p)
            outs["nsa_win_s"].append(win_s)
        elif kind == 1:
            hp, hs, st_p, st_s = _sconv_layer(hp, hs, g, state_sconv[j],
                                              (sconv_w_in[j], sconv_w_conv[j], sconv_w_out[j]), n_b=n_b, t=t)
            outs["sconv_p"].append(st_p)
            outs["sconv_s"].append(st_s)
        elif kind == 2:
            hp, hs, kv_p, kv_s = _moba_layer(hp, hs, g, cache_moba_kv[j], pt_flat, n_pages,
                                             (moba_w_qkv[j], moba_w_out[j]), n_b=n_b, t=t, n_s=n_s)
            outs["moba_p"].append(kv_p)
            outs["moba_s"].append(kv_s)
        else:
            w = (conf_w_pw1[j], conf_w_dw[j], conf_b_dw[j], conf_ln_g[j], conf_ln_b[j], conf_w_pw2[j])
            hp, hs, st_p, st_s = _conf_layer(hp, hs, g, state_conformer[j], w, n_b=n_b, t=t)
            outs["conf_p"].append(st_p)
            outs["conf_s"].append(st_s)
        gf = norm_ffn[i].reshape(1, d)
        wu = ffn_w_up[i].astype(BF16)
        wd = ffn_w_down[i].astype(BF16)
        final = i == depth - 1
        gfin = norm_final.reshape(1, d)
        hp = ffn(hp, gf, wu, wd, gfin, tm=512, final=final)
        hs = ffn(hs, gf, wu, wd, gfin, tm=n_s, final=final)
    return (hp.reshape(n_b, t, d), hs.reshape(n_s, 1, d),
            jnp.stack(outs["nsa_kv_p"]), jnp.stack(outs["nsa_kv_s"]),
            jnp.stack(outs["nsa_win_p"]), jnp.stack(outs["nsa_win_s"]),
            jnp.stack(outs["sconv_p"]), jnp.stack(outs["sconv_s"]),
            jnp.stack(outs["moba_p"]), jnp.stack(outs["moba_s"]),
            jnp.stack(outs["conf_p"]), jnp.stack(outs["conf_s"]))
```

```python
import functools

import jax
import jax.numpy as jnp
from jax import lax
from jax.experimental import pallas as pl
from jax.experimental.pallas import tpu as pltpu

F32 = jnp.float32
BF16 = jnp.bfloat16

HEAD_DIM = 64
ROPE_THETA = 10000.0
RMS_EPS = 1e-6
LN_EPS = 1e-5
NSA_KV_HEADS = 4
NSA_GROUP = 4
CMP_STRIDE = 16
CMP_BLOCK = 32
SEL_BLOCK = 64
SEL_TOPN = 16
WINDOW = 512
MOBA_BLOCK = 256
MOBA_TOPK = 3
PAGE_SIZE = 128
SCALE = HEAD_DIM ** -0.5

LANES = 128
NEG = -1e30
VMEM_LIMIT = 56 * 1024 * 1024


def _params(*sem):
    return pltpu.CompilerParams(dimension_semantics=sem, vmem_limit_bytes=VMEM_LIMIT)


def _dot(a, b):
    return jnp.dot(a, b, preferred_element_type=F32)


def _dot_nt(a, b):
    return lax.dot_general(a, b, (((1,), (1,)), ((), ())), preferred_element_type=F32)


def _split3(x):
    hi = x.astype(BF16)
    r = x - hi.astype(F32)
    mid = r.astype(BF16)
    lo = (r - mid.astype(F32)).astype(BF16)
    return hi, mid, lo


def _dot3(x, m):
    hi, mid, lo = _split3(x)
    return _dot(hi, m) + _dot(mid, m) + _dot(lo, m)


def _dot3_rhs(m, x):
    hi, mid, lo = _split3(x)
    return _dot(m, hi) + _dot(m, mid) + _dot(m, lo)


def _rms(x, g):
    return x * lax.rsqrt(jnp.mean(x * x, axis=-1, keepdims=True) + RMS_EPS) * g


def _sigmoid(x):
    return 1.0 / (1.0 + jnp.exp(-x))


def _rope(x, cos, sin):
    w = x.shape[-1]
    lane = lax.broadcasted_iota(jnp.int32, x.shape, 1)
    first = (lane % HEAD_DIM) < (HEAD_DIM // 2)
    rot = jnp.where(first, pltpu.roll(x, w - HEAD_DIM // 2, 1), pltpu.roll(x, HEAD_DIM // 2, 1))
    reps = w // LANES
    if reps > 1:
        cos = jnp.concatenate([cos] * reps, axis=1)
        sin = jnp.concatenate([sin] * reps, axis=1)
    return x * cos + rot * sin


def _masked_softmax(s, mask):
    s = jnp.where(mask, s, -jnp.inf)
    m = jnp.max(s, axis=-1, keepdims=True)
    m = jnp.where(m > -jnp.inf, m, 0.0)
    p = jnp.exp(s - m)
    return p / jnp.maximum(jnp.sum(p, axis=-1, keepdims=True), 1e-30)


def _topk_mask(score, k):
    lane = lax.broadcasted_iota(jnp.int32, score.shape, 1).astype(F32)
    sel = jnp.zeros(score.shape, F32)
    for _ in range(k):
        m = jnp.max(score, axis=-1, keepdims=True)
        idx = jnp.min(jnp.where(score == m, lane, 1e9), axis=-1, keepdims=True)
        hit = lane == idx
        sel = jnp.where(hit & (m > -jnp.inf), 1.0, sel)
        score = jnp.where(hit, -jnp.inf, score)
    return sel


def _online_update(carry, s, ok, v):
    m, l, acc = carry
    sm = jnp.where(ok, s, NEG)
    m_new = jnp.maximum(m, jnp.max(sm, axis=-1, keepdims=True))
    alpha = jnp.exp(m - m_new)
    p = jnp.where(ok, jnp.exp(sm - m_new), 0.0)
    l = alpha * l + jnp.sum(p, axis=-1, keepdims=True)
    acc = alpha * acc + _dot(p.astype(BF16), v)
    return m_new, l, acc


def _online_init(rows, width):
    return (jnp.full((rows, 1), NEG, F32), jnp.zeros((rows, 1), F32), jnp.zeros((rows, width), F32))


def _rope_t(x, cos, sin):
    r = x.shape[0]
    row = lax.broadcasted_iota(jnp.int32, x.shape, 0)
    first = (row % HEAD_DIM) < (HEAD_DIM // 2)
    rot = jnp.where(first, pltpu.roll(x, r - HEAD_DIM // 2, 0), pltpu.roll(x, HEAD_DIM // 2, 0))
    reps = r // HEAD_DIM
    if reps > 1:
        cos = jnp.concatenate([cos] * reps, axis=0)
        sin = jnp.concatenate([sin] * reps, axis=0)
    return x * cos + rot * sin


def _topk_mask_t(score, k):
    row = lax.broadcasted_iota(jnp.int32, score.shape, 0).astype(F32)
    sel = jnp.zeros(score.shape, F32)
    for _ in range(k):
        m = jnp.max(score, axis=0, keepdims=True)
        idx = jnp.min(jnp.where(score == m, row, 1e9), axis=0, keepdims=True)
        hit = row == idx
        sel = jnp.where(hit & (m > -jnp.inf), 1.0, sel)
        score = jnp.where(hit, -jnp.inf, score)
    return sel


def _online_init_t(cols):
    return (jnp.full((1, cols), NEG, F32), jnp.zeros((1, cols), F32), jnp.zeros((HEAD_DIM, cols), F32))


def _online_update_t(carry, s, bias, v_t):
    m, l, acc = carry
    sm = s if bias is None else s + bias
    m_new = jnp.maximum(m, jnp.max(sm, axis=0, keepdims=True))
    alpha = jnp.exp(m - m_new)
    p = jnp.exp(sm - m_new)
    l = alpha * l + jnp.sum(p, axis=0, keepdims=True)
    acc = alpha * acc + _dot(v_t, p.astype(BF16))
    return m_new, l, acc


def _pad_pair(q, odd):
    z = jnp.zeros_like(q)
    return jnp.concatenate([z, q] if odd else [q, z], axis=0)


def _ffn_kernel(x_ref, g_ref, wu_ref, wd_ref, gf_ref, o_ref, *, chunk, final):
    x = x_ref[...]
    xn = _rms(x, g_ref[...]).astype(BF16)
    acc = x
    for c in range(0, wu_ref.shape[1], chunk):
        u = _dot(xn, wu_ref[:, c:c + chunk])
        a = jnp.square(jnp.maximum(u, 0.0)).astype(BF16)
        acc = acc + _dot(a, wd_ref[c:c + chunk, :])
    if final:
        acc = _rms(acc, gf_ref[...])
    o_ref[...] = acc


def ffn(x, g, wu, wd, gf, *, tm, final):
    m, d = x.shape
    dff = wu.shape[1]
    row = lambda i: (i, 0)
    const = lambda i: (0, 0)
    return pl.pallas_call(
        functools.partial(_ffn_kernel, chunk=512, final=final),
        grid=(m // tm,),
        in_specs=[pl.BlockSpec((tm, d), row), pl.BlockSpec((1, d), const),
                  pl.BlockSpec((d, dff), const, pipeline_mode=pl.Buffered(1)),
                  pl.BlockSpec((dff, d), const, pipeline_mode=pl.Buffered(1)),
                  pl.BlockSpec((1, d), const)],
        out_specs=pl.BlockSpec((tm, d), row),
        out_shape=jax.ShapeDtypeStruct((m, d), F32),
        compiler_params=_params("parallel"),
        name="ffn",
    )(x, g, wu, wd, gf)


def _mm_res_kernel(x_ref, w_ref, r_ref, o_ref):
    o_ref[...] = r_ref[...] + _dot(x_ref[...], w_ref[...])


def mm_res(x, w, res, *, tm):
    m, k = x.shape
    n = w.shape[1]
    row = lambda i: (i, 0)
    return pl.pallas_call(
        _mm_res_kernel,
        grid=(m // tm,),
        in_specs=[pl.BlockSpec((tm, k), row), pl.BlockSpec((k, n), lambda i: (0, 0)),
                  pl.BlockSpec((tm, n), row)],
        out_specs=pl.BlockSpec((tm, n), row),
        out_shape=jax.ShapeDtypeStruct((m, n), F32),
        compiler_params=_params("parallel"),
        name="mm_res",
    )(x, w, res)


def _mm_res_t_kernel(xt_ref, wt_ref, r_ref, o_ref):
    o_ref[...] = r_ref[...] + _dot(wt_ref[...], xt_ref[...]).T


def mm_res_t(xt, wt, res, *, tm):
    k, m = xt.shape
    n = wt.shape[0]
    row = lambda i: (i, 0)
    return pl.pallas_call(
        _mm_res_t_kernel,
        grid=(m // tm,),
        in_specs=[pl.BlockSpec((k, tm), lambda i: (0, i)), pl.BlockSpec((n, k), lambda i: (0, 0)),
                  pl.BlockSpec((tm, n), row)],
        out_specs=pl.BlockSpec((tm, n), row),
        out_shape=jax.ShapeDtypeStruct((m, n), F32),
        compiler_params=_params("parallel"),
        name="mm_res_t",
    )(xt, wt, res)


def _nsa_in_kernel(x_ref, g_ref, w_ref, cos_ref, sin_ref, q_ref, rows_ref, kva_ref, win_ref, gate_ref):
    xn = _rms(x_ref[...], g_ref[...]).astype(BF16)
    cos = cos_ref[...]
    sin = sin_ref[...]
    q = _rope(_dot(xn, w_ref[:, 0:1024]), cos, sin) * SCALE
    q_ref[...] = q.astype(BF16)
    kv = _dot(xn, w_ref[:, 1024:2048])
    ks = _rope(kv[:, 512:768], cos, sin)
    rows_ref[:, 0:512] = kv[:, 0:512]
    rows_ref[:, 512:768] = ks
    rows_ref[:, 768:1024] = kv[:, 768:1024]
    wkv = _dot(xn, w_ref[:, 2048:2560])
    kw = _rope(wkv[:, 0:256], cos, sin)
    win_ref[:, 0:256] = kw
    win_ref[:, 256:512] = wkv[:, 256:512]
    kva_ref[:, 0:256] = ks.astype(BF16)
    kva_ref[:, 256:512] = kv[:, 768:1024].astype(BF16)
    kva_ref[:, 512:768] = kw.astype(BF16)
    kva_ref[:, 768:1024] = wkv[:, 256:512].astype(BF16)
    gate_ref[...] = _sigmoid(_dot(xn, w_ref[:, 2560:2688]))


def nsa_in(x, g, w, cos, sin, *, tm):
    m, d = x.shape
    nw = w.shape[1]
    nt = cos.shape[0] // tm
    row = lambda i: (i, 0)
    tab = lambda i: (i % nt, 0)
    const = lambda i: (0, 0)
    outs = [(1024, BF16), (1024, F32), (1024, BF16), (512, F32), (LANES, F32)]
    return pl.pallas_call(
        _nsa_in_kernel,
        grid=(m // tm,),
        in_specs=[pl.BlockSpec((tm, d), row), pl.BlockSpec((1, d), const),
                  pl.BlockSpec((d, nw), const, pipeline_mode=pl.Buffered(1)),
                  pl.BlockSpec((tm, LANES), tab), pl.BlockSpec((tm, LANES), tab)],
        out_specs=[pl.BlockSpec((tm, n), row) for n, _ in outs],
        out_shape=[jax.ShapeDtypeStruct((m, n), dt) for n, dt in outs],
        compiler_params=_params("parallel"),
        name="nsa_in",
    )(x, g, w, cos, sin)


def _nsa_in_t_kernel(x_ref, g_ref, w_ref, wqt_ref, wvt_ref, wgt_ref, cos_ref, sin_ref, cost_ref, sint_ref,
                     qt_ref, rows_ref, kk_ref, vt_ref, win_ref, gt_ref):
    xn = _rms(x_ref[...], g_ref[...]).astype(BF16)
    cos = cos_ref[...]
    sin = sin_ref[...]
    qt = _rope_t(_dot_nt(wqt_ref[...], xn), cost_ref[...], sint_ref[...]) * SCALE
    qt_ref[...] = qt.astype(BF16)
    kv = _dot(xn, w_ref[:, 0:1024])
    ks = _rope(kv[:, 512:768], cos, sin)
    rows_ref[:, 0:512] = kv[:, 0:512]
    rows_ref[:, 512:768] = ks
    rows_ref[:, 768:1024] = kv[:, 768:1024]
    wkv = _dot(xn, w_ref[:, 1024:1536])
    kw = _rope(wkv[:, 0:256], cos, sin)
    win_ref[:, 0:256] = kw
    win_ref[:, 256:512] = wkv[:, 256:512]
    kk_ref[:, 0:256] = ks.astype(BF16)
    kk_ref[:, 256:512] = kw.astype(BF16)
    vt_ref[...] = _dot_nt(wvt_ref[...], xn).astype(BF16)
    gt_ref[...] = _sigmoid(_dot_nt(wgt_ref[...], xn))


def nsa_in_t(x, g, w_kv, wq_t, wv_t, wg_t, cos, sin, cos_t, sin_t, *, tm):
    m, d = x.shape
    nt = cos.shape[0] // tm
    row = lambda i: (i, 0)
    col = lambda i: (0, i)
    const = lambda i: (0, 0)
    one = pl.Buffered(1)
    return pl.pallas_call(
        _nsa_in_t_kernel,
        grid=(m // tm,),
        in_specs=[pl.BlockSpec((tm, d), row), pl.BlockSpec((1, d), const),
                  pl.BlockSpec(w_kv.shape, const, pipeline_mode=one), pl.BlockSpec(wq_t.shape, const, pipeline_mode=one),
                  pl.BlockSpec(wv_t.shape, const, pipeline_mode=one), pl.BlockSpec(wg_t.shape, const, pipeline_mode=one),
                  pl.BlockSpec((tm, LANES), lambda i: (i % nt, 0)), pl.BlockSpec((tm, LANES), lambda i: (i % nt, 0)),
                  pl.BlockSpec((HEAD_DIM, tm), lambda i: (0, i % nt)), pl.BlockSpec((HEAD_DIM, tm), lambda i: (0, i % nt))],
        out_specs=[pl.BlockSpec((1024, tm), col), pl.BlockSpec((tm, 1024), row), pl.BlockSpec((tm, 512), row),
                   pl.BlockSpec((None, 512, tm), lambda i: (i, 0, 0)), pl.BlockSpec((tm, 512), row),
                   pl.BlockSpec((LANES, tm), col)],
        out_shape=[jax.ShapeDtypeStruct((1024, m), BF16), jax.ShapeDtypeStruct((m, 1024), F32),
                   jax.ShapeDtypeStruct((m, 512), BF16), jax.ShapeDtypeStruct((m // tm, 512, tm), BF16),
                   jax.ShapeDtypeStruct((m, 512), F32), jax.ShapeDtypeStruct((LANES, m), F32)],
        compiler_params=_params("parallel"),
        name="nsa_in_t",
    )(x, g, w_kv, wq_t, wv_t, wg_t, cos, sin, cos_t, sin_t)


def _compress_kernel(pt_ref, *refs, pps, feature_major):
    del pt_ref
    pages = refs[:pps + 1]
    (pelo_k, pehi_k, w1lo_k, w1hi_k, w2_k, pelo_v, pehi_v, w1lo_v, w1hi_v, w2_v, w2t_v,
     cos_ref, sin_ref, eye_ref, kc_ref, vc_ref, vct_ref, xs_ref) = refs[pps + 1:]
    npc = pps * 8
    rows = (pps + 1) * 8
    low = lax.broadcasted_iota(jnp.int32, (rows, LANES), 1) < HEAD_DIM
    for k, pg in enumerate(pages):
        x = pg[...]
        if feature_major:
            hi, mid, lo = _split3(x)
            eye = eye_ref[...]
            x = _dot_nt(eye, hi) + _dot_nt(eye, mid) + _dot_nt(eye, lo)
        for lb in range(4):
            xs_ref[lb, k * PAGE_SIZE:(k + 1) * PAGE_SIZE, :] = x[:, lb * LANES:(lb + 1) * LANES]
    streams = ((pelo_k, pehi_k, w1lo_k, w1hi_k, w2_k, kc_ref), (pelo_v, pehi_v, w1lo_v, w1hi_v, w2_v, vc_ref))
    for s, (pelo, pehi, w1lo, w1hi, w2, out_ref) in enumerate(streams):
        heads = [[] for _ in range(NSA_KV_HEADS)]
        for lb in range(2):
            for q in range(CMP_STRIDE // 2):
                a = xs_ref[2 * s + lb, pl.ds(2 * q, rows, stride=CMP_STRIDE), :]
                b = xs_ref[2 * s + lb, pl.ds(2 * q + 1, rows, stride=CMP_STRIDE), :]
                heads[2 * lb].append(jnp.where(low, a, pltpu.roll(b, HEAD_DIM, 1)))
                heads[2 * lb + 1].append(jnp.where(low, pltpu.roll(a, HEAD_DIM, 1), b))
        x = jnp.concatenate([jnp.concatenate(hh, axis=1) for hh in heads], axis=0)
        first = _dot((x + pelo[...]).astype(BF16), w1lo[...])
        second = _dot((x + pehi[...]).astype(BF16), w1hi[...])
        outs = []
        outs_t = []
        for h in range(NSA_KV_HEADS):
            pre = first[h * rows:h * rows + npc] + second[h * rows + 1:h * rows + 1 + npc]
            hid = (pre * _sigmoid(pre)).astype(BF16)
            outs.append(_dot(hid, w2[...]))
            if s == 1:
                outs_t.append(_dot_nt(w2t_v[...], hid))
        res = jnp.concatenate(outs, axis=1)
        if s == 0:
            res = _rope(res, cos_ref[...], sin_ref[...])
        else:
            vct_ref[...] = jnp.concatenate(outs_t, axis=0).astype(BF16)
        out_ref[...] = res.astype(BF16)


def nsa_compress(rows3d, pt_flat, n_seq, n_pages, wk, wv, cos_c, sin_c, *, feature_major):
    pps = min(16, n_pages)
    steps = n_pages // pps
    npc = pps * 8
    page_block = (None, 512, PAGE_SIZE) if feature_major else (None, PAGE_SIZE, 512)
    eye = jnp.eye(PAGE_SIZE, dtype=BF16)

    def page_map(k):
        return lambda b, s, pt: (pt[b * n_pages + jnp.minimum(s * pps + k, n_pages - 1)], 0, 0)

    const2 = lambda b, s, pt: (0, 0)
    wspecs = []
    for _ in range(2):
        wspecs += [pl.BlockSpec((1, 1024), const2), pl.BlockSpec((1, 1024), const2),
                   pl.BlockSpec((1024, 256), const2), pl.BlockSpec((1024, 256), const2),
                   pl.BlockSpec((256, HEAD_DIM), const2)]
    wspecs.append(pl.BlockSpec((HEAD_DIM, 256), const2))
    grid_spec = pltpu.PrefetchScalarGridSpec(
        num_scalar_prefetch=1,
        grid=(n_seq, steps),
        in_specs=[pl.BlockSpec(page_block, page_map(k)) for k in range(pps + 1)] + wspecs
        + [pl.BlockSpec((npc, LANES), lambda b, s, pt: (s, 0))] * 2 + [pl.BlockSpec(eye.shape, const2)],
        out_specs=[pl.BlockSpec((None, npc, 256), lambda b, s, pt: (b, s, 0))] * 2
        + [pl.BlockSpec((None, 256, npc), lambda b, s, pt: (b, 0, s))],
        scratch_shapes=[pltpu.VMEM((4, (pps + 1) * PAGE_SIZE, LANES), F32)],
    )
    return pl.pallas_call(
        functools.partial(_compress_kernel, pps=pps, feature_major=feature_major),
        grid_spec=grid_spec,
        out_shape=[jax.ShapeDtypeStruct((n_seq, n_pages * 8, 256), BF16)] * 2
        + [jax.ShapeDtypeStruct((n_seq, 256, n_pages * 8), BF16)],
        compiler_params=_params("parallel", "arbitrary"),
        name="nsa_compress",
    )(pt_flat, *([rows3d] * (pps + 1)), *wk, *wv, cos_c, sin_c, eye)


def _nsa_attn_kernel(qt_ref, gt_ref, kc_ref, vct_ref, kk_ref, vt_ref, ovlt_ref, emt_ref, ot_ref, *,
                     tq, tks, n_top, nblk):
    i = pl.program_id(1)
    c0 = i * tq
    n4 = NSA_GROUP * tq
    pos1 = c0 + lax.broadcasted_iota(jnp.int32, (1, tq), 1)
    pos4 = c0 + lax.broadcasted_iota(jnp.int32, (1, n4), 1) % tq
    blk = lax.broadcasted_iota(jnp.int32, (nblk, tq), 0)
    cur = pos1 // SEL_BLOCK
    heads = range(NSA_KV_HEADS)
    pair = [slice((h // 2) * LANES, (h // 2 + 1) * LANES) for h in heads]
    vrow = [slice(h * HEAD_DIM, (h + 1) * HEAD_DIM) for h in heads]
    qpad = []
    for h in heads:
        q4 = jnp.concatenate(
            [qt_ref[(NSA_GROUP * h + g) * HEAD_DIM:(NSA_GROUP * h + g + 1) * HEAD_DIM, :] for g in range(NSA_GROUP)],
            axis=1)
        qpad.append(_pad_pair(q4, h % 2 == 1))

    o_c, selb = [], []
    for h in heads:
        s = _dot(kc_ref[:, pair[h]], qpad[h])
        cend = lax.broadcasted_iota(jnp.int32, s.shape, 0) * CMP_STRIDE + (CMP_BLOCK - 1)
        s = jnp.where(cend <= pos4, s, -jnp.inf)
        m = jnp.max(s, axis=0, keepdims=True)
        p = jnp.exp(s - jnp.where(m > -jnp.inf, m, 0.0))
        p = p * (1.0 / jnp.maximum(jnp.sum(p, axis=0, keepdims=True), 1e-30))
        o_c.append(_dot(vct_ref[vrow[h], :], p.astype(BF16)))
        psum = p[:, 0:tq] + p[:, tq:2 * tq] + p[:, 2 * tq:3 * tq] + p[:, 3 * tq:4 * tq]
        imp = _dot3_rhs(ovlt_ref[...], psum)
        forced = (blk == 0) | (blk == cur) | (blk == cur - 1)
        imp = jnp.where(forced, jnp.inf, imp)
        imp = jnp.where(blk <= cur, imp, -jnp.inf)
        sel = _topk_mask_t(imp, n_top)
        sel_bias = jnp.where(sel > 0.5, 0.0, NEG).astype(BF16)
        sel_bias = jnp.concatenate([sel_bias] * NSA_GROUP, axis=1)
        fill = jnp.zeros((LANES - nblk, n4), BF16)
        selb.append(jnp.concatenate([qpad[h], sel_bias, fill], axis=0))

    def finish(carry):
        return [acc * (1.0 / jnp.maximum(l, 1e-30)) for _, l, acc in carry]

    init = tuple(_online_init_t(n4) for _ in heads)
    t_last = (c0 + tq - 1) // tks

    def sel_tile(t, carry, bias):
        k0 = pl.multiple_of(t * tks, tks)
        new = []
        for h in heads:
            k_aug = jnp.concatenate([kk_ref[pl.ds(k0, tks), pair[h]], emt_ref[t]], axis=1)
            new.append(_online_update_t(carry[h], _dot(k_aug, selb[h]), bias, vt_ref[t, vrow[h], :]))
        return tuple(new)

    carry = lax.fori_loop(0, t_last, lambda t, c: sel_tile(t, c, None), init)
    kpos = t_last * tks + lax.broadcasted_iota(jnp.int32, (tks, n4), 0)
    o_s = finish(sel_tile(t_last, carry, jnp.where(kpos <= pos4, 0.0, NEG)))

    t_hi = t_last + 1

    def win_body(t, carry):
        k0 = pl.multiple_of(t * tks, tks)
        diff = pos4 - (k0 + lax.broadcasted_iota(jnp.int32, (tks, n4), 0))
        bias = jnp.where((diff >= 0) & (diff <= WINDOW), 0.0, NEG)
        new = []
        for h in heads:
            s = _dot(kk_ref[pl.ds(k0, tks), 256 + pair[h].start:256 + pair[h].stop], qpad[h])
            new.append(_online_update_t(carry[h], s, bias, vt_ref[t, 256 + vrow[h].start:256 + vrow[h].stop, :]))
        return tuple(new)

    o_w = finish(lax.fori_loop(jnp.maximum(c0 - WINDOW, 0) // tks, t_hi, win_body, init))

    for h in heads:
        for g in range(NSA_GROUP):
            j = (h * NSA_GROUP + g) * 3
            cs = slice(g * tq, (g + 1) * tq)
            o = (gt_ref[j:j + 1, :] * o_c[h][:, cs] + gt_ref[j + 1:j + 2, :] * o_s[h][:, cs]
                 + gt_ref[j + 2:j + 3, :] * o_w[h][:, cs])
            ot_ref[(NSA_GROUP * h + g) * HEAD_DIM:(NSA_GROUP * h + g + 1) * HEAD_DIM, :] = o.astype(BF16)


def nsa_attn(qt, gt, kc, vct, kk, vt, ovlt, emt, *, n_b, t, tq):
    nq = t // tq
    tks = vt.shape[2]
    npiece = kc.shape[1]
    nblk = ovlt.shape[0]
    n_top = min(SEL_TOPN, t // SEL_BLOCK)
    col = lambda b, i: (0, b * nq + i)
    per_b = lambda b, i: (b, 0, 0)
    return pl.pallas_call(
        functools.partial(_nsa_attn_kernel, tq=tq, tks=tks, n_top=n_top, nblk=nblk),
        grid=(n_b, nq),
        in_specs=[pl.BlockSpec((1024, tq), col), pl.BlockSpec((LANES, tq), col),
                  pl.BlockSpec((None, npiece, 256), per_b), pl.BlockSpec((None, 256, npiece), per_b),
                  pl.BlockSpec((None, t, 512), per_b), pl.BlockSpec((t // tks, 512, tks), per_b),
                  pl.BlockSpec(ovlt.shape, lambda b, i: (0, 0)),
                  pl.BlockSpec(emt.shape, lambda b, i: (0, 0, 0))],
        out_specs=pl.BlockSpec((1024, tq), col),
        out_shape=jax.ShapeDtypeStruct((1024, n_b * t), BF16),
        compiler_params=_params("parallel", "arbitrary"),
        name="nsa_attn",
    )(qt, gt, kc, vct, kk.reshape(n_b, t, 512), vt, ovlt, emt)


def _nsa_s_sel_kernel(q_ref, kc_ref, vc_ref, ovl_ref, gm_ref, oc_ref, idx_ref, *, pos, n_pick):
    s = _dot_nt(q_ref[...], kc_ref[...])
    cend = lax.broadcasted_iota(jnp.int32, s.shape, 1) * CMP_STRIDE + (CMP_BLOCK - 1)
    p = _masked_softmax(s, cend <= pos)
    oc_ref[...] = _dot(p.astype(BF16), vc_ref[...])
    psum = _dot3_rhs(gm_ref[...], p)
    imp = _dot3(psum, ovl_ref[...])
    blk = lax.broadcasted_iota(jnp.int32, imp.shape, 1)
    cur = pos // SEL_BLOCK
    imp = jnp.where((blk == 0) | (blk == cur - 1), jnp.inf, imp)
    imp = jnp.where(blk < cur, imp, -jnp.inf)
    lane = blk.astype(F32)
    slot = lax.broadcasted_iota(jnp.int32, (8, LANES), 1)
    picks = jnp.zeros((8, LANES), F32)
    for r in range(n_pick):
        m = jnp.max(imp, axis=-1, keepdims=True)
        idx = jnp.min(jnp.where(imp == m, lane, 1e9), axis=-1, keepdims=True)
        imp = jnp.where(lane == idx, -jnp.inf, imp)
        picks = jnp.where(slot == r, idx, picks)
    idx_ref[...] = picks.astype(jnp.int32)


def nsa_s_sel(qbd, kc, vc, ovl, gm, *, pos, n_pick):
    n_b, _, npiece = kc.shape[0], None, kc.shape[1]
    per_b = lambda b: (b, 0, 0)
    return pl.pallas_call(
        functools.partial(_nsa_s_sel_kernel, pos=pos, n_pick=n_pick),
        grid=(n_b,),
        in_specs=[pl.BlockSpec((None, 16, 256), per_b), pl.BlockSpec((None, npiece, 256), per_b),
                  pl.BlockSpec((None, npiece, 256), per_b),
                  pl.BlockSpec(ovl.shape, lambda b: (0, 0)), pl.BlockSpec(gm.shape, lambda b: (0, 0))],
        out_specs=[pl.BlockSpec((None, 16, 256), per_b), pl.BlockSpec((None, 8, LANES), per_b)],
        out_shape=[jax.ShapeDtypeStruct((n_b, 16, 256), F32), jax.ShapeDtypeStruct((n_b, 8, LANES), jnp.int32)],
        compiler_params=_params("parallel"),
        name="nsa_s_sel",
    )(qbd, kc, vc, ovl, gm)


def _attend_with_new(q, kts, vts, biases, k_new, v_new):
    ss = []
    for kt, bias in zip(kts, biases):
        s = _dot(q, kt.astype(BF16))
        ss.append(s if bias is None else s + bias)
    s = ss[0] if len(ss) == 1 else jnp.concatenate(ss, axis=1)
    s_new = jnp.sum(q.astype(F32) * k_new.astype(F32), axis=-1, keepdims=True)
    m = jnp.maximum(jnp.max(s, axis=-1, keepdims=True), s_new)
    p = jnp.exp(s - m)
    p_new = jnp.exp(s_new - m)
    l = jnp.sum(p, axis=-1, keepdims=True) + p_new
    acc = p_new * v_new.astype(F32)
    off = 0
    for vt in vts:
        n = vt.shape[1]
        acc = acc + _dot_nt(p[:, off:off + n].astype(BF16), vt.astype(BF16))
        off += n
    return acc / l


def _nsa_s_attn_kernel(pt_ref, ix_ref, q_ref, *refs, n_pick):
    del pt_ref
    kblk = refs[:n_pick]
    vblk = refs[n_pick:2 * n_pick]
    ks_new, vs_new, kw_new, vw_new, kw_ref, vw_ref, os_ref, ow_ref = refs[2 * n_pick:]
    base = (pl.program_id(0) * NSA_KV_HEADS + pl.program_id(1)) * n_pick
    half = lax.broadcasted_iota(jnp.int32, (1, PAGE_SIZE), 1) // SEL_BLOCK
    biases = [jnp.where(half == ix_ref[base + r] % 2, 0.0, NEG) for r in range(n_pick)]
    q = q_ref[...]
    os_ref[...] = _attend_with_new(q, [r[...] for r in kblk], [r[...] for r in vblk], biases, ks_new[...], vs_new[...])
    ow_ref[...] = _attend_with_new(q, [kw_ref[...]], [vw_ref[...]], [None], kw_new[...], vw_new[...])


def nsa_s_attn(pt_flat, ix_flat, q4, cache_fm, new_rows, win_fm, *, n_pages, n_pick):
    n_b = q4.shape[0]
    nwin = win_fm.shape[2]
    per_page = PAGE_SIZE // SEL_BLOCK

    def kv_map(r, stream):
        def f(b, h, pt, ix):
            j = ix[(b * NSA_KV_HEADS + h) * n_pick + r]
            return (pt[b * n_pages + j // per_page], stream * NSA_KV_HEADS + h, 0)
        return f

    new_map = lambda s: (lambda b, h, pt, ix: (b, s * NSA_KV_HEADS + h, 0, 0))
    qo_spec = pl.BlockSpec((None, None, 8, HEAD_DIM), lambda b, h, pt, ix: (b, h, 0, 0))
    grid_spec = pltpu.PrefetchScalarGridSpec(
        num_scalar_prefetch=2,
        grid=(n_b, NSA_KV_HEADS),
        in_specs=[qo_spec]
        + [pl.BlockSpec((None, HEAD_DIM, PAGE_SIZE), kv_map(r, 2)) for r in range(n_pick)]
        + [pl.BlockSpec((None, HEAD_DIM, PAGE_SIZE), kv_map(r, 3)) for r in range(n_pick)]
        + [pl.BlockSpec((None, None, 1, HEAD_DIM), new_map(s)) for s in range(4)]
        + [pl.BlockSpec((None, HEAD_DIM, nwin), lambda b, h, pt, ix: (b, h, 0)),
           pl.BlockSpec((None, HEAD_DIM, nwin), lambda b, h, pt, ix: (b, NSA_KV_HEADS + h, 0))],
        out_specs=[qo_spec, qo_spec],
    )
    return pl.pallas_call(
        functools.partial(_nsa_s_attn_kernel, n_pick=n_pick),
        grid_spec=grid_spec,
        out_shape=[jax.ShapeDtypeStruct((n_b, NSA_KV_HEADS, 8, HEAD_DIM), F32)] * 2,
        compiler_params=_params("parallel", "arbitrary"),
        name="nsa_s_attn",
    )(pt_flat, ix_flat, q4, *([cache_fm] * (2 * n_pick)), *([new_rows] * 4), win_fm, win_fm)


def _nsa_out_s_kernel(oc_ref, os_ref, ow_ref, g0_ref, g1_ref, g2_ref, w_ref, r_ref, o_ref):
    o = g0_ref[...] * oc_ref[...] + g1_ref[...] * os_ref[...] + g2_ref[...] * ow_ref[...]
    o_ref[...] = r_ref[...] + _dot(o.astype(BF16), w_ref[...])


def nsa_out_s(oc, osel, ow, g0, g1, g2, w, res):
    m, d = res.shape
    full = pl.BlockSpec((m, d), lambda i: (0, 0))
    return pl.pallas_call(
        _nsa_out_s_kernel,
        grid=(1,),
        in_specs=[full] * 6 + [pl.BlockSpec(w.shape, lambda i: (0, 0)), full],
        out_specs=full,
        out_shape=jax.ShapeDtypeStruct((m, d), F32),
        compiler_params=_params("arbitrary"),
        name="nsa_out_s",
    )(oc, osel, ow, g0, g1, g2, w, res)


def _sconv_kernel(x_ref, g_ref, win_ref, wc_ref, wout_ref, o_ref, st_ref, carry_ref):
    d = x_ref.shape[1]
    tm = x_ref.shape[0]

    @pl.when(pl.program_id(1) == 0)
    def _():
        carry_ref[...] = jnp.zeros(carry_ref.shape, F32)

    x = x_ref[...]
    xn = _rms(x, g_ref[...]).astype(BF16)
    b_gate = _dot(xn, win_ref[:, 0:d])
    pre = _dot(xn, win_ref[:, d:2 * d]) * _dot(xn, win_ref[:, 2 * d:3 * d])
    row = lax.broadcasted_iota(jnp.int32, (tm, d), 0)
    back1 = jnp.where(row == 0, carry_ref[7:8, :], pltpu.roll(pre, 1, 0))
    back2 = jnp.where(row == 0, carry_ref[6:7, :], jnp.where(row == 1, carry_ref[7:8, :], pltpu.roll(pre, 2, 0)))
    y = back2 * wc_ref[0:1, :] + back1 * wc_ref[1:2, :] + pre * wc_ref[2:3, :]
    tail = pre[tm - 8:tm]
    carry_ref[...] = tail
    st_ref[...] = tail
    o_ref[...] = x + _dot((b_gate * y).astype(BF16), wout_ref[...])


def sconv_prompt(x, g, w_in, w_conv, w_out, *, n_b, t, tm):
    d = x.shape[1]
    nt = t // tm
    row = lambda b, i: (b * nt + i, 0)
    const = lambda b, i: (0, 0)
    return pl.pallas_call(
        _sconv_kernel,
        grid=(n_b, nt),
        in_specs=[pl.BlockSpec((tm, d), row), pl.BlockSpec((1, d), const),
                  pl.BlockSpec((d, 3 * d), const, pipeline_mode=pl.Buffered(1)),
                  pl.BlockSpec(w_conv.shape, const),
                  pl.BlockSpec((d, d), const, pipeline_mode=pl.Buffered(1))],
        out_specs=[pl.BlockSpec((tm, d), row), pl.BlockSpec((None, 8, d), lambda b, i: (b, 0, 0))],
        out_shape=[jax.ShapeDtypeStruct((n_b * t, d), F32), jax.ShapeDtypeStruct((n_b, 8, d), F32)],
        scratch_shapes=[pltpu.VMEM((8, d), F32)],
        compiler_params=_params("parallel", "arbitrary"),
        name="sconv_prompt",
    )(x, g, w_in, w_conv, w_out)


def _sconv_s_kernel(x_ref, g_ref, win_ref, wc_ref, wout_ref, p0_ref, p1_ref, o_ref, pre_ref):
    d = x_ref.shape[1]
    x = x_ref[...]
    xn = _rms(x, g_ref[...]).astype(BF16)
    b_gate = _dot(xn, win_ref[:, 0:d])
    pre = _dot(xn, win_ref[:, d:2 * d]) * _dot(xn, win_ref[:, 2 * d:3 * d])
    y = p0_ref[...] * wc_ref[0:1, :] + p1_ref[...] * wc_ref[1:2, :] + pre * wc_ref[2:3, :]
    pre_ref[...] = pre
    o_ref[...] = x + _dot((b_gate * y).astype(BF16), wout_ref[...])


def sconv_sample(x, g, w_in, w_conv, w_out, past0, past1):
    m, d = x.shape
    full = lambda a: pl.BlockSpec(a.shape, lambda i: (0,) * a.ndim)
    args = (x, g, w_in, w_conv, w_out, past0, past1)
    return pl.pallas_call(
        _sconv_s_kernel,
        grid=(1,),
        in_specs=[full(a) for a in args],
        out_specs=[pl.BlockSpec((m, d), lambda i: (0, 0))] * 2,
        out_shape=[jax.ShapeDtypeStruct((m, d), F32)] * 2,
        compiler_params=_params("arbitrary"),
        name="sconv_sample",
    )(*args)


def _moba_in_kernel(x_ref, g_ref, w_ref, cos_ref, sin_ref, q_ref, rows_ref, kb_ref, vb_ref, km_ref):
    d = x_ref.shape[1]
    xn = _rms(x_ref[...], g_ref[...]).astype(BF16)
    cos = cos_ref[...]
    sin = sin_ref[...]
    q_ref[...] = (_rope(_dot(xn, w_ref[:, 0:d]), cos, sin) * SCALE).astype(BF16)
    k = _rope(_dot(xn, w_ref[:, d:2 * d]), cos, sin)
    v = _dot(xn, w_ref[:, 2 * d:3 * d])
    rows_ref[:, 0:d] = k
    rows_ref[:, d:2 * d] = v
    kb_ref[...] = k.astype(BF16)
    vb_ref[...] = v.astype(BF16)
    km_ref[...] = jnp.sum(k, axis=0, keepdims=True) * (1.0 / MOBA_BLOCK)


def moba_in(x, g, w, cos, sin, *, tm):
    m, d = x.shape
    nt = cos.shape[0] // tm
    row = lambda i: (i, 0)
    tab = lambda i: (i % nt, 0)
    const = lambda i: (0, 0)
    return pl.pallas_call(
        _moba_in_kernel,
        grid=(m // tm,),
        in_specs=[pl.BlockSpec((tm, d), row), pl.BlockSpec((1, d), const),
                  pl.BlockSpec((d, 3 * d), const, pipeline_mode=pl.Buffered(1)),
                  pl.BlockSpec((tm, LANES), tab), pl.BlockSpec((tm, LANES), tab)],
        out_specs=[pl.BlockSpec((tm, d), row), pl.BlockSpec((tm, 2 * d), row), pl.BlockSpec((tm, d), row),
                   pl.BlockSpec((tm, d), row), pl.BlockSpec((None, 1, d), lambda i: (i, 0, 0))],
        out_shape=[jax.ShapeDtypeStruct((m, d), BF16), jax.ShapeDtypeStruct((m, 2 * d), F32),
                   jax.ShapeDtypeStruct((m, d), BF16), jax.ShapeDtypeStruct((m, d), BF16),
                   jax.ShapeDtypeStruct((m // tm, 1, d), F32)],
        compiler_params=_params("parallel"),
        name="moba_in",
    )(x, g, w, cos, sin)


def _moba_in_t_kernel(x_ref, g_ref, w_ref, wqt_ref, wvt_ref, cos_ref, sin_ref, cost_ref, sint_ref,
                      qt_ref, rows_ref, kb_ref, vt_ref, km_ref):
    d = x_ref.shape[1]
    xn = _rms(x_ref[...], g_ref[...]).astype(BF16)
    qt = _rope_t(_dot_nt(wqt_ref[...], xn), cost_ref[...], sint_ref[...]) * SCALE
    qt_ref[...] = qt.astype(BF16)
    k = _rope(_dot(xn, w_ref[:, 0:d]), cos_ref[...], sin_ref[...])
    rows_ref[:, 0:d] = k
    rows_ref[:, d:2 * d] = _dot(xn, w_ref[:, d:2 * d])
    kb_ref[...] = k.astype(BF16)
    vt_ref[...] = _dot_nt(wvt_ref[...], xn).astype(BF16)
    km_ref[...] = jnp.sum(k, axis=0, keepdims=True) * (1.0 / MOBA_BLOCK)


def moba_in_t(x, g, w_kv, wq_t, wv_t, cos, sin, cos_t, sin_t):
    m, d = x.shape
    tm = MOBA_BLOCK
    nt = cos.shape[0] // tm
    row = lambda i: (i, 0)
    const = lambda i: (0, 0)
    one = pl.Buffered(1)
    return pl.pallas_call(
        _moba_in_t_kernel,
        grid=(m // tm,),
        in_specs=[pl.BlockSpec((tm, d), row), pl.BlockSpec((1, d), const),
                  pl.BlockSpec(w_kv.shape, const, pipeline_mode=one), pl.BlockSpec(wq_t.shape, const, pipeline_mode=one),
                  pl.BlockSpec(wv_t.shape, const, pipeline_mode=one),
                  pl.BlockSpec((tm, LANES), lambda i: (i % nt, 0)), pl.BlockSpec((tm, LANES), lambda i: (i % nt, 0)),
                  pl.BlockSpec((HEAD_DIM, tm), lambda i: (0, i % nt)), pl.BlockSpec((HEAD_DIM, tm), lambda i: (0, i % nt))],
        out_specs=[pl.BlockSpec((d, tm), lambda i: (0, i)), pl.BlockSpec((tm, 2 * d), row), pl.BlockSpec((tm, d), row),
                   pl.BlockSpec((None, d, tm), lambda i: (i, 0, 0)), pl.BlockSpec((None, 1, d), lambda i: (i, 0, 0))],
        out_shape=[jax.ShapeDtypeStruct((d, m), BF16), jax.ShapeDtypeStruct((m, 2 * d), F32),
                   jax.ShapeDtypeStruct((m, d), BF16), jax.ShapeDtypeStruct((m // tm, d, tm), BF16),
                   jax.ShapeDtypeStruct((m // tm, 1, d), F32)],
        compiler_params=_params("parallel"),
        name="moba_in_t",
    )(x, g, w_kv, wq_t, wv_t, cos, sin, cos_t, sin_t)


def _moba_attn_kernel(qt_ref, k_ref, vt_ref, km_ref, ot_ref, *, tq, n_top):
    i = pl.program_id(2)
    c0 = i * tq
    nblk = km_ref.shape[0]
    pos = c0 + lax.broadcasted_iota(jnp.int32, (1, tq), 1)
    cur = pos // MOBA_BLOCK
    blk = lax.broadcasted_iota(jnp.int32, (nblk, tq), 0)
    m1, m2, m3 = _split3(km_ref[...])
    vrow = [slice(hh * HEAD_DIM, (hh + 1) * HEAD_DIM) for hh in range(2)]
    qpad, allow = [], []
    for hh in range(2):
        qp = _pad_pair(qt_ref[vrow[hh], :], hh == 1)
        gate = _dot(m1, qp) + _dot(m2, qp) + _dot(m3, qp)
        gate = jnp.where(blk < cur, gate, -jnp.inf)
        qpad.append(qp)
        allow.append(jnp.where(blk == cur, 1.0, _topk_mask_t(gate, n_top)))

    def tile(t, carry, causal):
        k0 = pl.multiple_of(t * MOBA_BLOCK, MOBA_BLOCK)
        k = k_ref[pl.ds(k0, MOBA_BLOCK), :]
        new = []
        for hh in range(2):
            chosen = jnp.sum(jnp.where(blk == t, allow[hh], 0.0), axis=0, keepdims=True)
            bias = jnp.where(chosen > 0.5, 0.0, NEG)
            if causal:
                kpos = k0 + lax.broadcasted_iota(jnp.int32, (MOBA_BLOCK, tq), 0)
                bias = jnp.where(kpos <= pos, bias, NEG)
            new.append(_online_update_t(carry[hh], _dot(k, qpad[hh]), bias, vt_ref[t, vrow[hh], :]))
        return tuple(new)

    t_diag = c0 // MOBA_BLOCK
    res = lax.fori_loop(0, t_diag, lambda t, c: tile(t, c, False), (_online_init_t(tq), _online_init_t(tq)))
    for e in range(tq // MOBA_BLOCK):
        res = tile(t_diag + e, res, True)
    for hh in range(2):
        _, l, acc = res[hh]
        ot_ref[vrow[hh], :] = (acc * (1.0 / jnp.maximum(l, 1e-30))).astype(BF16)


def moba_attn(qt, kb, vt, kmean, *, n_b, t, tq):
    nq = t // tq
    d = qt.shape[0]
    nblk = kmean.shape[1]
    n_top = min(MOBA_TOPK, t // MOBA_BLOCK)
    qmap = lambda b, hp, i: (hp, b * nq + i)
    kvmap = lambda b, hp, i: (b, 0, hp)
    return pl.pallas_call(
        functools.partial(_moba_attn_kernel, tq=tq, n_top=n_top),
        grid=(n_b, d // LANES, nq),
        in_specs=[pl.BlockSpec((LANES, tq), qmap), pl.BlockSpec((None, t, LANES), kvmap),
                  pl.BlockSpec((t // MOBA_BLOCK, LANES, MOBA_BLOCK), lambda b, hp, i: (b, hp, 0)),
                  pl.BlockSpec((None, nblk, LANES), kvmap)],
        out_specs=pl.BlockSpec((LANES, tq), qmap),
        out_shape=jax.ShapeDtypeStruct((d, n_b * t), BF16),
        compiler_params=_params("parallel", "parallel", "arbitrary"),
        name="moba_attn",
    )(qt, kb.reshape(n_b, t, d), vt, kmean)


def _moba_s_select_kernel(pt_ref, *refs, pps, n_top, n_blocks):
    del pt_ref
    pages = refs[:pps]
    qbd_ref, idx_ref, km_ref = refs[pps:]
    s = pl.program_id(1)
    per_blk = MOBA_BLOCK // PAGE_SIZE
    blocks_per_step = pps // per_blk

    @pl.when(s == 0)
    def _():
        km_ref[...] = jnp.zeros(km_ref.shape, F32)

    lane = lax.broadcasted_iota(jnp.int32, km_ref.shape, 1)
    km = km_ref[...]
    for j in range(blocks_per_step):
        tot = pages[per_blk * j][...]
        for e in range(1, per_blk):
            tot = tot + pages[per_blk * j + e][...]
        mean = jnp.sum(tot, axis=1, keepdims=True) * (1.0 / MOBA_BLOCK)
        km = jnp.where(lane == s * blocks_per_step + j, mean, km)
    km_ref[...] = km

    @pl.when(s == pl.num_programs(1) - 1)
    def _():
        gate = _dot3_rhs(qbd_ref[...], km_ref[...])
        blk = lax.broadcasted_iota(jnp.int32, gate.shape, 1)
        gate = jnp.where(blk < n_blocks, gate, -jnp.inf)
        lane_f = blk.astype(F32)
        picks = jnp.zeros(gate.shape, F32)
        for r in range(n_top):
            m = jnp.max(gate, axis=-1, keepdims=True)
            idx = jnp.min(jnp.where(gate == m, lane_f, 1e9), axis=-1, keepdims=True)
            gate = jnp.where(lane_f == idx, -jnp.inf, gate)
            picks = jnp.where(blk == r, idx, picks)
        idx_ref[...] = picks.astype(jnp.int32)


def moba_s_select(cache_fm, pt_flat, qbd, *, n_pages, n_top):
    n_b, _, d = qbd.shape
    pps = min(16, n_pages)
    per_blk = MOBA_BLOCK // PAGE_SIZE
    page_map = lambda k: (lambda b, s, pt: (pt[b * n_pages + s * pps + k], 0, 0))
    grid_spec = pltpu.PrefetchScalarGridSpec(
        num_scalar_prefetch=1,
        grid=(n_b, n_pages // pps),
        in_specs=[pl.BlockSpec((None, d, PAGE_SIZE), page_map(k)) for k in range(pps)]
        + [pl.BlockSpec((None, LANES, d), lambda b, s, pt: (b, 0, 0))],
        out_specs=pl.BlockSpec((None, LANES, LANES), lambda b, s, pt: (b, 0, 0)),
        scratch_shapes=[pltpu.VMEM((d, LANES), F32)],
    )
    return pl.pallas_call(
        functools.partial(_moba_s_select_kernel, pps=pps, n_top=n_top, n_blocks=n_pages // per_blk),
        grid_spec=grid_spec,
        out_shape=jax.ShapeDtypeStruct((n_b, LANES, LANES), jnp.int32),
        compiler_params=_params("parallel", "arbitrary"),
        name="moba_s_select",
    )(pt_flat, *([cache_fm] * pps), qbd)


def _moba_s_attn_kernel(pt_ref, ix_ref, q_ref, *refs, n_pg):
    del pt_ref, ix_ref
    kblk = refs[:n_pg]
    vblk = refs[n_pg:2 * n_pg]
    k_new, v_new, o_ref = refs[2 * n_pg:]
    o_ref[...] = _attend_with_new(q_ref[...], [r[...] for r in kblk], [r[...] for r in vblk], [None] * n_pg,
                                  k_new[...], v_new[...])


def moba_s_attn(pt_flat, ix_flat, q4, cache_fm, k_new, v_new, *, n_pages, n_top, n_heads):
    n_b = q4.shape[0]
    per_blk = MOBA_BLOCK // PAGE_SIZE
    n_pg = n_top * per_blk

    def kv_map(r, row0):
        def f(b, h, pt, ix):
            j = ix[(b * n_heads + h) * n_top + r // per_blk]
            return (pt[b * n_pages + j * per_blk + r % per_blk], row0 + h, 0)
        return f

    new_spec = pl.BlockSpec((None, None, 1, HEAD_DIM), lambda b, h, pt, ix: (b, h, 0, 0))
    qo_spec = pl.BlockSpec((None, None, 8, HEAD_DIM), lambda b, h, pt, ix: (b, h, 0, 0))
    grid_spec = pltpu.PrefetchScalarGridSpec(
        num_scalar_prefetch=2,
        grid=(n_b, n_heads),
        in_specs=[qo_spec]
        + [pl.BlockSpec((None, HEAD_DIM, PAGE_SIZE), kv_map(r, 0)) for r in range(n_pg)]
        + [pl.BlockSpec((None, HEAD_DIM, PAGE_SIZE), kv_map(r, n_heads)) for r in range(n_pg)]
        + [new_spec, new_spec],
        out_specs=qo_spec,
    )
    return pl.pallas_call(
        functools.partial(_moba_s_attn_kernel, n_pg=n_pg),
        grid_spec=grid_spec,
        out_shape=jax.ShapeDtypeStruct((n_b, n_heads, 8, HEAD_DIM), F32),
        compiler_params=_params("parallel", "arbitrary"),
        name="moba_s_attn",
    )(pt_flat, ix_flat, q4, *([cache_fm] * (2 * n_pg)), k_new, v_new)


def _layer_norm_silu(y, g, b):
    yc = y - jnp.mean(y, axis=-1, keepdims=True)
    yn = yc * lax.rsqrt(jnp.mean(yc * yc, axis=-1, keepdims=True) + LN_EPS) * g + b
    return yn * _sigmoid(yn)


def _conf_kernel(x_ref, g_ref, w1_ref, wdw_ref, bdw_ref, lg_ref, lb_ref, w2_ref, o_ref, st_ref, ubuf_ref, *, hist):
    tm, d = x_ref.shape
    width = wdw_ref.shape[0]

    @pl.when(pl.program_id(1) == 0)
    def _():
        ubuf_ref[0:hist, :] = jnp.zeros((hist, d), F32)

    x = x_ref[...]
    xn = _rms(x, g_ref[...]).astype(BF16)
    u = _dot(xn, w1_ref[:, 0:d]) * _sigmoid(_dot(xn, w1_ref[:, d:2 * d]))
    ubuf_ref[hist:hist + tm, :] = u
    base = hist - (width - 1)
    y = bdw_ref[...] + ubuf_ref[base:base + tm, :] * wdw_ref[0:1, :]
    for k in range(1, width):
        y = y + ubuf_ref[base + k:base + k + tm, :] * wdw_ref[k:k + 1, :]
    z = _layer_norm_silu(y, lg_ref[...], lb_ref[...])
    o_ref[...] = x + _dot(z.astype(BF16), w2_ref[...])
    tail = ubuf_ref[tm:tm + hist, :]
    st_ref[...] = tail
    ubuf_ref[0:hist, :] = tail


def conf_prompt(x, g, w1, wdw, bdw, lg, lb, w2, *, n_b, t, tm):
    d = x.shape[1]
    nt = t // tm
    hist = 32
    row = lambda b, i: (b * nt + i, 0)
    const = lambda b, i: (0, 0)
    return pl.pallas_call(
        functools.partial(_conf_kernel, hist=hist),
        grid=(n_b, nt),
        in_specs=[pl.BlockSpec((tm, d), row), pl.BlockSpec((1, d), const),
                  pl.BlockSpec((d, 2 * d), const, pipeline_mode=pl.Buffered(1)),
                  pl.BlockSpec(wdw.shape, const), pl.BlockSpec((1, d), const), pl.BlockSpec((1, d), const),
                  pl.BlockSpec((1, d), const), pl.BlockSpec((d, d), const, pipeline_mode=pl.Buffered(1))],
        out_specs=[pl.BlockSpec((tm, d), row), pl.BlockSpec((None, hist, d), lambda b, i: (b, 0, 0))],
        out_shape=[jax.ShapeDtypeStruct((n_b * t, d), F32), jax.ShapeDtypeStruct((n_b, hist, d), F32)],
        scratch_shapes=[pltpu.VMEM((hist + tm, d), F32)],
        compiler_params=_params("parallel", "arbitrary"),
        name="conf_prompt",
    )(x, g, w1, wdw, bdw, lg, lb, w2)


def _conf_s_kernel(x_ref, g_ref, w1_ref, wdw_ref, bdw_ref, lg_ref, lb_ref, w2_ref, past_ref, o_ref, u_ref):
    d = x_ref.shape[1]
    width = wdw_ref.shape[0]
    x = x_ref[...]
    xn = _rms(x, g_ref[...]).astype(BF16)
    u = _dot(xn, w1_ref[:, 0:d]) * _sigmoid(_dot(xn, w1_ref[:, d:2 * d]))
    y = bdw_ref[...] + past_ref[0] * wdw_ref[0:1, :]
    for k in range(1, width - 1):
        y = y + past_ref[k] * wdw_ref[k:k + 1, :]
    y = y + u * wdw_ref[width - 1:width, :]
    z = _layer_norm_silu(y, lg_ref[...], lb_ref[...])
    u_ref[...] = u
    o_ref[...] = x + _dot(z.astype(BF16), w2_ref[...])


def conf_sample(x, g, w1, wdw, bdw, lg, lb, w2, past_t):
    m, d = x.shape
    full = lambda a: pl.BlockSpec(a.shape, lambda i: (0,) * a.ndim)
    args = (x, g, w1, wdw, bdw, lg, lb, w2, past_t)
    return pl.pallas_call(
        _conf_s_kernel,
        grid=(1,),
        in_specs=[full(a) for a in args],
        out_specs=[pl.BlockSpec((m, d), lambda i: (0, 0))] * 2,
        out_shape=[jax.ShapeDtypeStruct((m, d), F32)] * 2,
        compiler_params=_params("arbitrary"),
        name="conf_sample",
    )(*args)


def _rope_tables(pos):
    half = HEAD_DIM // 2
    inv_freq = ROPE_THETA ** (-jnp.arange(half, dtype=F32) / half)
    ang = pos.astype(F32)[:, None] * inv_freq[None, :]
    cos = jnp.cos(ang)
    sin = jnp.sin(ang)
    cos = jnp.concatenate([cos, cos], axis=-1)
    sin = jnp.concatenate([-sin, sin], axis=-1)
    return jnp.tile(cos, (1, LANES // HEAD_DIM)), jnp.tile(sin, (1, LANES // HEAD_DIM))


def _overlap(n_cmp_rows, n_cols):
    i = jnp.arange(n_cmp_rows, dtype=jnp.int32)[:, None]
    j = jnp.arange(n_cols, dtype=jnp.int32)[None, :]
    start = i * CMP_STRIDE
    hit = (start <= j * SEL_BLOCK + (SEL_BLOCK - 1)) & (start + (CMP_BLOCK - 1) >= j * SEL_BLOCK)
    return hit.astype(BF16)


def _pad_heads_to_lane_pairs(x, rows):
    n, nh, r, hd = x.shape
    z = jnp.zeros_like(x)
    even = jnp.concatenate([x, z], axis=-1)
    odd = jnp.concatenate([z, x], axis=-1)
    is_even = (jnp.arange(nh) % 2 == 0)[None, :, None, None]
    out = jnp.where(is_even, even, odd)
    return jnp.pad(out, ((0, 0), (0, 0), (0, rows - r), (0, 0)))


def _take_lane_half(x, r):
    nh = x.shape[1]
    is_even = (jnp.arange(nh) % 2 == 0)[None, :, None, None]
    return jnp.where(is_even, x[:, :, :r, :HEAD_DIM], x[:, :, :r, HEAD_DIM:])


def _nsa_layer(hp, hs, g, cache_kv, cache_win, pt_flat, n_pages, w, *, n_b, t, n_s):
    w_in, pe_k, w1_k, w2_k, pe_v, w1_v, w2_v, w_out = w
    d = hp.shape[1]
    past_len = n_pages * PAGE_SIZE
    w_in_p = jnp.pad(w_in, ((0, 0), (0, 2688 - w_in.shape[1]))).astype(BF16)
    w_out_b = w_out.astype(BF16)
    half = CMP_BLOCK * HEAD_DIM // 2

    def cmp_weights(pe, w1, w2):
        return (pe[:CMP_STRIDE].reshape(1, half), pe[CMP_STRIDE:].reshape(1, half),
                w1[:half].astype(BF16), w1[half:].astype(BF16), w2.astype(BF16))

    wk = cmp_weights(pe_k, w1_k, w2_k)
    wv = cmp_weights(pe_v, w1_v, w2_v) + (w2_v.T.astype(BF16),)

    cos_p, sin_p = _rope_tables(jnp.arange(t, dtype=jnp.int32))
    cos_pt, sin_pt = cos_p[:, :HEAD_DIM].T, sin_p[:, :HEAD_DIM].T
    kv0, kv1, n_gate = 1024, 2560, NSA_KV_HEADS * NSA_GROUP * 3
    wq_t = w_in[:, :kv0].T.astype(BF16)
    wv_t = jnp.concatenate([w_in[:, kv0 + 768:kv0 + 1024], w_in[:, kv0 + 1280:kv1]], axis=1).T.astype(BF16)
    wg_t = jnp.pad(w_in[:, kv1:kv1 + n_gate].T, ((0, LANES - n_gate), (0, 0))).astype(BF16)
    tks = 256
    qt, rows, kk, vt, win, gt = nsa_in_t(hp, g, w_in[:, kv0:kv1].astype(BF16), wq_t, wv_t, wg_t,
                                         cos_p, sin_p, cos_pt, sin_pt, tm=tks)
    npg_p = t // PAGE_SIZE
    cend_p = jnp.arange(npg_p * 8, dtype=jnp.int32) * CMP_STRIDE + (CMP_BLOCK - 1)
    kc, _, vct = nsa_compress(rows.reshape(n_b * npg_p, PAGE_SIZE, 1024), jnp.arange(n_b * npg_p, dtype=jnp.int32),
                              n_b, npg_p, wk, wv, *_rope_tables(cend_p), feature_major=False)
    nblk = -(-(t // SEL_BLOCK) // 16) * 16
    tile = jnp.arange(t // tks, dtype=jnp.int32)[:, None, None]
    key = jnp.arange(tks, dtype=jnp.int32)[None, :, None]
    blk = jnp.arange(LANES, dtype=jnp.int32)[None, None, :]
    emt = (blk == (tile * tks + key) // SEL_BLOCK).astype(BF16)
    ot = nsa_attn(qt, gt, kc, vct, kk, vt, _overlap(npg_p * 8, nblk).T, emt, n_b=n_b, t=t, tq=128)
    hp = mm_res_t(ot, w_out.T.astype(BF16), hp, tm=512)
    kv_p = rows.reshape(n_b, t, 4, NSA_KV_HEADS, HEAD_DIM)
    keep = min(WINDOW, t)
    win_p = win.reshape(n_b, t, 2, NSA_KV_HEADS, HEAD_DIM)[:, t - keep:]

    cos_s, sin_s = _rope_tables(jnp.full((n_s,), past_len, jnp.int32))
    q_s, rows_s, kva_s, win_s, gates_s = nsa_in(hs, g, w_in_p, cos_s, sin_s, tm=n_s)
    cend_s = jnp.arange(n_pages * 8, dtype=jnp.int32) * CMP_STRIDE + (CMP_BLOCK - 1)
    cache_fm = jnp.transpose(cache_kv, (0, 2, 3, 4, 1)).reshape(cache_kv.shape[0], 1024, PAGE_SIZE)
    kc_s, vc_s, _ = nsa_compress(cache_fm, pt_flat, n_s, n_pages, wk, wv, *_rope_tables(cend_s), feature_major=True)
    n_sel = -(-(past_len + 1) // SEL_BLOCK)
    n_pick = min(SEL_TOPN, n_sel) - 1
    q4 = q_s.reshape(n_s, NSA_KV_HEADS, NSA_GROUP, HEAD_DIM)
    eye = jnp.eye(NSA_KV_HEADS, dtype=bool)[None, :, None, :, None]
    qbd = jnp.where(eye, q4[:, :, :, None, :], jnp.zeros((), BF16)).reshape(n_s, 16, 256)
    gm = (jnp.arange(8)[:, None] == jnp.arange(16)[None, :] // NSA_GROUP).astype(BF16)
    n_blk_pad = -(-n_sel // LANES) * LANES
    oc16, idx = nsa_s_sel(qbd, kc_s, vc_s, _overlap(n_pages * 8, n_blk_pad), gm, pos=past_len, n_pick=n_pick)
    oc5 = oc16.reshape(n_s, NSA_KV_HEADS, NSA_GROUP, NSA_KV_HEADS, HEAD_DIM)
    o_c = jnp.sum(jnp.where(eye, oc5, 0.0), axis=3).reshape(n_s, d)
    ix_flat = idx[:, :NSA_KV_HEADS, :n_pick].reshape(-1)
    q8 = jnp.pad(q4, ((0, 0), (0, 0), (0, 8 - NSA_GROUP), (0, 0)))
    win_fm = jnp.transpose(cache_win, (0, 2, 3, 4, 1)).reshape(n_s, 512, cache_win.shape[1])
    os_p, ow_p = nsa_s_attn(pt_flat, ix_flat, q8, cache_fm, kva_s.reshape(n_s, 16, 1, HEAD_DIM), win_fm,
                            n_pages=n_pages, n_pick=n_pick)
    o_s = os_p[:, :, :NSA_GROUP].reshape(n_s, d)
    o_w = ow_p[:, :, :NSA_GROUP].reshape(n_s, d)
    g3 = jnp.repeat(gates_s[:, :48].reshape(n_s, 16, 3), HEAD_DIM, axis=1)
    hs = nsa_out_s(o_c, o_s, o_w, g3[:, :, 0], g3[:, :, 1], g3[:, :, 2], w_out_b, hs)
    kv_s = rows_s.reshape(n_s, 1, 4, NSA_KV_HEADS, HEAD_DIM)
    win_new = win_s.reshape(n_s, 1, 2, NSA_KV_HEADS, HEAD_DIM)
    win_all = jnp.concatenate([cache_win, win_new], axis=1)
    win_s_out = win_all[:, win_all.shape[1] - cache_win.shape[1]:]
    return hp, hs, kv_p, kv_s, win_p, win_s_out


def _sconv_layer(hp, hs, g, state, w, *, n_b, t):
    w_in, w_conv, w_out = w
    w_in_b = w_in.astype(BF16)
    w_out_b = w_out.astype(BF16)
    hp, st = sconv_prompt(hp, g, w_in_b, w_conv, w_out_b, n_b=n_b, t=t, tm=256)
    st_p = st[:, 8 - (w_conv.shape[0] - 1):]
    hs, pre = sconv_sample(hs, g, w_in_b, w_conv, w_out_b, state[:, 0], state[:, 1])
    st_s = jnp.concatenate([state[:, 1:], pre[:, None, :]], axis=1)
    return hp, hs, st_p, st_s


def _moba_layer(hp, hs, g, cache_kv, pt_flat, n_pages, w, *, n_b, t, n_s):
    w_qkv, w_out = w
    d = hp.shape[1]
    n_heads = d // HEAD_DIM
    past_len = n_pages * PAGE_SIZE
    w_qkv_b = w_qkv.astype(BF16)
    w_out_b = w_out.astype(BF16)

    cos_p, sin_p = _rope_tables(jnp.arange(t, dtype=jnp.int32))
    cos_pt, sin_pt = cos_p[:, :HEAD_DIM].T, sin_p[:, :HEAD_DIM].T
    qt, rows, kb, vt, km = moba_in_t(hp, g, w_qkv_b[:, d:], w_qkv_b[:, :d].T, w_qkv_b[:, 2 * d:].T,
                                     cos_p, sin_p, cos_pt, sin_pt)
    nblk = t // MOBA_BLOCK
    kmean = jnp.pad(km.reshape(n_b, nblk, d), ((0, 0), (0, -(-nblk // 16) * 16 - nblk), (0, 0)))
    ot = moba_attn(qt, kb, vt, kmean, n_b=n_b, t=t, tq=512)
    hp = mm_res_t(ot, w_out_b.T, hp, tm=512)
    kv_p = rows.reshape(n_b, t, 2, n_heads, HEAD_DIM)

    cos_s, sin_s = _rope_tables(jnp.full((n_s,), past_len, jnp.int32))
    q_s, rows_s, kb_s, vb_s, _ = moba_in(hs, g, w_qkv_b, cos_s, sin_s, tm=n_s)
    cache_fm = jnp.transpose(cache_kv, (0, 2, 3, 4, 1)).reshape(cache_kv.shape[0], 2 * d, PAGE_SIZE)
    n_top = min(MOBA_TOPK, -(-(past_len + 1) // MOBA_BLOCK))
    qh = q_s.reshape(n_s, n_heads, 1, HEAD_DIM)
    eye = jnp.eye(n_heads, dtype=bool)[None, :, :, None]
    qbd = jnp.where(eye, qh, jnp.zeros((), BF16)).reshape(n_s, n_heads, d)
    qbd = jnp.pad(qbd, ((0, 0), (0, LANES - n_heads), (0, 0)))
    idx = moba_s_select(cache_fm, pt_flat, qbd, n_pages=n_pages, n_top=n_top)
    ix_flat = idx[:, :n_heads, :n_top].reshape(-1)
    q8 = jnp.pad(qh, ((0, 0), (0, 0), (0, 7), (0, 0)))
    o_p = moba_s_attn(pt_flat, ix_flat, q8, cache_fm, kb_s.reshape(n_s, n_heads, 1, HEAD_DIM),
                      vb_s.reshape(n_s, n_heads, 1, HEAD_DIM), n_pages=n_pages, n_top=n_top, n_heads=n_heads)
    o_s = o_p[:, :, 0].reshape(n_s, d).astype(BF16)
    hs = mm_res(o_s, w_out_b, hs, tm=n_s)
    kv_s = rows_s.reshape(n_s, 1, 2, n_heads, HEAD_DIM)
    return hp, hs, kv_p, kv_s


def _conf_layer(hp, hs, g, state, w, *, n_b, t):
    w_pw1, w_dw, b_dw, ln_g, ln_b, w_pw2 = w
    d = hp.shape[1]
    r = lambda a: a.reshape(1, d)
    args = (w_pw1.astype(BF16), w_dw, r(b_dw), r(ln_g), r(ln_b), w_pw2.astype(BF16))
    hp, st = conf_prompt(hp, g, *args, n_b=n_b, t=t, tm=256)
    st_p = st[:, st.shape[1] - (w_dw.shape[0] - 1):]
    hs, u = conf_sample(hs, g, *args, jnp.transpose(state, (1, 0, 2)))
    st_s = jnp.concatenate([state[:, 1:], u[:, None, :]], axis=1)
    return hp, hs, st_p, st_s


def kernel(x_prompt, x_sample, cache_nsa_kv, cache_nsa_win, state_sconv, cache_moba_kv, state_conformer,
           page_table, norm_mix, norm_ffn, norm_final, ffn_w_up, ffn_w_down,
           nsa_w_in, nsa_pe_k, nsa_w1_k, nsa_w2_k, nsa_pe_v, nsa_w1_v, nsa_w2_v, nsa_w_out,
           sconv_w_in, sconv_w_conv, sconv_w_out, moba_w_qkv, moba_w_out,
           conf_w_pw1, conf_w_dw, conf_b_dw, conf_ln_g, conf_ln_b, conf_w_pw2):
    n_b, t, d = x_prompt.shape
    n_s = x_sample.shape[0]
    depth = norm_mix.shape[0]
    n_pages = page_table.shape[1]
    pt_flat = page_table.reshape(-1).astype(jnp.int32)
    hp = x_prompt.reshape(n_b * t, d)
    hs = x_sample.reshape(n_s, d)
    outs = {k: [] for k in ("nsa_kv_p", "nsa_kv_s", "nsa_win_p", "nsa_win_s", "sconv_p", "sconv_s",
                            "moba_p", "moba_s", "conf_p", "conf_s")}
    for i in range(depth):
        kind, j = i % 4, i // 4
        g = norm_mix[i].reshape(1, d)
        if kind == 0:
            w = (nsa_w_in[j], nsa_pe_k[j], nsa_w1_k[j], nsa_w2_k[j], nsa_pe_v[j], nsa_w1_v[j], nsa_w2_v[j],
                 nsa_w_out[j])
            hp, hs, kv_p, kv_s, win_p, win_s = _nsa_layer(hp, hs, g, cache_nsa_kv[j], cache_nsa_win[j], pt_flat,
                                                          n_pages, w, n_b=n_b, t=t, n_s=n_s)
            outs["nsa_kv_p"].append(kv_p)
            outs["nsa_kv_s"].append(kv_s)
            outs["nsa_win_p"].append(win_p)
            outs["nsa_win_s"].append(win_s)
        elif kind == 1:
            hp, hs, st_p, st_s = _sconv_layer(hp, hs, g, state_sconv[j],
                                              (sconv_w_in[j], sconv_w_conv[j], sconv_w_out[j]), n_b=n_b, t=t)
            outs["sconv_p"].append(st_p)
            outs["sconv_s"].append(st_s)
        elif kind == 2:
            hp, hs, kv_p, kv_s = _moba_layer(hp, hs, g, cache_moba_kv[j], pt_flat, n_pages,
                                             (moba_w_qkv[j], moba_w_out[j]), n_b=n_b, t=t, n_s=n_s)
            outs["moba_p"].append(kv_p)
            outs["moba_s"].append(kv_s)
        else:
            w = (conf_w_pw1[j], conf_w_dw[j], conf_b_dw[j], conf_ln_g[j], conf_ln_b[j], conf_w_pw2[j])
            hp, hs, st_p, st_s = _conf_layer(hp, hs, g, state_conformer[j], w, n_b=n_b, t=t)
            outs["conf_p"].append(st_p)
            outs["conf_s"].append(st_s)
        gf = norm_ffn[i].reshape(1, d)
        wu = ffn_w_up[i].astype(BF16)
        wd = ffn_w_down[i].astype(BF16)
        final = i == depth - 1
        gfin = norm_final.reshape(1, d)
        hp = ffn(hp, gf, wu, wd, gfin, tm=512, final=final)
        hs = ffn(hs, gf, wu, wd, gfin, tm=n_s, final=final)
    return (hp.reshape(n_b, t, d), hs.reshape(n_s, 1, d),
            jnp.stack(outs["nsa_kv_p"]), jnp.stack(outs["nsa_kv_s"]),
            jnp.stack(outs["nsa_win_p"]), jnp.stack(outs["nsa_win_s"]),
            jnp.stack(outs["sconv_p"]), jnp.stack(outs["sconv_s"]),
            jnp.stack(outs["moba_p"]), jnp.stack(outs["moba_s"]),
            jnp.stack(outs["conf_p"]), jnp.stack(outs["conf_s"]))
```

```python
import functools

import jax
import jax.numpy as jnp
from jax import lax
from jax.experimental import pallas as pl
from jax.experimental.pallas import tpu as pltpu

F32 = jnp.float32
BF16 = jnp.bfloat16

HEAD_DIM = 64
ROPE_THETA = 10000.0
RMS_EPS = 1e-6
LN_EPS = 1e-5
NSA_KV_HEADS = 4
NSA_GROUP = 4
CMP_STRIDE = 16
CMP_BLOCK = 32
SEL_BLOCK = 64
SEL_TOPN = 16
WINDOW = 512
MOBA_BLOCK = 256
MOBA_TOPK = 3
PAGE_SIZE = 128
SCALE = HEAD_DIM ** -0.5

LANES = 128
NEG = -1e30
VMEM_LIMIT = 56 * 1024 * 1024


def _params(*sem):
    return pltpu.CompilerParams(dimension_semantics=sem, vmem_limit_bytes=VMEM_LIMIT)


def _dot(a, b):
    return jnp.dot(a, b, preferred_element_type=F32)


def _dot_nt(a, b):
    return lax.dot_general(a, b, (((1,), (1,)), ((), ())), preferred_element_type=F32)


def _split3(x):
    hi = x.astype(BF16)
    r = x - hi.astype(F32)
    mid = r.astype(BF16)
    lo = (r - mid.astype(F32)).astype(BF16)
    return hi, mid, lo


def _dot3(x, m):
    hi, mid, lo = _split3(x)
    return _dot(hi, m) + _dot(mid, m) + _dot(lo, m)


def _dot3_rhs(m, x):
    hi, mid, lo = _split3(x)
    return _dot(m, hi) + _dot(m, mid) + _dot(m, lo)


def _rms(x, g):
    return x * lax.rsqrt(jnp.mean(x * x, axis=-1, keepdims=True) + RMS_EPS) * g


def _sigmoid(x):
    return 1.0 / (1.0 + jnp.exp(-x))


def _rope(x, cos, sin):
    w = x.shape[-1]
    lane = lax.broadcasted_iota(jnp.int32, x.shape, 1)
    first = (lane % HEAD_DIM) < (HEAD_DIM // 2)
    rot = jnp.where(first, pltpu.roll(x, w - HEAD_DIM // 2, 1), pltpu.roll(x, HEAD_DIM // 2, 1))
    reps = w // LANES
    if reps > 1:
        cos = jnp.concatenate([cos] * reps, axis=1)
        sin = jnp.concatenate([sin] * reps, axis=1)
    return x * cos + rot * sin


def _masked_softmax(s, mask):
    s = jnp.where(mask, s, -jnp.inf)
    m = jnp.max(s, axis=-1, keepdims=True)
    m = jnp.where(m > -jnp.inf, m, 0.0)
    p = jnp.exp(s - m)
    return p / jnp.maximum(jnp.sum(p, axis=-1, keepdims=True), 1e-30)


def _topk_mask(score, k):
    lane = lax.broadcasted_iota(jnp.int32, score.shape, 1).astype(F32)
    sel = jnp.zeros(score.shape, F32)
    for _ in range(k):
        m = jnp.max(score, axis=-1, keepdims=True)
        idx = jnp.min(jnp.where(score == m, lane, 1e9), axis=-1, keepdims=True)
        hit = lane == idx
        sel = jnp.where(hit & (m > -jnp.inf), 1.0, sel)
        score = jnp.where(hit, -jnp.inf, score)
    return sel


def _online_update(carry, s, ok, v):
    m, l, acc = carry
    sm = jnp.where(ok, s, NEG)
    m_new = jnp.maximum(m, jnp.max(sm, axis=-1, keepdims=True))
    alpha = jnp.exp(m - m_new)
    p = jnp.where(ok, jnp.exp(sm - m_new), 0.0)
    l = alpha * l + jnp.sum(p, axis=-1, keepdims=True)
    acc = alpha * acc + _dot(p.astype(BF16), v)
    return m_new, l, acc


def _online_init(rows, width):
    return (jnp.full((rows, 1), NEG, F32), jnp.zeros((rows, 1), F32), jnp.zeros((rows, width), F32))


def _rope_t(x, cos, sin):
    r = x.shape[0]
    row = lax.broadcasted_iota(jnp.int32, x.shape, 0)
    first = (row % HEAD_DIM) < (HEAD_DIM // 2)
    rot = jnp.where(first, pltpu.roll(x, r - HEAD_DIM // 2, 0), pltpu.roll(x, HEAD_DIM // 2, 0))
    reps = r // HEAD_DIM
    if reps > 1:
        cos = jnp.concatenate([cos] * reps, axis=0)
        sin = jnp.concatenate([sin] * reps, axis=0)
    return x * cos + rot * sin


def _topk_mask_t(score, k):
    row = lax.broadcasted_iota(jnp.int32, score.shape, 0).astype(F32)
    sel = jnp.zeros(score.shape, F32)
    for _ in range(k):
        m = jnp.max(score, axis=0, keepdims=True)
        idx = jnp.min(jnp.where(score == m, row, 1e9), axis=0, keepdims=True)
        hit = row == idx
        sel = jnp.where(hit & (m > -jnp.inf), 1.0, sel)
        score = jnp.where(hit, -jnp.inf, score)
    return sel


def _online_init_t(cols):
    return (jnp.full((1, cols), NEG, F32), jnp.zeros((1, cols), F32), jnp.zeros((HEAD_DIM, cols), F32))


def _online_update_t(carry, s, bias, v_t):
    m, l, acc = carry
    sm = s if bias is None else s + bias
    m_new = jnp.maximum(m, jnp.max(sm, axis=0, keepdims=True))
    alpha = jnp.exp(m - m_new)
    p = jnp.exp(sm - m_new)
    l = alpha * l + jnp.sum(p, axis=0, keepdims=True)
    acc = alpha * acc + _dot(v_t, p.astype(BF16))
    return m_new, l, acc


def _pad_pair(q, odd):
    z = jnp.zeros_like(q)
    return jnp.concatenate([z, q] if odd else [q, z], axis=0)


def _ffn_kernel(x_ref, g_ref, wu_ref, wd_ref, gf_ref, o_ref, *, chunk, final):
    x = x_ref[...]
    xn = _rms(x, g_ref[...]).astype(BF16)
    acc = x
    for c in range(0, wu_ref.shape[1], chunk):
        u = _dot(xn, wu_ref[:, c:c + chunk])
        a = jnp.square(jnp.maximum(u, 0.0)).astype(BF16)
        acc = acc + _dot(a, wd_ref[c:c + chunk, :])
    if final:
        acc = _rms(acc, gf_ref[...])
    o_ref[...] = acc


def ffn(x, g, wu, wd, gf, *, tm, final):
    m, d = x.shape
    dff = wu.shape[1]
    row = lambda i: (i, 0)
    const = lambda i: (0, 0)
    return pl.pallas_call(
        functools.partial(_ffn_kernel, chunk=512, final=final),
        grid=(m // tm,),
        in_specs=[pl.BlockSpec((tm, d), row), pl.BlockSpec((1, d), const),
                  pl.BlockSpec((d, dff), const, pipeline_mode=pl.Buffered(1)),
                  pl.BlockSpec((dff, d), const, pipeline_mode=pl.Buffered(1)),
                  pl.BlockSpec((1, d), const)],
        out_specs=pl.BlockSpec((tm, d), row),
        out_shape=jax.ShapeDtypeStruct((m, d), F32),
        compiler_params=_params("parallel"),
        name="ffn",
    )(x, g, wu, wd, gf)


def _mm_res_kernel(x_ref, w_ref, r_ref, o_ref):
    o_ref[...] = r_ref[...] + _dot(x_ref[...], w_ref[...])


def mm_res(x, w, res, *, tm):
    m, k = x.shape
    n = w.shape[1]
    row = lambda i: (i, 0)
    return pl.pallas_call(
        _mm_res_kernel,
        grid=(m // tm,),
        in_specs=[pl.BlockSpec((tm, k), row), pl.BlockSpec((k, n), lambda i: (0, 0)),
                  pl.BlockSpec((tm, n), row)],
        out_specs=pl.BlockSpec((tm, n), row),
        out_shape=jax.ShapeDtypeStruct((m, n), F32),
        compiler_params=_params("parallel"),
        name="mm_res",
    )(x, w, res)


def _mm_res_t_kernel(xt_ref, wt_ref, r_ref, o_ref):
    o_ref[...] = r_ref[...] + _dot(wt_ref[...], xt_ref[...]).T


def mm_res_t(xt, wt, res, *, tm):
    k, m = xt.shape
    n = wt.shape[0]
    row = lambda i: (i, 0)
    return pl.pallas_call(
        _mm_res_t_kernel,
        grid=(m // tm,),
        in_specs=[pl.BlockSpec((k, tm), lambda i: (0, i)), pl.BlockSpec((n, k), lambda i: (0, 0)),
                  pl.BlockSpec((tm, n), row)],
        out_specs=pl.BlockSpec((tm, n), row),
        out_shape=jax.ShapeDtypeStruct((m, n), F32),
        compiler_params=_params("parallel"),
        name="mm_res_t",
    )(xt, wt, res)


def _nsa_in_kernel(x_ref, g_ref, w_ref, cos_ref, sin_ref, q_ref, rows_ref, kva_ref, win_ref, gate_ref):
    xn = _rms(x_ref[...], g_ref[...]).astype(BF16)
    cos = cos_ref[...]
    sin = sin_ref[...]
    q = _rope(_dot(xn, w_ref[:, 0:1024]), cos, sin) * SCALE
    q_ref[...] = q.astype(BF16)
    kv = _dot(xn, w_ref[:, 1024:2048])
    ks = _rope(kv[:, 512:768], cos, sin)
    rows_ref[:, 0:512] = kv[:, 0:512]
    rows_ref[:, 512:768] = ks
    rows_ref[:, 768:1024] = kv[:, 768:1024]
    wkv = _dot(xn, w_ref[:, 2048:2560])
    kw = _rope(wkv[:, 0:256], cos, sin)
    win_ref[:, 0:256] = kw
    win_ref[:, 256:512] = wkv[:, 256:512]
    kva_ref[:, 0:256] = ks.astype(BF16)
    kva_ref[:, 256:512] = kv[:, 768:1024].astype(BF16)
    kva_ref[:, 512:768] = kw.astype(BF16)
    kva_ref[:, 768:1024] = wkv[:, 256:512].astype(BF16)
    gate_ref[...] = _sigmoid(_dot(xn, w_ref[:, 2560:2688]))


def nsa_in(x, g, w, cos, sin, *, tm):
    m, d = x.shape
    nw = w.shape[1]
    nt = cos.shape[0] // tm
    row = lambda i: (i, 0)
    tab = lambda i: (i % nt, 0)
    const = lambda i: (0, 0)
    outs = [(1024, BF16), (1024, F32), (1024, BF16), (512, F32), (LANES, F32)]
    return pl.pallas_call(
        _nsa_in_kernel,
        grid=(m // tm,),
        in_specs=[pl.BlockSpec((tm, d), row), pl.BlockSpec((1, d), const),
                  pl.BlockSpec((d, nw), const, pipeline_mode=pl.Buffered(1)),
                  pl.BlockSpec((tm, LANES), tab), pl.BlockSpec((tm, LANES), tab)],
        out_specs=[pl.BlockSpec((tm, n), row) for n, _ in outs],
        out_shape=[jax.ShapeDtypeStruct((m, n), dt) for n, dt in outs],
        compiler_params=_params("parallel"),
        name="nsa_in",
    )(x, g, w, cos, sin)


def _nsa_in_t_kernel(x_ref, g_ref, w_ref, wqt_ref, wkvt_ref, wvt_ref, wgt_ref, cos_ref, sin_ref, cost_ref, sint_ref,
                     qt_ref, rowst_ref, cmp_ref, kk_ref, vt_ref, win_ref, gt_ref):
    xn = _rms(x_ref[...], g_ref[...]).astype(BF16)
    cos = cos_ref[...]
    sin = sin_ref[...]
    cos_t = cost_ref[...]
    sin_t = sint_ref[...]
    qt_ref[...] = (_rope_t(_dot_nt(wqt_ref[...], xn), cos_t, sin_t) * SCALE).astype(BF16)
    rt = _dot_nt(wkvt_ref[...], xn)
    rowst_ref[0:512, :] = rt[0:512]
    rowst_ref[512:768, :] = _rope_t(rt[512:768], cos_t, sin_t)
    rowst_ref[768:1024, :] = rt[768:1024]
    vt_ref[0:256, :] = rt[768:1024].astype(BF16)
    vt_ref[256:512, :] = _dot_nt(wvt_ref[...], xn).astype(BF16)
    kv = _dot(xn, w_ref[:, 0:768])
    cmp_ref[...] = kv[:, 0:512]
    wkv = _dot(xn, w_ref[:, 768:1280])
    kw = _rope(wkv[:, 0:256], cos, sin)
    win_ref[:, 0:256] = kw
    win_ref[:, 256:512] = wkv[:, 256:512]
    kk_ref[:, 0:256] = _rope(kv[:, 512:768], cos, sin).astype(BF16)
    kk_ref[:, 256:512] = kw.astype(BF16)
    gt_ref[...] = _sigmoid(_dot_nt(wgt_ref[...], xn))


def nsa_in_t(x, g, w_rm, wq_t, wkv_t, wv_t, wg_t, cos, sin, cos_t, sin_t, *, tm, n_b):
    m, d = x.shape
    nt = cos.shape[0] // tm
    row = lambda i: (i, 0)
    col = lambda i: (0, i)
    const = lambda i: (0, 0)
    one = pl.Buffered(1)
    weights = (w_rm, wq_t, wkv_t, wv_t, wg_t)
    return pl.pallas_call(
        _nsa_in_t_kernel,
        grid=(m // tm,),
        in_specs=[pl.BlockSpec((tm, d), row), pl.BlockSpec((1, d), const)]
        + [pl.BlockSpec(w.shape, const, pipeline_mode=one) for w in weights]
        + [pl.BlockSpec((tm, LANES), lambda i: (i % nt, 0)), pl.BlockSpec((tm, LANES), lambda i: (i % nt, 0)),
           pl.BlockSpec((HEAD_DIM, tm), lambda i: (0, i % nt)), pl.BlockSpec((HEAD_DIM, tm), lambda i: (0, i % nt))],
        out_specs=[pl.BlockSpec((1024, tm), col), pl.BlockSpec((None, 1024, tm), lambda i: (i // nt, 0, i % nt)),
                   pl.BlockSpec((tm, 512), row), pl.BlockSpec((tm, 512), row),
                   pl.BlockSpec((None, 512, tm), lambda i: (i, 0, 0)), pl.BlockSpec((tm, 512), row),
                   pl.BlockSpec((LANES, tm), col)],
        out_shape=[jax.ShapeDtypeStruct((1024, m), BF16), jax.ShapeDtypeStruct((n_b, 1024, m // n_b), F32),
                   jax.ShapeDtypeStruct((m, 512), F32), jax.ShapeDtypeStruct((m, 512), BF16),
                   jax.ShapeDtypeStruct((m // tm, 512, tm), BF16), jax.ShapeDtypeStruct((m, 512), F32),
                   jax.ShapeDtypeStruct((LANES, m), F32)],
        compiler_params=_params("parallel"),
        name="nsa_in_t",
    )(x, g, *weights, cos, sin, cos_t, sin_t)


def _compress_kernel(pt_ref, *refs, pps, feature_major):
    del pt_ref
    pages = refs[:pps + 1]
    (pelo_k, pehi_k, w1lo_k, w1hi_k, w2_k, pelo_v, pehi_v, w1lo_v, w1hi_v, w2_v, w2t_v,
     cos_ref, sin_ref, eye_ref, kc_ref, vc_ref, vct_ref, xs_ref) = refs[pps + 1:]
    npc = pps * 8
    rows = (pps + 1) * 8
    low = lax.broadcasted_iota(jnp.int32, (rows, LANES), 1) < HEAD_DIM
    for k, pg in enumerate(pages):
        x = pg[...]
        if feature_major:
            hi, mid, lo = _split3(x)
            eye = eye_ref[...]
            x = _dot_nt(eye, hi) + _dot_nt(eye, mid) + _dot_nt(eye, lo)
        for lb in range(4):
            xs_ref[lb, k * PAGE_SIZE:(k + 1) * PAGE_SIZE, :] = x[:, lb * LANES:(lb + 1) * LANES]
    streams = ((pelo_k, pehi_k, w1lo_k, w1hi_k, w2_k, kc_ref), (pelo_v, pehi_v, w1lo_v, w1hi_v, w2_v, vc_ref))
    for s, (pelo, pehi, w1lo, w1hi, w2, out_ref) in enumerate(streams):
        heads = [[] for _ in range(NSA_KV_HEADS)]
        for lb in range(2):
            for q in range(CMP_STRIDE // 2):
                a = xs_ref[2 * s + lb, pl.ds(2 * q, rows, stride=CMP_STRIDE), :]
                b = xs_ref[2 * s + lb, pl.ds(2 * q + 1, rows, stride=CMP_STRIDE), :]
                heads[2 * lb].append(jnp.where(low, a, pltpu.roll(b, HEAD_DIM, 1)))
                heads[2 * lb + 1].append(jnp.where(low, pltpu.roll(a, HEAD_DIM, 1), b))
        x = jnp.concatenate([jnp.concatenate(hh, axis=1) for hh in heads], axis=0)
        first = _dot((x + pelo[...]).astype(BF16), w1lo[...])
        second = _dot((x + pehi[...]).astype(BF16), w1hi[...])
        outs = []
        outs_t = []
        for h in range(NSA_KV_HEADS):
            pre = first[h * rows:h * rows + npc] + second[h * rows + 1:h * rows + 1 + npc]
            hid = (pre * _sigmoid(pre)).astype(BF16)
            outs.append(_dot(hid, w2[...]))
            if s == 1:
                outs_t.append(_dot_nt(w2t_v[...], hid))
        res = jnp.concatenate(outs, axis=1)
        if s == 0:
            res = _rope(res, cos_ref[...], sin_ref[...])
        else:
            vct_ref[...] = jnp.concatenate(outs_t, axis=0).astype(BF16)
        out_ref[...] = res.astype(BF16)


def nsa_compress(rows3d, pt_flat, n_seq, n_pages, wk, wv, cos_c, sin_c, *, feature_major):
    pps = min(16, n_pages)
    steps = n_pages // pps
    npc = pps * 8
    page_block = (None, 512, PAGE_SIZE) if feature_major else (None, PAGE_SIZE, 512)
    eye = jnp.eye(PAGE_SIZE, dtype=BF16)

    def page_map(k):
        return lambda b, s, pt: (pt[b * n_pages + jnp.minimum(s * pps + k, n_pages - 1)], 0, 0)

    const2 = lambda b, s, pt: (0, 0)
    wspecs = []
    for _ in range(2):
        wspecs += [pl.BlockSpec((1, 1024), const2), pl.BlockSpec((1, 1024), const2),
                   pl.BlockSpec((1024, 256), const2), pl.BlockSpec((1024, 256), const2),
                   pl.BlockSpec((256, HEAD_DIM), const2)]
    wspecs.append(pl.BlockSpec((HEAD_DIM, 256), const2))
    grid_spec = pltpu.PrefetchScalarGridSpec(
        num_scalar_prefetch=1,
        grid=(n_seq, steps),
        in_specs=[pl.BlockSpec(page_block, page_map(k)) for k in range(pps + 1)] + wspecs
        + [pl.BlockSpec((npc, LANES), lambda b, s, pt: (s, 0))] * 2 + [pl.BlockSpec(eye.shape, const2)],
        out_specs=[pl.BlockSpec((None, npc, 256), lambda b, s, pt: (b, s, 0))] * 2
        + [pl.BlockSpec((None, 256, npc), lambda b, s, pt: (b, 0, s))],
        scratch_shapes=[pltpu.VMEM((4, (pps + 1) * PAGE_SIZE, LANES), F32)],
    )
    return pl.pallas_call(
        functools.partial(_compress_kernel, pps=pps, feature_major=feature_major),
        grid_spec=grid_spec,
        out_shape=[jax.ShapeDtypeStruct((n_seq, n_pages * 8, 256), BF16)] * 2
        + [jax.ShapeDtypeStruct((n_seq, 256, n_pages * 8), BF16)],
        compiler_params=_params("parallel", "arbitrary"),
        name="nsa_compress",
    )(pt_flat, *([rows3d] * (pps + 1)), *wk, *wv, cos_c, sin_c, eye)


def _nsa_attn_kernel(qt_ref, gt_ref, kc_ref, vct_ref, kk_ref, vt_ref, ovlt_ref, emt_ref, ot_ref, s0_ref, s1_ref, *,
                     tq, tks, n_top, nblk):
    i = pl.program_id(1)
    c0 = i * tq
    n4 = NSA_GROUP * tq
    pos1 = c0 + lax.broadcasted_iota(jnp.int32, (1, tq), 1)
    pos4 = c0 + lax.broadcasted_iota(jnp.int32, (1, n4), 1) % tq
    blk = lax.broadcasted_iota(jnp.int32, (nblk, tq), 0)
    cur = pos1 // SEL_BLOCK
    heads = range(NSA_KV_HEADS)
    pair = [slice((h // 2) * LANES, (h // 2 + 1) * LANES) for h in heads]
    vrow = [slice(h * HEAD_DIM, (h + 1) * HEAD_DIM) for h in heads]
    qpad = []
    for h in heads:
        q4 = jnp.concatenate(
            [qt_ref[(NSA_GROUP * h + g) * HEAD_DIM:(NSA_GROUP * h + g + 1) * HEAD_DIM, :] for g in range(NSA_GROUP)],
            axis=1)
        qpad.append(_pad_pair(q4, h % 2 == 1))

    o_c, selb = [], []
    for h in heads:
        s = _dot(kc_ref[:, pair[h]], qpad[h])
        cend = lax.broadcasted_iota(jnp.int32, s.shape, 0) * CMP_STRIDE + (CMP_BLOCK - 1)
        s = jnp.where(cend <= pos4, s, -jnp.inf)
        m = jnp.max(s, axis=0, keepdims=True)
        p = jnp.exp(s - jnp.where(m > -jnp.inf, m, 0.0))
        p = p * (1.0 / jnp.maximum(jnp.sum(p, axis=0, keepdims=True), 1e-30))
        o_c.append(_dot(vct_ref[vrow[h], :], p.astype(BF16)))
        psum = p[:, 0:tq] + p[:, tq:2 * tq] + p[:, 2 * tq:3 * tq] + p[:, 3 * tq:4 * tq]
        imp = _dot3_rhs(ovlt_ref[...], psum)
        forced = (blk == 0) | (blk == cur) | (blk == cur - 1)
        imp = jnp.where(forced, jnp.inf, imp)
        imp = jnp.where(blk <= cur, imp, -jnp.inf)
        sel = _topk_mask_t(imp, n_top)
        sel_bias = jnp.where(sel > 0.5, 0.0, NEG).astype(BF16)
        sel_bias = jnp.concatenate([sel_bias] * NSA_GROUP, axis=1)
        fill = jnp.zeros((LANES - nblk, n4), BF16)
        selb.append(jnp.concatenate([qpad[h], sel_bias, fill], axis=0))

    def finish(carry):
        return [acc * (1.0 / jnp.maximum(l, 1e-30)) for _, l, acc in carry]

    init = tuple(_online_init_t(n4) for _ in heads)
    t_last = (c0 + tq - 1) // tks

    n_tiles = vt_ref.shape[0]

    def run(t_first, scores, consume):
        def put(t, s_ref):
            for h, s in enumerate(scores(jnp.minimum(t, n_tiles - 1))):
                s_ref[h] = s

        def body(u, carry):
            t = t_first + 2 * u
            put(t + 1, s1_ref)
            carry = consume(t, s0_ref, carry, False)
            put(t + 2, s0_ref)
            return consume(t + 1, s1_ref, carry, False)

        put(t_first, s0_ref)
        pairs = (t_last - t_first) // 2
        carry = lax.fori_loop(0, pairs, body, init)
        t = t_first + 2 * pairs
        put(t + 1, s1_ref)
        carry = consume(t, s0_ref, carry, True)
        return finish(consume(t + 1, s1_ref, carry, True))

    def sel_scores(t):
        k0 = pl.multiple_of(t * tks, tks)
        e = emt_ref[t]
        return [_dot(jnp.concatenate([kk_ref[pl.ds(k0, tks), pair[h]], e], axis=1), selb[h]) for h in heads]

    def sel_consume(t, s_ref, carry, masked):
        bias = None
        if masked:
            kpos = t * tks + lax.broadcasted_iota(jnp.int32, (tks, n4), 0)
            bias = jnp.where(kpos <= pos4, 0.0, NEG)
        tv = jnp.minimum(t, n_tiles - 1)
        return tuple(_online_update_t(carry[h], s_ref[h], bias, vt_ref[tv, vrow[h], :]) for h in heads)

    o_s = run(0, sel_scores, sel_consume)

    def win_scores(t):
        k0 = pl.multiple_of(t * tks, tks)
        return [_dot(kk_ref[pl.ds(k0, tks), 256 + pair[h].start:256 + pair[h].stop], qpad[h]) for h in heads]

    def win_consume(t, s_ref, carry, masked):
        del masked
        diff = pos4 - (t * tks + lax.broadcasted_iota(jnp.int32, (tks, n4), 0))
        bias = jnp.where((diff >= 0) & (diff <= WINDOW), 0.0, NEG)
        tv = jnp.minimum(t, n_tiles - 1)
        return tuple(_online_update_t(carry[h], s_ref[h], bias,
                                      vt_ref[tv, 256 + vrow[h].start:256 + vrow[h].stop, :]) for h in heads)

    o_w = run(jnp.maximum(c0 - WINDOW, 0) // tks, win_scores, win_consume)

    for h in heads:
        for g in range(NSA_GROUP):
            j = (h * NSA_GROUP + g) * 3
            cs = slice(g * tq, (g + 1) * tq)
            o = (gt_ref[j:j + 1, :] * o_c[h][:, cs] + gt_ref[j + 1:j + 2, :] * o_s[h][:, cs]
                 + gt_ref[j + 2:j + 3, :] * o_w[h][:, cs])
            ot_ref[(NSA_GROUP * h + g) * HEAD_DIM:(NSA_GROUP * h + g + 1) * HEAD_DIM, :] = o.astype(BF16)


def nsa_attn(qt, gt, kc, vct, kk, vt, ovlt, emt, *, n_b, t, tq):
    nq = t // tq
    tks = vt.shape[2]
    npiece = kc.shape[1]
    nblk = ovlt.shape[0]
    n_top = min(SEL_TOPN, t // SEL_BLOCK)
    col = lambda b, i: (0, b * nq + i)
    per_b = lambda b, i: (b, 0, 0)
    return pl.pallas_call(
        functools.partial(_nsa_attn_kernel, tq=tq, tks=tks, n_top=n_top, nblk=nblk),
        grid=(n_b, nq),
        in_specs=[pl.BlockSpec((1024, tq), col), pl.BlockSpec((LANES, tq), col),
                  pl.BlockSpec((None, npiece, 256), per_b), pl.BlockSpec((None, 256, npiece), per_b),
                  pl.BlockSpec((None, t, 512), per_b), pl.BlockSpec((t // tks, 512, tks), per_b),
                  pl.BlockSpec(ovlt.shape, lambda b, i: (0, 0)),
                  pl.BlockSpec(emt.shape, lambda b, i: (0, 0, 0))],
        out_specs=pl.BlockSpec((1024, tq), col),
        out_shape=jax.ShapeDtypeStruct((1024, n_b * t), BF16),
        scratch_shapes=[pltpu.VMEM((NSA_KV_HEADS, tks, NSA_GROUP * tq), F32)] * 2,
        compiler_params=_params("parallel", "arbitrary"),
        name="nsa_attn",
    )(qt, gt, kc, vct, kk.reshape(n_b, t, 512), vt, ovlt, emt)


def _nsa_s_sel_kernel(q_ref, kc_ref, vc_ref, ovl_ref, gm_ref, oc_ref, idx_ref, *, pos, n_pick):
    s = _dot_nt(q_ref[...], kc_ref[...])
    cend = lax.broadcasted_iota(jnp.int32, s.shape, 1) * CMP_STRIDE + (CMP_BLOCK - 1)
    p = _masked_softmax(s, cend <= pos)
    oc_ref[...] = _dot(p.astype(BF16), vc_ref[...])
    psum = _dot3_rhs(gm_ref[...], p)
    imp = _dot3(psum, ovl_ref[...])
    blk = lax.broadcasted_iota(jnp.int32, imp.shape, 1)
    cur = pos // SEL_BLOCK
    imp = jnp.where((blk == 0) | (blk == cur - 1), jnp.inf, imp)
    imp = jnp.where(blk < cur, imp, -jnp.inf)
    lane = blk.astype(F32)
    slot = lax.broadcasted_iota(jnp.int32, (8, LANES), 1)
    picks = jnp.zeros((8, LANES), F32)
    for r in range(n_pick):
        m = jnp.max(imp, axis=-1, keepdims=True)
        idx = jnp.min(jnp.where(imp == m, lane, 1e9), axis=-1, keepdims=True)
        imp = jnp.where(lane == idx, -jnp.inf, imp)
        picks = jnp.where(slot == r, idx, picks)
    idx_ref[...] = picks.astype(jnp.int32)


def nsa_s_sel(qbd, kc, vc, ovl, gm, *, pos, n_pick):
    n_b, _, npiece = kc.shape[0], None, kc.shape[1]
    per_b = lambda b: (b, 0, 0)
    return pl.pallas_call(
        functools.partial(_nsa_s_sel_kernel, pos=pos, n_pick=n_pick),
        grid=(n_b,),
        in_specs=[pl.BlockSpec((None, 16, 256), per_b), pl.BlockSpec((None, npiece, 256), per_b),
                  pl.BlockSpec((None, npiece, 256), per_b),
                  pl.BlockSpec(ovl.shape, lambda b: (0, 0)), pl.BlockSpec(gm.shape, lambda b: (0, 0))],
        out_specs=[pl.BlockSpec((None, 16, 256), per_b), pl.BlockSpec((None, 8, LANES), per_b)],
        out_shape=[jax.ShapeDtypeStruct((n_b, 16, 256), F32), jax.ShapeDtypeStruct((n_b, 8, LANES), jnp.int32)],
        compiler_params=_params("parallel"),
        name="nsa_s_sel",
    )(qbd, kc, vc, ovl, gm)


def _attend_with_new(q, kts, vts, biases, k_new, v_new):
    ss = []
    for kt, bias in zip(kts, biases):
        s = _dot(q, kt.astype(BF16))
        ss.append(s if bias is None else s + bias)
    s = ss[0] if len(ss) == 1 else jnp.concatenate(ss, axis=1)
    s_new = jnp.sum(q.astype(F32) * k_new.astype(F32), axis=-1, keepdims=True)
    m = jnp.maximum(jnp.max(s, axis=-1, keepdims=True), s_new)
    p = jnp.exp(s - m)
    p_new = jnp.exp(s_new - m)
    l = jnp.sum(p, axis=-1, keepdims=True) + p_new
    acc = p_new * v_new.astype(F32)
    off = 0
    for vt in vts:
        n = vt.shape[1]
        acc = acc + _dot_nt(p[:, off:off + n].astype(BF16), vt.astype(BF16))
        off += n
    return acc / l


def _nsa_s_attn_kernel(pt_ref, ix_ref, q_ref, *refs, n_pick):
    del pt_ref
    kblk = refs[:n_pick]
    vblk = refs[n_pick:2 * n_pick]
    ks_new, vs_new, kw_new, vw_new, kw_ref, vw_ref, os_ref, ow_ref = refs[2 * n_pick:]
    base = (pl.program_id(0) * NSA_KV_HEADS + pl.program_id(1)) * n_pick
    half = lax.broadcasted_iota(jnp.int32, (1, PAGE_SIZE), 1) // SEL_BLOCK
    biases = [jnp.where(half == ix_ref[base + r] % 2, 0.0, NEG) for r in range(n_pick)]
    q = q_ref[...]
    os_ref[...] = _attend_with_new(q, [r[...] for r in kblk], [r[...] for r in vblk], biases, ks_new[...], vs_new[...])
    ow_ref[...] = _attend_with_new(q, [kw_ref[...]], [vw_ref[...]], [None], kw_new[...], vw_new[...])


def nsa_s_attn(pt_flat, ix_flat, q4, cache_fm, new_rows, win_fm, *, n_pages, n_pick):
    n_b = q4.shape[0]
    nwin = win_fm.shape[2]
    per_page = PAGE_SIZE // SEL_BLOCK

    def kv_map(r, stream):
        def f(b, h, pt, ix):
            j = ix[(b * NSA_KV_HEADS + h) * n_pick + r]
            return (pt[b * n_pages + j // per_page], stream * NSA_KV_HEADS + h, 0)
        return f

    new_map = lambda s: (lambda b, h, pt, ix: (b, s * NSA_KV_HEADS + h, 0, 0))
    qo_spec = pl.BlockSpec((None, None, 8, HEAD_DIM), lambda b, h, pt, ix: (b, h, 0, 0))
    grid_spec = pltpu.PrefetchScalarGridSpec(
        num_scalar_prefetch=2,
        grid=(n_b, NSA_KV_HEADS),
        in_specs=[qo_spec]
        + [pl.BlockSpec((None, HEAD_DIM, PAGE_SIZE), kv_map(r, 2)) for r in range(n_pick)]
        + [pl.BlockSpec((None, HEAD_DIM, PAGE_SIZE), kv_map(r, 3)) for r in range(n_pick)]
        + [pl.BlockSpec((None, None, 1, HEAD_DIM), new_map(s)) for s in range(4)]
        + [pl.BlockSpec((None, HEAD_DIM, nwin), lambda b, h, pt, ix: (b, h, 0)),
           pl.BlockSpec((None, HEAD_DIM, nwin), lambda b, h, pt, ix: (b, NSA_KV_HEADS + h, 0))],
        out_specs=[qo_spec, qo_spec],
    )
    return pl.pallas_call(
        functools.partial(_nsa_s_attn_kernel, n_pick=n_pick),
        grid_spec=grid_spec,
        out_shape=[jax.ShapeDtypeStruct((n_b, NSA_KV_HEADS, 8, HEAD_DIM), F32)] * 2,
        compiler_params=_params("parallel", "arbitrary"),
        name="nsa_s_attn",
    )(pt_flat, ix_flat, q4, *([cache_fm] * (2 * n_pick)), *([new_rows] * 4), win_fm, win_fm)


def _nsa_out_s_kernel(oc_ref, os_ref, ow_ref, g0_ref, g1_ref, g2_ref, w_ref, r_ref, o_ref):
    o = g0_ref[...] * oc_ref[...] + g1_ref[...] * os_ref[...] + g2_ref[...] * ow_ref[...]
    o_ref[...] = r_ref[...] + _dot(o.astype(BF16), w_ref[...])


def nsa_out_s(oc, osel, ow, g0, g1, g2, w, res):
    m, d = res.shape
    full = pl.BlockSpec((m, d), lambda i: (0, 0))
    return pl.pallas_call(
        _nsa_out_s_kernel,
        grid=(1,),
        in_specs=[full] * 6 + [pl.BlockSpec(w.shape, lambda i: (0, 0)), full],
        out_specs=full,
        out_shape=jax.ShapeDtypeStruct((m, d), F32),
        compiler_params=_params("arbitrary"),
        name="nsa_out_s",
    )(oc, osel, ow, g0, g1, g2, w, res)


def _sconv_kernel(x_ref, g_ref, win_ref, wc_ref, wout_ref, o_ref, st_ref, carry_ref):
    d = x_ref.shape[1]
    tm = x_ref.shape[0]

    @pl.when(pl.program_id(1) == 0)
    def _():
        carry_ref[...] = jnp.zeros(carry_ref.shape, F32)

    x = x_ref[...]
    xn = _rms(x, g_ref[...]).astype(BF16)
    b_gate = _dot(xn, win_ref[:, 0:d])
    pre = _dot(xn, win_ref[:, d:2 * d]) * _dot(xn, win_ref[:, 2 * d:3 * d])
    row = lax.broadcasted_iota(jnp.int32, (tm, d), 0)
    back1 = jnp.where(row == 0, carry_ref[7:8, :], pltpu.roll(pre, 1, 0))
    back2 = jnp.where(row == 0, carry_ref[6:7, :], jnp.where(row == 1, carry_ref[7:8, :], pltpu.roll(pre, 2, 0)))
    y = back2 * wc_ref[0:1, :] + back1 * wc_ref[1:2, :] + pre * wc_ref[2:3, :]
    tail = pre[tm - 8:tm]
    carry_ref[...] = tail
    st_ref[...] = tail
    o_ref[...] = x + _dot((b_gate * y).astype(BF16), wout_ref[...])


def sconv_prompt(x, g, w_in, w_conv, w_out, *, n_b, t, tm):
    d = x.shape[1]
    nt = t // tm
    row = lambda b, i: (b * nt + i, 0)
    const = lambda b, i: (0, 0)
    return pl.pallas_call(
        _sconv_kernel,
        grid=(n_b, nt),
        in_specs=[pl.BlockSpec((tm, d), row), pl.BlockSpec((1, d), const),
                  pl.BlockSpec((d, 3 * d), const, pipeline_mode=pl.Buffered(1)),
                  pl.BlockSpec(w_conv.shape, const),
                  pl.BlockSpec((d, d), const, pipeline_mode=pl.Buffered(1))],
        out_specs=[pl.BlockSpec((tm, d), row), pl.BlockSpec((None, 8, d), lambda b, i: (b, 0, 0))],
        out_shape=[jax.ShapeDtypeStruct((n_b * t, d), F32), jax.ShapeDtypeStruct((n_b, 8, d), F32)],
        scratch_shapes=[pltpu.VMEM((8, d), F32)],
        compiler_params=_params("parallel", "arbitrary"),
        name="sconv_prompt",
    )(x, g, w_in, w_conv, w_out)


def _sconv_s_kernel(x_ref, g_ref, win_ref, wc_ref, wout_ref, p0_ref, p1_ref, o_ref, pre_ref):
    d = x_ref.shape[1]
    x = x_ref[...]
    xn = _rms(x, g_ref[...]).astype(BF16)
    b_gate = _dot(xn, win_ref[:, 0:d])
    pre = _dot(xn, win_ref[:, d:2 * d]) * _dot(xn, win_ref[:, 2 * d:3 * d])
    y = p0_ref[...] * wc_ref[0:1, :] + p1_ref[...] * wc_ref[1:2, :] + pre * wc_ref[2:3, :]
    pre_ref[...] = pre
    o_ref[...] = x + _dot((b_gate * y).astype(BF16), wout_ref[...])


def sconv_sample(x, g, w_in, w_conv, w_out, past0, past1):
    m, d = x.shape
    full = lambda a: pl.BlockSpec(a.shape, lambda i: (0,) * a.ndim)
    args = (x, g, w_in, w_conv, w_out, past0, past1)
    return pl.pallas_call(
        _sconv_s_kernel,
        grid=(1,),
        in_specs=[full(a) for a in args],
        out_specs=[pl.BlockSpec((m, d), lambda i: (0, 0))] * 2,
        out_shape=[jax.ShapeDtypeStruct((m, d), F32)] * 2,
        compiler_params=_params("arbitrary"),
        name="sconv_sample",
    )(*args)


def _moba_in_kernel(x_ref, g_ref, w_ref, cos_ref, sin_ref, q_ref, rows_ref, kb_ref, vb_ref, km_ref):
    d = x_ref.shape[1]
    xn = _rms(x_ref[...], g_ref[...]).astype(BF16)
    cos = cos_ref[...]
    sin = sin_ref[...]
    q_ref[...] = (_rope(_dot(xn, w_ref[:, 0:d]), cos, sin) * SCALE).astype(BF16)
    k = _rope(_dot(xn, w_ref[:, d:2 * d]), cos, sin)
    v = _dot(xn, w_ref[:, 2 * d:3 * d])
    rows_ref[:, 0:d] = k
    rows_ref[:, d:2 * d] = v
    kb_ref[...] = k.astype(BF16)
    vb_ref[...] = v.astype(BF16)
    km_ref[...] = jnp.sum(k, axis=0, keepdims=True) * (1.0 / MOBA_BLOCK)


def moba_in(x, g, w, cos, sin, *, tm):
    m, d = x.shape
    nt = cos.shape[0] // tm
    row = lambda i: (i, 0)
    tab = lambda i: (i % nt, 0)
    const = lambda i: (0, 0)
    return pl.pallas_call(
        _moba_in_kernel,
        grid=(m // tm,),
        in_specs=[pl.BlockSpec((tm, d), row), pl.BlockSpec((1, d), const),
                  pl.BlockSpec((d, 3 * d), const, pipeline_mode=pl.Buffered(1)),
                  pl.BlockSpec((tm, LANES), tab), pl.BlockSpec((tm, LANES), tab)],
        out_specs=[pl.BlockSpec((tm, d), row), pl.BlockSpec((tm, 2 * d), row), pl.BlockSpec((tm, d), row),
                   pl.BlockSpec((tm, d), row), pl.BlockSpec((None, 1, d), lambda i: (i, 0, 0))],
        out_shape=[jax.ShapeDtypeStruct((m, d), BF16), jax.ShapeDtypeStruct((m, 2 * d), F32),
                   jax.ShapeDtypeStruct((m, d), BF16), jax.ShapeDtypeStruct((m, d), BF16),
                   jax.ShapeDtypeStruct((m // tm, 1, d), F32)],
        compiler_params=_params("parallel"),
        name="moba_in",
    )(x, g, w, cos, sin)


def _moba_in_t_kernel(x_ref, g_ref, wk_ref, wqkt_ref, wvt_ref, cos_ref, sin_ref, cost_ref, sint_ref,
                      qt_ref, rowst_ref, kb_ref, vt_ref, km_ref):
    d = x_ref.shape[1]
    xn = _rms(x_ref[...], g_ref[...]).astype(BF16)
    qk = _rope_t(_dot_nt(wqkt_ref[...], xn), cost_ref[...], sint_ref[...])
    qt_ref[...] = (qk[0:d] * SCALE).astype(BF16)
    v_t = _dot_nt(wvt_ref[...], xn)
    rowst_ref[0:d, :] = qk[d:2 * d]
    rowst_ref[d:2 * d, :] = v_t
    vt_ref[...] = v_t.astype(BF16)
    k = _rope(_dot(xn, wk_ref[...]), cos_ref[...], sin_ref[...])
    kb_ref[...] = k.astype(BF16)
    km_ref[...] = jnp.sum(k, axis=0, keepdims=True) * (1.0 / MOBA_BLOCK)


def moba_in_t(x, g, w_k, wqk_t, wv_t, cos, sin, cos_t, sin_t, *, n_b):
    m, d = x.shape
    tm = MOBA_BLOCK
    nt = cos.shape[0] // tm
    row = lambda i: (i, 0)
    const = lambda i: (0, 0)
    one = pl.Buffered(1)
    return pl.pallas_call(
        _moba_in_t_kernel,
        grid=(m // tm,),
        in_specs=[pl.BlockSpec((tm, d), row), pl.BlockSpec((1, d), const),
                  pl.BlockSpec(w_k.shape, const, pipeline_mode=one), pl.BlockSpec(wqk_t.shape, const, pipeline_mode=one),
                  pl.BlockSpec(wv_t.shape, const, pipeline_mode=one),
                  pl.BlockSpec((tm, LANES), lambda i: (i % nt, 0)), pl.BlockSpec((tm, LANES), lambda i: (i % nt, 0)),
                  pl.BlockSpec((HEAD_DIM, tm), lambda i: (0, i % nt)), pl.BlockSpec((HEAD_DIM, tm), lambda i: (0, i % nt))],
        out_specs=[pl.BlockSpec((d, tm), lambda i: (0, i)),
                   pl.BlockSpec((None, 2 * d, tm), lambda i: (i // nt, 0, i % nt)), pl.BlockSpec((tm, d), row),
                   pl.BlockSpec((None, d, tm), lambda i: (i, 0, 0)), pl.BlockSpec((None, 1, d), lambda i: (i, 0, 0))],
        out_shape=[jax.ShapeDtypeStruct((d, m), BF16), jax.ShapeDtypeStruct((n_b, 2 * d, m // n_b), F32),
                   jax.ShapeDtypeStruct((m, d), BF16), jax.ShapeDtypeStruct((m // tm, d, tm), BF16),
                   jax.ShapeDtypeStruct((m // tm, 1, d), F32)],
        compiler_params=_params("parallel"),
        name="moba_in_t",
    )(x, g, w_k, wqk_t, wv_t, cos, sin, cos_t, sin_t)


def _moba_attn_kernel(qt_ref, k_ref, vt_ref, km_ref, ot_ref, s0_ref, s1_ref, *, tq, n_top):
    i = pl.program_id(2)
    c0 = i * tq
    nblk = km_ref.shape[0]
    pos = c0 + lax.broadcasted_iota(jnp.int32, (1, tq), 1)
    cur = pos // MOBA_BLOCK
    blk = lax.broadcasted_iota(jnp.int32, (nblk, tq), 0)
    m1, m2, m3 = _split3(km_ref[...])
    vrow = [slice(hh * HEAD_DIM, (hh + 1) * HEAD_DIM) for hh in range(2)]
    qpad, allow = [], []
    for hh in range(2):
        qp = _pad_pair(qt_ref[vrow[hh], :], hh == 1)
        gate = _dot(m1, qp) + _dot(m2, qp) + _dot(m3, qp)
        gate = jnp.where(blk < cur, gate, -jnp.inf)
        qpad.append(qp)
        allow.append(jnp.where(blk == cur, 1.0, _topk_mask_t(gate, n_top)))

    def put(t, s_ref):
        k = k_ref[pl.ds(pl.multiple_of(t * MOBA_BLOCK, MOBA_BLOCK), MOBA_BLOCK), :]
        for hh in range(2):
            s_ref[hh] = _dot(k, qpad[hh])

    def consume(t, s_ref, carry, causal):
        new = []
        for hh in range(2):
            chosen = jnp.sum(jnp.where(blk == t, allow[hh], 0.0), axis=0, keepdims=True)
            bias = jnp.where(chosen > 0.5, 0.0, NEG)
            if causal:
                kpos = t * MOBA_BLOCK + lax.broadcasted_iota(jnp.int32, (MOBA_BLOCK, tq), 0)
                bias = jnp.where(kpos <= pos, bias, NEG)
            new.append(_online_update_t(carry[hh], s_ref[hh], bias, vt_ref[t, vrow[hh], :]))
        return tuple(new)

    def body(u, carry):
        t = 2 * u
        put(t + 1, s1_ref)
        carry = consume(t, s0_ref, carry, False)
        put(t + 2, s0_ref)
        return consume(t + 1, s1_ref, carry, False)

    t_diag = c0 // MOBA_BLOCK
    put(0, s0_ref)
    res = lax.fori_loop(0, t_diag // 2, body, (_online_init_t(tq), _online_init_t(tq)))
    put(t_diag + 1, s1_ref)
    res = consume(t_diag, s0_ref, res, True)
    res = consume(t_diag + 1, s1_ref, res, True)
    for hh in range(2):
        _, l, acc = res[hh]
        ot_ref[vrow[hh], :] = (acc * (1.0 / jnp.maximum(l, 1e-30))).astype(BF16)


def moba_attn(qt, kb, vt, kmean, *, n_b, t, tq):
    assert tq == 2 * MOBA_BLOCK, "the kernel's tile pairing assumes two MoBA blocks per query tile"
    nq = t // tq
    d = qt.shape[0]
    nblk = kmean.shape[1]
    n_top = min(MOBA_TOPK, t // MOBA_BLOCK)
    qmap = lambda b, hp, i: (hp, b * nq + i)
    kvmap = lambda b, hp, i: (b, 0, hp)
    return pl.pallas_call(
        functools.partial(_moba_attn_kernel, tq=tq, n_top=n_top),
        grid=(n_b, d // LANES, nq),
        in_specs=[pl.BlockSpec((LANES, tq), qmap), pl.BlockSpec((None, t, LANES), kvmap),
                  pl.BlockSpec((t // MOBA_BLOCK, LANES, MOBA_BLOCK), lambda b, hp, i: (b, hp, 0)),
                  pl.BlockSpec((None, nblk, LANES), kvmap)],
        out_specs=pl.BlockSpec((LANES, tq), qmap),
        out_shape=jax.ShapeDtypeStruct((d, n_b * t), BF16),
        scratch_shapes=[pltpu.VMEM((2, MOBA_BLOCK, tq), F32)] * 2,
        compiler_params=_params("parallel", "parallel", "arbitrary"),
        name="moba_attn",
    )(qt, kb.reshape(n_b, t, d), vt, kmean)


def _moba_s_select_kernel(pt_ref, *refs, pps, n_top, n_blocks):
    del pt_ref
    pages = refs[:pps]
    qbd_ref, idx_ref, km_ref = refs[pps:]
    s = pl.program_id(1)
    per_blk = MOBA_BLOCK // PAGE_SIZE
    blocks_per_step = pps // per_blk

    @pl.when(s == 0)
    def _():
        km_ref[...] = jnp.zeros(km_ref.shape, F32)

    lane = lax.broadcasted_iota(jnp.int32, km_ref.shape, 1)
    km = km_ref[...]
    for j in range(blocks_per_step):
        tot = pages[per_blk * j][...]
        for e in range(1, per_blk):
            tot = tot + pages[per_blk * j + e][...]
        mean = jnp.sum(tot, axis=1, keepdims=True) * (1.0 / MOBA_BLOCK)
        km = jnp.where(lane == s * blocks_per_step + j, mean, km)
    km_ref[...] = km

    @pl.when(s == pl.num_programs(1) - 1)
    def _():
        gate = _dot3_rhs(qbd_ref[...], km_ref[...])
        blk = lax.broadcasted_iota(jnp.int32, gate.shape, 1)
        gate = jnp.where(blk < n_blocks, gate, -jnp.inf)
        lane_f = blk.astype(F32)
        picks = jnp.zeros(gate.shape, F32)
        for r in range(n_top):
            m = jnp.max(gate, axis=-1, keepdims=True)
            idx = jnp.min(jnp.where(gate == m, lane_f, 1e9), axis=-1, keepdims=True)
            gate = jnp.where(lane_f == idx, -jnp.inf, gate)
            picks = jnp.where(blk == r, idx, picks)
        idx_ref[...] = picks.astype(jnp.int32)


def moba_s_select(cache_fm, pt_flat, qbd, *, n_pages, n_top):
    n_b, _, d = qbd.shape
    pps = min(16, n_pages)
    per_blk = MOBA_BLOCK // PAGE_SIZE
    page_map = lambda k: (lambda b, s, pt: (pt[b * n_pages + s * pps + k], 0, 0))
    grid_spec = pltpu.PrefetchScalarGridSpec(
        num_scalar_prefetch=1,
        grid=(n_b, n_pages // pps),
        in_specs=[pl.BlockSpec((None, d, PAGE_SIZE), page_map(k)) for k in range(pps)]
        + [pl.BlockSpec((None, LANES, d), lambda b, s, pt: (b, 0, 0))],
        out_specs=pl.BlockSpec((None, LANES, LANES), lambda b, s, pt: (b, 0, 0)),
        scratch_shapes=[pltpu.VMEM((d, LANES), F32)],
    )
    return pl.pallas_call(
        functools.partial(_moba_s_select_kernel, pps=pps, n_top=n_top, n_blocks=n_pages // per_blk),
        grid_spec=grid_spec,
        out_shape=jax.ShapeDtypeStruct((n_b, LANES, LANES), jnp.int32),
        compiler_params=_params("parallel", "arbitrary"),
        name="moba_s_select",
    )(pt_flat, *([cache_fm] * pps), qbd)


def _moba_s_attn_kernel(pt_ref, ix_ref, q_ref, *refs, n_pg):
    del pt_ref, ix_ref
    kblk = refs[:n_pg]
    vblk = refs[n_pg:2 * n_pg]
    k_new, v_new, o_ref = refs[2 * n_pg:]
    o_ref[...] = _attend_with_new(q_ref[...], [r[...] for r in kblk], [r[...] for r in vblk], [None] * n_pg,
                                  k_new[...], v_new[...])


def moba_s_attn(pt_flat, ix_flat, q4, cache_fm, k_new, v_new, *, n_pages, n_top, n_heads):
    n_b = q4.shape[0]
    per_blk = MOBA_BLOCK // PAGE_SIZE
    n_pg = n_top * per_blk

    def kv_map(r, row0):
        def f(b, h, pt, ix):
            j = ix[(b * n_heads + h) * n_top + r // per_blk]
            return (pt[b * n_pages + j * per_blk + r % per_blk], row0 + h, 0)
        return f

    new_spec = pl.BlockSpec((None, None, 1, HEAD_DIM), lambda b, h, pt, ix: (b, h, 0, 0))
    qo_spec = pl.BlockSpec((None, None, 8, HEAD_DIM), lambda b, h, pt, ix: (b, h, 0, 0))
    grid_spec = pltpu.PrefetchScalarGridSpec(
        num_scalar_prefetch=2,
        grid=(n_b, n_heads),
        in_specs=[qo_spec]
        + [pl.BlockSpec((None, HEAD_DIM, PAGE_SIZE), kv_map(r, 0)) for r in range(n_pg)]
        + [pl.BlockSpec((None, HEAD_DIM, PAGE_SIZE), kv_map(r, n_heads)) for r in range(n_pg)]
        + [new_spec, new_spec],
        out_specs=qo_spec,
    )
    return pl.pallas_call(
        functools.partial(_moba_s_attn_kernel, n_pg=n_pg),
        grid_spec=grid_spec,
        out_shape=jax.ShapeDtypeStruct((n_b, n_heads, 8, HEAD_DIM), F32),
        compiler_params=_params("parallel", "arbitrary"),
        name="moba_s_attn",
    )(pt_flat, ix_flat, q4, *([cache_fm] * (2 * n_pg)), k_new, v_new)


def _layer_norm_silu(y, g, b):
    yc = y - jnp.mean(y, axis=-1, keepdims=True)
    yn = yc * lax.rsqrt(jnp.mean(yc * yc, axis=-1, keepdims=True) + LN_EPS) * g + b
    return yn * _sigmoid(yn)


def _conf_kernel(x_ref, g_ref, w1_ref, wdw_ref, bdw_ref, lg_ref, lb_ref, w2_ref, o_ref, st_ref, ubuf_ref, sh_ref,
                 *, hist):
    tm, d = x_ref.shape
    width = wdw_ref.shape[0]

    @pl.when(pl.program_id(1) == 0)
    def _():
        ubuf_ref[0:hist, :] = jnp.zeros((hist, d), F32)

    x = x_ref[...]
    xn = _rms(x, g_ref[...]).astype(BF16)
    u = _dot(xn, w1_ref[:, 0:d]) * _sigmoid(_dot(xn, w1_ref[:, d:2 * d]))
    ubuf_ref[hist:hist + tm, :] = u
    base = hist - (width - 1)
    rows = ubuf_ref.shape[0]
    y = bdw_ref[...]
    for r in range(8):
        taps = [k for k in range(width) if (base + k) % 8 == r]
        if not taps:
            continue
        src = ubuf_ref
        if r:
            sh_ref[...] = pltpu.roll(ubuf_ref[...], rows - r, 0)
            src = sh_ref
        for k in taps:
            y = y + src[base + k - r:base + k - r + tm, :] * wdw_ref[k:k + 1, :]
    z = _layer_norm_silu(y, lg_ref[...], lb_ref[...])
    o_ref[...] = x + _dot(z.astype(BF16), w2_ref[...])
    tail = ubuf_ref[tm:tm + hist, :]
    st_ref[...] = tail
    ubuf_ref[0:hist, :] = tail


def conf_prompt(x, g, w1, wdw, bdw, lg, lb, w2, *, n_b, t, tm):
    d = x.shape[1]
    nt = t // tm
    hist = 32
    row = lambda b, i: (b * nt + i, 0)
    const = lambda b, i: (0, 0)
    return pl.pallas_call(
        functools.partial(_conf_kernel, hist=hist),
        grid=(n_b, nt),
        in_specs=[pl.BlockSpec((tm, d), row), pl.BlockSpec((1, d), const),
                  pl.BlockSpec((d, 2 * d), const, pipeline_mode=pl.Buffered(1)),
                  pl.BlockSpec(wdw.shape, const), pl.BlockSpec((1, d), const), pl.BlockSpec((1, d), const),
                  pl.BlockSpec((1, d), const), pl.BlockSpec((d, d), const, pipeline_mode=pl.Buffered(1))],
        out_specs=[pl.BlockSpec((tm, d), row), pl.BlockSpec((None, hist, d), lambda b, i: (b, 0, 0))],
        out_shape=[jax.ShapeDtypeStruct((n_b * t, d), F32), jax.ShapeDtypeStruct((n_b, hist, d), F32)],
        scratch_shapes=[pltpu.VMEM((hist + tm, d), F32)] * 2,
        compiler_params=_params("parallel", "arbitrary"),
        name="conf_prompt",
    )(x, g, w1, wdw, bdw, lg, lb, w2)


def _conf_s_kernel(x_ref, g_ref, w1_ref, wdw_ref, bdw_ref, lg_ref, lb_ref, w2_ref, past_ref, o_ref, u_ref):
    d = x_ref.shape[1]
    width = wdw_ref.shape[0]
    x = x_ref[...]
    xn = _rms(x, g_ref[...]).astype(BF16)
    u = _dot(xn, w1_ref[:, 0:d]) * _sigmoid(_dot(xn, w1_ref[:, d:2 * d]))
    y = bdw_ref[...] + past_ref[0] * wdw_ref[0:1, :]
    for k in range(1, width - 1):
        y = y + past_ref[k] * wdw_ref[k:k + 1, :]
    y = y + u * wdw_ref[width - 1:width, :]
    z = _layer_norm_silu(y, lg_ref[...], lb_ref[...])
    u_ref[...] = u
    o_ref[...] = x + _dot(z.astype(BF16), w2_ref[...])


def conf_sample(x, g, w1, wdw, bdw, lg, lb, w2, past_t):
    m, d = x.shape
    full = lambda a: pl.BlockSpec(a.shape, lambda i: (0,) * a.ndim)
    args = (x, g, w1, wdw, bdw, lg, lb, w2, past_t)
    return pl.pallas_call(
        _conf_s_kernel,
        grid=(1,),
        in_specs=[full(a) for a in args],
        out_specs=[pl.BlockSpec((m, d), lambda i: (0, 0))] * 2,
        out_shape=[jax.ShapeDtypeStruct((m, d), F32)] * 2,
        compiler_params=_params("arbitrary"),
        name="conf_sample",
    )(*args)


def _rope_tables(pos):
    half = HEAD_DIM // 2
    inv_freq = ROPE_THETA ** (-jnp.arange(half, dtype=F32) / half)
    ang = pos.astype(F32)[:, None] * inv_freq[None, :]
    cos = jnp.cos(ang)
    sin = jnp.sin(ang)
    cos = jnp.concatenate([cos, cos], axis=-1)
    sin = jnp.concatenate([-sin, sin], axis=-1)
    return jnp.tile(cos, (1, LANES // HEAD_DIM)), jnp.tile(sin, (1, LANES // HEAD_DIM))


def _overlap(n_cmp_rows, n_cols):
    i = jnp.arange(n_cmp_rows, dtype=jnp.int32)[:, None]
    j = jnp.arange(n_cols, dtype=jnp.int32)[None, :]
    start = i * CMP_STRIDE
    hit = (start <= j * SEL_BLOCK + (SEL_BLOCK - 1)) & (start + (CMP_BLOCK - 1) >= j * SEL_BLOCK)
    return hit.astype(BF16)


def _pad_heads_to_lane_pairs(x, rows):
    n, nh, r, hd = x.shape
    z = jnp.zeros_like(x)
    even = jnp.concatenate([x, z], axis=-1)
    odd = jnp.concatenate([z, x], axis=-1)
    is_even = (jnp.arange(nh) % 2 == 0)[None, :, None, None]
    out = jnp.where(is_even, even, odd)
    return jnp.pad(out, ((0, 0), (0, 0), (0, rows - r), (0, 0)))


def _take_lane_half(x, r):
    nh = x.shape[1]
    is_even = (jnp.arange(nh) % 2 == 0)[None, :, None, None]
    return jnp.where(is_even, x[:, :, :r, :HEAD_DIM], x[:, :, :r, HEAD_DIM:])


def _nsa_layer(hp, hs, g, cache_kv, cache_win, pt_flat, n_pages, w, *, n_b, t, n_s):
    w_in, pe_k, w1_k, w2_k, pe_v, w1_v, w2_v, w_out = w
    d = hp.shape[1]
    past_len = n_pages * PAGE_SIZE
    w_in_p = jnp.pad(w_in, ((0, 0), (0, 2688 - w_in.shape[1]))).astype(BF16)
    w_out_b = w_out.astype(BF16)
    half = CMP_BLOCK * HEAD_DIM // 2

    def cmp_weights(pe, w1, w2):
        return (pe[:CMP_STRIDE].reshape(1, half), pe[CMP_STRIDE:].reshape(1, half),
                w1[:half].astype(BF16), w1[half:].astype(BF16), w2.astype(BF16))

    wk = cmp_weights(pe_k, w1_k, w2_k)
    wv = cmp_weights(pe_v, w1_v, w2_v) + (w2_v.T.astype(BF16),)

    cos_p, sin_p = _rope_tables(jnp.arange(t, dtype=jnp.int32))
    cos_pt, sin_pt = cos_p[:, :HEAD_DIM].T, sin_p[:, :HEAD_DIM].T
    kv0, kv1, n_gate = 1024, 2560, NSA_KV_HEADS * NSA_GROUP * 3
    wq_t = w_in[:, :kv0].T.astype(BF16)
    wkv_t = w_in[:, kv0:kv0 + 1024].T.astype(BF16)
    wv_t = w_in[:, kv0 + 1280:kv1].T.astype(BF16)
    wg_t = jnp.pad(w_in[:, kv1:kv1 + n_gate].T, ((0, LANES - n_gate), (0, 0))).astype(BF16)
    w_rm = jnp.concatenate([w_in[:, kv0:kv0 + 768], w_in[:, kv0 + 1024:kv1]], axis=1).astype(BF16)
    tks = 256
    qt, rows_t, cmp_rows, kk, vt, win, gt = nsa_in_t(hp, g, w_rm, wq_t, wkv_t, wv_t, wg_t,
                                                     cos_p, sin_p, cos_pt, sin_pt, tm=tks, n_b=n_b)
    npg_p = t // PAGE_SIZE
    cend_p = jnp.arange(npg_p * 8, dtype=jnp.int32) * CMP_STRIDE + (CMP_BLOCK - 1)
    kc, _, vct = nsa_compress(cmp_rows.reshape(n_b * npg_p, PAGE_SIZE, 512), jnp.arange(n_b * npg_p, dtype=jnp.int32),
                              n_b, npg_p, wk, wv, *_rope_tables(cend_p), feature_major=False)
    nblk = -(-(t // SEL_BLOCK) // 16) * 16
    tile = jnp.arange(t // tks, dtype=jnp.int32)[:, None, None]
    key = jnp.arange(tks, dtype=jnp.int32)[None, :, None]
    blk = jnp.arange(LANES, dtype=jnp.int32)[None, None, :]
    emt = (blk == (tile * tks + key) // SEL_BLOCK).astype(BF16)
    ot = nsa_attn(qt, gt, kc, vct, kk, vt, _overlap(npg_p * 8, nblk).T, emt, n_b=n_b, t=t, tq=128)
    hp = mm_res_t(ot, w_out.T.astype(BF16), hp, tm=512)
    kv_p = jnp.transpose(rows_t.reshape(n_b, 4, NSA_KV_HEADS, HEAD_DIM, t), (0, 4, 1, 2, 3))
    keep = min(WINDOW, t)
    win_p = win.reshape(n_b, t, 2, NSA_KV_HEADS, HEAD_DIM)[:, t - keep:]

    cos_s, sin_s = _rope_tables(jnp.full((n_s,), past_len, jnp.int32))
    q_s, rows_s, kva_s, win_s, gates_s = nsa_in(hs, g, w_in_p, cos_s, sin_s, tm=n_s)
    cend_s = jnp.arange(n_pages * 8, dtype=jnp.int32) * CMP_STRIDE + (CMP_BLOCK - 1)
    cache_fm = jnp.transpose(cache_kv, (0, 2, 3, 4, 1)).reshape(cache_kv.shape[0], 1024, PAGE_SIZE)
    kc_s, vc_s, _ = nsa_compress(cache_fm, pt_flat, n_s, n_pages, wk, wv, *_rope_tables(cend_s), feature_major=True)
    n_sel = -(-(past_len + 1) // SEL_BLOCK)
    n_pick = min(SEL_TOPN, n_sel) - 1
    q4 = q_s.reshape(n_s, NSA_KV_HEADS, NSA_GROUP, HEAD_DIM)
    eye = jnp.eye(NSA_KV_HEADS, dtype=bool)[None, :, None, :, None]
    qbd = jnp.where(eye, q4[:, :, :, None, :], jnp.zeros((), BF16)).reshape(n_s, 16, 256)
    gm = (jnp.arange(8)[:, None] == jnp.arange(16)[None, :] // NSA_GROUP).astype(BF16)
    n_blk_pad = -(-n_sel // LANES) * LANES
    oc16, idx = nsa_s_sel(qbd, kc_s, vc_s, _overlap(n_pages * 8, n_blk_pad), gm, pos=past_len, n_pick=n_pick)
    oc5 = oc16.reshape(n_s, NSA_KV_HEADS, NSA_GROUP, NSA_KV_HEADS, HEAD_DIM)
    o_c = jnp.sum(jnp.where(eye, oc5, 0.0), axis=3).reshape(n_s, d)
    ix_flat = idx[:, :NSA_KV_HEADS, :n_pick].reshape(-1)
    q8 = jnp.pad(q4, ((0, 0), (0, 0), (0, 8 - NSA_GROUP), (0, 0)))
    win_fm = jnp.transpose(cache_win, (0, 2, 3, 4, 1)).reshape(n_s, 512, cache_win.shape[1])
    os_p, ow_p = nsa_s_attn(pt_flat, ix_flat, q8, cache_fm, kva_s.reshape(n_s, 16, 1, HEAD_DIM), win_fm,
                            n_pages=n_pages, n_pick=n_pick)
    o_s = os_p[:, :, :NSA_GROUP].reshape(n_s, d)
    o_w = ow_p[:, :, :NSA_GROUP].reshape(n_s, d)
    g3 = jnp.repeat(gates_s[:, :48].reshape(n_s, 16, 3), HEAD_DIM, axis=1)
    hs = nsa_out_s(o_c, o_s, o_w, g3[:, :, 0], g3[:, :, 1], g3[:, :, 2], w_out_b, hs)
    kv_s = rows_s.reshape(n_s, 1, 4, NSA_KV_HEADS, HEAD_DIM)
    win_new = win_s.reshape(n_s, 1, 2, NSA_KV_HEADS, HEAD_DIM)
    win_all = jnp.concatenate([cache_win, win_new], axis=1)
    win_s_out = win_all[:, win_all.shape[1] - cache_win.shape[1]:]
    return hp, hs, kv_p, kv_s, win_p, win_s_out


def _sconv_layer(hp, hs, g, state, w, *, n_b, t):
    w_in, w_conv, w_out = w
    w_in_b = w_in.astype(BF16)
    w_out_b = w_out.astype(BF16)
    hp, st = sconv_prompt(hp, g, w_in_b, w_conv, w_out_b, n_b=n_b, t=t, tm=256)
    st_p = st[:, 8 - (w_conv.shape[0] - 1):]
    hs, pre = sconv_sample(hs, g, w_in_b, w_conv, w_out_b, state[:, 0], state[:, 1])
    st_s = jnp.concatenate([state[:, 1:], pre[:, None, :]], axis=1)
    return hp, hs, st_p, st_s


def _moba_layer(hp, hs, g, cache_kv, pt_flat, n_pages, w, *, n_b, t, n_s):
    w_qkv, w_out = w
    d = hp.shape[1]
    n_heads = d // HEAD_DIM
    past_len = n_pages * PAGE_SIZE
    w_qkv_b = w_qkv.astype(BF16)
    w_out_b = w_out.astype(BF16)

    cos_p, sin_p = _rope_tables(jnp.arange(t, dtype=jnp.int32))
    cos_pt, sin_pt = cos_p[:, :HEAD_DIM].T, sin_p[:, :HEAD_DIM].T
    qt, rows_t, kb, vt, km = moba_in_t(hp, g, w_qkv_b[:, d:2 * d], w_qkv_b[:, :2 * d].T, w_qkv_b[:, 2 * d:].T,
                                       cos_p, sin_p, cos_pt, sin_pt, n_b=n_b)
    nblk = t // MOBA_BLOCK
    kmean = jnp.pad(km.reshape(n_b, nblk, d), ((0, 0), (0, -(-nblk // 16) * 16 - nblk), (0, 0)))
    ot = moba_attn(qt, kb, vt, kmean, n_b=n_b, t=t, tq=512)
    hp = mm_res_t(ot, w_out_b.T, hp, tm=512)
    kv_p = jnp.transpose(rows_t.reshape(n_b, 2, n_heads, HEAD_DIM, t), (0, 4, 1, 2, 3))

    cos_s, sin_s = _rope_tables(jnp.full((n_s,), past_len, jnp.int32))
    q_s, rows_s, kb_s, vb_s, _ = moba_in(hs, g, w_qkv_b, cos_s, sin_s, tm=n_s)
    cache_fm = jnp.transpose(cache_kv, (0, 2, 3, 4, 1)).reshape(cache_kv.shape[0], 2 * d, PAGE_SIZE)
    n_top = min(MOBA_TOPK, -(-(past_len + 1) // MOBA_BLOCK))
    qh = q_s.reshape(n_s, n_heads, 1, HEAD_DIM)
    eye = jnp.eye(n_heads, dtype=bool)[None, :, :, None]
    qbd = jnp.where(eye, qh, jnp.zeros((), BF16)).reshape(n_s, n_heads, d)
    qbd = jnp.pad(qbd, ((0, 0), (0, LANES - n_heads), (0, 0)))
    idx = moba_s_select(cache_fm, pt_flat, qbd, n_pages=n_pages, n_top=n_top)
    ix_flat = idx[:, :n_heads, :n_top].reshape(-1)
    q8 = jnp.pad(qh, ((0, 0), (0, 0), (0, 7), (0, 0)))
    o_p = moba_s_attn(pt_flat, ix_flat, q8, cache_fm, kb_s.reshape(n_s, n_heads, 1, HEAD_DIM),
                      vb_s.reshape(n_s, n_heads, 1, HEAD_DIM), n_pages=n_pages, n_top=n_top, n_heads=n_heads)
    o_s = o_p[:, :, 0].reshape(n_s, d).astype(BF16)
    hs = mm_res(o_s, w_out_b, hs, tm=n_s)
    kv_s = rows_s.reshape(n_s, 1, 2, n_heads, HEAD_DIM)
    return hp, hs, kv_p, kv_s


def _conf_layer(hp, hs, g, state, w, *, n_b, t):
    w_pw1, w_dw, b_dw, ln_g, ln_b, w_pw2 = w
    d = hp.shape[1]
    r = lambda a: a.reshape(1, d)
    args = (w_pw1.astype(BF16), w_dw, r(b_dw), r(ln_g), r(ln_b), w_pw2.astype(BF16))
    hp, st = conf_prompt(hp, g, *args, n_b=n_b, t=t, tm=256)
    st_p = st[:, st.shape[1] - (w_dw.shape[0] - 1):]
    hs, u = conf_sample(hs, g, *args, jnp.transpose(state, (1, 0, 2)))
    st_s = jnp.concatenate([state[:, 1:], u[:, None, :]], axis=1)
    return hp, hs, st_p, st_s


def kernel(x_prompt, x_sample, cache_nsa_kv, cache_nsa_win, state_sconv, cache_moba_kv, state_conformer,
           page_table, norm_mix, norm_ffn, norm_final, ffn_w_up, ffn_w_down,
           nsa_w_in, nsa_pe_k, nsa_w1_k, nsa_w2_k, nsa_pe_v, nsa_w1_v, nsa_w2_v, nsa_w_out,
           sconv_w_in, sconv_w_conv, sconv_w_out, moba_w_qkv, moba_w_out,
           conf_w_pw1, conf_w_dw, conf_b_dw, conf_ln_g, conf_ln_b, conf_w_pw2):
    n_b, t, d = x_prompt.shape
    n_s = x_sample.shape[0]
    depth = norm_mix.shape[0]
    n_pages = page_table.shape[1]
    pt_flat = page_table.reshape(-1).astype(jnp.int32)
    hp = x_prompt.reshape(n_b * t, d)
    hs = x_sample.reshape(n_s, d)
    outs = {k: [] for k in ("nsa_kv_p", "nsa_kv_s", "nsa_win_p", "nsa_win_s", "sconv_p", "sconv_s",
                            "moba_p", "moba_s", "conf_p", "conf_s")}
    for i in range(depth):
        kind, j = i % 4, i // 4
        g = norm_mix[i].reshape(1, d)
        if kind == 0:
            w = (nsa_w_in[j], nsa_pe_k[j], nsa_w1_k[j], nsa_w2_k[j], nsa_pe_v[j], nsa_w1_v[j], nsa_w2_v[j],
                 nsa_w_out[j])
            hp, hs, kv_p, kv_s, win_p, win_s = _nsa_layer(hp, hs, g, cache_nsa_kv[j], cache_nsa_win[j], pt_flat,
                                                          n_pages, w, n_b=n_b, t=t, n_s=n_s)
            outs["nsa_kv_p"].append(kv_p)
            outs["nsa_kv_s"].append(kv_s)
            outs["nsa_win_p"].append(win_p)
            outs["nsa_win_s"].append(win_s)
        elif kind == 1:
            hp, hs, st_p, st_s = _sconv_layer(hp, hs, g, state_sconv[j],
                                              (sconv_w_in[j], sconv_w_conv[j], sconv_w_out[j]), n_b=n_b, t=t)
            outs["sconv_p"].append(st_p)
            outs["sconv_s"].append(st_s)
        elif kind == 2:
            hp, hs, kv_p, kv_s = _moba_layer(hp, hs, g, cache_moba_kv[j], pt_flat, n_pages,
                                             (moba_w_qkv[j], moba_w_out[j]), n_b=n_b, t=t, n_s=n_s)
            outs["moba_p"].append(kv_p)
            outs["moba_s"].append(kv_s)
        else:
            w = (conf_w_pw1[j], conf_w_dw[j], conf_b_dw[j], conf_ln_g[j], conf_ln_b[j], conf_w_pw2[j])
            hp, hs, st_p, st_s = _conf_layer(hp, hs, g, state_conformer[j], w, n_b=n_b, t=t)
            outs["conf_p"].append(st_p)
            outs["conf_s"].append(st_s)
        gf = norm_ffn[i].reshape(1, d)
        wu = ffn_w_up[i].astype(BF16)
        wd = ffn_w_down[i].astype(BF16)
        final = i == depth - 1
        gfin = norm_final.reshape(1, d)
        hp = ffn(hp, gf, wu, wd, gfin, tm=512, final=final)
        hs = ffn(hs, gf, wu, wd, gfin, tm=n_s, final=final)
    return (hp.reshape(n_b, t, d), hs.reshape(n_s, 1, d),
            jnp.stack(outs["nsa_kv_p"]), jnp.stack(outs["nsa_kv_s"]),
            jnp.stack(outs["nsa_win_p"]), jnp.stack(outs["nsa_win_s"]),
            jnp.stack(outs["sconv_p"]), jnp.stack(outs["sconv_s"]),
            jnp.stack(outs["moba_p"]), jnp.stack(outs["moba_s"]),
            jnp.stack(outs["conf_p"]), jnp.stack(outs["conf_s"]))
```

```python
import functools

import jax
import jax.numpy as jnp
from jax import lax
from jax.experimental import pallas as pl
from jax.experimental.pallas import tpu as pltpu

F32 = jnp.float32
BF16 = jnp.bfloat16

HEAD_DIM = 64
ROPE_THETA = 10000.0
RMS_EPS = 1e-6
LN_EPS = 1e-5
NSA_KV_HEADS = 4
NSA_GROUP = 4
CMP_STRIDE = 16
CMP_BLOCK = 32
SEL_BLOCK = 64
SEL_TOPN = 16
WINDOW = 512
MOBA_BLOCK = 256
MOBA_TOPK = 3
PAGE_SIZE = 128
SCALE = HEAD_DIM ** -0.5
SCALE_LOG2 = SCALE * 1.4426950408889634

LANES = 128
NEG = -1e30
VMEM_LIMIT = 56 * 1024 * 1024


def _params(*sem):
    return pltpu.CompilerParams(dimension_semantics=sem, vmem_limit_bytes=VMEM_LIMIT)


def _dot(a, b):
    return jnp.dot(a, b, preferred_element_type=F32)


def _dot_nt(a, b):
    return lax.dot_general(a, b, (((1,), (1,)), ((), ())), preferred_element_type=F32)


def _split3(x):
    hi = x.astype(BF16)
    r = x - hi.astype(F32)
    mid = r.astype(BF16)
    lo = (r - mid.astype(F32)).astype(BF16)
    return hi, mid, lo


def _dot3(x, m):
    hi, mid, lo = _split3(x)
    return _dot(hi, m) + _dot(mid, m) + _dot(lo, m)


def _dot3_rhs(m, x):
    hi, mid, lo = _split3(x)
    return _dot(m, hi) + _dot(m, mid) + _dot(m, lo)


def _rms(x, g):
    return x * lax.rsqrt(jnp.mean(x * x, axis=-1, keepdims=True) + RMS_EPS) * g


def _sigmoid(x):
    return 1.0 / (1.0 + jnp.exp(-x))


def _rope(x, cos, sin):
    w = x.shape[-1]
    lane = lax.broadcasted_iota(jnp.int32, x.shape, 1)
    first = (lane % HEAD_DIM) < (HEAD_DIM // 2)
    rot = jnp.where(first, pltpu.roll(x, w - HEAD_DIM // 2, 1), pltpu.roll(x, HEAD_DIM // 2, 1))
    reps = w // LANES
    if reps > 1:
        cos = jnp.concatenate([cos] * reps, axis=1)
        sin = jnp.concatenate([sin] * reps, axis=1)
    return x * cos + rot * sin


def _masked_softmax(s, mask):
    s = jnp.where(mask, s, -jnp.inf)
    m = jnp.max(s, axis=-1, keepdims=True)
    m = jnp.where(m > -jnp.inf, m, 0.0)
    p = jnp.exp(s - m)
    return p / jnp.maximum(jnp.sum(p, axis=-1, keepdims=True), 1e-30)


def _topk_mask(score, k):
    lane = lax.broadcasted_iota(jnp.int32, score.shape, 1).astype(F32)
    sel = jnp.zeros(score.shape, F32)
    for _ in range(k):
        m = jnp.max(score, axis=-1, keepdims=True)
        idx = jnp.min(jnp.where(score == m, lane, 1e9), axis=-1, keepdims=True)
        hit = lane == idx
        sel = jnp.where(hit & (m > -jnp.inf), 1.0, sel)
        score = jnp.where(hit, -jnp.inf, score)
    return sel


def _online_update(carry, s, ok, v):
    m, l, acc = carry
    sm = jnp.where(ok, s, NEG)
    m_new = jnp.maximum(m, jnp.max(sm, axis=-1, keepdims=True))
    alpha = jnp.exp(m - m_new)
    p = jnp.where(ok, jnp.exp(sm - m_new), 0.0)
    l = alpha * l + jnp.sum(p, axis=-1, keepdims=True)
    acc = alpha * acc + _dot(p.astype(BF16), v)
    return m_new, l, acc


def _online_init(rows, width):
    return (jnp.full((rows, 1), NEG, F32), jnp.zeros((rows, 1), F32), jnp.zeros((rows, width), F32))


def _rope_t(x, cos, sin):
    r = x.shape[0]
    row = lax.broadcasted_iota(jnp.int32, x.shape, 0)
    first = (row % HEAD_DIM) < (HEAD_DIM // 2)
    rot = jnp.where(first, pltpu.roll(x, r - HEAD_DIM // 2, 0), pltpu.roll(x, HEAD_DIM // 2, 0))
    reps = r // HEAD_DIM
    if reps > 1:
        cos = jnp.concatenate([cos] * reps, axis=0)
        sin = jnp.concatenate([sin] * reps, axis=0)
    return x * cos + rot * sin


def _topk_mask_t(score, k):
    row = lax.broadcasted_iota(jnp.int32, score.shape, 0).astype(F32)
    sel = jnp.zeros(score.shape, F32)
    for _ in range(k):
        m = jnp.max(score, axis=0, keepdims=True)
        idx = jnp.min(jnp.where(score == m, row, 1e9), axis=0, keepdims=True)
        hit = row == idx
        sel = jnp.where(hit & (m > -jnp.inf), 1.0, sel)
        score = jnp.where(hit, -jnp.inf, score)
    return sel


ONES_ROWS = 16


def _online_init_t(cols):
    return (jnp.full((1, cols), NEG, F32), jnp.zeros((HEAD_DIM + ONES_ROWS, cols), F32))


def _online_update_t(carry, s, bias, v_t):
    m, acc = carry
    sm = s if bias is None else s + bias
    m_new = jnp.maximum(m, jnp.max(sm, axis=0, keepdims=True))
    p = jnp.exp2(sm - m_new).astype(BF16)
    v_aug = jnp.concatenate([v_t, jnp.ones((ONES_ROWS, v_t.shape[1]), BF16)], axis=0)
    return m_new, jnp.exp2(m - m_new) * acc + _dot(v_aug, p)


def _online_finish_t(carry):
    _, acc = carry
    return acc[0:HEAD_DIM] * (1.0 / jnp.maximum(acc[HEAD_DIM:HEAD_DIM + 1], 1e-30))


def _pad_pair(q, odd):
    z = jnp.zeros_like(q)
    return jnp.concatenate([z, q] if odd else [q, z], axis=0)


def _ffn_kernel(x_ref, g_ref, wu_ref, wd_ref, gf_ref, o_ref, *, chunk, final):
    x = x_ref[...]
    xn = _rms(x, g_ref[...]).astype(BF16)
    acc = x
    for c in range(0, wu_ref.shape[1], chunk):
        u = _dot(xn, wu_ref[:, c:c + chunk])
        a = jnp.square(jnp.maximum(u, 0.0)).astype(BF16)
        acc = acc + _dot(a, wd_ref[c:c + chunk, :])
    if final:
        acc = _rms(acc, gf_ref[...])
    o_ref[...] = acc


def ffn(x, g, wu, wd, gf, *, tm, final):
    m, d = x.shape
    dff = wu.shape[1]
    row = lambda i: (i, 0)
    const = lambda i: (0, 0)
    return pl.pallas_call(
        functools.partial(_ffn_kernel, chunk=512, final=final),
        grid=(m // tm,),
        in_specs=[pl.BlockSpec((tm, d), row), pl.BlockSpec((1, d), const),
                  pl.BlockSpec((d, dff), const, pipeline_mode=pl.Buffered(1)),
                  pl.BlockSpec((dff, d), const, pipeline_mode=pl.Buffered(1)),
                  pl.BlockSpec((1, d), const)],
        out_specs=pl.BlockSpec((tm, d), row),
        out_shape=jax.ShapeDtypeStruct((m, d), F32),
        compiler_params=_params("parallel"),
        name="ffn",
    )(x, g, wu, wd, gf)


def _mm_res_kernel(x_ref, w_ref, r_ref, o_ref):
    o_ref[...] = r_ref[...] + _dot(x_ref[...], w_ref[...])


def mm_res(x, w, res, *, tm):
    m, k = x.shape
    n = w.shape[1]
    row = lambda i: (i, 0)
    return pl.pallas_call(
        _mm_res_kernel,
        grid=(m // tm,),
        in_specs=[pl.BlockSpec((tm, k), row), pl.BlockSpec((k, n), lambda i: (0, 0)),
                  pl.BlockSpec((tm, n), row)],
        out_specs=pl.BlockSpec((tm, n), row),
        out_shape=jax.ShapeDtypeStruct((m, n), F32),
        compiler_params=_params("parallel"),
        name="mm_res",
    )(x, w, res)


def _mm_res_t_kernel(xt_ref, wt_ref, r_ref, o_ref):
    o_ref[...] = r_ref[...] + _dot(wt_ref[...], xt_ref[...]).T


def mm_res_t(xt, wt, res, *, tm):
    k, m = xt.shape
    n = wt.shape[0]
    row = lambda i: (i, 0)
    return pl.pallas_call(
        _mm_res_t_kernel,
        grid=(m // tm,),
        in_specs=[pl.BlockSpec((k, tm), lambda i: (0, i)), pl.BlockSpec((n, k), lambda i: (0, 0)),
                  pl.BlockSpec((tm, n), row)],
        out_specs=pl.BlockSpec((tm, n), row),
        out_shape=jax.ShapeDtypeStruct((m, n), F32),
        compiler_params=_params("parallel"),
        name="mm_res_t",
    )(xt, wt, res)


def _nsa_in_kernel(x_ref, g_ref, w_ref, cos_ref, sin_ref, q_ref, rows_ref, kva_ref, win_ref, gate_ref):
    xn = _rms(x_ref[...], g_ref[...]).astype(BF16)
    cos = cos_ref[...]
    sin = sin_ref[...]
    q = _rope(_dot(xn, w_ref[:, 0:1024]), cos, sin) * SCALE
    q_ref[...] = q.astype(BF16)
    kv = _dot(xn, w_ref[:, 1024:2048])
    ks = _rope(kv[:, 512:768], cos, sin)
    rows_ref[:, 0:512] = kv[:, 0:512]
    rows_ref[:, 512:768] = ks
    rows_ref[:, 768:1024] = kv[:, 768:1024]
    wkv = _dot(xn, w_ref[:, 2048:2560])
    kw = _rope(wkv[:, 0:256], cos, sin)
    win_ref[:, 0:256] = kw
    win_ref[:, 256:512] = wkv[:, 256:512]
    kva_ref[:, 0:256] = ks.astype(BF16)
    kva_ref[:, 256:512] = kv[:, 768:1024].astype(BF16)
    kva_ref[:, 512:768] = kw.astype(BF16)
    kva_ref[:, 768:1024] = wkv[:, 256:512].astype(BF16)
    gate_ref[...] = _sigmoid(_dot(xn, w_ref[:, 2560:2688]))


def nsa_in(x, g, w, cos, sin, *, tm):
    m, d = x.shape
    nw = w.shape[1]
    nt = cos.shape[0] // tm
    row = lambda i: (i, 0)
    tab = lambda i: (i % nt, 0)
    const = lambda i: (0, 0)
    outs = [(1024, BF16), (1024, F32), (1024, BF16), (512, F32), (LANES, F32)]
    return pl.pallas_call(
        _nsa_in_kernel,
        grid=(m // tm,),
        in_specs=[pl.BlockSpec((tm, d), row), pl.BlockSpec((1, d), const),
                  pl.BlockSpec((d, nw), const, pipeline_mode=pl.Buffered(1)),
                  pl.BlockSpec((tm, LANES), tab), pl.BlockSpec((tm, LANES), tab)],
        out_specs=[pl.BlockSpec((tm, n), row) for n, _ in outs],
        out_shape=[jax.ShapeDtypeStruct((m, n), dt) for n, dt in outs],
        compiler_params=_params("parallel"),
        name="nsa_in",
    )(x, g, w, cos, sin)


def _nsa_in_t_kernel(x_ref, g_ref, w_ref, wqt_ref, wkvt_ref, wvt_ref, wgt_ref, cos_ref, sin_ref, cost_ref, sint_ref,
                     qt_ref, rowst_ref, cmp_ref, kk_ref, vt_ref, win_ref, gt_ref):
    xn = _rms(x_ref[...], g_ref[...]).astype(BF16)
    cos = cos_ref[...]
    sin = sin_ref[...]
    cos_t = cost_ref[...]
    sin_t = sint_ref[...]
    qt_ref[...] = (_rope_t(_dot_nt(wqt_ref[...], xn), cos_t, sin_t) * SCALE_LOG2).astype(BF16)
    rt = _dot_nt(wkvt_ref[...], xn)
    rowst_ref[0:512, :] = rt[0:512]
    rowst_ref[512:768, :] = _rope_t(rt[512:768], cos_t, sin_t)
    rowst_ref[768:1024, :] = rt[768:1024]
    vt_ref[0:256, :] = rt[768:1024].astype(BF16)
    vt_ref[256:512, :] = _dot_nt(wvt_ref[...], xn).astype(BF16)
    kv = _dot(xn, w_ref[:, 0:768])
    cmp_ref[...] = kv[:, 0:512]
    wkv = _dot(xn, w_ref[:, 768:1280])
    kw = _rope(wkv[:, 0:256], cos, sin)
    win_ref[:, 0:256] = kw
    win_ref[:, 256:512] = wkv[:, 256:512]
    kk_ref[:, 0:256] = _rope(kv[:, 512:768], cos, sin).astype(BF16)
    kk_ref[:, 256:512] = kw.astype(BF16)
    gt_ref[...] = _sigmoid(_dot_nt(wgt_ref[...], xn))


def nsa_in_t(x, g, w_rm, wq_t, wkv_t, wv_t, wg_t, cos, sin, cos_t, sin_t, *, tm, n_b):
    m, d = x.shape
    nt = cos.shape[0] // tm
    row = lambda i: (i, 0)
    col = lambda i: (0, i)
    const = lambda i: (0, 0)
    one = pl.Buffered(1)
    weights = (w_rm, wq_t, wkv_t, wv_t, wg_t)
    return pl.pallas_call(
        _nsa_in_t_kernel,
        grid=(m // tm,),
        in_specs=[pl.BlockSpec((tm, d), row), pl.BlockSpec((1, d), const)]
        + [pl.BlockSpec(w.shape, const, pipeline_mode=one) for w in weights]
        + [pl.BlockSpec((tm, LANES), lambda i: (i % nt, 0)), pl.BlockSpec((tm, LANES), lambda i: (i % nt, 0)),
           pl.BlockSpec((HEAD_DIM, tm), lambda i: (0, i % nt)), pl.BlockSpec((HEAD_DIM, tm), lambda i: (0, i % nt))],
        out_specs=[pl.BlockSpec((1024, tm), col), pl.BlockSpec((None, 1024, tm), lambda i: (i // nt, 0, i % nt)),
                   pl.BlockSpec((tm, 512), row), pl.BlockSpec((tm, 512), row),
                   pl.BlockSpec((None, 512, tm), lambda i: (i, 0, 0)), pl.BlockSpec((tm, 512), row),
                   pl.BlockSpec((LANES, tm), col)],
        out_shape=[jax.ShapeDtypeStruct((1024, m), BF16), jax.ShapeDtypeStruct((n_b, 1024, m // n_b), F32),
                   jax.ShapeDtypeStruct((m, 512), F32), jax.ShapeDtypeStruct((m, 512), BF16),
                   jax.ShapeDtypeStruct((m // tm, 512, tm), BF16), jax.ShapeDtypeStruct((m, 512), F32),
                   jax.ShapeDtypeStruct((LANES, m), F32)],
        compiler_params=_params("parallel"),
        name="nsa_in_t",
    )(x, g, *weights, cos, sin, cos_t, sin_t)


def _compress_kernel(pt_ref, *refs, pps, feature_major):
    del pt_ref
    pages = refs[:pps + 1]
    (pelo_k, pehi_k, w1lo_k, w1hi_k, w2_k, pelo_v, pehi_v, w1lo_v, w1hi_v, w2_v, w2t_v,
     cos_ref, sin_ref, eye_ref, kc_ref, vc_ref, vct_ref, xs_ref) = refs[pps + 1:]
    npc = pps * 8
    rows = (pps + 1) * 8
    low = lax.broadcasted_iota(jnp.int32, (rows, LANES), 1) < HEAD_DIM
    for k, pg in enumerate(pages):
        x = pg[...]
        if feature_major:
            x = x.T
        for lb in range(4):
            xs_ref[lb, k * PAGE_SIZE:(k + 1) * PAGE_SIZE, :] = x[:, lb * LANES:(lb + 1) * LANES]
    streams = ((pelo_k, pehi_k, w1lo_k, w1hi_k, w2_k, kc_ref), (pelo_v, pehi_v, w1lo_v, w1hi_v, w2_v, vc_ref))
    for s, (pelo, pehi, w1lo, w1hi, w2, out_ref) in enumerate(streams):
        heads = [[] for _ in range(NSA_KV_HEADS)]
        for lb in range(2):
            for q in range(CMP_STRIDE // 2):
                a = xs_ref[2 * s + lb, pl.ds(2 * q, rows, stride=CMP_STRIDE), :]
                b = xs_ref[2 * s + lb, pl.ds(2 * q + 1, rows, stride=CMP_STRIDE), :]
                heads[2 * lb].append(jnp.where(low, a, pltpu.roll(b, HEAD_DIM, 1)))
                heads[2 * lb + 1].append(jnp.where(low, pltpu.roll(a, HEAD_DIM, 1), b))
        x = jnp.concatenate([jnp.concatenate(hh, axis=1) for hh in heads], axis=0)
        first = _dot((x + pelo[...]).astype(BF16), w1lo[...])
        second = _dot((x + pehi[...]).astype(BF16), w1hi[...])
        outs = []
        outs_t = []
        for h in range(NSA_KV_HEADS):
            pre = first[h * rows:h * rows + npc] + second[h * rows + 1:h * rows + 1 + npc]
            hid = (pre * _sigmoid(pre)).astype(BF16)
            outs.append(_dot(hid, w2[...]))
            if s == 1:
                outs_t.append(_dot_nt(w2t_v[...], hid))
        res = jnp.concatenate(outs, axis=1)
        if s == 0:
            res = _rope(res, cos_ref[...], sin_ref[...])
        else:
            vct_ref[...] = jnp.concatenate(outs_t, axis=0).astype(BF16)
        out_ref[...] = res.astype(BF16)


def nsa_compress(rows3d, pt_flat, n_seq, n_pages, wk, wv, cos_c, sin_c, *, feature_major):
    pps = min(16, n_pages)
    steps = n_pages // pps
    npc = pps * 8
    page_block = (None, 512, PAGE_SIZE) if feature_major else (None, PAGE_SIZE, 512)
    eye = jnp.eye(PAGE_SIZE, dtype=BF16)

    def page_map(k):
        return lambda b, s, pt: (pt[b * n_pages + jnp.minimum(s * pps + k, n_pages - 1)], 0, 0)

    const2 = lambda b, s, pt: (0, 0)
    wspecs = []
    for _ in range(2):
        wspecs += [pl.BlockSpec((1, 1024), const2), pl.BlockSpec((1, 1024), const2),
                   pl.BlockSpec((1024, 256), const2), pl.BlockSpec((1024, 256), const2),
                   pl.BlockSpec((256, HEAD_DIM), const2)]
    wspecs.append(pl.BlockSpec((HEAD_DIM, 256), const2))
    grid_spec = pltpu.PrefetchScalarGridSpec(
        num_scalar_prefetch=1,
        grid=(n_seq, steps),
        in_specs=[pl.BlockSpec(page_block, page_map(k)) for k in range(pps + 1)] + wspecs
        + [pl.BlockSpec((npc, LANES), lambda b, s, pt: (s, 0))] * 2 + [pl.BlockSpec(eye.shape, const2)],
        out_specs=[pl.BlockSpec((None, npc, 256), lambda b, s, pt: (b, s, 0))] * 2
        + [pl.BlockSpec((None, 256, npc), lambda b, s, pt: (b, 0, s))],
        scratch_shapes=[pltpu.VMEM((4, (pps + 1) * PAGE_SIZE, LANES), F32)],
    )
    return pl.pallas_call(
        functools.partial(_compress_kernel, pps=pps, feature_major=feature_major),
        grid_spec=grid_spec,
        out_shape=[jax.ShapeDtypeStruct((n_seq, n_pages * 8, 256), BF16)] * 2
        + [jax.ShapeDtypeStruct((n_seq, 256, n_pages * 8), BF16)],
        compiler_params=_params("parallel", "arbitrary"),
        name="nsa_compress",
    )(pt_flat, *([rows3d] * (pps + 1)), *wk, *wv, cos_c, sin_c, eye)


def _nsa_attn_kernel(qt_ref, gt_ref, kc_ref, vct_ref, kk_ref, vt_ref, ovlt_ref, emt_ref, ot_ref, s0_ref, s1_ref, *,
                     tq, tks, n_top, nblk):
    i = pl.program_id(1)
    c0 = i * tq
    n4 = NSA_GROUP * tq
    pos1 = c0 + lax.broadcasted_iota(jnp.int32, (1, tq), 1)
    pos4 = c0 + lax.broadcasted_iota(jnp.int32, (1, n4), 1) % tq
    blk = lax.broadcasted_iota(jnp.int32, (nblk, tq), 0)
    cur = pos1 // SEL_BLOCK
    heads = range(NSA_KV_HEADS)
    pair = [slice((h // 2) * LANES, (h // 2 + 1) * LANES) for h in heads]
    vrow = [slice(h * HEAD_DIM, (h + 1) * HEAD_DIM) for h in heads]
    qpad = []
    for h in heads:
        q4 = jnp.concatenate(
            [qt_ref[(NSA_GROUP * h + g) * HEAD_DIM:(NSA_GROUP * h + g + 1) * HEAD_DIM, :] for g in range(NSA_GROUP)],
            axis=1)
        qpad.append(_pad_pair(q4, h % 2 == 1))

    o_c, q_aug = [], []
    for h in heads:
        s = _dot(kc_ref[:, pair[h]], qpad[h])
        cend = lax.broadcasted_iota(jnp.int32, s.shape, 0) * CMP_STRIDE + (CMP_BLOCK - 1)
        s = jnp.where(cend <= pos4, s, -jnp.inf)
        m = jnp.max(s, axis=0, keepdims=True)
        p = jnp.exp2(s - jnp.where(m > -jnp.inf, m, 0.0))
        p = p * (1.0 / jnp.maximum(jnp.sum(p, axis=0, keepdims=True), 1e-30))
        o_c.append(_dot(vct_ref[vrow[h], :], p.astype(BF16)))
        psum = p[:, 0:tq] + p[:, tq:2 * tq] + p[:, 2 * tq:3 * tq] + p[:, 3 * tq:4 * tq]
        imp = _dot3_rhs(ovlt_ref[...], psum)
        forced = (blk == 0) | (blk == cur) | (blk == cur - 1)
        imp = jnp.where(forced, jnp.inf, imp)
        imp = jnp.where(blk <= cur, imp, -jnp.inf)
        sel = _topk_mask_t(imp, n_top)
        sel_bias = jnp.where(sel > 0.5, 0.0, NEG).astype(BF16)
        sel_bias = jnp.concatenate([sel_bias] * NSA_GROUP, axis=1)
        fill = jnp.zeros((LANES - nblk, n4), BF16)
        q_aug.append(jnp.concatenate([qpad[h], sel_bias, fill], axis=0))

    def finish(carry):
        return [_online_finish_t(c) for c in carry]

    init = tuple(_online_init_t(n4) for _ in heads)
    t_last = (c0 + tq - 1) // tks

    n_tiles = vt_ref.shape[0]

    def run(t_first, scores, consume):
        def put(t, s_ref):
            for h, s in enumerate(scores(jnp.minimum(t, n_tiles - 1))):
                s_ref[h] = s

        def body(u, carry):
            t = t_first + 2 * u
            put(t + 1, s1_ref)
            carry = consume(t, s0_ref, carry, False)
            put(t + 2, s0_ref)
            return consume(t + 1, s1_ref, carry, False)

        put(t_first, s0_ref)
        pairs = (t_last - t_first) // 2
        carry = lax.fori_loop(0, pairs, body, init)
        t = t_first + 2 * pairs
        put(t + 1, s1_ref)
        carry = consume(t, s0_ref, carry, True)
        return finish(consume(t + 1, s1_ref, carry, True))

    def sel_scores(t):
        k0 = pl.multiple_of(t * tks, tks)
        e = emt_ref[t]
        return [_dot(jnp.concatenate([kk_ref[pl.ds(k0, tks), pair[h]], e], axis=1), q_aug[h]) for h in heads]

    def sel_consume(t, s_ref, carry, masked):
        bias = None
        if masked:
            kpos = t * tks + lax.broadcasted_iota(jnp.int32, (tks, n4), 0)
            bias = jnp.where(kpos <= pos4, 0.0, NEG)
        tv = jnp.minimum(t, n_tiles - 1)
        return tuple(_online_update_t(carry[h], s_ref[h], bias, vt_ref[tv, vrow[h], :]) for h in heads)

    o_s = run(0, sel_scores, sel_consume)

    def win_put(t, s_ref):
        k0 = pl.multiple_of(jnp.maximum(t, 0) * tks, tks)
        for h in heads:
            s_ref[h] = _dot(kk_ref[pl.ds(k0, tks), 256 + pair[h].start:256 + pair[h].stop], qpad[h])

    def win_consume(t, s_ref, carry):
        tv = jnp.maximum(t, 0)
        diff = pos4 - (tv * tks + lax.broadcasted_iota(jnp.int32, (tks, n4), 0))
        bias = jnp.where((diff >= 0) & (diff <= WINDOW) & (t >= 0), 0.0, NEG)
        return tuple(_online_update_t(carry[h], s_ref[h], bias,
                                      vt_ref[tv, 256 + vrow[h].start:256 + vrow[h].stop, :]) for h in heads)

    n_win = -(-WINDOW // tks) + 1
    bufs = (s0_ref, s1_ref)
    carry = init
    win_put(t_last - (n_win - 1), bufs[0])
    for e in range(n_win):
        if e + 1 < n_win:
            win_put(t_last - (n_win - 2 - e), bufs[(e + 1) % 2])
        carry = win_consume(t_last - (n_win - 1 - e), bufs[e % 2], carry)
    o_w = finish(carry)

    for h in heads:
        for g in range(NSA_GROUP):
            j = (h * NSA_GROUP + g) * 3
            cs = slice(g * tq, (g + 1) * tq)
            o = (gt_ref[j:j + 1, :] * o_c[h][:, cs] + gt_ref[j + 1:j + 2, :] * o_s[h][:, cs]
                 + gt_ref[j + 2:j + 3, :] * o_w[h][:, cs])
            ot_ref[(NSA_GROUP * h + g) * HEAD_DIM:(NSA_GROUP * h + g + 1) * HEAD_DIM, :] = o.astype(BF16)


def nsa_attn(qt, gt, kc, vct, kk, vt, ovlt, emt, *, n_b, t, tq):
    nq = t // tq
    tks = vt.shape[2]
    npiece = kc.shape[1]
    nblk = ovlt.shape[0]
    n_top = min(SEL_TOPN, t // SEL_BLOCK)
    col = lambda b, i: (0, b * nq + i)
    per_b = lambda b, i: (b, 0, 0)
    return pl.pallas_call(
        functools.partial(_nsa_attn_kernel, tq=tq, tks=tks, n_top=n_top, nblk=nblk),
        grid=(n_b, nq),
        in_specs=[pl.BlockSpec((1024, tq), col), pl.BlockSpec((LANES, tq), col),
                  pl.BlockSpec((None, npiece, 256), per_b), pl.BlockSpec((None, 256, npiece), per_b),
                  pl.BlockSpec((None, t, 512), per_b), pl.BlockSpec((t // tks, 512, tks), per_b),
                  pl.BlockSpec(ovlt.shape, lambda b, i: (0, 0)),
                  pl.BlockSpec(emt.shape, lambda b, i: (0, 0, 0))],
        out_specs=pl.BlockSpec((1024, tq), col),
        out_shape=jax.ShapeDtypeStruct((1024, n_b * t), BF16),
        scratch_shapes=[pltpu.VMEM((NSA_KV_HEADS, tks, NSA_GROUP * tq), F32)] * 2,
        compiler_params=_params("parallel", "arbitrary"),
        name="nsa_attn",
    )(qt, gt, kc, vct, kk.reshape(n_b, t, 512), vt, ovlt, emt)


def _nsa_s_sel_kernel(q_ref, kc_ref, vc_ref, ovl_ref, gm_ref, oc_ref, idx_ref, *, pos, n_pick):
    s = _dot_nt(q_ref[...], kc_ref[...])
    cend = lax.broadcasted_iota(jnp.int32, s.shape, 1) * CMP_STRIDE + (CMP_BLOCK - 1)
    p = _masked_softmax(s, cend <= pos)
    oc_ref[...] = _dot(p.astype(BF16), vc_ref[...])
    psum = _dot3_rhs(gm_ref[...], p)
    imp = _dot3(psum, ovl_ref[...])
    blk = lax.broadcasted_iota(jnp.int32, imp.shape, 1)
    cur = pos // SEL_BLOCK
    imp = jnp.where((blk == 0) | (blk == cur - 1), jnp.inf, imp)
    imp = jnp.where(blk < cur, imp, -jnp.inf)
    lane = blk.astype(F32)
    slot = lax.broadcasted_iota(jnp.int32, (8, LANES), 1)
    picks = jnp.zeros((8, LANES), F32)
    for r in range(n_pick):
        m = jnp.max(imp, axis=-1, keepdims=True)
        idx = jnp.min(jnp.where(imp == m, lane, 1e9), axis=-1, keepdims=True)
        imp = jnp.where(lane == idx, -jnp.inf, imp)
        picks = jnp.where(slot == r, idx, picks)
    idx_ref[...] = picks.astype(jnp.int32)


def nsa_s_sel(qbd, kc, vc, ovl, gm, *, pos, n_pick):
    n_b, _, npiece = kc.shape[0], None, kc.shape[1]
    per_b = lambda b: (b, 0, 0)
    return pl.pallas_call(
        functools.partial(_nsa_s_sel_kernel, pos=pos, n_pick=n_pick),
        grid=(n_b,),
        in_specs=[pl.BlockSpec((None, 16, 256), per_b), pl.BlockSpec((None, npiece, 256), per_b),
                  pl.BlockSpec((None, npiece, 256), per_b),
                  pl.BlockSpec(ovl.shape, lambda b: (0, 0)), pl.BlockSpec(gm.shape, lambda b: (0, 0))],
        out_specs=[pl.BlockSpec((None, 16, 256), per_b), pl.BlockSpec((None, 8, LANES), per_b)],
        out_shape=[jax.ShapeDtypeStruct((n_b, 16, 256), F32), jax.ShapeDtypeStruct((n_b, 8, LANES), jnp.int32)],
        compiler_params=_params("parallel"),
        name="nsa_s_sel",
    )(qbd, kc, vc, ovl, gm)


def _attend_with_new(q, kts, vts, biases, k_new, v_new):
    ss = []
    for kt, bias in zip(kts, biases):
        s = _dot(q, kt.astype(BF16))
        ss.append(s if bias is None else s + bias)
    s = ss[0] if len(ss) == 1 else jnp.concatenate(ss, axis=1)
    s_new = jnp.sum(q.astype(F32) * k_new.astype(F32), axis=-1, keepdims=True)
    m = jnp.maximum(jnp.max(s, axis=-1, keepdims=True), s_new)
    p = jnp.exp(s - m)
    p_new = jnp.exp(s_new - m)
    l = jnp.sum(p, axis=-1, keepdims=True) + p_new
    acc = p_new * v_new.astype(F32)
    off = 0
    for vt in vts:
        n = vt.shape[1]
        acc = acc + _dot_nt(p[:, off:off + n].astype(BF16), vt.astype(BF16))
        off += n
    return acc / l


def _nsa_s_attn_kernel(pt_ref, ix_ref, q_ref, *refs, n_pick):
    del pt_ref
    kblk = refs[:n_pick]
    vblk = refs[n_pick:2 * n_pick]
    ks_new, vs_new, kw_new, vw_new, kw_ref, vw_ref, os_ref, ow_ref = refs[2 * n_pick:]
    base = (pl.program_id(0) * NSA_KV_HEADS + pl.program_id(1)) * n_pick
    half = lax.broadcasted_iota(jnp.int32, (1, PAGE_SIZE), 1) // SEL_BLOCK
    biases = [jnp.where(half == ix_ref[base + r] % 2, 0.0, NEG) for r in range(n_pick)]
    q = q_ref[...]
    os_ref[...] = _attend_with_new(q, [r[...] for r in kblk], [r[...] for r in vblk], biases, ks_new[...], vs_new[...])
    ow_ref[...] = _attend_with_new(q, [kw_ref[...]], [vw_ref[...]], [None], kw_new[...], vw_new[...])


def nsa_s_attn(pt_flat, ix_flat, q4, cache_fm, new_rows, win_fm, *, n_pages, n_pick):
    n_b = q4.shape[0]
    nwin = win_fm.shape[2]
    per_page = PAGE_SIZE // SEL_BLOCK

    def kv_map(r, stream):
        def f(b, h, pt, ix):
            j = ix[(b * NSA_KV_HEADS + h) * n_pick + r]
            return (pt[b * n_pages + j // per_page], stream * NSA_KV_HEADS + h, 0)
        return f

    new_map = lambda s: (lambda b, h, pt, ix: (b, s * NSA_KV_HEADS + h, 0, 0))
    qo_spec = pl.BlockSpec((None, None, 8, HEAD_DIM), lambda b, h, pt, ix: (b, h, 0, 0))
    grid_spec = pltpu.PrefetchScalarGridSpec(
        num_scalar_prefetch=2,
        grid=(n_b, NSA_KV_HEADS),
        in_specs=[qo_spec]
        + [pl.BlockSpec((None, HEAD_DIM, PAGE_SIZE), kv_map(r, 2)) for r in range(n_pick)]
        + [pl.BlockSpec((None, HEAD_DIM, PAGE_SIZE), kv_map(r, 3)) for r in range(n_pick)]
        + [pl.BlockSpec((None, None, 1, HEAD_DIM), new_map(s)) for s in range(4)]
        + [pl.BlockSpec((None, HEAD_DIM, nwin), lambda b, h, pt, ix: (b, h, 0)),
           pl.BlockSpec((None, HEAD_DIM, nwin), lambda b, h, pt, ix: (b, NSA_KV_HEADS + h, 0))],
        out_specs=[qo_spec, qo_spec],
    )
    return pl.pallas_call(
        functools.partial(_nsa_s_attn_kernel, n_pick=n_pick),
        grid_spec=grid_spec,
        out_shape=[jax.ShapeDtypeStruct((n_b, NSA_KV_HEADS, 8, HEAD_DIM), F32)] * 2,
        compiler_params=_params("parallel", "arbitrary"),
        name="nsa_s_attn",
    )(pt_flat, ix_flat, q4, *([cache_fm] * (2 * n_pick)), *([new_rows] * 4), win_fm, win_fm)


def _nsa_out_s_kernel(oc_ref, os_ref, ow_ref, g0_ref, g1_ref, g2_ref, w_ref, r_ref, o_ref):
    o = g0_ref[...] * oc_ref[...] + g1_ref[...] * os_ref[...] + g2_ref[...] * ow_ref[...]
    o_ref[...] = r_ref[...] + _dot(o.astype(BF16), w_ref[...])


def nsa_out_s(oc, osel, ow, g0, g1, g2, w, res):
    m, d = res.shape
    full = pl.BlockSpec((m, d), lambda i: (0, 0))
    return pl.pallas_call(
        _nsa_out_s_kernel,
        grid=(1,),
        in_specs=[full] * 6 + [pl.BlockSpec(w.shape, lambda i: (0, 0)), full],
        out_specs=full,
        out_shape=jax.ShapeDtypeStruct((m, d), F32),
        compiler_params=_params("arbitrary"),
        name="nsa_out_s",
    )(oc, osel, ow, g0, g1, g2, w, res)


def _sconv_kernel(x_ref, g_ref, win_ref, wc_ref, wout_ref, o_ref, st_ref, carry_ref):
    d = x_ref.shape[1]
    tm = x_ref.shape[0]

    @pl.when(pl.program_id(1) == 0)
    def _():
        carry_ref[...] = jnp.zeros(carry_ref.shape, F32)

    x = x_ref[...]
    xn = _rms(x, g_ref[...]).astype(BF16)
    b_gate = _dot(xn, win_ref[:, 0:d])
    pre = _dot(xn, win_ref[:, d:2 * d]) * _dot(xn, win_ref[:, 2 * d:3 * d])
    row = lax.broadcasted_iota(jnp.int32, (tm, d), 0)
    back1 = jnp.where(row == 0, carry_ref[7:8, :], pltpu.roll(pre, 1, 0))
    back2 = jnp.where(row == 0, carry_ref[6:7, :], jnp.where(row == 1, carry_ref[7:8, :], pltpu.roll(pre, 2, 0)))
    y = back2 * wc_ref[0:1, :] + back1 * wc_ref[1:2, :] + pre * wc_ref[2:3, :]
    tail = pre[tm - 8:tm]
    carry_ref[...] = tail
    st_ref[...] = tail
    o_ref[...] = x + _dot((b_gate * y).astype(BF16), wout_ref[...])


def sconv_prompt(x, g, w_in, w_conv, w_out, *, n_b, t, tm):
    d = x.shape[1]
    nt = t // tm
    row = lambda b, i: (b * nt + i, 0)
    const = lambda b, i: (0, 0)
    return pl.pallas_call(
        _sconv_kernel,
        grid=(n_b, nt),
        in_specs=[pl.BlockSpec((tm, d), row), pl.BlockSpec((1, d), const),
                  pl.BlockSpec((d, 3 * d), const, pipeline_mode=pl.Buffered(1)),
                  pl.BlockSpec(w_conv.shape, const),
                  pl.BlockSpec((d, d), const, pipeline_mode=pl.Buffered(1))],
        out_specs=[pl.BlockSpec((tm, d), row), pl.BlockSpec((None, 8, d), lambda b, i: (b, 0, 0))],
        out_shape=[jax.ShapeDtypeStruct((n_b * t, d), F32), jax.ShapeDtypeStruct((n_b, 8, d), F32)],
        scratch_shapes=[pltpu.VMEM((8, d), F32)],
        compiler_params=_params("parallel", "arbitrary"),
        name="sconv_prompt",
    )(x, g, w_in, w_conv, w_out)


def _sconv_s_kernel(x_ref, g_ref, win_ref, wc_ref, wout_ref, p0_ref, p1_ref, o_ref, pre_ref):
    d = x_ref.shape[1]
    x = x_ref[...]
    xn = _rms(x, g_ref[...]).astype(BF16)
    b_gate = _dot(xn, win_ref[:, 0:d])
    pre = _dot(xn, win_ref[:, d:2 * d]) * _dot(xn, win_ref[:, 2 * d:3 * d])
    y = p0_ref[...] * wc_ref[0:1, :] + p1_ref[...] * wc_ref[1:2, :] + pre * wc_ref[2:3, :]
    pre_ref[...] = pre
    o_ref[...] = x + _dot((b_gate * y).astype(BF16), wout_ref[...])


def sconv_sample(x, g, w_in, w_conv, w_out, past0, past1):
    m, d = x.shape
    full = lambda a: pl.BlockSpec(a.shape, lambda i: (0,) * a.ndim)
    args = (x, g, w_in, w_conv, w_out, past0, past1)
    return pl.pallas_call(
        _sconv_s_kernel,
        grid=(1,),
        in_specs=[full(a) for a in args],
        out_specs=[pl.BlockSpec((m, d), lambda i: (0, 0))] * 2,
        out_shape=[jax.ShapeDtypeStruct((m, d), F32)] * 2,
        compiler_params=_params("arbitrary"),
        name="sconv_sample",
    )(*args)


def _moba_in_kernel(x_ref, g_ref, w_ref, cos_ref, sin_ref, q_ref, rows_ref, kb_ref, vb_ref, km_ref):
    d = x_ref.shape[1]
    xn = _rms(x_ref[...], g_ref[...]).astype(BF16)
    cos = cos_ref[...]
    sin = sin_ref[...]
    q_ref[...] = (_rope(_dot(xn, w_ref[:, 0:d]), cos, sin) * SCALE).astype(BF16)
    k = _rope(_dot(xn, w_ref[:, d:2 * d]), cos, sin)
    v = _dot(xn, w_ref[:, 2 * d:3 * d])
    rows_ref[:, 0:d] = k
    rows_ref[:, d:2 * d] = v
    kb_ref[...] = k.astype(BF16)
    vb_ref[...] = v.astype(BF16)
    km_ref[...] = jnp.sum(k, axis=0, keepdims=True) * (1.0 / MOBA_BLOCK)


def moba_in(x, g, w, cos, sin, *, tm):
    m, d = x.shape
    nt = cos.shape[0] // tm
    row = lambda i: (i, 0)
    tab = lambda i: (i % nt, 0)
    const = lambda i: (0, 0)
    return pl.pallas_call(
        _moba_in_kernel,
        grid=(m // tm,),
        in_specs=[pl.BlockSpec((tm, d), row), pl.BlockSpec((1, d), const),
                  pl.BlockSpec((d, 3 * d), const, pipeline_mode=pl.Buffered(1)),
                  pl.BlockSpec((tm, LANES), tab), pl.BlockSpec((tm, LANES), tab)],
        out_specs=[pl.BlockSpec((tm, d), row), pl.BlockSpec((tm, 2 * d), row), pl.BlockSpec((tm, d), row),
                   pl.BlockSpec((tm, d), row), pl.BlockSpec((None, 1, d), lambda i: (i, 0, 0))],
        out_shape=[jax.ShapeDtypeStruct((m, d), BF16), jax.ShapeDtypeStruct((m, 2 * d), F32),
                   jax.ShapeDtypeStruct((m, d), BF16), jax.ShapeDtypeStruct((m, d), BF16),
                   jax.ShapeDtypeStruct((m // tm, 1, d), F32)],
        compiler_params=_params("parallel"),
        name="moba_in",
    )(x, g, w, cos, sin)


def _moba_in_t_kernel(x_ref, g_ref, wk_ref, wqkt_ref, wvt_ref, cos_ref, sin_ref, cost_ref, sint_ref,
                      qt_ref, rowst_ref, kb_ref, vt_ref, km_ref):
    d = x_ref.shape[1]
    xn = _rms(x_ref[...], g_ref[...]).astype(BF16)
    qk = _rope_t(_dot_nt(wqkt_ref[...], xn), cost_ref[...], sint_ref[...])
    qt_ref[...] = (qk[0:d] * SCALE_LOG2).astype(BF16)
    v_t = _dot_nt(wvt_ref[...], xn)
    rowst_ref[0:d, :] = qk[d:2 * d]
    rowst_ref[d:2 * d, :] = v_t
    vt_ref[...] = v_t.astype(BF16)
    k = _rope(_dot(xn, wk_ref[...]), cos_ref[...], sin_ref[...])
    kb_ref[...] = k.astype(BF16)
    km_ref[...] = jnp.sum(k, axis=0, keepdims=True) * (1.0 / MOBA_BLOCK)


def moba_in_t(x, g, w_k, wqk_t, wv_t, cos, sin, cos_t, sin_t, *, n_b):
    m, d = x.shape
    tm = MOBA_BLOCK
    nt = cos.shape[0] // tm
    row = lambda i: (i, 0)
    const = lambda i: (0, 0)
    one = pl.Buffered(1)
    return pl.pallas_call(
        _moba_in_t_kernel,
        grid=(m // tm,),
        in_specs=[pl.BlockSpec((tm, d), row), pl.BlockSpec((1, d), const),
                  pl.BlockSpec(w_k.shape, const, pipeline_mode=one), pl.BlockSpec(wqk_t.shape, const, pipeline_mode=one),
                  pl.BlockSpec(wv_t.shape, const, pipeline_mode=one),
                  pl.BlockSpec((tm, LANES), lambda i: (i % nt, 0)), pl.BlockSpec((tm, LANES), lambda i: (i % nt, 0)),
                  pl.BlockSpec((HEAD_DIM, tm), lambda i: (0, i % nt)), pl.BlockSpec((HEAD_DIM, tm), lambda i: (0, i % nt))],
        out_specs=[pl.BlockSpec((d, tm), lambda i: (0, i)),
                   pl.BlockSpec((None, 2 * d, tm), lambda i: (i // nt, 0, i % nt)), pl.BlockSpec((tm, d), row),
                   pl.BlockSpec((None, d, tm), lambda i: (i, 0, 0)), pl.BlockSpec((None, 1, d), lambda i: (i, 0, 0))],
        out_shape=[jax.ShapeDtypeStruct((d, m), BF16), jax.ShapeDtypeStruct((n_b, 2 * d, m // n_b), F32),
                   jax.ShapeDtypeStruct((m, d), BF16), jax.ShapeDtypeStruct((m // tm, d, tm), BF16),
                   jax.ShapeDtypeStruct((m // tm, 1, d), F32)],
        compiler_params=_params("parallel"),
        name="moba_in_t",
    )(x, g, w_k, wqk_t, wv_t, cos, sin, cos_t, sin_t)


def _moba_attn_kernel(qt_ref, k_ref, vt_ref, km_ref, ot_ref, s0_ref, s1_ref, *, tq, n_top):
    i = pl.program_id(2)
    c0 = i * tq
    nblk = km_ref.shape[0]
    pos = c0 + lax.broadcasted_iota(jnp.int32, (1, tq), 1)
    cur = pos // MOBA_BLOCK
    blk = lax.broadcasted_iota(jnp.int32, (nblk, tq), 0)
    m1, m2, m3 = _split3(km_ref[...])
    vrow = [slice(hh * HEAD_DIM, (hh + 1) * HEAD_DIM) for hh in range(2)]
    q_aug = []
    for hh in range(2):
        qp = _pad_pair(qt_ref[vrow[hh], :], hh == 1)
        gate = _dot(m1, qp) + _dot(m2, qp) + _dot(m3, qp)
        gate = jnp.where(blk < cur, gate, -jnp.inf)
        allow = (blk == cur) | (_topk_mask_t(gate, n_top) > 0.5)
        fill = jnp.zeros((LANES - nblk, tq), BF16)
        q_aug.append(jnp.concatenate([qp, jnp.where(allow, 0.0, NEG).astype(BF16), fill], axis=0))

    tile_id = lax.broadcasted_iota(jnp.int32, (MOBA_BLOCK, LANES), 1)

    def put(t, s_ref):
        k = k_ref[pl.ds(pl.multiple_of(t * MOBA_BLOCK, MOBA_BLOCK), MOBA_BLOCK), :]
        k_aug = jnp.concatenate([k, jnp.where(tile_id == t, 1.0, 0.0).astype(BF16)], axis=1)
        for hh in range(2):
            s_ref[hh] = _dot(k_aug, q_aug[hh])

    def consume(t, s_ref, carry, causal):
        bias = None
        if causal:
            kpos = t * MOBA_BLOCK + lax.broadcasted_iota(jnp.int32, (MOBA_BLOCK, tq), 0)
            bias = jnp.where(kpos <= pos, 0.0, NEG)
        return tuple(_online_update_t(carry[hh], s_ref[hh], bias, vt_ref[t, vrow[hh], :]) for hh in range(2))

    def body(u, carry):
        t = 2 * u
        put(t + 1, s1_ref)
        carry = consume(t, s0_ref, carry, False)
        put(t + 2, s0_ref)
        return consume(t + 1, s1_ref, carry, False)

    t_diag = c0 // MOBA_BLOCK
    put(0, s0_ref)
    res = lax.fori_loop(0, t_diag // 2, body, (_online_init_t(tq), _online_init_t(tq)))
    put(t_diag + 1, s1_ref)
    res = consume(t_diag, s0_ref, res, True)
    res = consume(t_diag + 1, s1_ref, res, True)
    for hh in range(2):
        ot_ref[vrow[hh], :] = _online_finish_t(res[hh]).astype(BF16)


def moba_attn(qt, kb, vt, kmean, *, n_b, t, tq):
    assert tq == 2 * MOBA_BLOCK, "the kernel's tile pairing assumes two MoBA blocks per query tile"
    nq = t // tq
    d = qt.shape[0]
    nblk = kmean.shape[1]
    n_top = min(MOBA_TOPK, t // MOBA_BLOCK)
    qmap = lambda b, hp, i: (hp, b * nq + i)
    kvmap = lambda b, hp, i: (b, 0, hp)
    return pl.pallas_call(
        functools.partial(_moba_attn_kernel, tq=tq, n_top=n_top),
        grid=(n_b, d // LANES, nq),
        in_specs=[pl.BlockSpec((LANES, tq), qmap), pl.BlockSpec((None, t, LANES), kvmap),
                  pl.BlockSpec((t // MOBA_BLOCK, LANES, MOBA_BLOCK), lambda b, hp, i: (b, hp, 0)),
                  pl.BlockSpec((None, nblk, LANES), kvmap)],
        out_specs=pl.BlockSpec((LANES, tq), qmap),
        out_shape=jax.ShapeDtypeStruct((d, n_b * t), BF16),
        scratch_shapes=[pltpu.VMEM((2, MOBA_BLOCK, tq), F32)] * 2,
        compiler_params=_params("parallel", "parallel", "arbitrary"),
        name="moba_attn",
    )(qt, kb.reshape(n_b, t, d), vt, kmean)


def _moba_s_select_kernel(pt_ref, *refs, pps, n_top, n_blocks):
    del pt_ref
    pages = refs[:pps]
    qbd_ref, idx_ref, km_ref = refs[pps:]
    s = pl.program_id(1)
    per_blk = MOBA_BLOCK // PAGE_SIZE
    blocks_per_step = pps // per_blk

    @pl.when(s == 0)
    def _():
        km_ref[...] = jnp.zeros(km_ref.shape, F32)

    lane = lax.broadcasted_iota(jnp.int32, km_ref.shape, 1)
    km = km_ref[...]
    for j in range(blocks_per_step):
        tot = pages[per_blk * j][...]
        for e in range(1, per_blk):
            tot = tot + pages[per_blk * j + e][...]
        mean = jnp.sum(tot, axis=1, keepdims=True) * (1.0 / MOBA_BLOCK)
        km = jnp.where(lane == s * blocks_per_step + j, mean, km)
    km_ref[...] = km

    @pl.when(s == pl.num_programs(1) - 1)
    def _():
        gate = _dot3_rhs(qbd_ref[...], km_ref[...])
        blk = lax.broadcasted_iota(jnp.int32, gate.shape, 1)
        gate = jnp.where(blk < n_blocks, gate, -jnp.inf)
        lane_f = blk.astype(F32)
        picks = jnp.zeros(gate.shape, F32)
        for r in range(n_top):
            m = jnp.max(gate, axis=-1, keepdims=True)
            idx = jnp.min(jnp.where(gate == m, lane_f, 1e9), axis=-1, keepdims=True)
            gate = jnp.where(lane_f == idx, -jnp.inf, gate)
            picks = jnp.where(blk == r, idx, picks)
        idx_ref[...] = picks.astype(jnp.int32)


def moba_s_select(cache_fm, pt_flat, qbd, *, n_pages, n_top):
    n_b, _, d = qbd.shape
    pps = min(16, n_pages)
    per_blk = MOBA_BLOCK // PAGE_SIZE
    page_map = lambda k: (lambda b, s, pt: (pt[b * n_pages + s * pps + k], 0, 0))
    grid_spec = pltpu.PrefetchScalarGridSpec(
        num_scalar_prefetch=1,
        grid=(n_b, n_pages // pps),
        in_specs=[pl.BlockSpec((None, d, PAGE_SIZE), page_map(k)) for k in range(pps)]
        + [pl.BlockSpec((None, LANES, d), lambda b, s, pt: (b, 0, 0))],
        out_specs=pl.BlockSpec((None, LANES, LANES), lambda b, s, pt: (b, 0, 0)),
        scratch_shapes=[pltpu.VMEM((d, LANES), F32)],
    )
    return pl.pallas_call(
        functools.partial(_moba_s_select_kernel, pps=pps, n_top=n_top, n_blocks=n_pages // per_blk),
        grid_spec=grid_spec,
        out_shape=jax.ShapeDtypeStruct((n_b, LANES, LANES), jnp.int32),
        compiler_params=_params("parallel", "arbitrary"),
        name="moba_s_select",
    )(pt_flat, *([cache_fm] * pps), qbd)


def _moba_s_attn_kernel(pt_ref, ix_ref, q_ref, *refs, n_pg):
    del pt_ref, ix_ref
    kblk = refs[:n_pg]
    vblk = refs[n_pg:2 * n_pg]
    k_new, v_new, o_ref = refs[2 * n_pg:]
    o_ref[...] = _attend_with_new(q_ref[...], [r[...] for r in kblk], [r[...] for r in vblk], [None] * n_pg,
                                  k_new[...], v_new[...])


def moba_s_attn(pt_flat, ix_flat, q4, cache_fm, k_new, v_new, *, n_pages, n_top, n_heads):
    n_b = q4.shape[0]
    per_blk = MOBA_BLOCK // PAGE_SIZE
    n_pg = n_top * per_blk

    def kv_map(r, row0):
        def f(b, h, pt, ix):
            j = ix[(b * n_heads + h) * n_top + r // per_blk]
            return (pt[b * n_pages + j * per_blk + r % per_blk], row0 + h, 0)
        return f

    new_spec = pl.BlockSpec((None, None, 1, HEAD_DIM), lambda b, h, pt, ix: (b, h, 0, 0))
    qo_spec = pl.BlockSpec((None, None, 8, HEAD_DIM), lambda b, h, pt, ix: (b, h, 0, 0))
    grid_spec = pltpu.PrefetchScalarGridSpec(
        num_scalar_prefetch=2,
        grid=(n_b, n_heads),
        in_specs=[qo_spec]
        + [pl.BlockSpec((None, HEAD_DIM, PAGE_SIZE), kv_map(r, 0)) for r in range(n_pg)]
        + [pl.BlockSpec((None, HEAD_DIM, PAGE_SIZE), kv_map(r, n_heads)) for r in range(n_pg)]
        + [new_spec, new_spec],
        out_specs=qo_spec,
    )
    return pl.pallas_call(
        functools.partial(_moba_s_attn_kernel, n_pg=n_pg),
        grid_spec=grid_spec,
        out_shape=jax.ShapeDtypeStruct((n_b, n_heads, 8, HEAD_DIM), F32),
        compiler_params=_params("parallel", "arbitrary"),
        name="moba_s_attn",
    )(pt_flat, ix_flat, q4, *([cache_fm] * (2 * n_pg)), k_new, v_new)


def _layer_norm_silu(y, g, b):
    yc = y - jnp.mean(y, axis=-1, keepdims=True)
    yn = yc * lax.rsqrt(jnp.mean(yc * yc, axis=-1, keepdims=True) + LN_EPS) * g + b
    return yn * _sigmoid(yn)


def _conf_kernel(x_ref, g_ref, w1_ref, wdw_ref, bdw_ref, lg_ref, lb_ref, w2_ref, o_ref, st_ref, ubuf_ref, sh_ref,
                 *, hist):
    tm, d = x_ref.shape
    width = wdw_ref.shape[0]

    @pl.when(pl.program_id(1) == 0)
    def _():
        ubuf_ref[0:hist, :] = jnp.zeros((hist, d), F32)

    x = x_ref[...]
    xn = _rms(x, g_ref[...]).astype(BF16)
    u = _dot(xn, w1_ref[:, 0:d]) * _sigmoid(_dot(xn, w1_ref[:, d:2 * d]))
    ubuf_ref[hist:hist + tm, :] = u
    base = hist - (width - 1)
    rows = ubuf_ref.shape[0]
    y = bdw_ref[...]
    for r in range(8):
        taps = [k for k in range(width) if (base + k) % 8 == r]
        if not taps:
            continue
        src = ubuf_ref
        if r:
            sh_ref[...] = pltpu.roll(ubuf_ref[...], rows - r, 0)
            src = sh_ref
        for k in taps:
            y = y + src[base + k - r:base + k - r + tm, :] * wdw_ref[k:k + 1, :]
    z = _layer_norm_silu(y, lg_ref[...], lb_ref[...])
    o_ref[...] = x + _dot(z.astype(BF16), w2_ref[...])
    tail = ubuf_ref[tm:tm + hist, :]
    st_ref[...] = tail
    ubuf_ref[0:hist, :] = tail


def conf_prompt(x, g, w1, wdw, bdw, lg, lb, w2, *, n_b, t, tm):
    d = x.shape[1]
    nt = t // tm
    hist = 32
    row = lambda b, i: (b * nt + i, 0)
    const = lambda b, i: (0, 0)
    return pl.pallas_call(
        functools.partial(_conf_kernel, hist=hist),
        grid=(n_b, nt),
        in_specs=[pl.BlockSpec((tm, d), row), pl.BlockSpec((1, d), const),
                  pl.BlockSpec((d, 2 * d), const, pipeline_mode=pl.Buffered(1)),
                  pl.BlockSpec(wdw.shape, const), pl.BlockSpec((1, d), const), pl.BlockSpec((1, d), const),
                  pl.BlockSpec((1, d), const), pl.BlockSpec((d, d), const, pipeline_mode=pl.Buffered(1))],
        out_specs=[pl.BlockSpec((tm, d), row), pl.BlockSpec((None, hist, d), lambda b, i: (b, 0, 0))],
        out_shape=[jax.ShapeDtypeStruct((n_b * t, d), F32), jax.ShapeDtypeStruct((n_b, hist, d), F32)],
        scratch_shapes=[pltpu.VMEM((hist + tm, d), F32)] * 2,
        compiler_params=_params("parallel", "arbitrary"),
        name="conf_prompt",
    )(x, g, w1, wdw, bdw, lg, lb, w2)


def _conf_s_kernel(x_ref, g_ref, w1_ref, wdw_ref, bdw_ref, lg_ref, lb_ref, w2_ref, past_ref, o_ref, u_ref):
    d = x_ref.shape[1]
    width = wdw_ref.shape[0]
    x = x_ref[...]
    xn = _rms(x, g_ref[...]).astype(BF16)
    u = _dot(xn, w1_ref[:, 0:d]) * _sigmoid(_dot(xn, w1_ref[:, d:2 * d]))
    y = bdw_ref[...] + past_ref[0] * wdw_ref[0:1, :]
    for k in range(1, width - 1):
        y = y + past_ref[k] * wdw_ref[k:k + 1, :]
    y = y + u * wdw_ref[width - 1:width, :]
    z = _layer_norm_silu(y, lg_ref[...], lb_ref[...])
    u_ref[...] = u
    o_ref[...] = x + _dot(z.astype(BF16), w2_ref[...])


def conf_sample(x, g, w1, wdw, bdw, lg, lb, w2, past_t):
    m, d = x.shape
    full = lambda a: pl.BlockSpec(a.shape, lambda i: (0,) * a.ndim)
    args = (x, g, w1, wdw, bdw, lg, lb, w2, past_t)
    return pl.pallas_call(
        _conf_s_kernel,
        grid=(1,),
        in_specs=[full(a) for a in args],
        out_specs=[pl.BlockSpec((m, d), lambda i: (0, 0))] * 2,
        out_shape=[jax.ShapeDtypeStruct((m, d), F32)] * 2,
        compiler_params=_params("arbitrary"),
        name="conf_sample",
    )(*args)


def _rope_tables(pos):
    half = HEAD_DIM // 2
    inv_freq = ROPE_THETA ** (-jnp.arange(half, dtype=F32) / half)
    ang = pos.astype(F32)[:, None] * inv_freq[None, :]
    cos = jnp.cos(ang)
    sin = jnp.sin(ang)
    cos = jnp.concatenate([cos, cos], axis=-1)
    sin = jnp.concatenate([-sin, sin], axis=-1)
    return jnp.tile(cos, (1, LANES // HEAD_DIM)), jnp.tile(sin, (1, LANES // HEAD_DIM))


def _overlap(n_cmp_rows, n_cols):
    i = jnp.arange(n_cmp_rows, dtype=jnp.int32)[:, None]
    j = jnp.arange(n_cols, dtype=jnp.int32)[None, :]
    start = i * CMP_STRIDE
    hit = (start <= j * SEL_BLOCK + (SEL_BLOCK - 1)) & (start + (CMP_BLOCK - 1) >= j * SEL_BLOCK)
    return hit.astype(BF16)


def _pad_heads_to_lane_pairs(x, rows):
    n, nh, r, hd = x.shape
    z = jnp.zeros_like(x)
    even = jnp.concatenate([x, z], axis=-1)
    odd = jnp.concatenate([z, x], axis=-1)
    is_even = (jnp.arange(nh) % 2 == 0)[None, :, None, None]
    out = jnp.where(is_even, even, odd)
    return jnp.pad(out, ((0, 0), (0, 0), (0, rows - r), (0, 0)))


def _take_lane_half(x, r):
    nh = x.shape[1]
    is_even = (jnp.arange(nh) % 2 == 0)[None, :, None, None]
    return jnp.where(is_even, x[:, :, :r, :HEAD_DIM], x[:, :, :r, HEAD_DIM:])


def _nsa_layer(hp, hs, g, cache_kv, cache_win, pt_flat, n_pages, w, *, n_b, t, n_s):
    w_in, pe_k, w1_k, w2_k, pe_v, w1_v, w2_v, w_out = w
    d = hp.shape[1]
    past_len = n_pages * PAGE_SIZE
    w_in_p = jnp.pad(w_in, ((0, 0), (0, 2688 - w_in.shape[1]))).astype(BF16)
    w_out_b = w_out.astype(BF16)
    half = CMP_BLOCK * HEAD_DIM // 2

    def cmp_weights(pe, w1, w2):
        return (pe[:CMP_STRIDE].reshape(1, half), pe[CMP_STRIDE:].reshape(1, half),
                w1[:half].astype(BF16), w1[half:].astype(BF16), w2.astype(BF16))

    wk = cmp_weights(pe_k, w1_k, w2_k)
    wv = cmp_weights(pe_v, w1_v, w2_v) + (w2_v.T.astype(BF16),)

    cos_p, sin_p = _rope_tables(jnp.arange(t, dtype=jnp.int32))
    cos_pt, sin_pt = cos_p[:, :HEAD_DIM].T, sin_p[:, :HEAD_DIM].T
    kv0, kv1, n_gate = 1024, 2560, NSA_KV_HEADS * NSA_GROUP * 3
    wq_t = w_in[:, :kv0].T.astype(BF16)
    wkv_t = w_in[:, kv0:kv0 + 1024].T.astype(BF16)
    wv_t = w_in[:, kv0 + 1280:kv1].T.astype(BF16)
    wg_t = jnp.pad(w_in[:, kv1:kv1 + n_gate].T, ((0, LANES - n_gate), (0, 0))).astype(BF16)
    w_rm = jnp.concatenate([w_in[:, kv0:kv0 + 768], w_in[:, kv0 + 1024:kv1]], axis=1).astype(BF16)
    tks = 256
    qt, rows_t, cmp_rows, kk, vt, win, gt = nsa_in_t(hp, g, w_rm, wq_t, wkv_t, wv_t, wg_t,
                                                     cos_p, sin_p, cos_pt, sin_pt, tm=tks, n_b=n_b)
    npg_p = t // PAGE_SIZE
    cend_p = jnp.arange(npg_p * 8, dtype=jnp.int32) * CMP_STRIDE + (CMP_BLOCK - 1)
    kc, _, vct = nsa_compress(cmp_rows.reshape(n_b * npg_p, PAGE_SIZE, 512), jnp.arange(n_b * npg_p, dtype=jnp.int32),
                              n_b, npg_p, wk, wv, *_rope_tables(cend_p), feature_major=False)
    nblk = -(-(t // SEL_BLOCK) // 16) * 16
    tile = jnp.arange(t // tks, dtype=jnp.int32)[:, None, None]
    key = jnp.arange(tks, dtype=jnp.int32)[None, :, None]
    blk = jnp.arange(LANES, dtype=jnp.int32)[None, None, :]
    emt = (blk == (tile * tks + key) // SEL_BLOCK).astype(BF16)
    ot = nsa_attn(qt, gt, kc, vct, kk, vt, _overlap(npg_p * 8, nblk).T, emt, n_b=n_b, t=t, tq=128)
    hp = mm_res_t(ot, w_out.T.astype(BF16), hp, tm=512)
    kv_p = jnp.transpose(rows_t.reshape(n_b, 4, NSA_KV_HEADS, HEAD_DIM, t), (0, 4, 1, 2, 3))
    keep = min(WINDOW, t)
    win_p = win.reshape(n_b, t, 2, NSA_KV_HEADS, HEAD_DIM)[:, t - keep:]

    cos_s, sin_s = _rope_tables(jnp.full((n_s,), past_len, jnp.int32))
    q_s, rows_s, kva_s, win_s, gates_s = nsa_in(hs, g, w_in_p, cos_s, sin_s, tm=n_s)
    cend_s = jnp.arange(n_pages * 8, dtype=jnp.int32) * CMP_STRIDE + (CMP_BLOCK - 1)
    cache_fm = jnp.transpose(cache_kv, (0, 2, 3, 4, 1)).reshape(cache_kv.shape[0], 1024, PAGE_SIZE)
    kc_s, vc_s, _ = nsa_compress(cache_fm, pt_flat, n_s, n_pages, wk, wv, *_rope_tables(cend_s), feature_major=True)
    n_sel = -(-(past_len + 1) // SEL_BLOCK)
    n_pick = min(SEL_TOPN, n_sel) - 1
    q4 = q_s.reshape(n_s, NSA_KV_HEADS, NSA_GROUP, HEAD_DIM)
    eye = jnp.eye(NSA_KV_HEADS, dtype=bool)[None, :, None, :, None]
    qbd = jnp.where(eye, q4[:, :, :, None, :], jnp.zeros((), BF16)).reshape(n_s, 16, 256)
    gm = (jnp.arange(8)[:, None] == jnp.arange(16)[None, :] // NSA_GROUP).astype(BF16)
    n_blk_pad = -(-n_sel // LANES) * LANES
    oc16, idx = nsa_s_sel(qbd, kc_s, vc_s, _overlap(n_pages * 8, n_blk_pad), gm, pos=past_len, n_pick=n_pick)
    oc5 = oc16.reshape(n_s, NSA_KV_HEADS, NSA_GROUP, NSA_KV_HEADS, HEAD_DIM)
    o_c = jnp.sum(jnp.where(eye, oc5, 0.0), axis=3).reshape(n_s, d)
    ix_flat = idx[:, :NSA_KV_HEADS, :n_pick].reshape(-1)
    q8 = jnp.pad(q4, ((0, 0), (0, 0), (0, 8 - NSA_GROUP), (0, 0)))
    win_fm = jnp.transpose(cache_win, (0, 2, 3, 4, 1)).reshape(n_s, 512, cache_win.shape[1])
    os_p, ow_p = nsa_s_attn(pt_flat, ix_flat, q8, cache_fm, kva_s.reshape(n_s, 16, 1, HEAD_DIM), win_fm,
                            n_pages=n_pages, n_pick=n_pick)
    o_s = os_p[:, :, :NSA_GROUP].reshape(n_s, d)
    o_w = ow_p[:, :, :NSA_GROUP].reshape(n_s, d)
    g3 = jnp.repeat(gates_s[:, :48].reshape(n_s, 16, 3), HEAD_DIM, axis=1)
    hs = nsa_out_s(o_c, o_s, o_w, g3[:, :, 0], g3[:, :, 1], g3[:, :, 2], w_out_b, hs)
    kv_s = rows_s.reshape(n_s, 1, 4, NSA_KV_HEADS, HEAD_DIM)
    win_new = win_s.reshape(n_s, 1, 2, NSA_KV_HEADS, HEAD_DIM)
    win_all = jnp.concatenate([cache_win, win_new], axis=1)
    win_s_out = win_all[:, win_all.shape[1] - cache_win.shape[1]:]
    return hp, hs, kv_p, kv_s, win_p, win_s_out


def _sconv_layer(hp, hs, g, state, w, *, n_b, t):
    w_in, w_conv, w_out = w
    w_in_b = w_in.astype(BF16)
    w_out_b = w_out.astype(BF16)
    hp, st = sconv_prompt(hp, g, w_in_b, w_conv, w_out_b, n_b=n_b, t=t, tm=256)
    st_p = st[:, 8 - (w_conv.shape[0] - 1):]
    hs, pre = sconv_sample(hs, g, w_in_b, w_conv, w_out_b, state[:, 0], state[:, 1])
    st_s = jnp.concatenate([state[:, 1:], pre[:, None, :]], axis=1)
    return hp, hs, st_p, st_s


def _moba_layer(hp, hs, g, cache_kv, pt_flat, n_pages, w, *, n_b, t, n_s):
    w_qkv, w_out = w
    d = hp.shape[1]
    n_heads = d // HEAD_DIM
    past_len = n_pages * PAGE_SIZE
    w_qkv_b = w_qkv.astype(BF16)
    w_out_b = w_out.astype(BF16)

    cos_p, sin_p = _rope_tables(jnp.arange(t, dtype=jnp.int32))
    cos_pt, sin_pt = cos_p[:, :HEAD_DIM].T, sin_p[:, :HEAD_DIM].T
    qt, rows_t, kb, vt, km = moba_in_t(hp, g, w_qkv_b[:, d:2 * d], w_qkv_b[:, :2 * d].T, w_qkv_b[:, 2 * d:].T,
                                       cos_p, sin_p, cos_pt, sin_pt, n_b=n_b)
    nblk = t // MOBA_BLOCK
    kmean = jnp.pad(km.reshape(n_b, nblk, d), ((0, 0), (0, -(-nblk // 16) * 16 - nblk), (0, 0)))
    ot = moba_attn(qt, kb, vt, kmean, n_b=n_b, t=t, tq=512)
    hp = mm_res_t(ot, w_out_b.T, hp, tm=512)
    kv_p = jnp.transpose(rows_t.reshape(n_b, 2, n_heads, HEAD_DIM, t), (0, 4, 1, 2, 3))

    cos_s, sin_s = _rope_tables(jnp.full((n_s,), past_len, jnp.int32))
    q_s, rows_s, kb_s, vb_s, _ = moba_in(hs, g, w_qkv_b, cos_s, sin_s, tm=n_s)
    cache_fm = jnp.transpose(cache_kv, (0, 2, 3, 4, 1)).reshape(cache_kv.shape[0], 2 * d, PAGE_SIZE)
    n_top = min(MOBA_TOPK, -(-(past_len + 1) // MOBA_BLOCK))
    qh = q_s.reshape(n_s, n_heads, 1, HEAD_DIM)
    eye = jnp.eye(n_heads, dtype=bool)[None, :, :, None]
    qbd = jnp.where(eye, qh, jnp.zeros((), BF16)).reshape(n_s, n_heads, d)
    qbd = jnp.pad(qbd, ((0, 0), (0, LANES - n_heads), (0, 0)))
    idx = moba_s_select(cache_fm, pt_flat, qbd, n_pages=n_pages, n_top=n_top)
    ix_flat = idx[:, :n_heads, :n_top].reshape(-1)
    q8 = jnp.pad(qh, ((0, 0), (0, 0), (0, 7), (0, 0)))
    o_p = moba_s_attn(pt_flat, ix_flat, q8, cache_fm, kb_s.reshape(n_s, n_heads, 1, HEAD_DIM),
                      vb_s.reshape(n_s, n_heads, 1, HEAD_DIM), n_pages=n_pages, n_top=n_top, n_heads=n_heads)
    o_s = o_p[:, :, 0].reshape(n_s, d).astype(BF16)
    hs = mm_res(o_s, w_out_b, hs, tm=n_s)
    kv_s = rows_s.reshape(n_s, 1, 2, n_heads, HEAD_DIM)
    return hp, hs, kv_p, kv_s


def _conf_layer(hp, hs, g, state, w, *, n_b, t):
    w_pw1, w_dw, b_dw, ln_g, ln_b, w_pw2 = w
    d = hp.shape[1]
    r = lambda a: a.reshape(1, d)
    args = (w_pw1.astype(BF16), w_dw, r(b_dw), r(ln_g), r(ln_b), w_pw2.astype(BF16))
    hp, st = conf_prompt(hp, g, *args, n_b=n_b, t=t, tm=256)
    st_p = st[:, st.shape[1] - (w_dw.shape[0] - 1):]
    hs, u = conf_sample(hs, g, *args, jnp.transpose(state, (1, 0, 2)))
    st_s = jnp.concatenate([state[:, 1:], u[:, None, :]], axis=1)
    return hp, hs, st_p, st_s


def kernel(x_prompt, x_sample, cache_nsa_kv, cache_nsa_win, state_sconv, cache_moba_kv, state_conformer,
           page_table, norm_mix, norm_ffn, norm_final, ffn_w_up, ffn_w_down,
           nsa_w_in, nsa_pe_k, nsa_w1_k, nsa_w2_k, nsa_pe_v, nsa_w1_v, nsa_w2_v, nsa_w_out,
           sconv_w_in, sconv_w_conv, sconv_w_out, moba_w_qkv, moba_w_out,
           conf_w_pw1, conf_w_dw, conf_b_dw, conf_ln_g, conf_ln_b, conf_w_pw2):
    n_b, t, d = x_prompt.shape
    n_s = x_sample.shape[0]
    depth = norm_mix.shape[0]
    n_pages = page_table.shape[1]
    pt_flat = page_table.reshape(-1).astype(jnp.int32)
    hp = x_prompt.reshape(n_b * t, d)
    hs = x_sample.reshape(n_s, d)
    outs = {k: [] for k in ("nsa_kv_p", "nsa_kv_s", "nsa_win_p", "nsa_win_s", "sconv_p", "sconv_s",
                            "moba_p", "moba_s", "conf_p", "conf_s")}
    for i in range(depth):
        kind, j = i % 4, i // 4
        g = norm_mix[i].reshape(1, d)
        if kind == 0:
            w = (nsa_w_in[j], nsa_pe_k[j], nsa_w1_k[j], nsa_w2_k[j], nsa_pe_v[j], nsa_w1_v[j], nsa_w2_v[j],
                 nsa_w_out[j])
            hp, hs, kv_p, kv_s, win_p, win_s = _nsa_layer(hp, hs, g, cache_nsa_kv[j], cache_nsa_win[j], pt_flat,
                                                          n_pages, w, n_b=n_b, t=t, n_s=n_s)
            outs["nsa_kv_p"].append(kv_p)
            outs["nsa_kv_s"].append(kv_s)
            outs["nsa_win_p"].append(win_p)
            outs["nsa_win_s"].append(win_s)
        elif kind == 1:
            hp, hs, st_p, st_s = _sconv_layer(hp, hs, g, state_sconv[j],
                                              (sconv_w_in[j], sconv_w_conv[j], sconv_w_out[j]), n_b=n_b, t=t)
            outs["sconv_p"].append(st_p)
            outs["sconv_s"].append(st_s)
        elif kind == 2:
            hp, hs, kv_p, kv_s = _moba_layer(hp, hs, g, cache_moba_kv[j], pt_flat, n_pages,
                                             (moba_w_qkv[j], moba_w_out[j]), n_b=n_b, t=t, n_s=n_s)
            outs["moba_p"].append(kv_p)
            outs["moba_s"].append(kv_s)
        else:
            w = (conf_w_pw1[j], conf_w_dw[j], conf_b_dw[j], conf_ln_g[j], conf_ln_b[j], conf_w_pw2[j])
            hp, hs, st_p, st_s = _conf_layer(hp, hs, g, state_conformer[j], w, n_b=n_b, t=t)
            outs["conf_p"].append(st_p)
            outs["conf_s"].append(st_s)
        gf = norm_ffn[i].reshape(1, d)
        wu = ffn_w_up[i].astype(BF16)
        wd = ffn_w_down[i].astype(BF16)
        final = i == depth - 1
        gfin = norm_final.reshape(1, d)
        hp = ffn(hp, gf, wu, wd, gfin, tm=512, final=final)
        hs = ffn(hs, gf, wu, wd, gfin, tm=n_s, final=final)
    return (hp.reshape(n_b, t, d), hs.reshape(n_s, 1, d),
            jnp.stack(outs["nsa_kv_p"]), jnp.stack(outs["nsa_kv_s"]),
            jnp.stack(outs["nsa_win_p"]), jnp.stack(outs["nsa_win_s"]),
            jnp.stack(outs["sconv_p"]), jnp.stack(outs["sconv_s"]),
            jnp.stack(outs["moba_p"]), jnp.stack(outs["moba_s"]),
            jnp.stack(outs["conf_p"]), jnp.stack(outs["conf_s"]))
```

```python
import functools

import jax
import jax.numpy as jnp
from jax import lax
from jax.experimental import pallas as pl
from jax.experimental.pallas import tpu as pltpu

F32 = jnp.float32
BF16 = jnp.bfloat16

HEAD_DIM = 64
ROPE_THETA = 10000.0
RMS_EPS = 1e-6
LN_EPS = 1e-5
NSA_KV_HEADS = 4
NSA_GROUP = 4
CMP_STRIDE = 16
CMP_BLOCK = 32
SEL_BLOCK = 64
SEL_TOPN = 16
WINDOW = 512
MOBA_BLOCK = 256
MOBA_TOPK = 3
PAGE_SIZE = 128
SCALE = HEAD_DIM ** -0.5
SCALE_LOG2 = SCALE * 1.4426950408889634

LANES = 128
NEG = -1e30
VMEM_LIMIT = 56 * 1024 * 1024


def _params(*sem):
    return pltpu.CompilerParams(dimension_semantics=sem, vmem_limit_bytes=VMEM_LIMIT)


def _dot(a, b):
    return jnp.dot(a, b, preferred_element_type=F32)


def _dot_nt(a, b):
    return lax.dot_general(a, b, (((1,), (1,)), ((), ())), preferred_element_type=F32)


def _split3(x):
    hi = x.astype(BF16)
    r = x - hi.astype(F32)
    mid = r.astype(BF16)
    lo = (r - mid.astype(F32)).astype(BF16)
    return hi, mid, lo


def _dot3(x, m):
    hi, mid, lo = _split3(x)
    return _dot(hi, m) + _dot(mid, m) + _dot(lo, m)


def _dot3_rhs(m, x):
    hi, mid, lo = _split3(x)
    return _dot(m, hi) + _dot(m, mid) + _dot(m, lo)


def _rms(x, g):
    return x * lax.rsqrt(jnp.mean(x * x, axis=-1, keepdims=True) + RMS_EPS) * g


def _sigmoid(x):
    return 1.0 / (1.0 + jnp.exp(-x))


def _rope(x, cos, sin):
    w = x.shape[-1]
    lane = lax.broadcasted_iota(jnp.int32, x.shape, 1)
    first = (lane % HEAD_DIM) < (HEAD_DIM // 2)
    rot = jnp.where(first, pltpu.roll(x, w - HEAD_DIM // 2, 1), pltpu.roll(x, HEAD_DIM // 2, 1))
    reps = w // LANES
    if reps > 1:
        cos = jnp.concatenate([cos] * reps, axis=1)
        sin = jnp.concatenate([sin] * reps, axis=1)
    return x * cos + rot * sin


def _masked_softmax(s, mask):
    s = jnp.where(mask, s, -jnp.inf)
    m = jnp.max(s, axis=-1, keepdims=True)
    m = jnp.where(m > -jnp.inf, m, 0.0)
    p = jnp.exp(s - m)
    return p / jnp.maximum(jnp.sum(p, axis=-1, keepdims=True), 1e-30)


def _rope_t(x, cos, sin):
    r = x.shape[0]
    row = lax.broadcasted_iota(jnp.int32, x.shape, 0)
    first = (row % HEAD_DIM) < (HEAD_DIM // 2)
    rot = jnp.where(first, pltpu.roll(x, r - HEAD_DIM // 2, 0), pltpu.roll(x, HEAD_DIM // 2, 0))
    reps = r // HEAD_DIM
    if reps > 1:
        cos = jnp.concatenate([cos] * reps, axis=0)
        sin = jnp.concatenate([sin] * reps, axis=0)
    return x * cos + rot * sin


def _topk_mask_t(score, k):
    row = lax.broadcasted_iota(jnp.int32, score.shape, 0).astype(F32)
    sel = jnp.zeros(score.shape, F32)
    for _ in range(k):
        m = jnp.max(score, axis=0, keepdims=True)
        idx = jnp.min(jnp.where(score == m, row, 1e9), axis=0, keepdims=True)
        hit = row == idx
        sel = jnp.where(hit & (m > -jnp.inf), 1.0, sel)
        score = jnp.where(hit, -jnp.inf, score)
    return sel


ONES_ROWS = 16


def _online_init_t(cols):
    return (jnp.full((1, cols), NEG, F32), jnp.zeros((HEAD_DIM + ONES_ROWS, cols), F32))


def _online_update_t(carry, s, bias, v_t):
    m, acc = carry
    sm = s if bias is None else s + bias
    m_new = jnp.maximum(m, jnp.max(sm, axis=0, keepdims=True))
    p = jnp.exp2(sm - m_new).astype(BF16)
    v_aug = jnp.concatenate([v_t, jnp.ones((ONES_ROWS, v_t.shape[1]), BF16)], axis=0)
    return m_new, jnp.exp2(m - m_new) * acc + _dot(v_aug, p)


def _online_finish_t(carry):
    _, acc = carry
    return acc[0:HEAD_DIM] * (1.0 / jnp.maximum(acc[HEAD_DIM:HEAD_DIM + 1], 1e-30))


def _pad_pair(q, odd):
    z = jnp.zeros_like(q)
    return jnp.concatenate([z, q] if odd else [q, z], axis=0)


def _ffn_kernel(x_ref, g_ref, wu_ref, wd_ref, gf_ref, o_ref, *, chunk, final):
    x = x_ref[...]
    xn = _rms(x, g_ref[...]).astype(BF16)
    acc = x
    for c in range(0, wu_ref.shape[1], chunk):
        u = _dot(xn, wu_ref[:, c:c + chunk])
        a = jnp.square(jnp.maximum(u, 0.0)).astype(BF16)
        acc = acc + _dot(a, wd_ref[c:c + chunk, :])
    if final:
        acc = _rms(acc, gf_ref[...])
    o_ref[...] = acc


def ffn(x, g, wu, wd, gf, *, tm, final):
    m, d = x.shape
    dff = wu.shape[1]
    row = lambda i: (i, 0)
    const = lambda i: (0, 0)
    return pl.pallas_call(
        functools.partial(_ffn_kernel, chunk=512, final=final),
        grid=(m // tm,),
        in_specs=[pl.BlockSpec((tm, d), row), pl.BlockSpec((1, d), const),
                  pl.BlockSpec((d, dff), const, pipeline_mode=pl.Buffered(1)),
                  pl.BlockSpec((dff, d), const, pipeline_mode=pl.Buffered(1)),
                  pl.BlockSpec((1, d), const)],
        out_specs=pl.BlockSpec((tm, d), row),
        out_shape=jax.ShapeDtypeStruct((m, d), F32),
        compiler_params=_params("parallel"),
        name="ffn",
    )(x, g, wu, wd, gf)


def _mm_res_kernel(x_ref, w_ref, r_ref, o_ref):
    o_ref[...] = r_ref[...] + _dot(x_ref[...], w_ref[...])


def mm_res(x, w, res, *, tm):
    m, k = x.shape
    n = w.shape[1]
    row = lambda i: (i, 0)
    return pl.pallas_call(
        _mm_res_kernel,
        grid=(m // tm,),
        in_specs=[pl.BlockSpec((tm, k), row), pl.BlockSpec((k, n), lambda i: (0, 0)),
                  pl.BlockSpec((tm, n), row)],
        out_specs=pl.BlockSpec((tm, n), row),
        out_shape=jax.ShapeDtypeStruct((m, n), F32),
        compiler_params=_params("parallel"),
        name="mm_res",
    )(x, w, res)


def _mm_res_t_kernel(xt_ref, wt_ref, r_ref, o_ref):
    o_ref[...] = r_ref[...] + _dot(wt_ref[...], xt_ref[...]).T


def mm_res_t(xt, wt, res, *, tm):
    k, m = xt.shape
    n = wt.shape[0]
    row = lambda i: (i, 0)
    return pl.pallas_call(
        _mm_res_t_kernel,
        grid=(m // tm,),
        in_specs=[pl.BlockSpec((k, tm), lambda i: (0, i)), pl.BlockSpec((n, k), lambda i: (0, 0)),
                  pl.BlockSpec((tm, n), row)],
        out_specs=pl.BlockSpec((tm, n), row),
        out_shape=jax.ShapeDtypeStruct((m, n), F32),
        compiler_params=_params("parallel"),
        name="mm_res_t",
    )(xt, wt, res)


def _nsa_in_kernel(x_ref, g_ref, w_ref, cos_ref, sin_ref, q_ref, rows_ref, kva_ref, win_ref, gate_ref):
    xn = _rms(x_ref[...], g_ref[...]).astype(BF16)
    cos = cos_ref[...]
    sin = sin_ref[...]
    q = _rope(_dot(xn, w_ref[:, 0:1024]), cos, sin) * SCALE
    q_ref[...] = q.astype(BF16)
    kv = _dot(xn, w_ref[:, 1024:2048])
    ks = _rope(kv[:, 512:768], cos, sin)
    rows_ref[:, 0:512] = kv[:, 0:512]
    rows_ref[:, 512:768] = ks
    rows_ref[:, 768:1024] = kv[:, 768:1024]
    wkv = _dot(xn, w_ref[:, 2048:2560])
    kw = _rope(wkv[:, 0:256], cos, sin)
    win_ref[:, 0:256] = kw
    win_ref[:, 256:512] = wkv[:, 256:512]
    kva_ref[:, 0:256] = ks.astype(BF16)
    kva_ref[:, 256:512] = kv[:, 768:1024].astype(BF16)
    kva_ref[:, 512:768] = kw.astype(BF16)
    kva_ref[:, 768:1024] = wkv[:, 256:512].astype(BF16)
    gate_ref[...] = _sigmoid(_dot(xn, w_ref[:, 2560:2688]))


def nsa_in(x, g, w, cos, sin, *, tm):
    m, d = x.shape
    nw = w.shape[1]
    nt = cos.shape[0] // tm
    row = lambda i: (i, 0)
    tab = lambda i: (i % nt, 0)
    const = lambda i: (0, 0)
    outs = [(1024, BF16), (1024, F32), (1024, BF16), (512, F32), (LANES, F32)]
    return pl.pallas_call(
        _nsa_in_kernel,
        grid=(m // tm,),
        in_specs=[pl.BlockSpec((tm, d), row), pl.BlockSpec((1, d), const),
                  pl.BlockSpec((d, nw), const, pipeline_mode=pl.Buffered(1)),
                  pl.BlockSpec((tm, LANES), tab), pl.BlockSpec((tm, LANES), tab)],
        out_specs=[pl.BlockSpec((tm, n), row) for n, _ in outs],
        out_shape=[jax.ShapeDtypeStruct((m, n), dt) for n, dt in outs],
        compiler_params=_params("parallel"),
        name="nsa_in",
    )(x, g, w, cos, sin)


def _nsa_in_t_kernel(x_ref, g_ref, w_ref, wqt_ref, wkvt_ref, wvt_ref, wgt_ref, cos_ref, sin_ref, cost_ref, sint_ref,
                     qt_ref, rowst_ref, cmp_ref, kk_ref, vt_ref, win_ref, gt_ref):
    xn = _rms(x_ref[...], g_ref[...]).astype(BF16)
    cos = cos_ref[...]
    sin = sin_ref[...]
    cos_t = cost_ref[...]
    sin_t = sint_ref[...]
    qt_ref[...] = (_rope_t(_dot_nt(wqt_ref[...], xn), cos_t, sin_t) * SCALE_LOG2).astype(BF16)
    rt = _dot_nt(wkvt_ref[...], xn)
    rowst_ref[0:512, :] = rt[0:512]
    rowst_ref[512:768, :] = _rope_t(rt[512:768], cos_t, sin_t)
    rowst_ref[768:1024, :] = rt[768:1024]
    vt_ref[0:256, :] = rt[768:1024].astype(BF16)
    vt_ref[256:512, :] = _dot_nt(wvt_ref[...], xn).astype(BF16)
    kv = _dot(xn, w_ref[:, 0:768])
    cmp_ref[...] = kv[:, 0:512]
    wkv = _dot(xn, w_ref[:, 768:1280])
    kw = _rope(wkv[:, 0:256], cos, sin)
    win_ref[:, 0:256] = kw
    win_ref[:, 256:512] = wkv[:, 256:512]
    kk_ref[:, 0:256] = _rope(kv[:, 512:768], cos, sin).astype(BF16)
    kk_ref[:, 256:512] = kw.astype(BF16)
    gt_ref[...] = _sigmoid(_dot_nt(wgt_ref[...], xn))


def nsa_in_t(x, g, w_rm, wq_t, wkv_t, wv_t, wg_t, cos, sin, cos_t, sin_t, *, tm, n_b):
    m, d = x.shape
    nt = cos.shape[0] // tm
    row = lambda i: (i, 0)
    col = lambda i: (0, i)
    const = lambda i: (0, 0)
    one = pl.Buffered(1)
    weights = (w_rm, wq_t, wkv_t, wv_t, wg_t)
    return pl.pallas_call(
        _nsa_in_t_kernel,
        grid=(m // tm,),
        in_specs=[pl.BlockSpec((tm, d), row), pl.BlockSpec((1, d), const)]
        + [pl.BlockSpec(w.shape, const, pipeline_mode=one) for w in weights]
        + [pl.BlockSpec((tm, LANES), lambda i: (i % nt, 0)), pl.BlockSpec((tm, LANES), lambda i: (i % nt, 0)),
           pl.BlockSpec((HEAD_DIM, tm), lambda i: (0, i % nt)), pl.BlockSpec((HEAD_DIM, tm), lambda i: (0, i % nt))],
        out_specs=[pl.BlockSpec((1024, tm), col), pl.BlockSpec((None, 1024, tm), lambda i: (i // nt, 0, i % nt)),
                   pl.BlockSpec((tm, 512), row), pl.BlockSpec((tm, 512), row),
                   pl.BlockSpec((None, 512, tm), lambda i: (i, 0, 0)), pl.BlockSpec((tm, 512), row),
                   pl.BlockSpec((LANES, tm), col)],
        out_shape=[jax.ShapeDtypeStruct((1024, m), BF16), jax.ShapeDtypeStruct((n_b, 1024, m // n_b), F32),
                   jax.ShapeDtypeStruct((m, 512), F32), jax.ShapeDtypeStruct((m, 512), BF16),
                   jax.ShapeDtypeStruct((m // tm, 512, tm), BF16), jax.ShapeDtypeStruct((m, 512), F32),
                   jax.ShapeDtypeStruct((LANES, m), F32)],
        compiler_params=_params("parallel"),
        name="nsa_in_t",
    )(x, g, *weights, cos, sin, cos_t, sin_t)


def _compress_kernel(pt_ref, *refs, pps, feature_major):
    del pt_ref
    pages = refs[:pps + 1]
    (pelo_k, pehi_k, w1lo_k, w1hi_k, w2_k, pelo_v, pehi_v, w1lo_v, w1hi_v, w2_v, w2t_v,
     cos_ref, sin_ref, kc_ref, vc_ref, vct_ref, xs_ref) = refs[pps + 1:]
    npc = pps * 8
    rows = (pps + 1) * 8
    low = lax.broadcasted_iota(jnp.int32, (rows, LANES), 1) < HEAD_DIM
    for k, pg in enumerate(pages):
        x = pg[...]
        if feature_major:
            x = x.T
        for lb in range(4):
            xs_ref[lb, k * PAGE_SIZE:(k + 1) * PAGE_SIZE, :] = x[:, lb * LANES:(lb + 1) * LANES]
    streams = ((pelo_k, pehi_k, w1lo_k, w1hi_k, w2_k, kc_ref), (pelo_v, pehi_v, w1lo_v, w1hi_v, w2_v, vc_ref))
    for s, (pelo, pehi, w1lo, w1hi, w2, out_ref) in enumerate(streams):
        heads = [[] for _ in range(NSA_KV_HEADS)]
        for lb in range(2):
            for q in range(CMP_STRIDE // 2):
                a = xs_ref[2 * s + lb, pl.ds(2 * q, rows, stride=CMP_STRIDE), :]
                b = xs_ref[2 * s + lb, pl.ds(2 * q + 1, rows, stride=CMP_STRIDE), :]
                heads[2 * lb].append(jnp.where(low, a, pltpu.roll(b, HEAD_DIM, 1)))
                heads[2 * lb + 1].append(jnp.where(low, pltpu.roll(a, HEAD_DIM, 1), b))
        x = jnp.concatenate([jnp.concatenate(hh, axis=1) for hh in heads], axis=0)
        first = _dot((x + pelo[...]).astype(BF16), w1lo[...])
        second = _dot((x + pehi[...]).astype(BF16), w1hi[...])
        outs = []
        outs_t = []
        for h in range(NSA_KV_HEADS):
            pre = first[h * rows:h * rows + npc] + second[h * rows + 1:h * rows + 1 + npc]
            hid = (pre * _sigmoid(pre)).astype(BF16)
            outs.append(_dot(hid, w2[...]))
            if s == 1:
                outs_t.append(_dot_nt(w2t_v[...], hid))
        res = jnp.concatenate(outs, axis=1)
        if s == 0:
            res = _rope(res, cos_ref[...], sin_ref[...])
        else:
            vct_ref[...] = jnp.concatenate(outs_t, axis=0).astype(BF16)
        out_ref[...] = res.astype(BF16)


def nsa_compress(rows3d, pt_flat, n_seq, n_pages, wk, wv, cos_c, sin_c, *, feature_major):
    pps = min(16, n_pages)
    steps = n_pages // pps
    npc = pps * 8
    page_block = (None, 512, PAGE_SIZE) if feature_major else (None, PAGE_SIZE, 512)

    def page_map(k):
        return lambda b, s, pt: (pt[b * n_pages + jnp.minimum(s * pps + k, n_pages - 1)], 0, 0)

    const2 = lambda b, s, pt: (0, 0)
    wspecs = []
    for _ in range(2):
        wspecs += [pl.BlockSpec((1, 1024), const2), pl.BlockSpec((1, 1024), const2),
                   pl.BlockSpec((1024, 256), const2), pl.BlockSpec((1024, 256), const2),
                   pl.BlockSpec((256, HEAD_DIM), const2)]
    wspecs.append(pl.BlockSpec((HEAD_DIM, 256), const2))
    grid_spec = pltpu.PrefetchScalarGridSpec(
        num_scalar_prefetch=1,
        grid=(n_seq, steps),
        in_specs=[pl.BlockSpec(page_block, page_map(k)) for k in range(pps + 1)] + wspecs
        + [pl.BlockSpec((npc, LANES), lambda b, s, pt: (s, 0))] * 2,
        out_specs=[pl.BlockSpec((None, npc, 256), lambda b, s, pt: (b, s, 0))] * 2
        + [pl.BlockSpec((None, 256, npc), lambda b, s, pt: (b, 0, s))],
        scratch_shapes=[pltpu.VMEM((4, (pps + 1) * PAGE_SIZE, LANES), F32)],
    )
    return pl.pallas_call(
        functools.partial(_compress_kernel, pps=pps, feature_major=feature_major),
        grid_spec=grid_spec,
        out_shape=[jax.ShapeDtypeStruct((n_seq, n_pages * 8, 256), BF16)] * 2
        + [jax.ShapeDtypeStruct((n_seq, 256, n_pages * 8), BF16)],
        compiler_params=_params("parallel", "arbitrary"),
        name="nsa_compress",
    )(pt_flat, *([rows3d] * (pps + 1)), *wk, *wv, cos_c, sin_c)


def _nsa_attn_kernel(qt_ref, gt_ref, kc_ref, vct_ref, kk_ref, vt_ref, ovlt_ref, emt_ref, ot_ref, s0_ref, s1_ref, *,
                     tq, tks, n_top, nblk):
    i = pl.program_id(1)
    c0 = i * tq
    n4 = NSA_GROUP * tq
    pos1 = c0 + lax.broadcasted_iota(jnp.int32, (1, tq), 1)
    pos4 = c0 + lax.broadcasted_iota(jnp.int32, (1, n4), 1) % tq
    blk = lax.broadcasted_iota(jnp.int32, (nblk, tq), 0)
    cur = pos1 // SEL_BLOCK
    heads = range(NSA_KV_HEADS)
    pair = [slice((h // 2) * LANES, (h // 2 + 1) * LANES) for h in heads]
    vrow = [slice(h * HEAD_DIM, (h + 1) * HEAD_DIM) for h in heads]
    qpad = []
    for h in heads:
        q4 = jnp.concatenate(
            [qt_ref[(NSA_GROUP * h + g) * HEAD_DIM:(NSA_GROUP * h + g + 1) * HEAD_DIM, :] for g in range(NSA_GROUP)],
            axis=1)
        qpad.append(_pad_pair(q4, h % 2 == 1))

    o_c, q_aug = [], []
    for h in heads:
        s = _dot(kc_ref[:, pair[h]], qpad[h])
        cend = lax.broadcasted_iota(jnp.int32, s.shape, 0) * CMP_STRIDE + (CMP_BLOCK - 1)
        s = jnp.where(cend <= pos4, s, -jnp.inf)
        m = jnp.max(s, axis=0, keepdims=True)
        p = jnp.exp2(s - jnp.where(m > -jnp.inf, m, 0.0))
        p = p * (1.0 / jnp.maximum(jnp.sum(p, axis=0, keepdims=True), 1e-30))
        o_c.append(_dot(vct_ref[vrow[h], :], p.astype(BF16)))
        psum = p[:, 0:tq] + p[:, tq:2 * tq] + p[:, 2 * tq:3 * tq] + p[:, 3 * tq:4 * tq]
        imp = _dot3_rhs(ovlt_ref[...], psum)
        forced = (blk == 0) | (blk == cur) | (blk == cur - 1)
        imp = jnp.where(forced, jnp.inf, imp)
        imp = jnp.where(blk <= cur, imp, -jnp.inf)
        sel = _topk_mask_t(imp, n_top)
        sel_bias = jnp.where(sel > 0.5, 0.0, NEG).astype(BF16)
        sel_bias = jnp.concatenate([sel_bias] * NSA_GROUP, axis=1)
        fill = jnp.zeros((LANES - nblk, n4), BF16)
        q_aug.append(jnp.concatenate([qpad[h], sel_bias, fill], axis=0))

    def finish(carry):
        return [_online_finish_t(c) for c in carry]

    init = tuple(_online_init_t(n4) for _ in heads)
    t_last = (c0 + tq - 1) // tks

    n_tiles = vt_ref.shape[0]

    def run(t_first, scores, consume):
        def put(t, s_ref):
            for h, s in enumerate(scores(jnp.minimum(t, n_tiles - 1))):
                s_ref[h] = s

        def body(u, carry):
            t = t_first + 2 * u
            put(t + 1, s1_ref)
            carry = consume(t, s0_ref, carry, False)
            put(t + 2, s0_ref)
            return consume(t + 1, s1_ref, carry, False)

        put(t_first, s0_ref)
        pairs = (t_last - t_first) // 2
        carry = lax.fori_loop(0, pairs, body, init)
        t = t_first + 2 * pairs
        put(t + 1, s1_ref)
        carry = consume(t, s0_ref, carry, True)
        return finish(consume(t + 1, s1_ref, carry, True))

    def sel_scores(t):
        k0 = pl.multiple_of(t * tks, tks)
        e = emt_ref[t]
        return [_dot(jnp.concatenate([kk_ref[pl.ds(k0, tks), pair[h]], e], axis=1), q_aug[h]) for h in heads]

    def sel_consume(t, s_ref, carry, masked):
        bias = None
        if masked:
            kpos = t * tks + lax.broadcasted_iota(jnp.int32, (tks, n4), 0)
            bias = jnp.where(kpos <= pos4, 0.0, NEG)
        tv = jnp.minimum(t, n_tiles - 1)
        return tuple(_online_update_t(carry[h], s_ref[h], bias, vt_ref[tv, vrow[h], :]) for h in heads)

    o_s = run(0, sel_scores, sel_consume)

    def win_put(t, s_ref):
        k0 = pl.multiple_of(jnp.maximum(t, 0) * tks, tks)
        for h in heads:
            s_ref[h] = _dot(kk_ref[pl.ds(k0, tks), 256 + pair[h].start:256 + pair[h].stop], qpad[h])

    def win_consume(t, s_ref, carry):
        tv = jnp.maximum(t, 0)
        diff = pos4 - (tv * tks + lax.broadcasted_iota(jnp.int32, (tks, n4), 0))
        bias = jnp.where((diff >= 0) & (diff <= WINDOW) & (t >= 0), 0.0, NEG)
        return tuple(_online_update_t(carry[h], s_ref[h], bias,
                                      vt_ref[tv, 256 + vrow[h].start:256 + vrow[h].stop, :]) for h in heads)

    n_win = -(-WINDOW // tks) + 1
    bufs = (s0_ref, s1_ref)
    carry = init
    win_put(t_last - (n_win - 1), bufs[0])
    for e in range(n_win):
        if e + 1 < n_win:
            win_put(t_last - (n_win - 2 - e), bufs[(e + 1) % 2])
        carry = win_consume(t_last - (n_win - 1 - e), bufs[e % 2], carry)
    o_w = finish(carry)

    for h in heads:
        for g in range(NSA_GROUP):
            j = (h * NSA_GROUP + g) * 3
            cs = slice(g * tq, (g + 1) * tq)
            o = (gt_ref[j:j + 1, :] * o_c[h][:, cs] + gt_ref[j + 1:j + 2, :] * o_s[h][:, cs]
                 + gt_ref[j + 2:j + 3, :] * o_w[h][:, cs])
            ot_ref[(NSA_GROUP * h + g) * HEAD_DIM:(NSA_GROUP * h + g + 1) * HEAD_DIM, :] = o.astype(BF16)


def nsa_attn(qt, gt, kc, vct, kk, vt, ovlt, emt, *, n_b, t, tq):
    nq = t // tq
    tks = vt.shape[2]
    assert tks % tq == 0, "a query tile must sit inside one key tile"
    npiece = kc.shape[1]
    nblk = ovlt.shape[0]
    n_top = min(SEL_TOPN, t // SEL_BLOCK)
    col = lambda b, i: (0, b * nq + i)
    per_b = lambda b, i: (b, 0, 0)
    return pl.pallas_call(
        functools.partial(_nsa_attn_kernel, tq=tq, tks=tks, n_top=n_top, nblk=nblk),
        grid=(n_b, nq),
        in_specs=[pl.BlockSpec((1024, tq), col), pl.BlockSpec((LANES, tq), col),
                  pl.BlockSpec((None, npiece, 256), per_b), pl.BlockSpec((None, 256, npiece), per_b),
                  pl.BlockSpec((None, t, 512), per_b), pl.BlockSpec((t // tks, 512, tks), per_b),
                  pl.BlockSpec(ovlt.shape, lambda b, i: (0, 0)),
                  pl.BlockSpec(emt.shape, lambda b, i: (0, 0, 0))],
        out_specs=pl.BlockSpec((1024, tq), col),
        out_shape=jax.ShapeDtypeStruct((1024, n_b * t), BF16),
        scratch_shapes=[pltpu.VMEM((NSA_KV_HEADS, tks, NSA_GROUP * tq), F32)] * 2,
        compiler_params=_params("parallel", "arbitrary"),
        name="nsa_attn",
    )(qt, gt, kc, vct, kk.reshape(n_b, t, 512), vt, ovlt, emt)


def _nsa_s_sel_kernel(q_ref, kc_ref, vc_ref, ovl_ref, gm_ref, oc_ref, idx_ref, *, pos, n_pick):
    s = _dot_nt(q_ref[...], kc_ref[...])
    cend = lax.broadcasted_iota(jnp.int32, s.shape, 1) * CMP_STRIDE + (CMP_BLOCK - 1)
    p = _masked_softmax(s, cend <= pos)
    oc_ref[...] = _dot(p.astype(BF16), vc_ref[...])
    psum = _dot3_rhs(gm_ref[...], p)
    imp = _dot3(psum, ovl_ref[...])
    blk = lax.broadcasted_iota(jnp.int32, imp.shape, 1)
    cur = pos // SEL_BLOCK
    imp = jnp.where((blk == 0) | (blk == cur - 1), jnp.inf, imp)
    imp = jnp.where(blk < cur, imp, -jnp.inf)
    lane = blk.astype(F32)
    slot = lax.broadcasted_iota(jnp.int32, (8, LANES), 1)
    picks = jnp.zeros((8, LANES), F32)
    for r in range(n_pick):
        m = jnp.max(imp, axis=-1, keepdims=True)
        idx = jnp.min(jnp.where(imp == m, lane, 1e9), axis=-1, keepdims=True)
        imp = jnp.where(lane == idx, -jnp.inf, imp)
        picks = jnp.where(slot == r, idx, picks)
    idx_ref[...] = picks.astype(jnp.int32)


def nsa_s_sel(qbd, kc, vc, ovl, gm, *, pos, n_pick):
    n_b, _, npiece = kc.shape[0], None, kc.shape[1]
    per_b = lambda b: (b, 0, 0)
    return pl.pallas_call(
        functools.partial(_nsa_s_sel_kernel, pos=pos, n_pick=n_pick),
        grid=(n_b,),
        in_specs=[pl.BlockSpec((None, 16, 256), per_b), pl.BlockSpec((None, npiece, 256), per_b),
                  pl.BlockSpec((None, npiece, 256), per_b),
                  pl.BlockSpec(ovl.shape, lambda b: (0, 0)), pl.BlockSpec(gm.shape, lambda b: (0, 0))],
        out_specs=[pl.BlockSpec((None, 16, 256), per_b), pl.BlockSpec((None, 8, LANES), per_b)],
        out_shape=[jax.ShapeDtypeStruct((n_b, 16, 256), F32), jax.ShapeDtypeStruct((n_b, 8, LANES), jnp.int32)],
        compiler_params=_params("parallel"),
        name="nsa_s_sel",
    )(qbd, kc, vc, ovl, gm)


def _attend_with_new(q, kts, vts, biases, k_new, v_new):
    ss = []
    for kt, bias in zip(kts, biases):
        s = _dot(q, kt.astype(BF16))
        ss.append(s if bias is None else s + bias)
    s = ss[0] if len(ss) == 1 else jnp.concatenate(ss, axis=1)
    s_new = jnp.sum(q.astype(F32) * k_new.astype(F32), axis=-1, keepdims=True)
    m = jnp.maximum(jnp.max(s, axis=-1, keepdims=True), s_new)
    p = jnp.exp(s - m)
    p_new = jnp.exp(s_new - m)
    l = jnp.sum(p, axis=-1, keepdims=True) + p_new
    acc = p_new * v_new.astype(F32)
    off = 0
    for vt in vts:
        n = vt.shape[1]
        acc = acc + _dot_nt(p[:, off:off + n].astype(BF16), vt.astype(BF16))
        off += n
    return acc / l


def _nsa_s_attn_kernel(pt_ref, ix_ref, q_ref, *refs, n_pick):
    del pt_ref
    kblk = refs[:n_pick]
    vblk = refs[n_pick:2 * n_pick]
    ks_new, vs_new, kw_new, vw_new, kw_ref, vw_ref, os_ref, ow_ref = refs[2 * n_pick:]
    base = (pl.program_id(0) * NSA_KV_HEADS + pl.program_id(1)) * n_pick
    half = lax.broadcasted_iota(jnp.int32, (1, PAGE_SIZE), 1) // SEL_BLOCK
    biases = [jnp.where(half == ix_ref[base + r] % 2, 0.0, NEG) for r in range(n_pick)]
    q = q_ref[...]
    os_ref[...] = _attend_with_new(q, [r[...] for r in kblk], [r[...] for r in vblk], biases, ks_new[...], vs_new[...])
    ow_ref[...] = _attend_with_new(q, [kw_ref[...]], [vw_ref[...]], [None], kw_new[...], vw_new[...])


def nsa_s_attn(pt_flat, ix_flat, q4, cache_fm, new_rows, win_fm, *, n_pages, n_pick):
    n_b = q4.shape[0]
    nwin = win_fm.shape[2]
    per_page = PAGE_SIZE // SEL_BLOCK

    def kv_map(r, stream):
        def f(b, h, pt, ix):
            j = ix[(b * NSA_KV_HEADS + h) * n_pick + r]
            return (pt[b * n_pages + j // per_page], stream * NSA_KV_HEADS + h, 0)
        return f

    new_map = lambda s: (lambda b, h, pt, ix: (b, s * NSA_KV_HEADS + h, 0, 0))
    qo_spec = pl.BlockSpec((None, None, 8, HEAD_DIM), lambda b, h, pt, ix: (b, h, 0, 0))
    grid_spec = pltpu.PrefetchScalarGridSpec(
        num_scalar_prefetch=2,
        grid=(n_b, NSA_KV_HEADS),
        in_specs=[qo_spec]
        + [pl.BlockSpec((None, HEAD_DIM, PAGE_SIZE), kv_map(r, 2)) for r in range(n_pick)]
        + [pl.BlockSpec((None, HEAD_DIM, PAGE_SIZE), kv_map(r, 3)) for r in range(n_pick)]
        + [pl.BlockSpec((None, None, 1, HEAD_DIM), new_map(s)) for s in range(4)]
        + [pl.BlockSpec((None, HEAD_DIM, nwin), lambda b, h, pt, ix: (b, h, 0)),
           pl.BlockSpec((None, HEAD_DIM, nwin), lambda b, h, pt, ix: (b, NSA_KV_HEADS + h, 0))],
        out_specs=[qo_spec, qo_spec],
    )
    return pl.pallas_call(
        functools.partial(_nsa_s_attn_kernel, n_pick=n_pick),
        grid_spec=grid_spec,
        out_shape=[jax.ShapeDtypeStruct((n_b, NSA_KV_HEADS, 8, HEAD_DIM), F32)] * 2,
        compiler_params=_params("parallel", "arbitrary"),
        name="nsa_s_attn",
    )(pt_flat, ix_flat, q4, *([cache_fm] * (2 * n_pick)), *([new_rows] * 4), win_fm, win_fm)


def _nsa_out_s_kernel(oc_ref, os_ref, ow_ref, g0_ref, g1_ref, g2_ref, w_ref, r_ref, o_ref):
    o = g0_ref[...] * oc_ref[...] + g1_ref[...] * os_ref[...] + g2_ref[...] * ow_ref[...]
    o_ref[...] = r_ref[...] + _dot(o.astype(BF16), w_ref[...])


def nsa_out_s(oc, osel, ow, g0, g1, g2, w, res):
    m, d = res.shape
    full = pl.BlockSpec((m, d), lambda i: (0, 0))
    return pl.pallas_call(
        _nsa_out_s_kernel,
        grid=(1,),
        in_specs=[full] * 6 + [pl.BlockSpec(w.shape, lambda i: (0, 0)), full],
        out_specs=full,
        out_shape=jax.ShapeDtypeStruct((m, d), F32),
        compiler_params=_params("arbitrary"),
        name="nsa_out_s",
    )(oc, osel, ow, g0, g1, g2, w, res)


def _sconv_kernel(x_ref, g_ref, win_ref, wc_ref, wout_ref, o_ref, st_ref, carry_ref):
    d = x_ref.shape[1]
    tm = x_ref.shape[0]

    @pl.when(pl.program_id(1) == 0)
    def _():
        carry_ref[...] = jnp.zeros(carry_ref.shape, F32)

    x = x_ref[...]
    xn = _rms(x, g_ref[...]).astype(BF16)
    b_gate = _dot(xn, win_ref[:, 0:d])
    pre = _dot(xn, win_ref[:, d:2 * d]) * _dot(xn, win_ref[:, 2 * d:3 * d])
    row = lax.broadcasted_iota(jnp.int32, (tm, d), 0)
    back1 = jnp.where(row == 0, carry_ref[7:8, :], pltpu.roll(pre, 1, 0))
    back2 = jnp.where(row == 0, carry_ref[6:7, :], jnp.where(row == 1, carry_ref[7:8, :], pltpu.roll(pre, 2, 0)))
    y = back2 * wc_ref[0:1, :] + back1 * wc_ref[1:2, :] + pre * wc_ref[2:3, :]
    tail = pre[tm - 8:tm]
    carry_ref[...] = tail
    st_ref[...] = tail
    o_ref[...] = x + _dot((b_gate * y).astype(BF16), wout_ref[...])


def sconv_prompt(x, g, w_in, w_conv, w_out, *, n_b, t, tm):
    d = x.shape[1]
    nt = t // tm
    row = lambda b, i: (b * nt + i, 0)
    const = lambda b, i: (0, 0)
    return pl.pallas_call(
        _sconv_kernel,
        grid=(n_b, nt),
        in_specs=[pl.BlockSpec((tm, d), row), pl.BlockSpec((1, d), const),
                  pl.BlockSpec((d, 3 * d), const, pipeline_mode=pl.Buffered(1)),
                  pl.BlockSpec(w_conv.shape, const),
                  pl.BlockSpec((d, d), const, pipeline_mode=pl.Buffered(1))],
        out_specs=[pl.BlockSpec((tm, d), row), pl.BlockSpec((None, 8, d), lambda b, i: (b, 0, 0))],
        out_shape=[jax.ShapeDtypeStruct((n_b * t, d), F32), jax.ShapeDtypeStruct((n_b, 8, d), F32)],
        scratch_shapes=[pltpu.VMEM((8, d), F32)],
        compiler_params=_params("parallel", "arbitrary"),
        name="sconv_prompt",
    )(x, g, w_in, w_conv, w_out)


def _sconv_s_kernel(x_ref, g_ref, win_ref, wc_ref, wout_ref, p0_ref, p1_ref, o_ref, pre_ref):
    d = x_ref.shape[1]
    x = x_ref[...]
    xn = _rms(x, g_ref[...]).astype(BF16)
    b_gate = _dot(xn, win_ref[:, 0:d])
    pre = _dot(xn, win_ref[:, d:2 * d]) * _dot(xn, win_ref[:, 2 * d:3 * d])
    y = p0_ref[...] * wc_ref[0:1, :] + p1_ref[...] * wc_ref[1:2, :] + pre * wc_ref[2:3, :]
    pre_ref[...] = pre
    o_ref[...] = x + _dot((b_gate * y).astype(BF16), wout_ref[...])


def sconv_sample(x, g, w_in, w_conv, w_out, past0, past1):
    m, d = x.shape
    full = lambda a: pl.BlockSpec(a.shape, lambda i: (0,) * a.ndim)
    args = (x, g, w_in, w_conv, w_out, past0, past1)
    return pl.pallas_call(
        _sconv_s_kernel,
        grid=(1,),
        in_specs=[full(a) for a in args],
        out_specs=[pl.BlockSpec((m, d), lambda i: (0, 0))] * 2,
        out_shape=[jax.ShapeDtypeStruct((m, d), F32)] * 2,
        compiler_params=_params("arbitrary"),
        name="sconv_sample",
    )(*args)


def _moba_in_kernel(x_ref, g_ref, w_ref, cos_ref, sin_ref, q_ref, rows_ref, kb_ref, vb_ref, km_ref):
    d = x_ref.shape[1]
    xn = _rms(x_ref[...], g_ref[...]).astype(BF16)
    cos = cos_ref[...]
    sin = sin_ref[...]
    q_ref[...] = (_rope(_dot(xn, w_ref[:, 0:d]), cos, sin) * SCALE).astype(BF16)
    k = _rope(_dot(xn, w_ref[:, d:2 * d]), cos, sin)
    v = _dot(xn, w_ref[:, 2 * d:3 * d])
    rows_ref[:, 0:d] = k
    rows_ref[:, d:2 * d] = v
    kb_ref[...] = k.astype(BF16)
    vb_ref[...] = v.astype(BF16)
    km_ref[...] = jnp.sum(k, axis=0, keepdims=True) * (1.0 / MOBA_BLOCK)


def moba_in(x, g, w, cos, sin, *, tm):
    m, d = x.shape
    nt = cos.shape[0] // tm
    row = lambda i: (i, 0)
    tab = lambda i: (i % nt, 0)
    const = lambda i: (0, 0)
    return pl.pallas_call(
        _moba_in_kernel,
        grid=(m // tm,),
        in_specs=[pl.BlockSpec((tm, d), row), pl.BlockSpec((1, d), const),
                  pl.BlockSpec((d, 3 * d), const, pipeline_mode=pl.Buffered(1)),
                  pl.BlockSpec((tm, LANES), tab), pl.BlockSpec((tm, LANES), tab)],
        out_specs=[pl.BlockSpec((tm, d), row), pl.BlockSpec((tm, 2 * d), row), pl.BlockSpec((tm, d), row),
                   pl.BlockSpec((tm, d), row), pl.BlockSpec((None, 1, d), lambda i: (i, 0, 0))],
        out_shape=[jax.ShapeDtypeStruct((m, d), BF16), jax.ShapeDtypeStruct((m, 2 * d), F32),
                   jax.ShapeDtypeStruct((m, d), BF16), jax.ShapeDtypeStruct((m, d), BF16),
                   jax.ShapeDtypeStruct((m // tm, 1, d), F32)],
        compiler_params=_params("parallel"),
        name="moba_in",
    )(x, g, w, cos, sin)


def _moba_in_t_kernel(x_ref, g_ref, wk_ref, wqkt_ref, wvt_ref, cos_ref, sin_ref, cost_ref, sint_ref,
                      qt_ref, rowst_ref, kb_ref, vt_ref, km_ref):
    d = x_ref.shape[1]
    xn = _rms(x_ref[...], g_ref[...]).astype(BF16)
    qk = _rope_t(_dot_nt(wqkt_ref[...], xn), cost_ref[...], sint_ref[...])
    qt_ref[...] = (qk[0:d] * SCALE_LOG2).astype(BF16)
    v_t = _dot_nt(wvt_ref[...], xn)
    rowst_ref[0:d, :] = qk[d:2 * d]
    rowst_ref[d:2 * d, :] = v_t
    vt_ref[...] = v_t.astype(BF16)
    k = _rope(_dot(xn, wk_ref[...]), cos_ref[...], sin_ref[...])
    kb_ref[...] = k.astype(BF16)
    km_ref[...] = jnp.sum(k, axis=0, keepdims=True) * (1.0 / MOBA_BLOCK)


def moba_in_t(x, g, w_k, wqk_t, wv_t, cos, sin, cos_t, sin_t, *, n_b):
    m, d = x.shape
    tm = MOBA_BLOCK
    nt = cos.shape[0] // tm
    row = lambda i: (i, 0)
    const = lambda i: (0, 0)
    one = pl.Buffered(1)
    return pl.pallas_call(
        _moba_in_t_kernel,
        grid=(m // tm,),
        in_specs=[pl.BlockSpec((tm, d), row), pl.BlockSpec((1, d), const),
                  pl.BlockSpec(w_k.shape, const, pipeline_mode=one), pl.BlockSpec(wqk_t.shape, const, pipeline_mode=one),
                  pl.BlockSpec(wv_t.shape, const, pipeline_mode=one),
                  pl.BlockSpec((tm, LANES), lambda i: (i % nt, 0)), pl.BlockSpec((tm, LANES), lambda i: (i % nt, 0)),
                  pl.BlockSpec((HEAD_DIM, tm), lambda i: (0, i % nt)), pl.BlockSpec((HEAD_DIM, tm), lambda i: (0, i % nt))],
        out_specs=[pl.BlockSpec((d, tm), lambda i: (0, i)),
                   pl.BlockSpec((None, 2 * d, tm), lambda i: (i // nt, 0, i % nt)), pl.BlockSpec((tm, d), row),
                   pl.BlockSpec((None, d, tm), lambda i: (i, 0, 0)), pl.BlockSpec((None, 1, d), lambda i: (i, 0, 0))],
        out_shape=[jax.ShapeDtypeStruct((d, m), BF16), jax.ShapeDtypeStruct((n_b, 2 * d, m // n_b), F32),
                   jax.ShapeDtypeStruct((m, d), BF16), jax.ShapeDtypeStruct((m // tm, d, tm), BF16),
                   jax.ShapeDtypeStruct((m // tm, 1, d), F32)],
        compiler_params=_params("parallel"),
        name="moba_in_t",
    )(x, g, w_k, wqk_t, wv_t, cos, sin, cos_t, sin_t)


def _moba_attn_kernel(qt_ref, k_ref, vt_ref, km_ref, ot_ref, s0_ref, s1_ref, *, tq, n_top):
    i = pl.program_id(2)
    c0 = i * tq
    nblk = km_ref.shape[0]
    pos = c0 + lax.broadcasted_iota(jnp.int32, (1, tq), 1)
    cur = pos // MOBA_BLOCK
    blk = lax.broadcasted_iota(jnp.int32, (nblk, tq), 0)
    m1, m2, m3 = _split3(km_ref[...])
    vrow = [slice(hh * HEAD_DIM, (hh + 1) * HEAD_DIM) for hh in range(2)]
    q_aug = []
    for hh in range(2):
        qp = _pad_pair(qt_ref[vrow[hh], :], hh == 1)
        gate = _dot(m1, qp) + _dot(m2, qp) + _dot(m3, qp)
        gate = jnp.where(blk < cur, gate, -jnp.inf)
        allow = (blk == cur) | (_topk_mask_t(gate, n_top) > 0.5)
        fill = jnp.zeros((LANES - nblk, tq), BF16)
        q_aug.append(jnp.concatenate([qp, jnp.where(allow, 0.0, NEG).astype(BF16), fill], axis=0))

    tile_id = lax.broadcasted_iota(jnp.int32, (MOBA_BLOCK, LANES), 1)

    def put(t, s_ref):
        k = k_ref[pl.ds(pl.multiple_of(t * MOBA_BLOCK, MOBA_BLOCK), MOBA_BLOCK), :]
        k_aug = jnp.concatenate([k, jnp.where(tile_id == t, 1.0, 0.0).astype(BF16)], axis=1)
        for hh in range(2):
            s_ref[hh] = _dot(k_aug, q_aug[hh])

    def consume(t, s_ref, carry, causal):
        bias = None
        if causal:
            kpos = t * MOBA_BLOCK + lax.broadcasted_iota(jnp.int32, (MOBA_BLOCK, tq), 0)
            bias = jnp.where(kpos <= pos, 0.0, NEG)
        return tuple(_online_update_t(carry[hh], s_ref[hh], bias, vt_ref[t, vrow[hh], :]) for hh in range(2))

    def body(u, carry):
        t = 2 * u
        put(t + 1, s1_ref)
        carry = consume(t, s0_ref, carry, False)
        put(t + 2, s0_ref)
        return consume(t + 1, s1_ref, carry, False)

    t_diag = c0 // MOBA_BLOCK
    put(0, s0_ref)
    res = lax.fori_loop(0, t_diag // 2, body, (_online_init_t(tq), _online_init_t(tq)))
    put(t_diag + 1, s1_ref)
    res = consume(t_diag, s0_ref, res, True)
    res = consume(t_diag + 1, s1_ref, res, True)
    for hh in range(2):
        ot_ref[vrow[hh], :] = _online_finish_t(res[hh]).astype(BF16)


def moba_attn(qt, kb, vt, kmean, *, n_b, t, tq):
    assert tq == 2 * MOBA_BLOCK, "the kernel's tile pairing assumes two MoBA blocks per query tile"
    nq = t // tq
    d = qt.shape[0]
    nblk = kmean.shape[1]
    n_top = min(MOBA_TOPK, t // MOBA_BLOCK)
    qmap = lambda b, hp, i: (hp, b * nq + i)
    kvmap = lambda b, hp, i: (b, 0, hp)
    return pl.pallas_call(
        functools.partial(_moba_attn_kernel, tq=tq, n_top=n_top),
        grid=(n_b, d // LANES, nq),
        in_specs=[pl.BlockSpec((LANES, tq), qmap), pl.BlockSpec((None, t, LANES), kvmap),
                  pl.BlockSpec((t // MOBA_BLOCK, LANES, MOBA_BLOCK), lambda b, hp, i: (b, hp, 0)),
                  pl.BlockSpec((None, nblk, LANES), kvmap)],
        out_specs=pl.BlockSpec((LANES, tq), qmap),
        out_shape=jax.ShapeDtypeStruct((d, n_b * t), BF16),
        scratch_shapes=[pltpu.VMEM((2, MOBA_BLOCK, tq), F32)] * 2,
        compiler_params=_params("parallel", "parallel", "arbitrary"),
        name="moba_attn",
    )(qt, kb.reshape(n_b, t, d), vt, kmean)


def _moba_s_select_kernel(pt_ref, *refs, pps, n_top, n_blocks):
    del pt_ref
    pages = refs[:pps]
    qbd_ref, idx_ref, km_ref = refs[pps:]
    s = pl.program_id(1)
    per_blk = MOBA_BLOCK // PAGE_SIZE
    blocks_per_step = pps // per_blk

    @pl.when(s == 0)
    def _():
        km_ref[...] = jnp.zeros(km_ref.shape, F32)

    lane = lax.broadcasted_iota(jnp.int32, km_ref.shape, 1)
    km = km_ref[...]
    for j in range(blocks_per_step):
        tot = pages[per_blk * j][...]
        for e in range(1, per_blk):
            tot = tot + pages[per_blk * j + e][...]
        mean = jnp.sum(tot, axis=1, keepdims=True) * (1.0 / MOBA_BLOCK)
        km = jnp.where(lane == s * blocks_per_step + j, mean, km)
    km_ref[...] = km

    @pl.when(s == pl.num_programs(1) - 1)
    def _():
        gate = _dot3_rhs(qbd_ref[...], km_ref[...])
        blk = lax.broadcasted_iota(jnp.int32, gate.shape, 1)
        gate = jnp.where(blk < n_blocks, gate, -jnp.inf)
        lane_f = blk.astype(F32)
        picks = jnp.zeros(gate.shape, F32)
        for r in range(n_top):
            m = jnp.max(gate, axis=-1, keepdims=True)
            idx = jnp.min(jnp.where(gate == m, lane_f, 1e9), axis=-1, keepdims=True)
            gate = jnp.where(lane_f == idx, -jnp.inf, gate)
            picks = jnp.where(blk == r, idx, picks)
        idx_ref[...] = picks.astype(jnp.int32)


def moba_s_select(cache_fm, pt_flat, qbd, *, n_pages, n_top):
    n_b, _, d = qbd.shape
    pps = min(16, n_pages)
    per_blk = MOBA_BLOCK // PAGE_SIZE
    page_map = lambda k: (lambda b, s, pt: (pt[b * n_pages + s * pps + k], 0, 0))
    grid_spec = pltpu.PrefetchScalarGridSpec(
        num_scalar_prefetch=1,
        grid=(n_b, n_pages // pps),
        in_specs=[pl.BlockSpec((None, d, PAGE_SIZE), page_map(k)) for k in range(pps)]
        + [pl.BlockSpec((None, LANES, d), lambda b, s, pt: (b, 0, 0))],
        out_specs=pl.BlockSpec((None, LANES, LANES), lambda b, s, pt: (b, 0, 0)),
        scratch_shapes=[pltpu.VMEM((d, LANES), F32)],
    )
    return pl.pallas_call(
        functools.partial(_moba_s_select_kernel, pps=pps, n_top=n_top, n_blocks=n_pages // per_blk),
        grid_spec=grid_spec,
        out_shape=jax.ShapeDtypeStruct((n_b, LANES, LANES), jnp.int32),
        compiler_params=_params("parallel", "arbitrary"),
        name="moba_s_select",
    )(pt_flat, *([cache_fm] * pps), qbd)


def _moba_s_attn_kernel(pt_ref, ix_ref, q_ref, *refs, n_pg, hps):
    del pt_ref, ix_ref
    kblk = refs[:hps * n_pg]
    vblk = refs[hps * n_pg:2 * hps * n_pg]
    k_new, v_new, o_ref = refs[2 * hps * n_pg:]
    for hi in range(hps):
        mine = slice(hi * n_pg, (hi + 1) * n_pg)
        o_ref[hi] = _attend_with_new(q_ref[hi], [r[...] for r in kblk[mine]], [r[...] for r in vblk[mine]],
                                     [None] * n_pg, k_new[hi], v_new[hi])


def moba_s_attn(pt_flat, ix_flat, q4, cache_fm, k_new, v_new, *, n_pages, n_top, n_heads):
    n_b = q4.shape[0]
    per_blk = MOBA_BLOCK // PAGE_SIZE
    n_pg = n_top * per_blk
    hps = 4

    def kv_map(hi, r, row0):
        def f(b, hg, pt, ix):
            h = hg * hps + hi
            j = ix[(b * n_heads + h) * n_top + r // per_blk]
            return (pt[b * n_pages + j * per_blk + r % per_blk], row0 + h, 0)
        return f

    group = lambda b, hg, pt, ix: (b, hg, 0, 0)
    new_spec = pl.BlockSpec((None, hps, 1, HEAD_DIM), group)
    qo_spec = pl.BlockSpec((None, hps, 8, HEAD_DIM), group)
    page = (None, HEAD_DIM, PAGE_SIZE)
    grid_spec = pltpu.PrefetchScalarGridSpec(
        num_scalar_prefetch=2,
        grid=(n_b, n_heads // hps),
        in_specs=[qo_spec]
        + [pl.BlockSpec(page, kv_map(hi, r, 0)) for hi in range(hps) for r in range(n_pg)]
        + [pl.BlockSpec(page, kv_map(hi, r, n_heads)) for hi in range(hps) for r in range(n_pg)]
        + [new_spec, new_spec],
        out_specs=qo_spec,
    )
    return pl.pallas_call(
        functools.partial(_moba_s_attn_kernel, n_pg=n_pg, hps=hps),
        grid_spec=grid_spec,
        out_shape=jax.ShapeDtypeStruct((n_b, n_heads, 8, HEAD_DIM), F32),
        compiler_params=_params("parallel", "arbitrary"),
        name="moba_s_attn",
    )(pt_flat, ix_flat, q4, *([cache_fm] * (2 * hps * n_pg)), k_new, v_new)


def _layer_norm_silu(y, g, b):
    yc = y - jnp.mean(y, axis=-1, keepdims=True)
    yn = yc * lax.rsqrt(jnp.mean(yc * yc, axis=-1, keepdims=True) + LN_EPS) * g + b
    return yn * _sigmoid(yn)


def _conf_kernel(x_ref, g_ref, w1_ref, wdw_ref, bdw_ref, lg_ref, lb_ref, w2_ref, o_ref, st_ref, ubuf_ref, sh_ref,
                 *, hist):
    tm, d = x_ref.shape
    width = wdw_ref.shape[0]

    @pl.when(pl.program_id(1) == 0)
    def _():
        ubuf_ref[0:hist, :] = jnp.zeros((hist, d), F32)

    x = x_ref[...]
    xn = _rms(x, g_ref[...]).astype(BF16)
    u = _dot(xn, w1_ref[:, 0:d]) * _sigmoid(_dot(xn, w1_ref[:, d:2 * d]))
    ubuf_ref[hist:hist + tm, :] = u
    base = hist - (width - 1)
    rows = ubuf_ref.shape[0]
    y = bdw_ref[...]
    for r in range(8):
        taps = [k for k in range(width) if (base + k) % 8 == r]
        if not taps:
            continue
        src = ubuf_ref
        if r:
            sh_ref[...] = pltpu.roll(ubuf_ref[...], rows - r, 0)
            src = sh_ref
        for k in taps:
            y = y + src[base + k - r:base + k - r + tm, :] * wdw_ref[k:k + 1, :]
    z = _layer_norm_silu(y, lg_ref[...], lb_ref[...])
    o_ref[...] = x + _dot(z.astype(BF16), w2_ref[...])
    tail = ubuf_ref[tm:tm + hist, :]
    st_ref[...] = tail
    ubuf_ref[0:hist, :] = tail


def conf_prompt(x, g, w1, wdw, bdw, lg, lb, w2, *, n_b, t, tm):
    d = x.shape[1]
    nt = t // tm
    hist = 32
    row = lambda b, i: (b * nt + i, 0)
    const = lambda b, i: (0, 0)
    return pl.pallas_call(
        functools.partial(_conf_kernel, hist=hist),
        grid=(n_b, nt),
        in_specs=[pl.BlockSpec((tm, d), row), pl.BlockSpec((1, d), const),
                  pl.BlockSpec((d, 2 * d), const, pipeline_mode=pl.Buffered(1)),
                  pl.BlockSpec(wdw.shape, const), pl.BlockSpec((1, d), const), pl.BlockSpec((1, d), const),
                  pl.BlockSpec((1, d), const), pl.BlockSpec((d, d), const, pipeline_mode=pl.Buffered(1))],
        out_specs=[pl.BlockSpec((tm, d), row), pl.BlockSpec((None, hist, d), lambda b, i: (b, 0, 0))],
        out_shape=[jax.ShapeDtypeStruct((n_b * t, d), F32), jax.ShapeDtypeStruct((n_b, hist, d), F32)],
        scratch_shapes=[pltpu.VMEM((hist + tm, d), F32)] * 2,
        compiler_params=_params("parallel", "arbitrary"),
        name="conf_prompt",
    )(x, g, w1, wdw, bdw, lg, lb, w2)


def _conf_s_kernel(x_ref, g_ref, w1_ref, wdw_ref, bdw_ref, lg_ref, lb_ref, w2_ref, past_ref, o_ref, u_ref):
    d = x_ref.shape[1]
    width = wdw_ref.shape[0]
    x = x_ref[...]
    xn = _rms(x, g_ref[...]).astype(BF16)
    u = _dot(xn, w1_ref[:, 0:d]) * _sigmoid(_dot(xn, w1_ref[:, d:2 * d]))
    y = bdw_ref[...] + past_ref[0] * wdw_ref[0:1, :]
    for k in range(1, width - 1):
        y = y + past_ref[k] * wdw_ref[k:k + 1, :]
    y = y + u * wdw_ref[width - 1:width, :]
    z = _layer_norm_silu(y, lg_ref[...], lb_ref[...])
    u_ref[...] = u
    o_ref[...] = x + _dot(z.astype(BF16), w2_ref[...])


def conf_sample(x, g, w1, wdw, bdw, lg, lb, w2, past_t):
    m, d = x.shape
    full = lambda a: pl.BlockSpec(a.shape, lambda i: (0,) * a.ndim)
    args = (x, g, w1, wdw, bdw, lg, lb, w2, past_t)
    return pl.pallas_call(
        _conf_s_kernel,
        grid=(1,),
        in_specs=[full(a) for a in args],
        out_specs=[pl.BlockSpec((m, d), lambda i: (0, 0))] * 2,
        out_shape=[jax.ShapeDtypeStruct((m, d), F32)] * 2,
        compiler_params=_params("arbitrary"),
        name="conf_sample",
    )(*args)


def _rope_tables(pos):
    half = HEAD_DIM // 2
    inv_freq = ROPE_THETA ** (-jnp.arange(half, dtype=F32) / half)
    ang = pos.astype(F32)[:, None] * inv_freq[None, :]
    cos = jnp.cos(ang)
    sin = jnp.sin(ang)
    cos = jnp.concatenate([cos, cos], axis=-1)
    sin = jnp.concatenate([-sin, sin], axis=-1)
    return jnp.tile(cos, (1, LANES // HEAD_DIM)), jnp.tile(sin, (1, LANES // HEAD_DIM))


def _overlap(n_cmp_rows, n_cols):
    i = jnp.arange(n_cmp_rows, dtype=jnp.int32)[:, None]
    j = jnp.arange(n_cols, dtype=jnp.int32)[None, :]
    start = i * CMP_STRIDE
    hit = (start <= j * SEL_BLOCK + (SEL_BLOCK - 1)) & (start + (CMP_BLOCK - 1) >= j * SEL_BLOCK)
    return hit.astype(BF16)


def _nsa_layer(hp, hs, g, cache_kv, cache_win, pt_flat, n_pages, w, *, n_b, t, n_s):
    w_in, pe_k, w1_k, w2_k, pe_v, w1_v, w2_v, w_out = w
    d = hp.shape[1]
    past_len = n_pages * PAGE_SIZE
    w_in_p = jnp.pad(w_in, ((0, 0), (0, 2688 - w_in.shape[1]))).astype(BF16)
    w_out_b = w_out.astype(BF16)
    half = CMP_BLOCK * HEAD_DIM // 2

    def cmp_weights(pe, w1, w2):
        return (pe[:CMP_STRIDE].reshape(1, half), pe[CMP_STRIDE:].reshape(1, half),
                w1[:half].astype(BF16), w1[half:].astype(BF16), w2.astype(BF16))

    wk = cmp_weights(pe_k, w1_k, w2_k)
    wv = cmp_weights(pe_v, w1_v, w2_v) + (w2_v.T.astype(BF16),)

    cos_p, sin_p = _rope_tables(jnp.arange(t, dtype=jnp.int32))
    cos_pt, sin_pt = cos_p[:, :HEAD_DIM].T, sin_p[:, :HEAD_DIM].T
    kv0, kv1, n_gate = 1024, 2560, NSA_KV_HEADS * NSA_GROUP * 3
    wq_t = w_in[:, :kv0].T.astype(BF16)
    wkv_t = w_in[:, kv0:kv0 + 1024].T.astype(BF16)
    wv_t = w_in[:, kv0 + 1280:kv1].T.astype(BF16)
    wg_t = jnp.pad(w_in[:, kv1:kv1 + n_gate].T, ((0, LANES - n_gate), (0, 0))).astype(BF16)
    w_rm = jnp.concatenate([w_in[:, kv0:kv0 + 768], w_in[:, kv0 + 1024:kv1]], axis=1).astype(BF16)
    tks = 256
    qt, rows_t, cmp_rows, kk, vt, win, gt = nsa_in_t(hp, g, w_rm, wq_t, wkv_t, wv_t, wg_t,
                                                     cos_p, sin_p, cos_pt, sin_pt, tm=tks, n_b=n_b)
    npg_p = t // PAGE_SIZE
    cend_p = jnp.arange(npg_p * 8, dtype=jnp.int32) * CMP_STRIDE + (CMP_BLOCK - 1)
    kc, _, vct = nsa_compress(cmp_rows.reshape(n_b * npg_p, PAGE_SIZE, 512), jnp.arange(n_b * npg_p, dtype=jnp.int32),
                              n_b, npg_p, wk, wv, *_rope_tables(cend_p), feature_major=False)
    nblk = -(-(t // SEL_BLOCK) // 16) * 16
    tile = jnp.arange(t // tks, dtype=jnp.int32)[:, None, None]
    key = jnp.arange(tks, dtype=jnp.int32)[None, :, None]
    blk = jnp.arange(LANES, dtype=jnp.int32)[None, None, :]
    emt = (blk == (tile * tks + key) // SEL_BLOCK).astype(BF16)
    ot = nsa_attn(qt, gt, kc, vct, kk, vt, _overlap(npg_p * 8, nblk).T, emt, n_b=n_b, t=t, tq=128)
    hp = mm_res_t(ot, w_out.T.astype(BF16), hp, tm=512)
    kv_p = jnp.transpose(rows_t.reshape(n_b, 4, NSA_KV_HEADS, HEAD_DIM, t), (0, 4, 1, 2, 3))
    keep = min(WINDOW, t)
    win_p = win.reshape(n_b, t, 2, NSA_KV_HEADS, HEAD_DIM)[:, t - keep:]

    cos_s, sin_s = _rope_tables(jnp.full((n_s,), past_len, jnp.int32))
    q_s, rows_s, kva_s, win_s, gates_s = nsa_in(hs, g, w_in_p, cos_s, sin_s, tm=n_s)
    cend_s = jnp.arange(n_pages * 8, dtype=jnp.int32) * CMP_STRIDE + (CMP_BLOCK - 1)
    cache_fm = jnp.transpose(cache_kv, (0, 2, 3, 4, 1)).reshape(cache_kv.shape[0], 1024, PAGE_SIZE)
    kc_s, vc_s, _ = nsa_compress(cache_fm, pt_flat, n_s, n_pages, wk, wv, *_rope_tables(cend_s), feature_major=True)
    n_sel = -(-(past_len + 1) // SEL_BLOCK)
    n_pick = min(SEL_TOPN, n_sel) - 1
    q4 = q_s.reshape(n_s, NSA_KV_HEADS, NSA_GROUP, HEAD_DIM)
    eye = jnp.eye(NSA_KV_HEADS, dtype=bool)[None, :, None, :, None]
    qbd = jnp.where(eye, q4[:, :, :, None, :], jnp.zeros((), BF16)).reshape(n_s, 16, 256)
    gm = (jnp.arange(8)[:, None] == jnp.arange(16)[None, :] // NSA_GROUP).astype(BF16)
    n_blk_pad = -(-n_sel // LANES) * LANES
    oc16, idx = nsa_s_sel(qbd, kc_s, vc_s, _overlap(n_pages * 8, n_blk_pad), gm, pos=past_len, n_pick=n_pick)
    oc5 = oc16.reshape(n_s, NSA_KV_HEADS, NSA_GROUP, NSA_KV_HEADS, HEAD_DIM)
    o_c = jnp.sum(jnp.where(eye, oc5, 0.0), axis=3).reshape(n_s, d)
    ix_flat = idx[:, :NSA_KV_HEADS, :n_pick].reshape(-1)
    q8 = jnp.pad(q4, ((0, 0), (0, 0), (0, 8 - NSA_GROUP), (0, 0)))
    win_fm = jnp.transpose(cache_win, (0, 2, 3, 4, 1)).reshape(n_s, 512, cache_win.shape[1])
    os_p, ow_p = nsa_s_attn(pt_flat, ix_flat, q8, cache_fm, kva_s.reshape(n_s, 16, 1, HEAD_DIM), win_fm,
                            n_pages=n_pages, n_pick=n_pick)
    o_s = os_p[:, :, :NSA_GROUP].reshape(n_s, d)
    o_w = ow_p[:, :, :NSA_GROUP].reshape(n_s, d)
    g3 = jnp.repeat(gates_s[:, :48].reshape(n_s, 16, 3), HEAD_DIM, axis=1)
    hs = nsa_out_s(o_c, o_s, o_w, g3[:, :, 0], g3[:, :, 1], g3[:, :, 2], w_out_b, hs)
    kv_s = rows_s.reshape(n_s, 1, 4, NSA_KV_HEADS, HEAD_DIM)
    win_new = win_s.reshape(n_s, 1, 2, NSA_KV_HEADS, HEAD_DIM)
    win_all = jnp.concatenate([cache_win, win_new], axis=1)
    win_s_out = win_all[:, win_all.shape[1] - cache_win.shape[1]:]
    return hp, hs, kv_p, kv_s, win_p, win_s_out


def _sconv_layer(hp, hs, g, state, w, *, n_b, t):
    w_in, w_conv, w_out = w
    w_in_b = w_in.astype(BF16)
    w_out_b = w_out.astype(BF16)
    hp, st = sconv_prompt(hp, g, w_in_b, w_conv, w_out_b, n_b=n_b, t=t, tm=256)
    st_p = st[:, 8 - (w_conv.shape[0] - 1):]
    hs, pre = sconv_sample(hs, g, w_in_b, w_conv, w_out_b, state[:, 0], state[:, 1])
    st_s = jnp.concatenate([state[:, 1:], pre[:, None, :]], axis=1)
    return hp, hs, st_p, st_s


def _moba_layer(hp, hs, g, cache_kv, pt_flat, n_pages, w, *, n_b, t, n_s):
    w_qkv, w_out = w
    d = hp.shape[1]
    n_heads = d // HEAD_DIM
    past_len = n_pages * PAGE_SIZE
    w_qkv_b = w_qkv.astype(BF16)
    w_out_b = w_out.astype(BF16)

    cos_p, sin_p = _rope_tables(jnp.arange(t, dtype=jnp.int32))
    cos_pt, sin_pt = cos_p[:, :HEAD_DIM].T, sin_p[:, :HEAD_DIM].T
    qt, rows_t, kb, vt, km = moba_in_t(hp, g, w_qkv_b[:, d:2 * d], w_qkv_b[:, :2 * d].T, w_qkv_b[:, 2 * d:].T,
                                       cos_p, sin_p, cos_pt, sin_pt, n_b=n_b)
    nblk = t // MOBA_BLOCK
    kmean = jnp.pad(km.reshape(n_b, nblk, d), ((0, 0), (0, -(-nblk // 16) * 16 - nblk), (0, 0)))
    ot = moba_attn(qt, kb, vt, kmean, n_b=n_b, t=t, tq=512)
    hp = mm_res_t(ot, w_out_b.T, hp, tm=512)
    kv_p = jnp.transpose(rows_t.reshape(n_b, 2, n_heads, HEAD_DIM, t), (0, 4, 1, 2, 3))

    cos_s, sin_s = _rope_tables(jnp.full((n_s,), past_len, jnp.int32))
    q_s, rows_s, kb_s, vb_s, _ = moba_in(hs, g, w_qkv_b, cos_s, sin_s, tm=n_s)
    cache_fm = jnp.transpose(cache_kv, (0, 2, 3, 4, 1)).reshape(cache_kv.shape[0], 2 * d, PAGE_SIZE)
    n_top = min(MOBA_TOPK, -(-(past_len + 1) // MOBA_BLOCK))
    qh = q_s.reshape(n_s, n_heads, 1, HEAD_DIM)
    eye = jnp.eye(n_heads, dtype=bool)[None, :, :, None]
    qbd = jnp.where(eye, qh, jnp.zeros((), BF16)).reshape(n_s, n_heads, d)
    qbd = jnp.pad(qbd, ((0, 0), (0, LANES - n_heads), (0, 0)))
    idx = moba_s_select(cache_fm, pt_flat, qbd, n_pages=n_pages, n_top=n_top)
    ix_flat = idx[:, :n_heads, :n_top].reshape(-1)
    q8 = jnp.pad(qh, ((0, 0), (0, 0), (0, 7), (0, 0)))
    o_p = moba_s_attn(pt_flat, ix_flat, q8, cache_fm, kb_s.reshape(n_s, n_heads, 1, HEAD_DIM),
                      vb_s.reshape(n_s, n_heads, 1, HEAD_DIM), n_pages=n_pages, n_top=n_top, n_heads=n_heads)
    o_s = o_p[:, :, 0].reshape(n_s, d).astype(BF16)
    hs = mm_res(o_s, w_out_b, hs, tm=n_s)
    kv_s = rows_s.reshape(n_s, 1, 2, n_heads, HEAD_DIM)
    return hp, hs, kv_p, kv_s


def _conf_layer(hp, hs, g, state, w, *, n_b, t):
    w_pw1, w_dw, b_dw, ln_g, ln_b, w_pw2 = w
    d = hp.shape[1]
    r = lambda a: a.reshape(1, d)
    args = (w_pw1.astype(BF16), w_dw, r(b_dw), r(ln_g), r(ln_b), w_pw2.astype(BF16))
    hp, st = conf_prompt(hp, g, *args, n_b=n_b, t=t, tm=256)
    st_p = st[:, st.shape[1] - (w_dw.shape[0] - 1):]
    hs, u = conf_sample(hs, g, *args, jnp.transpose(state, (1, 0, 2)))
    st_s = jnp.concatenate([state[:, 1:], u[:, None, :]], axis=1)
    return hp, hs, st_p, st_s


def kernel(x_prompt, x_sample, cache_nsa_kv, cache_nsa_win, state_sconv, cache_moba_kv, state_conformer,
           page_table, norm_mix, norm_ffn, norm_final, ffn_w_up, ffn_w_down,
           nsa_w_in, nsa_pe_k, nsa_w1_k, nsa_w2_k, nsa_pe_v, nsa_w1_v, nsa_w2_v, nsa_w_out,
           sconv_w_in, sconv_w_conv, sconv_w_out, moba_w_qkv, moba_w_out,
           conf_w_pw1, conf_w_dw, conf_b_dw, conf_ln_g, conf_ln_b, conf_w_pw2):
    n_b, t, d = x_prompt.shape
    n_s = x_sample.shape[0]
    depth = norm_mix.shape[0]
    n_pages = page_table.shape[1]
    pt_flat = page_table.reshape(-1).astype(jnp.int32)
    hp = x_prompt.reshape(n_b * t, d)
    hs = x_sample.reshape(n_s, d)
    outs = {k: [] for k in ("nsa_kv_p", "nsa_kv_s", "nsa_win_p", "nsa_win_s", "sconv_p", "sconv_s",
                            "moba_p", "moba_s", "conf_p", "conf_s")}
    for i in range(depth):
        kind, j = i % 4, i // 4
        g = norm_mix[i].reshape(1, d)
        if kind == 0:
            w = (nsa_w_in[j], nsa_pe_k[j], nsa_w1_k[j], nsa_w2_k[j], nsa_pe_v[j], nsa_w1_v[j], nsa_w2_v[j],
                 nsa_w_out[j])
            hp, hs, kv_p, kv_s, win_p, win_s = _nsa_layer(hp, hs, g, cache_nsa_kv[j], cache_nsa_win[j], pt_flat,
                                                          n_pages, w, n_b=n_b, t=t, n_s=n_s)
            outs["nsa_kv_p"].append(kv_p)
            outs["nsa_kv_s"].append(kv_s)
            outs["nsa_win_p"].append(win_p)
            outs["nsa_win_s"].append(win_s)
        elif kind == 1:
            hp, hs, st_p, st_s = _sconv_layer(hp, hs, g, state_sconv[j],
                                              (sconv_w_in[j], sconv_w_conv[j], sconv_w_out[j]), n_b=n_b, t=t)
            outs["sconv_p"].append(st_p)
            outs["sconv_s"].append(st_s)
        elif kind == 2:
            hp, hs, kv_p, kv_s = _moba_layer(hp, hs, g, cache_moba_kv[j], pt_flat, n_pages,
                                             (moba_w_qkv[j], moba_w_out[j]), n_b=n_b, t=t, n_s=n_s)
            outs["moba_p"].append(kv_p)
            outs["moba_s"].append(kv_s)
        else:
            w = (conf_w_pw1[j], conf_w_dw[j], conf_b_dw[j], conf_ln_g[j], conf_ln_b[j], conf_w_pw2[j])
            hp, hs, st_p, st_s = _conf_layer(hp, hs, g, state_conformer[j], w, n_b=n_b, t=t)
            outs["conf_p"].append(st_p)
            outs["conf_s"].append(st_s)
        gf = norm_ffn[i].reshape(1, d)
        wu = ffn_w_up[i].astype(BF16)
        wd = ffn_w_down[i].astype(BF16)
        final = i == depth - 1
        gfin = norm_final.reshape(1, d)
        hp = ffn(hp, gf, wu, wd, gfin, tm=512, final=final)
        hs = ffn(hs, gf, wu, wd, gfin, tm=n_s, final=final)
    return (hp.reshape(n_b, t, d), hs.reshape(n_s, 1, d),
            jnp.stack(outs["nsa_kv_p"]), jnp.stack(outs["nsa_kv_s"]),
            jnp.stack(outs["nsa_win_p"]), jnp.stack(outs["nsa_win_s"]),
            jnp.stack(outs["sconv_p"]), jnp.stack(outs["sconv_s"]),
            jnp.stack(outs["moba_p"]), jnp.stack(outs["moba_s"]),
            jnp.stack(outs["conf_p"]), jnp.stack(outs["conf_s"]))
```

```python
import functools

import jax
import jax.numpy as jnp
from jax import lax
from jax.experimental import pallas as pl
from jax.experimental.pallas import tpu as pltpu

F32 = jnp.float32
BF16 = jnp.bfloat16

HEAD_DIM = 64
ROPE_THETA = 10000.0
RMS_EPS = 1e-6
LN_EPS = 1e-5
NSA_KV_HEADS = 4
NSA_GROUP = 4
CMP_STRIDE = 16
CMP_BLOCK = 32
SEL_BLOCK = 64
SEL_TOPN = 16
WINDOW = 512
MOBA_BLOCK = 256
MOBA_TOPK = 3
PAGE_SIZE = 128
SCALE = HEAD_DIM ** -0.5
SCALE_LOG2 = SCALE * 1.4426950408889634

LANES = 128
NEG = -1e30
VMEM_LIMIT = 56 * 1024 * 1024


def _params(*sem):
    return pltpu.CompilerParams(dimension_semantics=sem, vmem_limit_bytes=VMEM_LIMIT)


def _dot(a, b):
    return jnp.dot(a, b, preferred_element_type=F32)


def _dot_nt(a, b):
    return lax.dot_general(a, b, (((1,), (1,)), ((), ())), preferred_element_type=F32)


def _split3(x):
    hi = x.astype(BF16)
    r = x - hi.astype(F32)
    mid = r.astype(BF16)
    lo = (r - mid.astype(F32)).astype(BF16)
    return hi, mid, lo


def _dot3(x, m):
    hi, mid, lo = _split3(x)
    return _dot(hi, m) + _dot(mid, m) + _dot(lo, m)


def _dot3_rhs(m, x):
    hi, mid, lo = _split3(x)
    return _dot(m, hi) + _dot(m, mid) + _dot(m, lo)


def _rms(x, g):
    return x * lax.rsqrt(jnp.mean(x * x, axis=-1, keepdims=True) + RMS_EPS) * g


def _sigmoid(x):
    return 1.0 / (1.0 + jnp.exp(-x))


def _rope(x, cos, sin):
    w = x.shape[-1]
    lane = lax.broadcasted_iota(jnp.int32, x.shape, 1)
    first = (lane % HEAD_DIM) < (HEAD_DIM // 2)
    rot = jnp.where(first, pltpu.roll(x, w - HEAD_DIM // 2, 1), pltpu.roll(x, HEAD_DIM // 2, 1))
    reps = w // LANES
    if reps > 1:
        cos = jnp.concatenate([cos] * reps, axis=1)
        sin = jnp.concatenate([sin] * reps, axis=1)
    return x * cos + rot * sin


def _masked_softmax(s, mask):
    s = jnp.where(mask, s, -jnp.inf)
    m = jnp.max(s, axis=-1, keepdims=True)
    m = jnp.where(m > -jnp.inf, m, 0.0)
    p = jnp.exp(s - m)
    return p / jnp.maximum(jnp.sum(p, axis=-1, keepdims=True), 1e-30)


def _rope_t(x, cos, sin):
    r = x.shape[0]
    row = lax.broadcasted_iota(jnp.int32, x.shape, 0)
    first = (row % HEAD_DIM) < (HEAD_DIM // 2)
    rot = jnp.where(first, pltpu.roll(x, r - HEAD_DIM // 2, 0), pltpu.roll(x, HEAD_DIM // 2, 0))
    reps = r // HEAD_DIM
    if reps > 1:
        cos = jnp.concatenate([cos] * reps, axis=0)
        sin = jnp.concatenate([sin] * reps, axis=0)
    return x * cos + rot * sin


def _topk_mask_t(score, k):
    row = lax.broadcasted_iota(jnp.int32, score.shape, 0).astype(F32)
    sel = jnp.zeros(score.shape, F32)
    for _ in range(k):
        m = jnp.max(score, axis=0, keepdims=True)
        idx = jnp.min(jnp.where(score == m, row, 1e9), axis=0, keepdims=True)
        hit = row == idx
        sel = jnp.where(hit & (m > -jnp.inf), 1.0, sel)
        score = jnp.where(hit, -jnp.inf, score)
    return sel


ONES_ROWS = 16


def _online_init_t(cols):
    return (jnp.full((1, cols), NEG, F32), jnp.zeros((HEAD_DIM + ONES_ROWS, cols), F32))


def _online_update_t(carry, s, bias, v_t):
    m, acc = carry
    sm = s if bias is None else s + bias
    m_new = jnp.maximum(m, jnp.max(sm, axis=0, keepdims=True))
    p = jnp.exp2(sm - m_new).astype(BF16)
    v_aug = jnp.concatenate([v_t, jnp.ones((ONES_ROWS, v_t.shape[1]), BF16)], axis=0)
    return m_new, jnp.exp2(m - m_new) * acc + _dot(v_aug, p)


def _online_finish_t(carry):
    _, acc = carry
    return acc[0:HEAD_DIM] * (1.0 / jnp.maximum(acc[HEAD_DIM:HEAD_DIM + 1], 1e-30))


def _pad_pair(q, odd):
    z = jnp.zeros_like(q)
    return jnp.concatenate([z, q] if odd else [q, z], axis=0)


def _ffn_kernel(x_ref, g_ref, wu_ref, wd_ref, gf_ref, o_ref, *, chunk, final):
    x = x_ref[...]
    xn = _rms(x, g_ref[...]).astype(BF16)
    acc = x
    for c in range(0, wu_ref.shape[1], chunk):
        u = _dot(xn, wu_ref[:, c:c + chunk])
        a = jnp.square(jnp.maximum(u, 0.0)).astype(BF16)
        acc = acc + _dot(a, wd_ref[c:c + chunk, :])
    if final:
        acc = _rms(acc, gf_ref[...])
    o_ref[...] = acc


def ffn(x, g, wu, wd, gf, *, tm, final):
    m, d = x.shape
    dff = wu.shape[1]
    row = lambda i: (i, 0)
    const = lambda i: (0, 0)
    return pl.pallas_call(
        functools.partial(_ffn_kernel, chunk=512, final=final),
        grid=(m // tm,),
        in_specs=[pl.BlockSpec((tm, d), row), pl.BlockSpec((1, d), const),
                  pl.BlockSpec((d, dff), const, pipeline_mode=pl.Buffered(1)),
                  pl.BlockSpec((dff, d), const, pipeline_mode=pl.Buffered(1)),
                  pl.BlockSpec((1, d), const)],
        out_specs=pl.BlockSpec((tm, d), row),
        out_shape=jax.ShapeDtypeStruct((m, d), F32),
        compiler_params=_params("parallel"),
        name="ffn",
    )(x, g, wu, wd, gf)


def _mm_res_kernel(x_ref, w_ref, r_ref, o_ref):
    o_ref[...] = r_ref[...] + _dot(x_ref[...], w_ref[...])


def mm_res(x, w, res, *, tm):
    m, k = x.shape
    n = w.shape[1]
    row = lambda i: (i, 0)
    return pl.pallas_call(
        _mm_res_kernel,
        grid=(m // tm,),
        in_specs=[pl.BlockSpec((tm, k), row), pl.BlockSpec((k, n), lambda i: (0, 0)),
                  pl.BlockSpec((tm, n), row)],
        out_specs=pl.BlockSpec((tm, n), row),
        out_shape=jax.ShapeDtypeStruct((m, n), F32),
        compiler_params=_params("parallel"),
        name="mm_res",
    )(x, w, res)


def _mm_res_t_kernel(xt_ref, wt_ref, r_ref, o_ref):
    o_ref[...] = r_ref[...] + _dot(wt_ref[...], xt_ref[...]).T


def mm_res_t(xt, wt, res, *, tm):
    k, m = xt.shape
    n = wt.shape[0]
    row = lambda i: (i, 0)
    return pl.pallas_call(
        _mm_res_t_kernel,
        grid=(m // tm,),
        in_specs=[pl.BlockSpec((k, tm), lambda i: (0, i)), pl.BlockSpec((n, k), lambda i: (0, 0)),
                  pl.BlockSpec((tm, n), row)],
        out_specs=pl.BlockSpec((tm, n), row),
        out_shape=jax.ShapeDtypeStruct((m, n), F32),
        compiler_params=_params("parallel"),
        name="mm_res_t",
    )(xt, wt, res)


def _nsa_in_kernel(x_ref, g_ref, w_ref, cos_ref, sin_ref, q_ref, rows_ref, kva_ref, win_ref, gate_ref):
    xn = _rms(x_ref[...], g_ref[...]).astype(BF16)
    cos = cos_ref[...]
    sin = sin_ref[...]
    q = _rope(_dot(xn, w_ref[:, 0:1024]), cos, sin) * SCALE
    q_ref[...] = q.astype(BF16)
    kv = _dot(xn, w_ref[:, 1024:2048])
    ks = _rope(kv[:, 512:768], cos, sin)
    rows_ref[:, 0:512] = kv[:, 0:512]
    rows_ref[:, 512:768] = ks
    rows_ref[:, 768:1024] = kv[:, 768:1024]
    wkv = _dot(xn, w_ref[:, 2048:2560])
    kw = _rope(wkv[:, 0:256], cos, sin)
    win_ref[:, 0:256] = kw
    win_ref[:, 256:512] = wkv[:, 256:512]
    kva_ref[:, 0:256] = ks.astype(BF16)
    kva_ref[:, 256:512] = kv[:, 768:1024].astype(BF16)
    kva_ref[:, 512:768] = kw.astype(BF16)
    kva_ref[:, 768:1024] = wkv[:, 256:512].astype(BF16)
    gate_ref[...] = _sigmoid(_dot(xn, w_ref[:, 2560:2688]))


def nsa_in(x, g, w, cos, sin, *, tm):
    m, d = x.shape
    nw = w.shape[1]
    nt = cos.shape[0] // tm
    row = lambda i: (i, 0)
    tab = lambda i: (i % nt, 0)
    const = lambda i: (0, 0)
    outs = [(1024, BF16), (1024, F32), (1024, BF16), (512, F32), (LANES, F32)]
    return pl.pallas_call(
        _nsa_in_kernel,
        grid=(m // tm,),
        in_specs=[pl.BlockSpec((tm, d), row), pl.BlockSpec((1, d), const),
                  pl.BlockSpec((d, nw), const, pipeline_mode=pl.Buffered(1)),
                  pl.BlockSpec((tm, LANES), tab), pl.BlockSpec((tm, LANES), tab)],
        out_specs=[pl.BlockSpec((tm, n), row) for n, _ in outs],
        out_shape=[jax.ShapeDtypeStruct((m, n), dt) for n, dt in outs],
        compiler_params=_params("parallel"),
        name="nsa_in",
    )(x, g, w, cos, sin)


def _nsa_in_t_kernel(x_ref, g_ref, w_ref, wqt_ref, wkvt_ref, wvt_ref, wgt_ref, cos_ref, sin_ref, cost_ref, sint_ref,
                     qt_ref, rowst_ref, cmp_ref, kk_ref, vt_ref, win_ref, gt_ref):
    xn = _rms(x_ref[...], g_ref[...]).astype(BF16)
    cos = cos_ref[...]
    sin = sin_ref[...]
    cos_t = cost_ref[...]
    sin_t = sint_ref[...]
    qt_ref[...] = (_rope_t(_dot_nt(wqt_ref[...], xn), cos_t, sin_t) * SCALE_LOG2).astype(BF16)
    rt = _dot_nt(wkvt_ref[...], xn)
    rowst_ref[0:512, :] = rt[0:512]
    rowst_ref[512:768, :] = _rope_t(rt[512:768], cos_t, sin_t)
    rowst_ref[768:1024, :] = rt[768:1024]
    vt_ref[0:256, :] = rt[768:1024].astype(BF16)
    vt_ref[256:512, :] = _dot_nt(wvt_ref[...], xn).astype(BF16)
    kv = _dot(xn, w_ref[:, 0:768])
    cmp_ref[...] = kv[:, 0:512]
    wkv = _dot(xn, w_ref[:, 768:1280])
    kw = _rope(wkv[:, 0:256], cos, sin)
    win_ref[:, 0:256] = kw
    win_ref[:, 256:512] = wkv[:, 256:512]
    kk_ref[:, 0:256] = _rope(kv[:, 512:768], cos, sin).astype(BF16)
    kk_ref[:, 256:512] = kw.astype(BF16)
    gt_ref[...] = _sigmoid(_dot_nt(wgt_ref[...], xn))


def nsa_in_t(x, g, w_rm, wq_t, wkv_t, wv_t, wg_t, cos, sin, cos_t, sin_t, *, tm, n_b):
    m, d = x.shape
    nt = cos.shape[0] // tm
    row = lambda i: (i, 0)
    col = lambda i: (0, i)
    const = lambda i: (0, 0)
    one = pl.Buffered(1)
    weights = (w_rm, wq_t, wkv_t, wv_t, wg_t)
    return pl.pallas_call(
        _nsa_in_t_kernel,
        grid=(m // tm,),
        in_specs=[pl.BlockSpec((tm, d), row), pl.BlockSpec((1, d), const)]
        + [pl.BlockSpec(w.shape, const, pipeline_mode=one) for w in weights]
        + [pl.BlockSpec((tm, LANES), lambda i: (i % nt, 0)), pl.BlockSpec((tm, LANES), lambda i: (i % nt, 0)),
           pl.BlockSpec((HEAD_DIM, tm), lambda i: (0, i % nt)), pl.BlockSpec((HEAD_DIM, tm), lambda i: (0, i % nt))],
        out_specs=[pl.BlockSpec((1024, tm), col), pl.BlockSpec((None, 1024, tm), lambda i: (i // nt, 0, i % nt)),
                   pl.BlockSpec((tm, 512), row), pl.BlockSpec((tm, 512), row),
                   pl.BlockSpec((None, 512, tm), lambda i: (i, 0, 0)), pl.BlockSpec((tm, 512), row),
                   pl.BlockSpec((LANES, tm), col)],
        out_shape=[jax.ShapeDtypeStruct((1024, m), BF16), jax.ShapeDtypeStruct((n_b, 1024, m // n_b), F32),
                   jax.ShapeDtypeStruct((m, 512), F32), jax.ShapeDtypeStruct((m, 512), BF16),
                   jax.ShapeDtypeStruct((m // tm, 512, tm), BF16), jax.ShapeDtypeStruct((m, 512), F32),
                   jax.ShapeDtypeStruct((LANES, m), F32)],
        compiler_params=_params("parallel"),
        name="nsa_in_t",
    )(x, g, *weights, cos, sin, cos_t, sin_t)


def _compress_kernel(pt_ref, *refs, pps, feature_major):
    del pt_ref
    pages = refs[:pps + 1]
    (pelo_k, pehi_k, w1lo_k, w1hi_k, w2_k, pelo_v, pehi_v, w1lo_v, w1hi_v, w2_v, w2t_v,
     cos_ref, sin_ref, kc_ref, vc_ref, vct_ref, xs_ref) = refs[pps + 1:]
    npc = pps * 8
    rows = (pps + 1) * 8
    low = lax.broadcasted_iota(jnp.int32, (rows, LANES), 1) < HEAD_DIM
    for k, pg in enumerate(pages):
        x = pg[...]
        if feature_major:
            x = x.T
        for lb in range(4):
            xs_ref[lb, k * PAGE_SIZE:(k + 1) * PAGE_SIZE, :] = x[:, lb * LANES:(lb + 1) * LANES]
    streams = ((pelo_k, pehi_k, w1lo_k, w1hi_k, w2_k, kc_ref), (pelo_v, pehi_v, w1lo_v, w1hi_v, w2_v, vc_ref))
    for s, (pelo, pehi, w1lo, w1hi, w2, out_ref) in enumerate(streams):
        heads = [[] for _ in range(NSA_KV_HEADS)]
        for lb in range(2):
            for q in range(CMP_STRIDE // 2):
                a = xs_ref[2 * s + lb, pl.ds(2 * q, rows, stride=CMP_STRIDE), :]
                b = xs_ref[2 * s + lb, pl.ds(2 * q + 1, rows, stride=CMP_STRIDE), :]
                heads[2 * lb].append(jnp.where(low, a, pltpu.roll(b, HEAD_DIM, 1)))
                heads[2 * lb + 1].append(jnp.where(low, pltpu.roll(a, HEAD_DIM, 1), b))
        x = jnp.concatenate([jnp.concatenate(hh, axis=1) for hh in heads], axis=0)
        first = _dot((x + pelo[...]).astype(BF16), w1lo[...])
        second = _dot((x + pehi[...]).astype(BF16), w1hi[...])
        outs = []
        outs_t = []
        for h in range(NSA_KV_HEADS):
            pre = first[h * rows:h * rows + npc] + second[h * rows + 1:h * rows + 1 + npc]
            hid = (pre * _sigmoid(pre)).astype(BF16)
            outs.append(_dot(hid, w2[...]))
            if s == 1:
                outs_t.append(_dot_nt(w2t_v[...], hid))
        res = jnp.concatenate(outs, axis=1)
        if s == 0:
            res = _rope(res, cos_ref[...], sin_ref[...])
        else:
            vct_ref[...] = jnp.concatenate(outs_t, axis=0).astype(BF16)
        out_ref[...] = res.astype(BF16)


def nsa_compress(rows3d, pt_flat, n_seq, n_pages, wk, wv, cos_c, sin_c, *, feature_major):
    pps = min(16, n_pages)
    steps = n_pages // pps
    npc = pps * 8
    page_block = (None, 512, PAGE_SIZE) if feature_major else (None, PAGE_SIZE, 512)

    def page_map(k):
        return lambda b, s, pt: (pt[b * n_pages + jnp.minimum(s * pps + k, n_pages - 1)], 0, 0)

    const2 = lambda b, s, pt: (0, 0)
    wspecs = []
    for _ in range(2):
        wspecs += [pl.BlockSpec((1, 1024), const2), pl.BlockSpec((1, 1024), const2),
                   pl.BlockSpec((1024, 256), const2), pl.BlockSpec((1024, 256), const2),
                   pl.BlockSpec((256, HEAD_DIM), const2)]
    wspecs.append(pl.BlockSpec((HEAD_DIM, 256), const2))
    grid_spec = pltpu.PrefetchScalarGridSpec(
        num_scalar_prefetch=1,
        grid=(n_seq, steps),
        in_specs=[pl.BlockSpec(page_block, page_map(k)) for k in range(pps + 1)] + wspecs
        + [pl.BlockSpec((npc, LANES), lambda b, s, pt: (s, 0))] * 2,
        out_specs=[pl.BlockSpec((None, npc, 256), lambda b, s, pt: (b, s, 0))] * 2
        + [pl.BlockSpec((None, 256, npc), lambda b, s, pt: (b, 0, s))],
        scratch_shapes=[pltpu.VMEM((4, (pps + 1) * PAGE_SIZE, LANES), F32)],
    )
    return pl.pallas_call(
        functools.partial(_compress_kernel, pps=pps, feature_major=feature_major),
        grid_spec=grid_spec,
        out_shape=[jax.ShapeDtypeStruct((n_seq, n_pages * 8, 256), BF16)] * 2
        + [jax.ShapeDtypeStruct((n_seq, 256, n_pages * 8), BF16)],
        compiler_params=_params("parallel", "arbitrary"),
        name="nsa_compress",
    )(pt_flat, *([rows3d] * (pps + 1)), *wk, *wv, cos_c, sin_c)


def _nsa_attn_kernel(qt_ref, gt_ref, kc_ref, vct_ref, kk_ref, vt_ref, ovlt_ref, emt_ref, ot_ref, s0_ref, s1_ref, *,
                     tq, tks, n_top, nblk):
    i = pl.program_id(1)
    c0 = i * tq
    n4 = NSA_GROUP * tq
    pos1 = c0 + lax.broadcasted_iota(jnp.int32, (1, tq), 1)
    pos4 = c0 + lax.broadcasted_iota(jnp.int32, (1, n4), 1) % tq
    blk = lax.broadcasted_iota(jnp.int32, (nblk, tq), 0)
    cur = pos1 // SEL_BLOCK
    heads = range(NSA_KV_HEADS)
    pair = [slice((h // 2) * LANES, (h // 2 + 1) * LANES) for h in heads]
    vrow = [slice(h * HEAD_DIM, (h + 1) * HEAD_DIM) for h in heads]
    qpad = []
    for h in heads:
        q4 = jnp.concatenate(
            [qt_ref[(NSA_GROUP * h + g) * HEAD_DIM:(NSA_GROUP * h + g + 1) * HEAD_DIM, :] for g in range(NSA_GROUP)],
            axis=1)
        qpad.append(_pad_pair(q4, h % 2 == 1))

    o_c, q_aug = [], []
    for h in heads:
        s = _dot(kc_ref[:, pair[h]], qpad[h])
        cend = lax.broadcasted_iota(jnp.int32, s.shape, 0) * CMP_STRIDE + (CMP_BLOCK - 1)
        s = jnp.where(cend <= pos4, s, -jnp.inf)
        m = jnp.max(s, axis=0, keepdims=True)
        p = jnp.exp2(s - jnp.where(m > -jnp.inf, m, 0.0))
        p = p * (1.0 / jnp.maximum(jnp.sum(p, axis=0, keepdims=True), 1e-30))
        o_c.append(_dot(vct_ref[vrow[h], :], p.astype(BF16)))
        psum = p[:, 0:tq] + p[:, tq:2 * tq] + p[:, 2 * tq:3 * tq] + p[:, 3 * tq:4 * tq]
        imp = _dot3_rhs(ovlt_ref[...], psum)
        forced = (blk == 0) | (blk == cur) | (blk == cur - 1)
        imp = jnp.where(forced, jnp.inf, imp)
        imp = jnp.where(blk <= cur, imp, -jnp.inf)
        sel = _topk_mask_t(imp, n_top)
        sel_bias = jnp.where(sel > 0.5, 0.0, NEG).astype(BF16)
        sel_bias = jnp.concatenate([sel_bias] * NSA_GROUP, axis=1)
        fill = jnp.zeros((LANES - nblk, n4), BF16)
        q_aug.append(jnp.concatenate([qpad[h], sel_bias, fill], axis=0))

    def finish(carry):
        return [_online_finish_t(c) for c in carry]

    init = tuple(_online_init_t(n4) for _ in heads)
    t_last = (c0 + tq - 1) // tks

    n_tiles = vt_ref.shape[0]

    def run(t_first, scores, consume):
        def put(t, s_ref):
            for h, s in enumerate(scores(jnp.minimum(t, n_tiles - 1))):
                s_ref[h] = s

        def body(u, carry):
            t = t_first + 2 * u
            put(t + 1, s1_ref)
            carry = consume(t, s0_ref, carry, False)
            put(t + 2, s0_ref)
            return consume(t + 1, s1_ref, carry, False)

        put(t_first, s0_ref)
        pairs = (t_last - t_first) // 2
        carry = lax.fori_loop(0, pairs, body, init)
        t = t_first + 2 * pairs
        put(t + 1, s1_ref)
        carry = consume(t, s0_ref, carry, True)
        return finish(consume(t + 1, s1_ref, carry, True))

    def sel_scores(t):
        k0 = pl.multiple_of(t * tks, tks)
        e = emt_ref[t]
        return [_dot(jnp.concatenate([kk_ref[pl.ds(k0, tks), pair[h]], e], axis=1), q_aug[h]) for h in heads]

    def sel_consume(t, s_ref, carry, masked):
        bias = None
        if masked:
            kpos = t * tks + lax.broadcasted_iota(jnp.int32, (tks, n4), 0)
            bias = jnp.where(kpos <= pos4, 0.0, NEG)
        tv = jnp.minimum(t, n_tiles - 1)
        return tuple(_online_update_t(carry[h], s_ref[h], bias, vt_ref[tv, vrow[h], :]) for h in heads)

    o_s = run(0, sel_scores, sel_consume)

    def win_put(t, s_ref):
        k0 = pl.multiple_of(jnp.maximum(t, 0) * tks, tks)
        for h in heads:
            s_ref[h] = _dot(kk_ref[pl.ds(k0, tks), 256 + pair[h].start:256 + pair[h].stop], qpad[h])

    def win_consume(t, s_ref, carry):
        tv = jnp.maximum(t, 0)
        diff = pos4 - (tv * tks + lax.broadcasted_iota(jnp.int32, (tks, n4), 0))
        bias = jnp.where((diff >= 0) & (diff <= WINDOW) & (t >= 0), 0.0, NEG)
        return tuple(_online_update_t(carry[h], s_ref[h], bias,
                                      vt_ref[tv, 256 + vrow[h].start:256 + vrow[h].stop, :]) for h in heads)

    n_win = -(-WINDOW // tks) + 1
    bufs = (s0_ref, s1_ref)
    carry = init
    win_put(t_last - (n_win - 1), bufs[0])
    for e in range(n_win):
        if e + 1 < n_win:
            win_put(t_last - (n_win - 2 - e), bufs[(e + 1) % 2])
        carry = win_consume(t_last - (n_win - 1 - e), bufs[e % 2], carry)
    o_w = finish(carry)

    for h in heads:
        for g in range(NSA_GROUP):
            j = (h * NSA_GROUP + g) * 3
            cs = slice(g * tq, (g + 1) * tq)
            o = (gt_ref[j:j + 1, :] * o_c[h][:, cs] + gt_ref[j + 1:j + 2, :] * o_s[h][:, cs]
                 + gt_ref[j + 2:j + 3, :] * o_w[h][:, cs])
            ot_ref[(NSA_GROUP * h + g) * HEAD_DIM:(NSA_GROUP * h + g + 1) * HEAD_DIM, :] = o.astype(BF16)


def nsa_attn(qt, gt, kc, vct, kk, vt, ovlt, emt, *, n_b, t, tq):
    nq = t // tq
    tks = vt.shape[2]
    assert tks % tq == 0, "a query tile must sit inside one key tile"
    npiece = kc.shape[1]
    nblk = ovlt.shape[0]
    n_top = min(SEL_TOPN, t // SEL_BLOCK)
    col = lambda b, i: (0, b * nq + i)
    per_b = lambda b, i: (b, 0, 0)
    return pl.pallas_call(
        functools.partial(_nsa_attn_kernel, tq=tq, tks=tks, n_top=n_top, nblk=nblk),
        grid=(n_b, nq),
        in_specs=[pl.BlockSpec((1024, tq), col), pl.BlockSpec((LANES, tq), col),
                  pl.BlockSpec((None, npiece, 256), per_b), pl.BlockSpec((None, 256, npiece), per_b),
                  pl.BlockSpec((None, t, 512), per_b), pl.BlockSpec((t // tks, 512, tks), per_b),
                  pl.BlockSpec(ovlt.shape, lambda b, i: (0, 0)),
                  pl.BlockSpec(emt.shape, lambda b, i: (0, 0, 0))],
        out_specs=pl.BlockSpec((1024, tq), col),
        out_shape=jax.ShapeDtypeStruct((1024, n_b * t), BF16),
        scratch_shapes=[pltpu.VMEM((NSA_KV_HEADS, tks, NSA_GROUP * tq), F32)] * 2,
        compiler_params=_params("parallel", "arbitrary"),
        name="nsa_attn",
    )(qt, gt, kc, vct, kk.reshape(n_b, t, 512), vt, ovlt, emt)


def _nsa_s_sel_kernel(q_ref, kc_ref, vc_ref, ovl_ref, gm_ref, oc_ref, idx_ref, *, pos, n_pick):
    for i in range(q_ref.shape[0]):
        s = _dot_nt(q_ref[i], kc_ref[i])
        cend = lax.broadcasted_iota(jnp.int32, s.shape, 1) * CMP_STRIDE + (CMP_BLOCK - 1)
        p = _masked_softmax(s, cend <= pos)
        oc_ref[i] = _dot(p.astype(BF16), vc_ref[i])
        psum = _dot3_rhs(gm_ref[...], p)
        imp = _dot3(psum, ovl_ref[...])
        blk = lax.broadcasted_iota(jnp.int32, imp.shape, 1)
        cur = pos // SEL_BLOCK
        imp = jnp.where((blk == 0) | (blk == cur - 1), jnp.inf, imp)
        imp = jnp.where(blk < cur, imp, -jnp.inf)
        lane = blk.astype(F32)
        slot = lax.broadcasted_iota(jnp.int32, (8, LANES), 1)
        picks = jnp.zeros((8, LANES), F32)
        for r in range(n_pick):
            m = jnp.max(imp, axis=-1, keepdims=True)
            idx = jnp.min(jnp.where(imp == m, lane, 1e9), axis=-1, keepdims=True)
            imp = jnp.where(lane == idx, -jnp.inf, imp)
            picks = jnp.where(slot == r, idx, picks)
        idx_ref[i] = picks.astype(jnp.int32)


def nsa_s_sel(qbd, kc, vc, ovl, gm, *, pos, n_pick):
    n_b, npiece = kc.shape[0], kc.shape[1]
    spb = 4 if n_b % 4 == 0 else 1
    per_b = lambda b: (b, 0, 0)
    return pl.pallas_call(
        functools.partial(_nsa_s_sel_kernel, pos=pos, n_pick=n_pick),
        grid=(n_b // spb,),
        in_specs=[pl.BlockSpec((spb, 16, 256), per_b), pl.BlockSpec((spb, npiece, 256), per_b),
                  pl.BlockSpec((spb, npiece, 256), per_b),
                  pl.BlockSpec(ovl.shape, lambda b: (0, 0)), pl.BlockSpec(gm.shape, lambda b: (0, 0))],
        out_specs=[pl.BlockSpec((spb, 16, 256), per_b), pl.BlockSpec((spb, 8, LANES), per_b)],
        out_shape=[jax.ShapeDtypeStruct((n_b, 16, 256), F32), jax.ShapeDtypeStruct((n_b, 8, LANES), jnp.int32)],
        compiler_params=_params("parallel"),
        name="nsa_s_sel",
    )(qbd, kc, vc, ovl, gm)


def _attend_with_new(q, kts, vts, biases, k_new, v_new):
    ss = []
    for kt, bias in zip(kts, biases):
        s = _dot(q, kt.astype(BF16))
        ss.append(s if bias is None else s + bias)
    s = ss[0] if len(ss) == 1 else jnp.concatenate(ss, axis=1)
    s_new = jnp.sum(q.astype(F32) * k_new.astype(F32), axis=-1, keepdims=True)
    m = jnp.maximum(jnp.max(s, axis=-1, keepdims=True), s_new)
    p = jnp.exp(s - m)
    p_new = jnp.exp(s_new - m)
    l = jnp.sum(p, axis=-1, keepdims=True) + p_new
    acc = p_new * v_new.astype(F32)
    off = 0
    for vt in vts:
        n = vt.shape[1]
        acc = acc + _dot_nt(p[:, off:off + n].astype(BF16), vt.astype(BF16))
        off += n
    return acc / l


def _nsa_s_attn_kernel(pt_ref, ix_ref, q_ref, *refs, n_pick):
    del pt_ref
    kblk = refs[:n_pick]
    vblk = refs[n_pick:2 * n_pick]
    ks_new, vs_new, kw_new, vw_new, kw_ref, vw_ref, os_ref, ow_ref = refs[2 * n_pick:]
    base = (pl.program_id(0) * NSA_KV_HEADS + pl.program_id(1)) * n_pick
    half = lax.broadcasted_iota(jnp.int32, (1, PAGE_SIZE), 1) // SEL_BLOCK
    biases = [jnp.where(half == ix_ref[base + r] % 2, 0.0, NEG) for r in range(n_pick)]
    q = q_ref[...]
    os_ref[...] = _attend_with_new(q, [r[...] for r in kblk], [r[...] for r in vblk], biases, ks_new[...], vs_new[...])
    ow_ref[...] = _attend_with_new(q, [kw_ref[...]], [vw_ref[...]], [None], kw_new[...], vw_new[...])


def nsa_s_attn(pt_flat, ix_flat, q4, cache_fm, new_rows, win_fm, *, n_pages, n_pick):
    n_b = q4.shape[0]
    nwin = win_fm.shape[2]
    per_page = PAGE_SIZE // SEL_BLOCK

    def kv_map(r, stream):
        def f(b, h, pt, ix):
            j = ix[(b * NSA_KV_HEADS + h) * n_pick + r]
            return (pt[b * n_pages + j // per_page], stream * NSA_KV_HEADS + h, 0)
        return f

    new_map = lambda s: (lambda b, h, pt, ix: (b, s * NSA_KV_HEADS + h, 0, 0))
    qo_spec = pl.BlockSpec((None, None, 8, HEAD_DIM), lambda b, h, pt, ix: (b, h, 0, 0))
    grid_spec = pltpu.PrefetchScalarGridSpec(
        num_scalar_prefetch=2,
        grid=(n_b, NSA_KV_HEADS),
        in_specs=[qo_spec]
        + [pl.BlockSpec((None, HEAD_DIM, PAGE_SIZE), kv_map(r, 2)) for r in range(n_pick)]
        + [pl.BlockSpec((None, HEAD_DIM, PAGE_SIZE), kv_map(r, 3)) for r in range(n_pick)]
        + [pl.BlockSpec((None, None, 1, HEAD_DIM), new_map(s)) for s in range(4)]
        + [pl.BlockSpec((None, HEAD_DIM, nwin), lambda b, h, pt, ix: (b, h, 0)),
           pl.BlockSpec((None, HEAD_DIM, nwin), lambda b, h, pt, ix: (b, NSA_KV_HEADS + h, 0))],
        out_specs=[qo_spec, qo_spec],
    )
    return pl.pallas_call(
        functools.partial(_nsa_s_attn_kernel, n_pick=n_pick),
        grid_spec=grid_spec,
        out_shape=[jax.ShapeDtypeStruct((n_b, NSA_KV_HEADS, 8, HEAD_DIM), F32)] * 2,
        compiler_params=_params("parallel", "arbitrary"),
        name="nsa_s_attn",
    )(pt_flat, ix_flat, q4, *([cache_fm] * (2 * n_pick)), *([new_rows] * 4), win_fm, win_fm)


def _nsa_out_s_kernel(oc_ref, os_ref, ow_ref, g0_ref, g1_ref, g2_ref, w_ref, r_ref, o_ref):
    o = g0_ref[...] * oc_ref[...] + g1_ref[...] * os_ref[...] + g2_ref[...] * ow_ref[...]
    o_ref[...] = r_ref[...] + _dot(o.astype(BF16), w_ref[...])


def nsa_out_s(oc, osel, ow, g0, g1, g2, w, res):
    m, d = res.shape
    full = pl.BlockSpec((m, d), lambda i: (0, 0))
    return pl.pallas_call(
        _nsa_out_s_kernel,
        grid=(1,),
        in_specs=[full] * 6 + [pl.BlockSpec(w.shape, lambda i: (0, 0)), full],
        out_specs=full,
        out_shape=jax.ShapeDtypeStruct((m, d), F32),
        compiler_params=_params("arbitrary"),
        name="nsa_out_s",
    )(oc, osel, ow, g0, g1, g2, w, res)


def _sconv_kernel(x_ref, g_ref, win_ref, wc_ref, wout_ref, o_ref, st_ref, carry_ref):
    d = x_ref.shape[1]
    tm = x_ref.shape[0]

    @pl.when(pl.program_id(1) == 0)
    def _():
        carry_ref[...] = jnp.zeros(carry_ref.shape, F32)

    x = x_ref[...]
    xn = _rms(x, g_ref[...]).astype(BF16)
    b_gate = _dot(xn, win_ref[:, 0:d])
    pre = _dot(xn, win_ref[:, d:2 * d]) * _dot(xn, win_ref[:, 2 * d:3 * d])
    row = lax.broadcasted_iota(jnp.int32, (tm, d), 0)
    back1 = jnp.where(row == 0, carry_ref[7:8, :], pltpu.roll(pre, 1, 0))
    back2 = jnp.where(row == 0, carry_ref[6:7, :], jnp.where(row == 1, carry_ref[7:8, :], pltpu.roll(pre, 2, 0)))
    y = back2 * wc_ref[0:1, :] + back1 * wc_ref[1:2, :] + pre * wc_ref[2:3, :]
    tail = pre[tm - 8:tm]
    carry_ref[...] = tail
    st_ref[...] = tail
    o_ref[...] = x + _dot((b_gate * y).astype(BF16), wout_ref[...])


def sconv_prompt(x, g, w_in, w_conv, w_out, *, n_b, t, tm):
    d = x.shape[1]
    nt = t // tm
    row = lambda b, i: (b * nt + i, 0)
    const = lambda b, i: (0, 0)
    return pl.pallas_call(
        _sconv_kernel,
        grid=(n_b, nt),
        in_specs=[pl.BlockSpec((tm, d), row), pl.BlockSpec((1, d), const),
                  pl.BlockSpec((d, 3 * d), const, pipeline_mode=pl.Buffered(1)),
                  pl.BlockSpec(w_conv.shape, const),
                  pl.BlockSpec((d, d), const, pipeline_mode=pl.Buffered(1))],
        out_specs=[pl.BlockSpec((tm, d), row), pl.BlockSpec((None, 8, d), lambda b, i: (b, 0, 0))],
        out_shape=[jax.ShapeDtypeStruct((n_b * t, d), F32), jax.ShapeDtypeStruct((n_b, 8, d), F32)],
        scratch_shapes=[pltpu.VMEM((8, d), F32)],
        compiler_params=_params("parallel", "arbitrary"),
        name="sconv_prompt",
    )(x, g, w_in, w_conv, w_out)


def _sconv_s_kernel(x_ref, g_ref, win_ref, wc_ref, wout_ref, p0_ref, p1_ref, o_ref, pre_ref):
    d = x_ref.shape[1]
    x = x_ref[...]
    xn = _rms(x, g_ref[...]).astype(BF16)
    b_gate = _dot(xn, win_ref[:, 0:d])
    pre = _dot(xn, win_ref[:, d:2 * d]) * _dot(xn, win_ref[:, 2 * d:3 * d])
    y = p0_ref[...] * wc_ref[0:1, :] + p1_ref[...] * wc_ref[1:2, :] + pre * wc_ref[2:3, :]
    pre_ref[...] = pre
    o_ref[...] = x + _dot((b_gate * y).astype(BF16), wout_ref[...])


def sconv_sample(x, g, w_in, w_conv, w_out, past0, past1):
    m, d = x.shape
    full = lambda a: pl.BlockSpec(a.shape, lambda i: (0,) * a.ndim)
    args = (x, g, w_in, w_conv, w_out, past0, past1)
    return pl.pallas_call(
        _sconv_s_kernel,
        grid=(1,),
        in_specs=[full(a) for a in args],
        out_specs=[pl.BlockSpec((m, d), lambda i: (0, 0))] * 2,
        out_shape=[jax.ShapeDtypeStruct((m, d), F32)] * 2,
        compiler_params=_params("arbitrary"),
        name="sconv_sample",
    )(*args)


def _moba_in_kernel(x_ref, g_ref, w_ref, cos_ref, sin_ref, q_ref, rows_ref, kb_ref, vb_ref, km_ref):
    d = x_ref.shape[1]
    xn = _rms(x_ref[...], g_ref[...]).astype(BF16)
    cos = cos_ref[...]
    sin = sin_ref[...]
    q_ref[...] = (_rope(_dot(xn, w_ref[:, 0:d]), cos, sin) * SCALE).astype(BF16)
    k = _rope(_dot(xn, w_ref[:, d:2 * d]), cos, sin)
    v = _dot(xn, w_ref[:, 2 * d:3 * d])
    rows_ref[:, 0:d] = k
    rows_ref[:, d:2 * d] = v
    kb_ref[...] = k.astype(BF16)
    vb_ref[...] = v.astype(BF16)
    km_ref[...] = jnp.sum(k, axis=0, keepdims=True) * (1.0 / MOBA_BLOCK)


def moba_in(x, g, w, cos, sin, *, tm):
    m, d = x.shape
    nt = cos.shape[0] // tm
    row = lambda i: (i, 0)
    tab = lambda i: (i % nt, 0)
    const = lambda i: (0, 0)
    return pl.pallas_call(
        _moba_in_kernel,
        grid=(m // tm,),
        in_specs=[pl.BlockSpec((tm, d), row), pl.BlockSpec((1, d), const),
                  pl.BlockSpec((d, 3 * d), const, pipeline_mode=pl.Buffered(1)),
                  pl.BlockSpec((tm, LANES), tab), pl.BlockSpec((tm, LANES), tab)],
        out_specs=[pl.BlockSpec((tm, d), row), pl.BlockSpec((tm, 2 * d), row), pl.BlockSpec((tm, d), row),
                   pl.BlockSpec((tm, d), row), pl.BlockSpec((None, 1, d), lambda i: (i, 0, 0))],
        out_shape=[jax.ShapeDtypeStruct((m, d), BF16), jax.ShapeDtypeStruct((m, 2 * d), F32),
                   jax.ShapeDtypeStruct((m, d), BF16), jax.ShapeDtypeStruct((m, d), BF16),
                   jax.ShapeDtypeStruct((m // tm, 1, d), F32)],
        compiler_params=_params("parallel"),
        name="moba_in",
    )(x, g, w, cos, sin)


def _moba_in_t_kernel(x_ref, g_ref, wk_ref, wqkt_ref, wvt_ref, cos_ref, sin_ref, cost_ref, sint_ref,
                      qt_ref, rowst_ref, kb_ref, vt_ref, km_ref):
    d = x_ref.shape[1]
    xn = _rms(x_ref[...], g_ref[...]).astype(BF16)
    qk = _rope_t(_dot_nt(wqkt_ref[...], xn), cost_ref[...], sint_ref[...])
    qt_ref[...] = (qk[0:d] * SCALE_LOG2).astype(BF16)
    v_t = _dot_nt(wvt_ref[...], xn)
    rowst_ref[0:d, :] = qk[d:2 * d]
    rowst_ref[d:2 * d, :] = v_t
    vt_ref[...] = v_t.astype(BF16)
    k = _rope(_dot(xn, wk_ref[...]), cos_ref[...], sin_ref[...])
    kb_ref[...] = k.astype(BF16)
    km_ref[...] = jnp.sum(k, axis=0, keepdims=True) * (1.0 / MOBA_BLOCK)


def moba_in_t(x, g, w_k, wqk_t, wv_t, cos, sin, cos_t, sin_t, *, n_b):
    m, d = x.shape
    tm = MOBA_BLOCK
    nt = cos.shape[0] // tm
    row = lambda i: (i, 0)
    const = lambda i: (0, 0)
    one = pl.Buffered(1)
    return pl.pallas_call(
        _moba_in_t_kernel,
        grid=(m // tm,),
        in_specs=[pl.BlockSpec((tm, d), row), pl.BlockSpec((1, d), const),
                  pl.BlockSpec(w_k.shape, const, pipeline_mode=one), pl.BlockSpec(wqk_t.shape, const, pipeline_mode=one),
                  pl.BlockSpec(wv_t.shape, const, pipeline_mode=one),
                  pl.BlockSpec((tm, LANES), lambda i: (i % nt, 0)), pl.BlockSpec((tm, LANES), lambda i: (i % nt, 0)),
                  pl.BlockSpec((HEAD_DIM, tm), lambda i: (0, i % nt)), pl.BlockSpec((HEAD_DIM, tm), lambda i: (0, i % nt))],
        out_specs=[pl.BlockSpec((d, tm), lambda i: (0, i)),
                   pl.BlockSpec((None, 2 * d, tm), lambda i: (i // nt, 0, i % nt)), pl.BlockSpec((tm, d), row),
                   pl.BlockSpec((None, d, tm), lambda i: (i, 0, 0)), pl.BlockSpec((None, 1, d), lambda i: (i, 0, 0))],
        out_shape=[jax.ShapeDtypeStruct((d, m), BF16), jax.ShapeDtypeStruct((n_b, 2 * d, m // n_b), F32),
                   jax.ShapeDtypeStruct((m, d), BF16), jax.ShapeDtypeStruct((m // tm, d, tm), BF16),
                   jax.ShapeDtypeStruct((m // tm, 1, d), F32)],
        compiler_params=_params("parallel"),
        name="moba_in_t",
    )(x, g, w_k, wqk_t, wv_t, cos, sin, cos_t, sin_t)


def _moba_attn_kernel(qt_ref, k_ref, vt_ref, km_ref, ot_ref, s0_ref, s1_ref, *, tq, n_top):
    i = pl.program_id(2)
    c0 = i * tq
    nblk = km_ref.shape[0]
    pos = c0 + lax.broadcasted_iota(jnp.int32, (1, tq), 1)
    cur = pos // MOBA_BLOCK
    blk = lax.broadcasted_iota(jnp.int32, (nblk, tq), 0)
    m1, m2, m3 = _split3(km_ref[...])
    vrow = [slice(hh * HEAD_DIM, (hh + 1) * HEAD_DIM) for hh in range(2)]
    q_aug = []
    for hh in range(2):
        qp = _pad_pair(qt_ref[vrow[hh], :], hh == 1)
        gate = _dot(m1, qp) + _dot(m2, qp) + _dot(m3, qp)
        gate = jnp.where(blk < cur, gate, -jnp.inf)
        allow = (blk == cur) | (_topk_mask_t(gate, n_top) > 0.5)
        fill = jnp.zeros((LANES - nblk, tq), BF16)
        q_aug.append(jnp.concatenate([qp, jnp.where(allow, 0.0, NEG).astype(BF16), fill], axis=0))

    tile_id = lax.broadcasted_iota(jnp.int32, (MOBA_BLOCK, LANES), 1)

    def put(t, s_ref):
        k = k_ref[pl.ds(pl.multiple_of(t * MOBA_BLOCK, MOBA_BLOCK), MOBA_BLOCK), :]
        k_aug = jnp.concatenate([k, jnp.where(tile_id == t, 1.0, 0.0).astype(BF16)], axis=1)
        for hh in range(2):
            s_ref[hh] = _dot(k_aug, q_aug[hh])

    def consume(t, s_ref, carry, causal):
        bias = None
        if causal:
            kpos = t * MOBA_BLOCK + lax.broadcasted_iota(jnp.int32, (MOBA_BLOCK, tq), 0)
            bias = jnp.where(kpos <= pos, 0.0, NEG)
        return tuple(_online_update_t(carry[hh], s_ref[hh], bias, vt_ref[t, vrow[hh], :]) for hh in range(2))

    def body(u, carry):
        t = 2 * u
        put(t + 1, s1_ref)
        carry = consume(t, s0_ref, carry, False)
        put(t + 2, s0_ref)
        return consume(t + 1, s1_ref, carry, False)

    t_diag = c0 // MOBA_BLOCK
    put(0, s0_ref)
    res = lax.fori_loop(0, t_diag // 2, body, (_online_init_t(tq), _online_init_t(tq)))
    put(t_diag + 1, s1_ref)
    res = consume(t_diag, s0_ref, res, True)
    res = consume(t_diag + 1, s1_ref, res, True)
    for hh in range(2):
        ot_ref[vrow[hh], :] = _online_finish_t(res[hh]).astype(BF16)


def moba_attn(qt, kb, vt, kmean, *, n_b, t, tq):
    assert tq == 2 * MOBA_BLOCK, "the kernel's tile pairing assumes two MoBA blocks per query tile"
    nq = t // tq
    d = qt.shape[0]
    nblk = kmean.shape[1]
    n_top = min(MOBA_TOPK, t // MOBA_BLOCK)
    qmap = lambda b, hp, i: (hp, b * nq + i)
    kvmap = lambda b, hp, i: (b, 0, hp)
    return pl.pallas_call(
        functools.partial(_moba_attn_kernel, tq=tq, n_top=n_top),
        grid=(n_b, d // LANES, nq),
        in_specs=[pl.BlockSpec((LANES, tq), qmap), pl.BlockSpec((None, t, LANES), kvmap),
                  pl.BlockSpec((t // MOBA_BLOCK, LANES, MOBA_BLOCK), lambda b, hp, i: (b, hp, 0)),
                  pl.BlockSpec((None, nblk, LANES), kvmap)],
        out_specs=pl.BlockSpec((LANES, tq), qmap),
        out_shape=jax.ShapeDtypeStruct((d, n_b * t), BF16),
        scratch_shapes=[pltpu.VMEM((2, MOBA_BLOCK, tq), F32)] * 2,
        compiler_params=_params("parallel", "parallel", "arbitrary"),
        name="moba_attn",
    )(qt, kb.reshape(n_b, t, d), vt, kmean)


def _moba_s_select_kernel(pt_ref, *refs, pps, n_top, n_blocks):
    del pt_ref
    pages = refs[:pps]
    qbd_ref, idx_ref, km_ref = refs[pps:]
    s = pl.program_id(1)
    per_blk = MOBA_BLOCK // PAGE_SIZE
    blocks_per_step = pps // per_blk

    @pl.when(s == 0)
    def _():
        km_ref[...] = jnp.zeros(km_ref.shape, F32)

    lane = lax.broadcasted_iota(jnp.int32, km_ref.shape, 1)
    km = km_ref[...]
    for j in range(blocks_per_step):
        tot = pages[per_blk * j][...]
        for e in range(1, per_blk):
            tot = tot + pages[per_blk * j + e][...]
        mean = jnp.sum(tot, axis=1, keepdims=True) * (1.0 / MOBA_BLOCK)
        km = jnp.where(lane == s * blocks_per_step + j, mean, km)
    km_ref[...] = km

    @pl.when(s == pl.num_programs(1) - 1)
    def _():
        gate = _dot3_rhs(qbd_ref[...], km_ref[...])
        blk = lax.broadcasted_iota(jnp.int32, gate.shape, 1)
        gate = jnp.where(blk < n_blocks, gate, -jnp.inf)
        lane_f = blk.astype(F32)
        picks = jnp.zeros(gate.shape, F32)
        for r in range(n_top):
            m = jnp.max(gate, axis=-1, keepdims=True)
            idx = jnp.min(jnp.where(gate == m, lane_f, 1e9), axis=-1, keepdims=True)
            gate = jnp.where(lane_f == idx, -jnp.inf, gate)
            picks = jnp.where(blk == r, idx, picks)
        idx_ref[...] = picks.astype(jnp.int32)


def moba_s_select(cache_fm, pt_flat, qbd, *, n_pages, n_top):
    n_b, _, d = qbd.shape
    pps = min(16, n_pages)
    per_blk = MOBA_BLOCK // PAGE_SIZE
    page_map = lambda k: (lambda b, s, pt: (pt[b * n_pages + s * pps + k], 0, 0))
    grid_spec = pltpu.PrefetchScalarGridSpec(
        num_scalar_prefetch=1,
        grid=(n_b, n_pages // pps),
        in_specs=[pl.BlockSpec((None, d, PAGE_SIZE), page_map(k)) for k in range(pps)]
        + [pl.BlockSpec((None, LANES, d), lambda b, s, pt: (b, 0, 0))],
        out_specs=pl.BlockSpec((None, LANES, LANES), lambda b, s, pt: (b, 0, 0)),
        scratch_shapes=[pltpu.VMEM((d, LANES), F32)],
    )
    return pl.pallas_call(
        functools.partial(_moba_s_select_kernel, pps=pps, n_top=n_top, n_blocks=n_pages // per_blk),
        grid_spec=grid_spec,
        out_shape=jax.ShapeDtypeStruct((n_b, LANES, LANES), jnp.int32),
        compiler_params=_params("parallel", "arbitrary"),
        name="moba_s_select",
    )(pt_flat, *([cache_fm] * pps), qbd)


def _moba_s_attn_kernel(pt_ref, ix_ref, q_ref, *refs, n_pg, hps):
    del pt_ref, ix_ref
    kblk = refs[:hps * n_pg]
    vblk = refs[hps * n_pg:2 * hps * n_pg]
    k_new, v_new, o_ref = refs[2 * hps * n_pg:]
    for hi in range(hps):
        mine = slice(hi * n_pg, (hi + 1) * n_pg)
        o_ref[hi] = _attend_with_new(q_ref[hi], [r[...] for r in kblk[mine]], [r[...] for r in vblk[mine]],
                                     [None] * n_pg, k_new[hi], v_new[hi])


def moba_s_attn(pt_flat, ix_flat, q4, cache_fm, k_new, v_new, *, n_pages, n_top, n_heads):
    n_b = q4.shape[0]
    per_blk = MOBA_BLOCK // PAGE_SIZE
    n_pg = n_top * per_blk
    hps = 4

    def kv_map(hi, r, row0):
        def f(b, hg, pt, ix):
            h = hg * hps + hi
            j = ix[(b * n_heads + h) * n_top + r // per_blk]
            return (pt[b * n_pages + j * per_blk + r % per_blk], row0 + h, 0)
        return f

    group = lambda b, hg, pt, ix: (b, hg, 0, 0)
    new_spec = pl.BlockSpec((None, hps, 1, HEAD_DIM), group)
    qo_spec = pl.BlockSpec((None, hps, 8, HEAD_DIM), group)
    page = (None, HEAD_DIM, PAGE_SIZE)
    grid_spec = pltpu.PrefetchScalarGridSpec(
        num_scalar_prefetch=2,
        grid=(n_b, n_heads // hps),
        in_specs=[qo_spec]
        + [pl.BlockSpec(page, kv_map(hi, r, 0)) for hi in range(hps) for r in range(n_pg)]
        + [pl.BlockSpec(page, kv_map(hi, r, n_heads)) for hi in range(hps) for r in range(n_pg)]
        + [new_spec, new_spec],
        out_specs=qo_spec,
    )
    return pl.pallas_call(
        functools.partial(_moba_s_attn_kernel, n_pg=n_pg, hps=hps),
        grid_spec=grid_spec,
        out_shape=jax.ShapeDtypeStruct((n_b, n_heads, 8, HEAD_DIM), F32),
        compiler_params=_params("parallel", "arbitrary"),
        name="moba_s_attn",
    )(pt_flat, ix_flat, q4, *([cache_fm] * (2 * hps * n_pg)), k_new, v_new)


def _layer_norm_silu(y, g, b):
    yc = y - jnp.mean(y, axis=-1, keepdims=True)
    yn = yc * lax.rsqrt(jnp.mean(yc * yc, axis=-1, keepdims=True) + LN_EPS) * g + b
    return yn * _sigmoid(yn)


def _conf_kernel(x_ref, g_ref, w1_ref, wdw_ref, bdw_ref, lg_ref, lb_ref, w2_ref, o_ref, st_ref, ubuf_ref, sh_ref,
                 *, hist):
    tm, d = x_ref.shape
    width = wdw_ref.shape[0]

    @pl.when(pl.program_id(1) == 0)
    def _():
        ubuf_ref[0:hist, :] = jnp.zeros((hist, d), F32)

    x = x_ref[...]
    xn = _rms(x, g_ref[...]).astype(BF16)
    u = _dot(xn, w1_ref[:, 0:d]) * _sigmoid(_dot(xn, w1_ref[:, d:2 * d]))
    ubuf_ref[hist:hist + tm, :] = u
    base = hist - (width - 1)
    rows = ubuf_ref.shape[0]
    y = bdw_ref[...]
    for r in range(8):
        taps = [k for k in range(width) if (base + k) % 8 == r]
        if not taps:
            continue
        src = ubuf_ref
        if r:
            sh_ref[...] = pltpu.roll(ubuf_ref[...], rows - r, 0)
            src = sh_ref
        for k in taps:
            y = y + src[base + k - r:base + k - r + tm, :] * wdw_ref[k:k + 1, :]
    z = _layer_norm_silu(y, lg_ref[...], lb_ref[...])
    o_ref[...] = x + _dot(z.astype(BF16), w2_ref[...])
    tail = ubuf_ref[tm:tm + hist, :]
    st_ref[...] = tail
    ubuf_ref[0:hist, :] = tail


def conf_prompt(x, g, w1, wdw, bdw, lg, lb, w2, *, n_b, t, tm):
    d = x.shape[1]
    nt = t // tm
    hist = 32
    row = lambda b, i: (b * nt + i, 0)
    const = lambda b, i: (0, 0)
    return pl.pallas_call(
        functools.partial(_conf_kernel, hist=hist),
        grid=(n_b, nt),
        in_specs=[pl.BlockSpec((tm, d), row), pl.BlockSpec((1, d), const),
                  pl.BlockSpec((d, 2 * d), const, pipeline_mode=pl.Buffered(1)),
                  pl.BlockSpec(wdw.shape, const), pl.BlockSpec((1, d), const), pl.BlockSpec((1, d), const),
                  pl.BlockSpec((1, d), const), pl.BlockSpec((d, d), const, pipeline_mode=pl.Buffered(1))],
        out_specs=[pl.BlockSpec((tm, d), row), pl.BlockSpec((None, hist, d), lambda b, i: (b, 0, 0))],
        out_shape=[jax.ShapeDtypeStruct((n_b * t, d), F32), jax.ShapeDtypeStruct((n_b, hist, d), F32)],
        scratch_shapes=[pltpu.VMEM((hist + tm, d), F32)] * 2,
        compiler_params=_params("parallel", "arbitrary"),
        name="conf_prompt",
    )(x, g, w1, wdw, bdw, lg, lb, w2)


def _conf_s_kernel(x_ref, g_ref, w1_ref, wdw_ref, bdw_ref, lg_ref, lb_ref, w2_ref, past_ref, o_ref, u_ref):
    d = x_ref.shape[1]
    width = wdw_ref.shape[0]
    x = x_ref[...]
    xn = _rms(x, g_ref[...]).astype(BF16)
    u = _dot(xn, w1_ref[:, 0:d]) * _sigmoid(_dot(xn, w1_ref[:, d:2 * d]))
    y = bdw_ref[...] + past_ref[0] * wdw_ref[0:1, :]
    for k in range(1, width - 1):
        y = y + past_ref[k] * wdw_ref[k:k + 1, :]
    y = y + u * wdw_ref[width - 1:width, :]
    z = _layer_norm_silu(y, lg_ref[...], lb_ref[...])
    u_ref[...] = u
    o_ref[...] = x + _dot(z.astype(BF16), w2_ref[...])


def conf_sample(x, g, w1, wdw, bdw, lg, lb, w2, past_t):
    m, d = x.shape
    full = lambda a: pl.BlockSpec(a.shape, lambda i: (0,) * a.ndim)
    args = (x, g, w1, wdw, bdw, lg, lb, w2, past_t)
    return pl.pallas_call(
        _conf_s_kernel,
        grid=(1,),
        in_specs=[full(a) for a in args],
        out_specs=[pl.BlockSpec((m, d), lambda i: (0, 0))] * 2,
        out_shape=[jax.ShapeDtypeStruct((m, d), F32)] * 2,
        compiler_params=_params("arbitrary"),
        name="conf_sample",
    )(*args)


def _rope_tables(pos):
    half = HEAD_DIM // 2
    inv_freq = ROPE_THETA ** (-jnp.arange(half, dtype=F32) / half)
    ang = pos.astype(F32)[:, None] * inv_freq[None, :]
    cos = jnp.cos(ang)
    sin = jnp.sin(ang)
    cos = jnp.concatenate([cos, cos], axis=-1)
    sin = jnp.concatenate([-sin, sin], axis=-1)
    return jnp.tile(cos, (1, LANES // HEAD_DIM)), jnp.tile(sin, (1, LANES // HEAD_DIM))


def _overlap(n_cmp_rows, n_cols):
    i = jnp.arange(n_cmp_rows, dtype=jnp.int32)[:, None]
    j = jnp.arange(n_cols, dtype=jnp.int32)[None, :]
    start = i * CMP_STRIDE
    hit = (start <= j * SEL_BLOCK + (SEL_BLOCK - 1)) & (start + (CMP_BLOCK - 1) >= j * SEL_BLOCK)
    return hit.astype(BF16)


def _nsa_layer(hp, hs, g, cache_kv, cache_win, pt_flat, n_pages, w, *, n_b, t, n_s):
    w_in, pe_k, w1_k, w2_k, pe_v, w1_v, w2_v, w_out = w
    d = hp.shape[1]
    past_len = n_pages * PAGE_SIZE
    w_in_p = jnp.pad(w_in, ((0, 0), (0, 2688 - w_in.shape[1]))).astype(BF16)
    w_out_b = w_out.astype(BF16)
    half = CMP_BLOCK * HEAD_DIM // 2

    def cmp_weights(pe, w1, w2):
        return (pe[:CMP_STRIDE].reshape(1, half), pe[CMP_STRIDE:].reshape(1, half),
                w1[:half].astype(BF16), w1[half:].astype(BF16), w2.astype(BF16))

    wk = cmp_weights(pe_k, w1_k, w2_k)
    wv = cmp_weights(pe_v, w1_v, w2_v) + (w2_v.T.astype(BF16),)

    cos_p, sin_p = _rope_tables(jnp.arange(t, dtype=jnp.int32))
    cos_pt, sin_pt = cos_p[:, :HEAD_DIM].T, sin_p[:, :HEAD_DIM].T
    kv0, kv1, n_gate = 1024, 2560, NSA_KV_HEADS * NSA_GROUP * 3
    wq_t = w_in[:, :kv0].T.astype(BF16)
    wkv_t = w_in[:, kv0:kv0 + 1024].T.astype(BF16)
    wv_t = w_in[:, kv0 + 1280:kv1].T.astype(BF16)
    wg_t = jnp.pad(w_in[:, kv1:kv1 + n_gate].T, ((0, LANES - n_gate), (0, 0))).astype(BF16)
    w_rm = jnp.concatenate([w_in[:, kv0:kv0 + 768], w_in[:, kv0 + 1024:kv1]], axis=1).astype(BF16)
    tks = 256
    qt, rows_t, cmp_rows, kk, vt, win, gt = nsa_in_t(hp, g, w_rm, wq_t, wkv_t, wv_t, wg_t,
                                                     cos_p, sin_p, cos_pt, sin_pt, tm=tks, n_b=n_b)
    npg_p = t // PAGE_SIZE
    cend_p = jnp.arange(npg_p * 8, dtype=jnp.int32) * CMP_STRIDE + (CMP_BLOCK - 1)
    kc, _, vct = nsa_compress(cmp_rows.reshape(n_b * npg_p, PAGE_SIZE, 512), jnp.arange(n_b * npg_p, dtype=jnp.int32),
                              n_b, npg_p, wk, wv, *_rope_tables(cend_p), feature_major=False)
    nblk = -(-(t // SEL_BLOCK) // 16) * 16
    tile = jnp.arange(t // tks, dtype=jnp.int32)[:, None, None]
    key = jnp.arange(tks, dtype=jnp.int32)[None, :, None]
    blk = jnp.arange(LANES, dtype=jnp.int32)[None, None, :]
    emt = (blk == (tile * tks + key) // SEL_BLOCK).astype(BF16)
    ot = nsa_attn(qt, gt, kc, vct, kk, vt, _overlap(npg_p * 8, nblk).T, emt, n_b=n_b, t=t, tq=128)
    hp = mm_res_t(ot, w_out.T.astype(BF16), hp, tm=512)
    kv_p = jnp.transpose(rows_t.reshape(n_b, 4, NSA_KV_HEADS, HEAD_DIM, t), (0, 4, 1, 2, 3))
    keep = min(WINDOW, t)
    win_p = win.reshape(n_b, t, 2, NSA_KV_HEADS, HEAD_DIM)[:, t - keep:]

    cos_s, sin_s = _rope_tables(jnp.full((n_s,), past_len, jnp.int32))
    q_s, rows_s, kva_s, win_s, gates_s = nsa_in(hs, g, w_in_p, cos_s, sin_s, tm=n_s)
    cend_s = jnp.arange(n_pages * 8, dtype=jnp.int32) * CMP_STRIDE + (CMP_BLOCK - 1)
    cache_fm = jnp.transpose(cache_kv, (0, 2, 3, 4, 1)).reshape(cache_kv.shape[0], 1024, PAGE_SIZE)
    kc_s, vc_s, _ = nsa_compress(cache_fm, pt_flat, n_s, n_pages, wk, wv, *_rope_tables(cend_s), feature_major=True)
    n_sel = -(-(past_len + 1) // SEL_BLOCK)
    n_pick = min(SEL_TOPN, n_sel) - 1
    q4 = q_s.reshape(n_s, NSA_KV_HEADS, NSA_GROUP, HEAD_DIM)
    eye = jnp.eye(NSA_KV_HEADS, dtype=bool)[None, :, None, :, None]
    qbd = jnp.where(eye, q4[:, :, :, None, :], jnp.zeros((), BF16)).reshape(n_s, 16, 256)
    gm = (jnp.arange(8)[:, None] == jnp.arange(16)[None, :] // NSA_GROUP).astype(BF16)
    n_blk_pad = -(-n_sel // LANES) * LANES
    oc16, idx = nsa_s_sel(qbd, kc_s, vc_s, _overlap(n_pages * 8, n_blk_pad), gm, pos=past_len, n_pick=n_pick)
    oc5 = oc16.reshape(n_s, NSA_KV_HEADS, NSA_GROUP, NSA_KV_HEADS, HEAD_DIM)
    o_c = jnp.sum(jnp.where(eye, oc5, 0.0), axis=3).reshape(n_s, d)
    ix_flat = idx[:, :NSA_KV_HEADS, :n_pick].reshape(-1)
    q8 = jnp.pad(q4, ((0, 0), (0, 0), (0, 8 - NSA_GROUP), (0, 0)))
    win_fm = jnp.transpose(cache_win, (0, 2, 3, 4, 1)).reshape(n_s, 512, cache_win.shape[1])
    os_p, ow_p = nsa_s_attn(pt_flat, ix_flat, q8, cache_fm, kva_s.reshape(n_s, 16, 1, HEAD_DIM), win_fm,
                            n_pages=n_pages, n_pick=n_pick)
    o_s = os_p[:, :, :NSA_GROUP].reshape(n_s, d)
    o_w = ow_p[:, :, :NSA_GROUP].reshape(n_s, d)
    g3 = jnp.repeat(gates_s[:, :48].reshape(n_s, 16, 3), HEAD_DIM, axis=1)
    hs = nsa_out_s(o_c, o_s, o_w, g3[:, :, 0], g3[:, :, 1], g3[:, :, 2], w_out_b, hs)
    kv_s = rows_s.reshape(n_s, 1, 4, NSA_KV_HEADS, HEAD_DIM)
    win_new = win_s.reshape(n_s, 1, 2, NSA_KV_HEADS, HEAD_DIM)
    win_all = jnp.concatenate([cache_win, win_new], axis=1)
    win_s_out = win_all[:, win_all.shape[1] - cache_win.shape[1]:]
    return hp, hs, kv_p, kv_s, win_p, win_s_out


def _sconv_layer(hp, hs, g, state, w, *, n_b, t):
    w_in, w_conv, w_out = w
    w_in_b = w_in.astype(BF16)
    w_out_b = w_out.astype(BF16)
    hp, st = sconv_prompt(hp, g, w_in_b, w_conv, w_out_b, n_b=n_b, t=t, tm=256)
    st_p = st[:, 8 - (w_conv.shape[0] - 1):]
    hs, pre = sconv_sample(hs, g, w_in_b, w_conv, w_out_b, state[:, 0], state[:, 1])
    st_s = jnp.concatenate([state[:, 1:], pre[:, None, :]], axis=1)
    return hp, hs, st_p, st_s


def _moba_layer(hp, hs, g, cache_kv, pt_flat, n_pages, w, *, n_b, t, n_s):
    w_qkv, w_out = w
    d = hp.shape[1]
    n_heads = d // HEAD_DIM
    past_len = n_pages * PAGE_SIZE
    w_qkv_b = w_qkv.astype(BF16)
    w_out_b = w_out.astype(BF16)

    cos_p, sin_p = _rope_tables(jnp.arange(t, dtype=jnp.int32))
    cos_pt, sin_pt = cos_p[:, :HEAD_DIM].T, sin_p[:, :HEAD_DIM].T
    qt, rows_t, kb, vt, km = moba_in_t(hp, g, w_qkv_b[:, d:2 * d], w_qkv_b[:, :2 * d].T, w_qkv_b[:, 2 * d:].T,
                                       cos_p, sin_p, cos_pt, sin_pt, n_b=n_b)
    nblk = t // MOBA_BLOCK
    kmean = jnp.pad(km.reshape(n_b, nblk, d), ((0, 0), (0, -(-nblk // 16) * 16 - nblk), (0, 0)))
    ot = moba_attn(qt, kb, vt, kmean, n_b=n_b, t=t, tq=512)
    hp = mm_res_t(ot, w_out_b.T, hp, tm=512)
    kv_p = jnp.transpose(rows_t.reshape(n_b, 2, n_heads, HEAD_DIM, t), (0, 4, 1, 2, 3))

    cos_s, sin_s = _rope_tables(jnp.full((n_s,), past_len, jnp.int32))
    q_s, rows_s, kb_s, vb_s, _ = moba_in(hs, g, w_qkv_b, cos_s, sin_s, tm=n_s)
    cache_fm = jnp.transpose(cache_kv, (0, 2, 3, 4, 1)).reshape(cache_kv.shape[0], 2 * d, PAGE_SIZE)
    n_top = min(MOBA_TOPK, -(-(past_len + 1) // MOBA_BLOCK))
    qh = q_s.reshape(n_s, n_heads, 1, HEAD_DIM)
    eye = jnp.eye(n_heads, dtype=bool)[None, :, :, None]
    qbd = jnp.where(eye, qh, jnp.zeros((), BF16)).reshape(n_s, n_heads, d)
    qbd = jnp.pad(qbd, ((0, 0), (0, LANES - n_heads), (0, 0)))
    idx = moba_s_select(cache_fm, pt_flat, qbd, n_pages=n_pages, n_top=n_top)
    ix_flat = idx[:, :n_heads, :n_top].reshape(-1)
    q8 = jnp.pad(qh, ((0, 0), (0, 0), (0, 7), (0, 0)))
    o_p = moba_s_attn(pt_flat, ix_flat, q8, cache_fm, kb_s.reshape(n_s, n_heads, 1, HEAD_DIM),
                      vb_s.reshape(n_s, n_heads, 1, HEAD_DIM), n_pages=n_pages, n_top=n_top, n_heads=n_heads)
    o_s = o_p[:, :, 0].reshape(n_s, d).astype(BF16)
    hs = mm_res(o_s, w_out_b, hs, tm=n_s)
    kv_s = rows_s.reshape(n_s, 1, 2, n_heads, HEAD_DIM)
    return hp, hs, kv_p, kv_s


def _conf_layer(hp, hs, g, state, w, *, n_b, t):
    w_pw1, w_dw, b_dw, ln_g, ln_b, w_pw2 = w
    d = hp.shape[1]
    r = lambda a: a.reshape(1, d)
    args = (w_pw1.astype(BF16), w_dw, r(b_dw), r(ln_g), r(ln_b), w_pw2.astype(BF16))
    hp, st = conf_prompt(hp, g, *args, n_b=n_b, t=t, tm=256)
    st_p = st[:, st.shape[1] - (w_dw.shape[0] - 1):]
    hs, u = conf_sample(hs, g, *args, jnp.transpose(state, (1, 0, 2)))
    st_s = jnp.concatenate([state[:, 1:], u[:, None, :]], axis=1)
    return hp, hs, st_p, st_s


def kernel(x_prompt, x_sample, cache_nsa_kv, cache_nsa_win, state_sconv, cache_moba_kv, state_conformer,
           page_table, norm_mix, norm_ffn, norm_final, ffn_w_up, ffn_w_down,
           nsa_w_in, nsa_pe_k, nsa_w1_k, nsa_w2_k, nsa_pe_v, nsa_w1_v, nsa_w2_v, nsa_w_out,
           sconv_w_in, sconv_w_conv, sconv_w_out, moba_w_qkv, moba_w_out,
           conf_w_pw1, conf_w_dw, conf_b_dw, conf_ln_g, conf_ln_b, conf_w_pw2):
    n_b, t, d = x_prompt.shape
    n_s = x_sample.shape[0]
    depth = norm_mix.shape[0]
    n_pages = page_table.shape[1]
    pt_flat = page_table.reshape(-1).astype(jnp.int32)
    hp = x_prompt.reshape(n_b * t, d)
    hs = x_sample.reshape(n_s, d)
    outs = {k: [] for k in ("nsa_kv_p", "nsa_kv_s", "nsa_win_p", "nsa_win_s", "sconv_p", "sconv_s",
                            "moba_p", "moba_s", "conf_p", "conf_s")}
    for i in range(depth):
        kind, j = i % 4, i // 4
        g = norm_mix[i].reshape(1, d)
        if kind == 0:
            w = (nsa_w_in[j], nsa_pe_k[j], nsa_w1_k[j], nsa_w2_k[j], nsa_pe_v[j], nsa_w1_v[j], nsa_w2_v[j],
                 nsa_w_out[j])
            hp, hs, kv_p, kv_s, win_p, win_s = _nsa_layer(hp, hs, g, cache_nsa_kv[j], cache_nsa_win[j], pt_flat,
                                                          n_pages, w, n_b=n_b, t=t, n_s=n_s)
            outs["nsa_kv_p"].append(kv_p)
            outs["nsa_kv_s"].append(kv_s)
            outs["nsa_win_p"].append(win_p)
            outs["nsa_win_s"].append(win_s)
        elif kind == 1:
            hp, hs, st_p, st_s = _sconv_layer(hp, hs, g, state_sconv[j],
                                              (sconv_w_in[j], sconv_w_conv[j], sconv_w_out[j]), n_b=n_b, t=t)
            outs["sconv_p"].append(st_p)
            outs["sconv_s"].append(st_s)
        elif kind == 2:
            hp, hs, kv_p, kv_s = _moba_layer(hp, hs, g, cache_moba_kv[j], pt_flat, n_pages,
                                             (moba_w_qkv[j], moba_w_out[j]), n_b=n_b, t=t, n_s=n_s)
            outs["moba_p"].append(kv_p)
            outs["moba_s"].append(kv_s)
        else:
            w = (conf_w_pw1[j], conf_w_dw[j], conf_b_dw[j], conf_ln_g[j], conf_ln_b[j], conf_w_pw2[j])
            hp, hs, st_p, st_s = _conf_layer(hp, hs, g, state_conformer[j], w, n_b=n_b, t=t)
            outs["conf_p"].append(st_p)
            outs["conf_s"].append(st_s)
        gf = norm_ffn[i].reshape(1, d)
        wu = ffn_w_up[i].astype(BF16)
        wd = ffn_w_down[i].astype(BF16)
        final = i == depth - 1
        gfin = norm_final.reshape(1, d)
        hp = ffn(hp, gf, wu, wd, gfin, tm=512, final=final)
        hs = ffn(hs, gf, wu, wd, gfin, tm=n_s, final=final)
    return (hp.reshape(n_b, t, d), hs.reshape(n_s, 1, d),
            jnp.stack(outs["nsa_kv_p"]), jnp.stack(outs["nsa_kv_s"]),
            jnp.stack(outs["nsa_win_p"]), jnp.stack(outs["nsa_win_s"]),
            jnp.stack(outs["sconv_p"]), jnp.stack(outs["sconv_s"]),
            jnp.stack(outs["moba_p"]), jnp.stack(outs["moba_s"]),
            jnp.stack(outs["conf_p"]), jnp.stack(outs["conf_s"]))
```

```python
import functools

import jax
import jax.numpy as jnp
from jax import lax
from jax.experimental import pallas as pl
from jax.experimental.pallas import tpu as pltpu

F32 = jnp.float32
BF16 = jnp.bfloat16

HEAD_DIM = 64
ROPE_THETA = 10000.0
RMS_EPS = 1e-6
LN_EPS = 1e-5
NSA_KV_HEADS = 4
NSA_GROUP = 4
CMP_STRIDE = 16
CMP_BLOCK = 32
SEL_BLOCK = 64
SEL_TOPN = 16
WINDOW = 512
MOBA_BLOCK = 256
MOBA_TOPK = 3
PAGE_SIZE = 128
SCALE = HEAD_DIM ** -0.5
SCALE_LOG2 = SCALE * 1.4426950408889634

LANES = 128
NEG = -1e30
VMEM_LIMIT = 56 * 1024 * 1024


def _params(*sem):
    return pltpu.CompilerParams(dimension_semantics=sem, vmem_limit_bytes=VMEM_LIMIT)


def _dot(a, b):
    return jnp.dot(a, b, preferred_element_type=F32)


def _dot_nt(a, b):
    return lax.dot_general(a, b, (((1,), (1,)), ((), ())), preferred_element_type=F32)


def _split3(x):
    hi = x.astype(BF16)
    r = x - hi.astype(F32)
    mid = r.astype(BF16)
    lo = (r - mid.astype(F32)).astype(BF16)
    return hi, mid, lo


def _dot3(x, m):
    hi, mid, lo = _split3(x)
    return _dot(hi, m) + _dot(mid, m) + _dot(lo, m)


def _dot3_rhs(m, x):
    hi, mid, lo = _split3(x)
    return _dot(m, hi) + _dot(m, mid) + _dot(m, lo)


def _rms(x, g):
    return x * lax.rsqrt(jnp.mean(x * x, axis=-1, keepdims=True) + RMS_EPS) * g


def _sigmoid(x):
    return 1.0 / (1.0 + jnp.exp(-x))


def _rope(x, cos, sin):
    w = x.shape[-1]
    lane = lax.broadcasted_iota(jnp.int32, x.shape, 1)
    first = (lane % HEAD_DIM) < (HEAD_DIM // 2)
    rot = jnp.where(first, pltpu.roll(x, w - HEAD_DIM // 2, 1), pltpu.roll(x, HEAD_DIM // 2, 1))
    reps = w // LANES
    if reps > 1:
        cos = jnp.concatenate([cos] * reps, axis=1)
        sin = jnp.concatenate([sin] * reps, axis=1)
    return x * cos + rot * sin


def _masked_softmax(s, mask):
    s = jnp.where(mask, s, -jnp.inf)
    m = jnp.max(s, axis=-1, keepdims=True)
    m = jnp.where(m > -jnp.inf, m, 0.0)
    p = jnp.exp(s - m)
    return p / jnp.maximum(jnp.sum(p, axis=-1, keepdims=True), 1e-30)


def _rope_t(x, cos, sin):
    r = x.shape[0]
    row = lax.broadcasted_iota(jnp.int32, x.shape, 0)
    first = (row % HEAD_DIM) < (HEAD_DIM // 2)
    rot = jnp.where(first, pltpu.roll(x, r - HEAD_DIM // 2, 0), pltpu.roll(x, HEAD_DIM // 2, 0))
    reps = r // HEAD_DIM
    if reps > 1:
        cos = jnp.concatenate([cos] * reps, axis=0)
        sin = jnp.concatenate([sin] * reps, axis=0)
    return x * cos + rot * sin


def _topk_mask_t(score, k):
    row = lax.broadcasted_iota(jnp.int32, score.shape, 0).astype(F32)
    sel = jnp.zeros(score.shape, F32)
    for _ in range(k):
        m = jnp.max(score, axis=0, keepdims=True)
        idx = jnp.min(jnp.where(score == m, row, 1e9), axis=0, keepdims=True)
        hit = row == idx
        sel = jnp.where(hit & (m > -jnp.inf), 1.0, sel)
        score = jnp.where(hit, -jnp.inf, score)
    return sel


ONES_ROWS = 16


def _online_init_t(cols):
    return (jnp.full((1, cols), NEG, F32), jnp.zeros((HEAD_DIM + ONES_ROWS, cols), F32))


def _online_update_t(carry, s, bias, v_t):
    m, acc = carry
    sm = s if bias is None else s + bias
    m_new = jnp.maximum(m, jnp.max(sm, axis=0, keepdims=True))
    p = jnp.exp2(sm - m_new).astype(BF16)
    v_aug = jnp.concatenate([v_t, jnp.ones((ONES_ROWS, v_t.shape[1]), BF16)], axis=0)
    return m_new, jnp.exp2(m - m_new) * acc + _dot(v_aug, p)


def _online_finish_t(carry):
    _, acc = carry
    return acc[0:HEAD_DIM] * (1.0 / jnp.maximum(acc[HEAD_DIM:HEAD_DIM + 1], 1e-30))


def _pad_pair(q, odd):
    z = jnp.zeros_like(q)
    return jnp.concatenate([z, q] if odd else [q, z], axis=0)


def _ffn_kernel(x_ref, g_ref, wu_ref, wd_ref, gf_ref, o_ref, *, chunk, final):
    x = x_ref[...]
    xn = _rms(x, g_ref[...]).astype(BF16)
    acc = x
    for c in range(0, wu_ref.shape[1], chunk):
        u = _dot(xn, wu_ref[:, c:c + chunk])
        a = jnp.square(jnp.maximum(u, 0.0)).astype(BF16)
        acc = acc + _dot(a, wd_ref[c:c + chunk, :])
    if final:
        acc = _rms(acc, gf_ref[...])
    o_ref[...] = acc


def ffn(x, g, wu, wd, gf, *, tm, final):
    m, d = x.shape
    dff = wu.shape[1]
    row = lambda i: (i, 0)
    const = lambda i: (0, 0)
    return pl.pallas_call(
        functools.partial(_ffn_kernel, chunk=512, final=final),
        grid=(m // tm,),
        in_specs=[pl.BlockSpec((tm, d), row), pl.BlockSpec((1, d), const),
                  pl.BlockSpec((d, dff), const, pipeline_mode=pl.Buffered(1)),
                  pl.BlockSpec((dff, d), const, pipeline_mode=pl.Buffered(1)),
                  pl.BlockSpec((1, d), const)],
        out_specs=pl.BlockSpec((tm, d), row),
        out_shape=jax.ShapeDtypeStruct((m, d), F32),
        compiler_params=_params("parallel"),
        name="ffn",
    )(x, g, wu, wd, gf)


def _mm_res_kernel(x_ref, w_ref, r_ref, o_ref):
    o_ref[...] = r_ref[...] + _dot(x_ref[...], w_ref[...])


def mm_res(x, w, res, *, tm):
    m, k = x.shape
    n = w.shape[1]
    row = lambda i: (i, 0)
    return pl.pallas_call(
        _mm_res_kernel,
        grid=(m // tm,),
        in_specs=[pl.BlockSpec((tm, k), row), pl.BlockSpec((k, n), lambda i: (0, 0)),
                  pl.BlockSpec((tm, n), row)],
        out_specs=pl.BlockSpec((tm, n), row),
        out_shape=jax.ShapeDtypeStruct((m, n), F32),
        compiler_params=_params("parallel"),
        name="mm_res",
    )(x, w, res)


def _mm_res_t_kernel(xt_ref, wt_ref, r_ref, o_ref):
    o_ref[...] = r_ref[...] + _dot(wt_ref[...], xt_ref[...]).T


def mm_res_t(xt, wt, res, *, tm):
    k, m = xt.shape
    n = wt.shape[0]
    row = lambda i: (i, 0)
    return pl.pallas_call(
        _mm_res_t_kernel,
        grid=(m // tm,),
        in_specs=[pl.BlockSpec((k, tm), lambda i: (0, i)), pl.BlockSpec((n, k), lambda i: (0, 0)),
                  pl.BlockSpec((tm, n), row)],
        out_specs=pl.BlockSpec((tm, n), row),
        out_shape=jax.ShapeDtypeStruct((m, n), F32),
        compiler_params=_params("parallel"),
        name="mm_res_t",
    )(xt, wt, res)


def _nsa_in_kernel(x_ref, g_ref, w_ref, cos_ref, sin_ref, q_ref, rows_ref, kva_ref, win_ref, gate_ref):
    xn = _rms(x_ref[...], g_ref[...]).astype(BF16)
    cos = cos_ref[...]
    sin = sin_ref[...]
    q = _rope(_dot(xn, w_ref[:, 0:1024]), cos, sin) * SCALE
    q_ref[...] = q.astype(BF16)
    kv = _dot(xn, w_ref[:, 1024:2048])
    ks = _rope(kv[:, 512:768], cos, sin)
    rows_ref[:, 0:512] = kv[:, 0:512]
    rows_ref[:, 512:768] = ks
    rows_ref[:, 768:1024] = kv[:, 768:1024]
    wkv = _dot(xn, w_ref[:, 2048:2560])
    kw = _rope(wkv[:, 0:256], cos, sin)
    win_ref[:, 0:256] = kw
    win_ref[:, 256:512] = wkv[:, 256:512]
    kva_ref[:, 0:256] = ks.astype(BF16)
    kva_ref[:, 256:512] = kv[:, 768:1024].astype(BF16)
    kva_ref[:, 512:768] = kw.astype(BF16)
    kva_ref[:, 768:1024] = wkv[:, 256:512].astype(BF16)
    gate_ref[...] = _sigmoid(_dot(xn, w_ref[:, 2560:2688]))


def nsa_in(x, g, w, cos, sin, *, tm):
    m, d = x.shape
    nw = w.shape[1]
    nt = cos.shape[0] // tm
    row = lambda i: (i, 0)
    tab = lambda i: (i % nt, 0)
    const = lambda i: (0, 0)
    outs = [(1024, BF16), (1024, F32), (1024, BF16), (512, F32), (LANES, F32)]
    return pl.pallas_call(
        _nsa_in_kernel,
        grid=(m // tm,),
        in_specs=[pl.BlockSpec((tm, d), row), pl.BlockSpec((1, d), const),
                  pl.BlockSpec((d, nw), const, pipeline_mode=pl.Buffered(1)),
                  pl.BlockSpec((tm, LANES), tab), pl.BlockSpec((tm, LANES), tab)],
        out_specs=[pl.BlockSpec((tm, n), row) for n, _ in outs],
        out_shape=[jax.ShapeDtypeStruct((m, n), dt) for n, dt in outs],
        compiler_params=_params("parallel"),
        name="nsa_in",
    )(x, g, w, cos, sin)


def _nsa_in_t_kernel(x_ref, g_ref, w_ref, wqt_ref, wkvt_ref, wvt_ref, wgt_ref, cos_ref, sin_ref, cost_ref, sint_ref,
                     qt_ref, rowst_ref, cmp_ref, kk_ref, vt_ref, win_ref, gt_ref):
    xn = _rms(x_ref[...], g_ref[...]).astype(BF16)
    cos = cos_ref[...]
    sin = sin_ref[...]
    cos_t = cost_ref[...]
    sin_t = sint_ref[...]
    qt_ref[...] = (_rope_t(_dot_nt(wqt_ref[...], xn), cos_t, sin_t) * SCALE_LOG2).astype(BF16)
    rt = _dot_nt(wkvt_ref[...], xn)
    rowst_ref[0:512, :] = rt[0:512]
    rowst_ref[512:768, :] = _rope_t(rt[512:768], cos_t, sin_t)
    rowst_ref[768:1024, :] = rt[768:1024]
    vt_ref[0:256, :] = rt[768:1024].astype(BF16)
    vt_ref[256:512, :] = _dot_nt(wvt_ref[...], xn).astype(BF16)
    kv = _dot(xn, w_ref[:, 0:768])
    cmp_ref[...] = kv[:, 0:512]
    wkv = _dot(xn, w_ref[:, 768:1280])
    kw = _rope(wkv[:, 0:256], cos, sin)
    win_ref[:, 0:256] = kw
    win_ref[:, 256:512] = wkv[:, 256:512]
    kk_ref[:, 0:256] = _rope(kv[:, 512:768], cos, sin).astype(BF16)
    kk_ref[:, 256:512] = kw.astype(BF16)
    gt_ref[...] = _sigmoid(_dot_nt(wgt_ref[...], xn))


def nsa_in_t(x, g, w_rm, wq_t, wkv_t, wv_t, wg_t, cos, sin, cos_t, sin_t, *, tm, n_b):
    m, d = x.shape
    nt = cos.shape[0] // tm
    row = lambda i: (i, 0)
    col = lambda i: (0, i)
    const = lambda i: (0, 0)
    one = pl.Buffered(1)
    weights = (w_rm, wq_t, wkv_t, wv_t, wg_t)
    return pl.pallas_call(
        _nsa_in_t_kernel,
        grid=(m // tm,),
        in_specs=[pl.BlockSpec((tm, d), row), pl.BlockSpec((1, d), const)]
        + [pl.BlockSpec(w.shape, const, pipeline_mode=one) for w in weights]
        + [pl.BlockSpec((tm, LANES), lambda i: (i % nt, 0)), pl.BlockSpec((tm, LANES), lambda i: (i % nt, 0)),
           pl.BlockSpec((HEAD_DIM, tm), lambda i: (0, i % nt)), pl.BlockSpec((HEAD_DIM, tm), lambda i: (0, i % nt))],
        out_specs=[pl.BlockSpec((1024, tm), col), pl.BlockSpec((None, 1024, tm), lambda i: (i // nt, 0, i % nt)),
                   pl.BlockSpec((tm, 512), row), pl.BlockSpec((tm, 512), row),
                   pl.BlockSpec((None, 512, tm), lambda i: (i, 0, 0)), pl.BlockSpec((tm, 512), row),
                   pl.BlockSpec((LANES, tm), col)],
        out_shape=[jax.ShapeDtypeStruct((1024, m), BF16), jax.ShapeDtypeStruct((n_b, 1024, m // n_b), F32),
                   jax.ShapeDtypeStruct((m, 512), F32), jax.ShapeDtypeStruct((m, 512), BF16),
                   jax.ShapeDtypeStruct((m // tm, 512, tm), BF16), jax.ShapeDtypeStruct((m, 512), F32),
                   jax.ShapeDtypeStruct((LANES, m), F32)],
        compiler_params=_params("parallel"),
        name="nsa_in_t",
    )(x, g, *weights, cos, sin, cos_t, sin_t)


def _compress_kernel(pt_ref, *refs, pps, feature_major):
    del pt_ref
    pages = refs[:pps + 1]
    (pelo_k, pehi_k, w1lo_k, w1hi_k, w2_k, pelo_v, pehi_v, w1lo_v, w1hi_v, w2_v, w2t_v,
     cos_ref, sin_ref, kc_ref, vc_ref, vct_ref, xs_ref) = refs[pps + 1:]
    npc = pps * 8
    rows = (pps + 1) * 8
    low = lax.broadcasted_iota(jnp.int32, (rows, LANES), 1) < HEAD_DIM
    for k, pg in enumerate(pages):
        x = pg[...]
        if feature_major:
            x = x.T
        for lb in range(4):
            xs_ref[lb, k * PAGE_SIZE:(k + 1) * PAGE_SIZE, :] = x[:, lb * LANES:(lb + 1) * LANES]
    streams = ((pelo_k, pehi_k, w1lo_k, w1hi_k, w2_k, kc_ref), (pelo_v, pehi_v, w1lo_v, w1hi_v, w2_v, vc_ref))
    for s, (pelo, pehi, w1lo, w1hi, w2, out_ref) in enumerate(streams):
        heads = [[] for _ in range(NSA_KV_HEADS)]
        for lb in range(2):
            for q in range(CMP_STRIDE // 2):
                a = xs_ref[2 * s + lb, pl.ds(2 * q, rows, stride=CMP_STRIDE), :]
                b = xs_ref[2 * s + lb, pl.ds(2 * q + 1, rows, stride=CMP_STRIDE), :]
                heads[2 * lb].append(jnp.where(low, a, pltpu.roll(b, HEAD_DIM, 1)))
                heads[2 * lb + 1].append(jnp.where(low, pltpu.roll(a, HEAD_DIM, 1), b))
        x = jnp.concatenate([jnp.concatenate(hh, axis=1) for hh in heads], axis=0)
        first = _dot((x + pelo[...]).astype(BF16), w1lo[...])
        second = _dot((x + pehi[...]).astype(BF16), w1hi[...])
        outs = []
        outs_t = []
        for h in range(NSA_KV_HEADS):
            pre = first[h * rows:h * rows + npc] + second[h * rows + 1:h * rows + 1 + npc]
            hid = (pre * _sigmoid(pre)).astype(BF16)
            outs.append(_dot(hid, w2[...]))
            if s == 1:
                outs_t.append(_dot_nt(w2t_v[...], hid))
        res = jnp.concatenate(outs, axis=1)
        if s == 0:
            res = _rope(res, cos_ref[...], sin_ref[...])
        else:
            vct_ref[...] = jnp.concatenate(outs_t, axis=0).astype(BF16)
        out_ref[...] = res.astype(BF16)


def nsa_compress(rows3d, pt_flat, n_seq, n_pages, wk, wv, cos_c, sin_c, *, feature_major):
    pps = min(16, n_pages)
    steps = n_pages // pps
    npc = pps * 8
    page_block = (None, 512, PAGE_SIZE) if feature_major else (None, PAGE_SIZE, 512)

    def page_map(k):
        return lambda b, s, pt: (pt[b * n_pages + jnp.minimum(s * pps + k, n_pages - 1)], 0, 0)

    const2 = lambda b, s, pt: (0, 0)
    wspecs = []
    for _ in range(2):
        wspecs += [pl.BlockSpec((1, 1024), const2), pl.BlockSpec((1, 1024), const2),
                   pl.BlockSpec((1024, 256), const2), pl.BlockSpec((1024, 256), const2),
                   pl.BlockSpec((256, HEAD_DIM), const2)]
    wspecs.append(pl.BlockSpec((HEAD_DIM, 256), const2))
    grid_spec = pltpu.PrefetchScalarGridSpec(
        num_scalar_prefetch=1,
        grid=(n_seq, steps),
        in_specs=[pl.BlockSpec(page_block, page_map(k)) for k in range(pps + 1)] + wspecs
        + [pl.BlockSpec((npc, LANES), lambda b, s, pt: (s, 0))] * 2,
        out_specs=[pl.BlockSpec((None, npc, 256), lambda b, s, pt: (b, s, 0))] * 2
        + [pl.BlockSpec((None, 256, npc), lambda b, s, pt: (b, 0, s))],
        scratch_shapes=[pltpu.VMEM((4, (pps + 1) * PAGE_SIZE, LANES), F32)],
    )
    return pl.pallas_call(
        functools.partial(_compress_kernel, pps=pps, feature_major=feature_major),
        grid_spec=grid_spec,
        out_shape=[jax.ShapeDtypeStruct((n_seq, n_pages * 8, 256), BF16)] * 2
        + [jax.ShapeDtypeStruct((n_seq, 256, n_pages * 8), BF16)],
        compiler_params=_params("parallel", "arbitrary"),
        name="nsa_compress",
    )(pt_flat, *([rows3d] * (pps + 1)), *wk, *wv, cos_c, sin_c)


def _nsa_attn_kernel(qt_ref, gt_ref, kc_ref, vct_ref, kk_ref, vt_ref, ovlt_ref, emt_ref, ot_ref, s0_ref, s1_ref, *,
                     tq, tks, n_top, nblk):
    i = pl.program_id(1)
    c0 = i * tq
    n4 = NSA_GROUP * tq
    pos1 = c0 + lax.broadcasted_iota(jnp.int32, (1, tq), 1)
    pos4 = c0 + lax.broadcasted_iota(jnp.int32, (1, n4), 1) % tq
    blk = lax.broadcasted_iota(jnp.int32, (nblk, tq), 0)
    cur = pos1 // SEL_BLOCK
    heads = range(NSA_KV_HEADS)
    pair = [slice((h // 2) * LANES, (h // 2 + 1) * LANES) for h in heads]
    vrow = [slice(h * HEAD_DIM, (h + 1) * HEAD_DIM) for h in heads]
    qpad = []
    for h in heads:
        q4 = jnp.concatenate(
            [qt_ref[(NSA_GROUP * h + g) * HEAD_DIM:(NSA_GROUP * h + g + 1) * HEAD_DIM, :] for g in range(NSA_GROUP)],
            axis=1)
        qpad.append(_pad_pair(q4, h % 2 == 1))

    o_c, q_aug = [], []
    for h in heads:
        s = _dot(kc_ref[:, pair[h]], qpad[h])
        cend = lax.broadcasted_iota(jnp.int32, s.shape, 0) * CMP_STRIDE + (CMP_BLOCK - 1)
        s = jnp.where(cend <= pos4, s, -jnp.inf)
        m = jnp.max(s, axis=0, keepdims=True)
        p = jnp.exp2(s - jnp.where(m > -jnp.inf, m, 0.0))
        p = p * (1.0 / jnp.maximum(jnp.sum(p, axis=0, keepdims=True), 1e-30))
        o_c.append(_dot(vct_ref[vrow[h], :], p.astype(BF16)))
        psum = p[:, 0:tq] + p[:, tq:2 * tq] + p[:, 2 * tq:3 * tq] + p[:, 3 * tq:4 * tq]
        imp = _dot3_rhs(ovlt_ref[...], psum)
        forced = (blk == 0) | (blk == cur) | (blk == cur - 1)
        imp = jnp.where(forced | (blk > cur), -jnp.inf, imp)
        sel = jnp.where(forced, 1.0, _topk_mask_t(imp, n_top - 3))
        sel_bias = jnp.where(sel > 0.5, 0.0, NEG).astype(BF16)
        sel_bias = jnp.concatenate([sel_bias] * NSA_GROUP, axis=1)
        fill = jnp.zeros((LANES - nblk, n4), BF16)
        q_aug.append(jnp.concatenate([qpad[h], sel_bias, fill], axis=0))

    def finish(carry):
        return [_online_finish_t(c) for c in carry]

    init = tuple(_online_init_t(n4) for _ in heads)
    t_last = (c0 + tq - 1) // tks

    n_tiles = vt_ref.shape[0]

    def run(t_first, scores, consume):
        def put(t, s_ref):
            for h, s in enumerate(scores(jnp.minimum(t, n_tiles - 1))):
                s_ref[h] = s

        def body(u, carry):
            t = t_first + 2 * u
            put(t + 1, s1_ref)
            carry = consume(t, s0_ref, carry, False)
            put(t + 2, s0_ref)
            return consume(t + 1, s1_ref, carry, False)

        put(t_first, s0_ref)
        pairs = (t_last - t_first) // 2
        carry = lax.fori_loop(0, pairs, body, init)
        t = t_first + 2 * pairs
        put(t + 1, s1_ref)
        carry = consume(t, s0_ref, carry, True)
        return finish(consume(t + 1, s1_ref, carry, True))

    def sel_scores(t):
        k0 = pl.multiple_of(t * tks, tks)
        e = emt_ref[t]
        return [_dot(jnp.concatenate([kk_ref[pl.ds(k0, tks), pair[h]], e], axis=1), q_aug[h]) for h in heads]

    def sel_consume(t, s_ref, carry, masked):
        bias = None
        if masked:
            kpos = t * tks + lax.broadcasted_iota(jnp.int32, (tks, n4), 0)
            bias = jnp.where(kpos <= pos4, 0.0, NEG)
        tv = jnp.minimum(t, n_tiles - 1)
        return tuple(_online_update_t(carry[h], s_ref[h], bias, vt_ref[tv, vrow[h], :]) for h in heads)

    o_s = run(0, sel_scores, sel_consume)

    def win_put(t, s_ref):
        k0 = pl.multiple_of(jnp.maximum(t, 0) * tks, tks)
        for h in heads:
            s_ref[h] = _dot(kk_ref[pl.ds(k0, tks), 256 + pair[h].start:256 + pair[h].stop], qpad[h])

    def win_consume(t, s_ref, carry):
        tv = jnp.maximum(t, 0)
        diff = pos4 - (tv * tks + lax.broadcasted_iota(jnp.int32, (tks, n4), 0))
        bias = jnp.where((diff >= 0) & (diff <= WINDOW) & (t >= 0), 0.0, NEG)
        return tuple(_online_update_t(carry[h], s_ref[h], bias,
                                      vt_ref[tv, 256 + vrow[h].start:256 + vrow[h].stop, :]) for h in heads)

    n_win = -(-WINDOW // tks) + 1
    bufs = (s0_ref, s1_ref)
    carry = init
    win_put(t_last - (n_win - 1), bufs[0])
    for e in range(n_win):
        if e + 1 < n_win:
            win_put(t_last - (n_win - 2 - e), bufs[(e + 1) % 2])
        carry = win_consume(t_last - (n_win - 1 - e), bufs[e % 2], carry)
    o_w = finish(carry)

    for h in heads:
        for g in range(NSA_GROUP):
            j = (h * NSA_GROUP + g) * 3
            cs = slice(g * tq, (g + 1) * tq)
            o = (gt_ref[j:j + 1, :] * o_c[h][:, cs] + gt_ref[j + 1:j + 2, :] * o_s[h][:, cs]
                 + gt_ref[j + 2:j + 3, :] * o_w[h][:, cs])
            ot_ref[(NSA_GROUP * h + g) * HEAD_DIM:(NSA_GROUP * h + g + 1) * HEAD_DIM, :] = o.astype(BF16)


def nsa_attn(qt, gt, kc, vct, kk, vt, ovlt, emt, *, n_b, t, tq):
    nq = t // tq
    tks = vt.shape[2]
    assert tks % tq == 0, "a query tile must sit inside one key tile"
    assert min(SEL_TOPN, t // SEL_BLOCK) >= 3, "the kernel sets the three forced blocks outside its pick rounds"
    npiece = kc.shape[1]
    nblk = ovlt.shape[0]
    n_top = min(SEL_TOPN, t // SEL_BLOCK)
    col = lambda b, i: (0, b * nq + i)
    per_b = lambda b, i: (b, 0, 0)
    return pl.pallas_call(
        functools.partial(_nsa_attn_kernel, tq=tq, tks=tks, n_top=n_top, nblk=nblk),
        grid=(n_b, nq),
        in_specs=[pl.BlockSpec((1024, tq), col), pl.BlockSpec((LANES, tq), col),
                  pl.BlockSpec((None, npiece, 256), per_b), pl.BlockSpec((None, 256, npiece), per_b),
                  pl.BlockSpec((None, t, 512), per_b), pl.BlockSpec((t // tks, 512, tks), per_b),
                  pl.BlockSpec(ovlt.shape, lambda b, i: (0, 0)),
                  pl.BlockSpec(emt.shape, lambda b, i: (0, 0, 0))],
        out_specs=pl.BlockSpec((1024, tq), col),
        out_shape=jax.ShapeDtypeStruct((1024, n_b * t), BF16),
        scratch_shapes=[pltpu.VMEM((NSA_KV_HEADS, tks, NSA_GROUP * tq), F32)] * 2,
        compiler_params=_params("parallel", "arbitrary"),
        name="nsa_attn",
    )(qt, gt, kc, vct, kk.reshape(n_b, t, 512), vt, ovlt, emt)


def _nsa_s_sel_kernel(q_ref, kc_ref, vc_ref, ovl_ref, gm_ref, oc_ref, idx_ref, *, pos, n_pick):
    for i in range(q_ref.shape[0]):
        s = _dot_nt(q_ref[i], kc_ref[i])
        cend = lax.broadcasted_iota(jnp.int32, s.shape, 1) * CMP_STRIDE + (CMP_BLOCK - 1)
        p = _masked_softmax(s, cend <= pos)
        oc_ref[i] = _dot(p.astype(BF16), vc_ref[i])
        psum = _dot3_rhs(gm_ref[...], p)
        imp = _dot3(psum, ovl_ref[...])
        blk = lax.broadcasted_iota(jnp.int32, imp.shape, 1)
        cur = pos // SEL_BLOCK
        imp = jnp.where((blk == 0) | (blk == cur - 1), jnp.inf, imp)
        imp = jnp.where(blk < cur, imp, -jnp.inf)
        lane = blk.astype(F32)
        slot = lax.broadcasted_iota(jnp.int32, (8, LANES), 1)
        picks = jnp.zeros((8, LANES), F32)
        for r in range(n_pick):
            m = jnp.max(imp, axis=-1, keepdims=True)
            idx = jnp.min(jnp.where(imp == m, lane, 1e9), axis=-1, keepdims=True)
            imp = jnp.where(lane == idx, -jnp.inf, imp)
            picks = jnp.where(slot == r, idx, picks)
        idx_ref[i] = picks.astype(jnp.int32)


def nsa_s_sel(qbd, kc, vc, ovl, gm, *, pos, n_pick):
    n_b, npiece = kc.shape[0], kc.shape[1]
    spb = 4 if n_b % 4 == 0 else 1
    per_b = lambda b: (b, 0, 0)
    return pl.pallas_call(
        functools.partial(_nsa_s_sel_kernel, pos=pos, n_pick=n_pick),
        grid=(n_b // spb,),
        in_specs=[pl.BlockSpec((spb, 16, 256), per_b), pl.BlockSpec((spb, npiece, 256), per_b),
                  pl.BlockSpec((spb, npiece, 256), per_b),
                  pl.BlockSpec(ovl.shape, lambda b: (0, 0)), pl.BlockSpec(gm.shape, lambda b: (0, 0))],
        out_specs=[pl.BlockSpec((spb, 16, 256), per_b), pl.BlockSpec((spb, 8, LANES), per_b)],
        out_shape=[jax.ShapeDtypeStruct((n_b, 16, 256), F32), jax.ShapeDtypeStruct((n_b, 8, LANES), jnp.int32)],
        compiler_params=_params("parallel"),
        name="nsa_s_sel",
    )(qbd, kc, vc, ovl, gm)


def _attend_with_new(q, kts, vts, biases, k_new, v_new):
    ss = []
    for kt, bias in zip(kts, biases):
        s = _dot(q, kt.astype(BF16))
        ss.append(s if bias is None else s + bias)
    s = ss[0] if len(ss) == 1 else jnp.concatenate(ss, axis=1)
    s_new = jnp.sum(q.astype(F32) * k_new.astype(F32), axis=-1, keepdims=True)
    m = jnp.maximum(jnp.max(s, axis=-1, keepdims=True), s_new)
    p = jnp.exp(s - m)
    p_new = jnp.exp(s_new - m)
    l = jnp.sum(p, axis=-1, keepdims=True) + p_new
    acc = p_new * v_new.astype(F32)
    off = 0
    for vt in vts:
        n = vt.shape[1]
        acc = acc + _dot_nt(p[:, off:off + n].astype(BF16), vt.astype(BF16))
        off += n
    return acc / l


def _nsa_s_attn_kernel(pt_ref, ix_ref, q_ref, *refs, n_pick):
    del pt_ref
    kblk = refs[:n_pick]
    vblk = refs[n_pick:2 * n_pick]
    ks_new, vs_new, kw_new, vw_new, kw_ref, vw_ref, os_ref, ow_ref = refs[2 * n_pick:]
    base = (pl.program_id(0) * NSA_KV_HEADS + pl.program_id(1)) * n_pick
    half = lax.broadcasted_iota(jnp.int32, (1, PAGE_SIZE), 1) // SEL_BLOCK
    biases = [jnp.where(half == ix_ref[base + r] % 2, 0.0, NEG) for r in range(n_pick)]
    q = q_ref[...]
    os_ref[...] = _attend_with_new(q, [r[...] for r in kblk], [r[...] for r in vblk], biases, ks_new[...], vs_new[...])
    ow_ref[...] = _attend_with_new(q, [kw_ref[...]], [vw_ref[...]], [None], kw_new[...], vw_new[...])


def nsa_s_attn(pt_flat, ix_flat, q4, cache_fm, new_rows, win_fm, *, n_pages, n_pick):
    n_b = q4.shape[0]
    nwin = win_fm.shape[2]
    per_page = PAGE_SIZE // SEL_BLOCK

    def kv_map(r, stream):
        def f(b, h, pt, ix):
            j = ix[(b * NSA_KV_HEADS + h) * n_pick + r]
            return (pt[b * n_pages + j // per_page], stream * NSA_KV_HEADS + h, 0)
        return f

    new_map = lambda s: (lambda b, h, pt, ix: (b, s * NSA_KV_HEADS + h, 0, 0))
    qo_spec = pl.BlockSpec((None, None, 8, HEAD_DIM), lambda b, h, pt, ix: (b, h, 0, 0))
    grid_spec = pltpu.PrefetchScalarGridSpec(
        num_scalar_prefetch=2,
        grid=(n_b, NSA_KV_HEADS),
        in_specs=[qo_spec]
        + [pl.BlockSpec((None, HEAD_DIM, PAGE_SIZE), kv_map(r, 2)) for r in range(n_pick)]
        + [pl.BlockSpec((None, HEAD_DIM, PAGE_SIZE), kv_map(r, 3)) for r in range(n_pick)]
        + [pl.BlockSpec((None, None, 1, HEAD_DIM), new_map(s)) for s in range(4)]
        + [pl.BlockSpec((None, HEAD_DIM, nwin), lambda b, h, pt, ix: (b, h, 0)),
           pl.BlockSpec((None, HEAD_DIM, nwin), lambda b, h, pt, ix: (b, NSA_KV_HEADS + h, 0))],
        out_specs=[qo_spec, qo_spec],
    )
    return pl.pallas_call(
        functools.partial(_nsa_s_attn_kernel, n_pick=n_pick),
        grid_spec=grid_spec,
        out_shape=[jax.ShapeDtypeStruct((n_b, NSA_KV_HEADS, 8, HEAD_DIM), F32)] * 2,
        compiler_params=_params("parallel", "arbitrary"),
        name="nsa_s_attn",
    )(pt_flat, ix_flat, q4, *([cache_fm] * (2 * n_pick)), *([new_rows] * 4), win_fm, win_fm)


def _nsa_out_s_kernel(oc_ref, os_ref, ow_ref, g0_ref, g1_ref, g2_ref, w_ref, r_ref, o_ref):
    o = g0_ref[...] * oc_ref[...] + g1_ref[...] * os_ref[...] + g2_ref[...] * ow_ref[...]
    o_ref[...] = r_ref[...] + _dot(o.astype(BF16), w_ref[...])


def nsa_out_s(oc, osel, ow, g0, g1, g2, w, res):
    m, d = res.shape
    full = pl.BlockSpec((m, d), lambda i: (0, 0))
    return pl.pallas_call(
        _nsa_out_s_kernel,
        grid=(1,),
        in_specs=[full] * 6 + [pl.BlockSpec(w.shape, lambda i: (0, 0)), full],
        out_specs=full,
        out_shape=jax.ShapeDtypeStruct((m, d), F32),
        compiler_params=_params("arbitrary"),
        name="nsa_out_s",
    )(oc, osel, ow, g0, g1, g2, w, res)


def _sconv_kernel(x_ref, g_ref, win_ref, wc_ref, wout_ref, o_ref, st_ref, carry_ref):
    d = x_ref.shape[1]
    tm = x_ref.shape[0]

    @pl.when(pl.program_id(1) == 0)
    def _():
        carry_ref[...] = jnp.zeros(carry_ref.shape, F32)

    x = x_ref[...]
    xn = _rms(x, g_ref[...]).astype(BF16)
    b_gate = _dot(xn, win_ref[:, 0:d])
    pre = _dot(xn, win_ref[:, d:2 * d]) * _dot(xn, win_ref[:, 2 * d:3 * d])
    row = lax.broadcasted_iota(jnp.int32, (tm, d), 0)
    back1 = jnp.where(row == 0, carry_ref[7:8, :], pltpu.roll(pre, 1, 0))
    back2 = jnp.where(row == 0, carry_ref[6:7, :], jnp.where(row == 1, carry_ref[7:8, :], pltpu.roll(pre, 2, 0)))
    y = back2 * wc_ref[0:1, :] + back1 * wc_ref[1:2, :] + pre * wc_ref[2:3, :]
    tail = pre[tm - 8:tm]
    carry_ref[...] = tail
    st_ref[...] = tail
    o_ref[...] = x + _dot((b_gate * y).astype(BF16), wout_ref[...])


def sconv_prompt(x, g, w_in, w_conv, w_out, *, n_b, t, tm):
    d = x.shape[1]
    nt = t // tm
    row = lambda b, i: (b * nt + i, 0)
    const = lambda b, i: (0, 0)
    return pl.pallas_call(
        _sconv_kernel,
        grid=(n_b, nt),
        in_specs=[pl.BlockSpec((tm, d), row), pl.BlockSpec((1, d), const),
                  pl.BlockSpec((d, 3 * d), const, pipeline_mode=pl.Buffered(1)),
                  pl.BlockSpec(w_conv.shape, const),
                  pl.BlockSpec((d, d), const, pipeline_mode=pl.Buffered(1))],
        out_specs=[pl.BlockSpec((tm, d), row), pl.BlockSpec((None, 8, d), lambda b, i: (b, 0, 0))],
        out_shape=[jax.ShapeDtypeStruct((n_b * t, d), F32), jax.ShapeDtypeStruct((n_b, 8, d), F32)],
        scratch_shapes=[pltpu.VMEM((8, d), F32)],
        compiler_params=_params("parallel", "arbitrary"),
        name="sconv_prompt",
    )(x, g, w_in, w_conv, w_out)


def _sconv_s_kernel(x_ref, g_ref, win_ref, wc_ref, wout_ref, p0_ref, p1_ref, o_ref, pre_ref):
    d = x_ref.shape[1]
    x = x_ref[...]
    xn = _rms(x, g_ref[...]).astype(BF16)
    b_gate = _dot(xn, win_ref[:, 0:d])
    pre = _dot(xn, win_ref[:, d:2 * d]) * _dot(xn, win_ref[:, 2 * d:3 * d])
    y = p0_ref[...] * wc_ref[0:1, :] + p1_ref[...] * wc_ref[1:2, :] + pre * wc_ref[2:3, :]
    pre_ref[...] = pre
    o_ref[...] = x + _dot((b_gate * y).astype(BF16), wout_ref[...])


def sconv_sample(x, g, w_in, w_conv, w_out, past0, past1):
    m, d = x.shape
    full = lambda a: pl.BlockSpec(a.shape, lambda i: (0,) * a.ndim)
    args = (x, g, w_in, w_conv, w_out, past0, past1)
    return pl.pallas_call(
        _sconv_s_kernel,
        grid=(1,),
        in_specs=[full(a) for a in args],
        out_specs=[pl.BlockSpec((m, d), lambda i: (0, 0))] * 2,
        out_shape=[jax.ShapeDtypeStruct((m, d), F32)] * 2,
        compiler_params=_params("arbitrary"),
        name="sconv_sample",
    )(*args)


def _moba_in_kernel(x_ref, g_ref, w_ref, cos_ref, sin_ref, q_ref, rows_ref, kb_ref, vb_ref, km_ref):
    d = x_ref.shape[1]
    xn = _rms(x_ref[...], g_ref[...]).astype(BF16)
    cos = cos_ref[...]
    sin = sin_ref[...]
    q_ref[...] = (_rope(_dot(xn, w_ref[:, 0:d]), cos, sin) * SCALE).astype(BF16)
    k = _rope(_dot(xn, w_ref[:, d:2 * d]), cos, sin)
    v = _dot(xn, w_ref[:, 2 * d:3 * d])
    rows_ref[:, 0:d] = k
    rows_ref[:, d:2 * d] = v
    kb_ref[...] = k.astype(BF16)
    vb_ref[...] = v.astype(BF16)
    km_ref[...] = jnp.sum(k, axis=0, keepdims=True) * (1.0 / MOBA_BLOCK)


def moba_in(x, g, w, cos, sin, *, tm):
    m, d = x.shape
    nt = cos.shape[0] // tm
    row = lambda i: (i, 0)
    tab = lambda i: (i % nt, 0)
    const = lambda i: (0, 0)
    return pl.pallas_call(
        _moba_in_kernel,
        grid=(m // tm,),
        in_specs=[pl.BlockSpec((tm, d), row), pl.BlockSpec((1, d), const),
                  pl.BlockSpec((d, 3 * d), const, pipeline_mode=pl.Buffered(1)),
                  pl.BlockSpec((tm, LANES), tab), pl.BlockSpec((tm, LANES), tab)],
        out_specs=[pl.BlockSpec((tm, d), row), pl.BlockSpec((tm, 2 * d), row), pl.BlockSpec((tm, d), row),
                   pl.BlockSpec((tm, d), row), pl.BlockSpec((None, 1, d), lambda i: (i, 0, 0))],
        out_shape=[jax.ShapeDtypeStruct((m, d), BF16), jax.ShapeDtypeStruct((m, 2 * d), F32),
                   jax.ShapeDtypeStruct((m, d), BF16), jax.ShapeDtypeStruct((m, d), BF16),
                   jax.ShapeDtypeStruct((m // tm, 1, d), F32)],
        compiler_params=_params("parallel"),
        name="moba_in",
    )(x, g, w, cos, sin)


def _moba_in_t_kernel(x_ref, g_ref, wk_ref, wqkt_ref, wvt_ref, cos_ref, sin_ref, cost_ref, sint_ref,
                      qt_ref, rowst_ref, kb_ref, vt_ref, km_ref):
    d = x_ref.shape[1]
    xn = _rms(x_ref[...], g_ref[...]).astype(BF16)
    qk = _rope_t(_dot_nt(wqkt_ref[...], xn), cost_ref[...], sint_ref[...])
    qt_ref[...] = (qk[0:d] * SCALE_LOG2).astype(BF16)
    v_t = _dot_nt(wvt_ref[...], xn)
    rowst_ref[0:d, :] = qk[d:2 * d]
    rowst_ref[d:2 * d, :] = v_t
    vt_ref[...] = v_t.astype(BF16)
    k = _rope(_dot(xn, wk_ref[...]), cos_ref[...], sin_ref[...])
    kb_ref[...] = k.astype(BF16)
    km_ref[...] = jnp.sum(k, axis=0, keepdims=True) * (1.0 / MOBA_BLOCK)


def moba_in_t(x, g, w_k, wqk_t, wv_t, cos, sin, cos_t, sin_t, *, n_b):
    m, d = x.shape
    tm = MOBA_BLOCK
    nt = cos.shape[0] // tm
    row = lambda i: (i, 0)
    const = lambda i: (0, 0)
    one = pl.Buffered(1)
    return pl.pallas_call(
        _moba_in_t_kernel,
        grid=(m // tm,),
        in_specs=[pl.BlockSpec((tm, d), row), pl.BlockSpec((1, d), const),
                  pl.BlockSpec(w_k.shape, const, pipeline_mode=one), pl.BlockSpec(wqk_t.shape, const, pipeline_mode=one),
                  pl.BlockSpec(wv_t.shape, const, pipeline_mode=one),
                  pl.BlockSpec((tm, LANES), lambda i: (i % nt, 0)), pl.BlockSpec((tm, LANES), lambda i: (i % nt, 0)),
                  pl.BlockSpec((HEAD_DIM, tm), lambda i: (0, i % nt)), pl.BlockSpec((HEAD_DIM, tm), lambda i: (0, i % nt))],
        out_specs=[pl.BlockSpec((d, tm), lambda i: (0, i)),
                   pl.BlockSpec((None, 2 * d, tm), lambda i: (i // nt, 0, i % nt)), pl.BlockSpec((tm, d), row),
                   pl.BlockSpec((None, d, tm), lambda i: (i, 0, 0)), pl.BlockSpec((None, 1, d), lambda i: (i, 0, 0))],
        out_shape=[jax.ShapeDtypeStruct((d, m), BF16), jax.ShapeDtypeStruct((n_b, 2 * d, m // n_b), F32),
                   jax.ShapeDtypeStruct((m, d), BF16), jax.ShapeDtypeStruct((m // tm, d, tm), BF16),
                   jax.ShapeDtypeStruct((m // tm, 1, d), F32)],
        compiler_params=_params("parallel"),
        name="moba_in_t",
    )(x, g, w_k, wqk_t, wv_t, cos, sin, cos_t, sin_t)


def _moba_attn_kernel(qt_ref, k_ref, vt_ref, km_ref, ot_ref, s0_ref, s1_ref, *, tq, n_top):
    i = pl.program_id(2)
    c0 = i * tq
    nblk = km_ref.shape[0]
    pos = c0 + lax.broadcasted_iota(jnp.int32, (1, tq), 1)
    cur = pos // MOBA_BLOCK
    blk = lax.broadcasted_iota(jnp.int32, (nblk, tq), 0)
    m1, m2, m3 = _split3(km_ref[...])
    vrow = [slice(hh * HEAD_DIM, (hh + 1) * HEAD_DIM) for hh in range(2)]
    q_aug = []
    for hh in range(2):
        qp = _pad_pair(qt_ref[vrow[hh], :], hh == 1)
        gate = _dot(m1, qp) + _dot(m2, qp) + _dot(m3, qp)
        gate = jnp.where(blk < cur, gate, -jnp.inf)
        allow = (blk == cur) | (_topk_mask_t(gate, n_top) > 0.5)
        fill = jnp.zeros((LANES - nblk, tq), BF16)
        q_aug.append(jnp.concatenate([qp, jnp.where(allow, 0.0, NEG).astype(BF16), fill], axis=0))

    tile_id = lax.broadcasted_iota(jnp.int32, (MOBA_BLOCK, LANES), 1)

    def put(t, s_ref):
        k = k_ref[pl.ds(pl.multiple_of(t * MOBA_BLOCK, MOBA_BLOCK), MOBA_BLOCK), :]
        k_aug = jnp.concatenate([k, jnp.where(tile_id == t, 1.0, 0.0).astype(BF16)], axis=1)
        for hh in range(2):
            s_ref[hh] = _dot(k_aug, q_aug[hh])

    def consume(t, s_ref, carry, causal):
        bias = None
        if causal:
            kpos = t * MOBA_BLOCK + lax.broadcasted_iota(jnp.int32, (MOBA_BLOCK, tq), 0)
            bias = jnp.where(kpos <= pos, 0.0, NEG)
        return tuple(_online_update_t(carry[hh], s_ref[hh], bias, vt_ref[t, vrow[hh], :]) for hh in range(2))

    def body(u, carry):
        t = 2 * u
        put(t + 1, s1_ref)
        carry = consume(t, s0_ref, carry, False)
        put(t + 2, s0_ref)
        return consume(t + 1, s1_ref, carry, False)

    t_diag = c0 // MOBA_BLOCK
    put(0, s0_ref)
    res = lax.fori_loop(0, t_diag // 2, body, (_online_init_t(tq), _online_init_t(tq)))
    put(t_diag + 1, s1_ref)
    res = consume(t_diag, s0_ref, res, True)
    res = consume(t_diag + 1, s1_ref, res, True)
    for hh in range(2):
        ot_ref[vrow[hh], :] = _online_finish_t(res[hh]).astype(BF16)


def moba_attn(qt, kb, vt, kmean, *, n_b, t, tq):
    assert tq == 2 * MOBA_BLOCK, "the kernel's tile pairing assumes two MoBA blocks per query tile"
    nq = t // tq
    d = qt.shape[0]
    nblk = kmean.shape[1]
    n_top = min(MOBA_TOPK, t // MOBA_BLOCK)
    qmap = lambda b, hp, i: (hp, b * nq + i)
    kvmap = lambda b, hp, i: (b, 0, hp)
    return pl.pallas_call(
        functools.partial(_moba_attn_kernel, tq=tq, n_top=n_top),
        grid=(n_b, d // LANES, nq),
        in_specs=[pl.BlockSpec((LANES, tq), qmap), pl.BlockSpec((None, t, LANES), kvmap),
                  pl.BlockSpec((t // MOBA_BLOCK, LANES, MOBA_BLOCK), lambda b, hp, i: (b, hp, 0)),
                  pl.BlockSpec((None, nblk, LANES), kvmap)],
        out_specs=pl.BlockSpec((LANES, tq), qmap),
        out_shape=jax.ShapeDtypeStruct((d, n_b * t), BF16),
        scratch_shapes=[pltpu.VMEM((2, MOBA_BLOCK, tq), F32)] * 2,
        compiler_params=_params("parallel", "parallel", "arbitrary"),
        name="moba_attn",
    )(qt, kb.reshape(n_b, t, d), vt, kmean)


def _moba_s_select_kernel(pt_ref, *refs, pps, n_top, n_blocks):
    del pt_ref
    pages = refs[:pps]
    qbd_ref, idx_ref, km_ref = refs[pps:]
    s = pl.program_id(1)
    per_blk = MOBA_BLOCK // PAGE_SIZE
    blocks_per_step = pps // per_blk

    @pl.when(s == 0)
    def _():
        km_ref[...] = jnp.zeros(km_ref.shape, F32)

    lane = lax.broadcasted_iota(jnp.int32, km_ref.shape, 1)
    km = km_ref[...]
    for j in range(blocks_per_step):
        tot = pages[per_blk * j][...]
        for e in range(1, per_blk):
            tot = tot + pages[per_blk * j + e][...]
        mean = jnp.sum(tot, axis=1, keepdims=True) * (1.0 / MOBA_BLOCK)
        km = jnp.where(lane == s * blocks_per_step + j, mean, km)
    km_ref[...] = km

    @pl.when(s == pl.num_programs(1) - 1)
    def _():
        gate = _dot3_rhs(qbd_ref[...], km_ref[...])
        blk = lax.broadcasted_iota(jnp.int32, gate.shape, 1)
        gate = jnp.where(blk < n_blocks, gate, -jnp.inf)
        lane_f = blk.astype(F32)
        picks = jnp.zeros(gate.shape, F32)
        for r in range(n_top):
            m = jnp.max(gate, axis=-1, keepdims=True)
            idx = jnp.min(jnp.where(gate == m, lane_f, 1e9), axis=-1, keepdims=True)
            gate = jnp.where(lane_f == idx, -jnp.inf, gate)
            picks = jnp.where(blk == r, idx, picks)
        idx_ref[...] = picks.astype(jnp.int32)


def moba_s_select(cache_fm, pt_flat, qbd, *, n_pages, n_top):
    n_b, _, d = qbd.shape
    pps = min(16, n_pages)
    per_blk = MOBA_BLOCK // PAGE_SIZE
    page_map = lambda k: (lambda b, s, pt: (pt[b * n_pages + s * pps + k], 0, 0))
    grid_spec = pltpu.PrefetchScalarGridSpec(
        num_scalar_prefetch=1,
        grid=(n_b, n_pages // pps),
        in_specs=[pl.BlockSpec((None, d, PAGE_SIZE), page_map(k)) for k in range(pps)]
        + [pl.BlockSpec((None, LANES, d), lambda b, s, pt: (b, 0, 0))],
        out_specs=pl.BlockSpec((None, LANES, LANES), lambda b, s, pt: (b, 0, 0)),
        scratch_shapes=[pltpu.VMEM((d, LANES), F32)],
    )
    return pl.pallas_call(
        functools.partial(_moba_s_select_kernel, pps=pps, n_top=n_top, n_blocks=n_pages // per_blk),
        grid_spec=grid_spec,
        out_shape=jax.ShapeDtypeStruct((n_b, LANES, LANES), jnp.int32),
        compiler_params=_params("parallel", "arbitrary"),
        name="moba_s_select",
    )(pt_flat, *([cache_fm] * pps), qbd)


def _moba_s_attn_kernel(pt_ref, ix_ref, q_ref, *refs, n_pg, hps):
    del pt_ref, ix_ref
    kblk = refs[:hps * n_pg]
    vblk = refs[hps * n_pg:2 * hps * n_pg]
    k_new, v_new, o_ref = refs[2 * hps * n_pg:]
    for hi in range(hps):
        mine = slice(hi * n_pg, (hi + 1) * n_pg)
        o_ref[hi] = _attend_with_new(q_ref[hi], [r[...] for r in kblk[mine]], [r[...] for r in vblk[mine]],
                                     [None] * n_pg, k_new[hi], v_new[hi])


def moba_s_attn(pt_flat, ix_flat, q4, cache_fm, k_new, v_new, *, n_pages, n_top, n_heads):
    n_b = q4.shape[0]
    per_blk = MOBA_BLOCK // PAGE_SIZE
    n_pg = n_top * per_blk
    hps = 4

    def kv_map(hi, r, row0):
        def f(b, hg, pt, ix):
            h = hg * hps + hi
            j = ix[(b * n_heads + h) * n_top + r // per_blk]
            return (pt[b * n_pages + j * per_blk + r % per_blk], row0 + h, 0)
        return f

    group = lambda b, hg, pt, ix: (b, hg, 0, 0)
    new_spec = pl.BlockSpec((None, hps, 1, HEAD_DIM), group)
    qo_spec = pl.BlockSpec((None, hps, 8, HEAD_DIM), group)
    page = (None, HEAD_DIM, PAGE_SIZE)
    grid_spec = pltpu.PrefetchScalarGridSpec(
        num_scalar_prefetch=2,
        grid=(n_b, n_heads // hps),
        in_specs=[qo_spec]
        + [pl.BlockSpec(page, kv_map(hi, r, 0)) for hi in range(hps) for r in range(n_pg)]
        + [pl.BlockSpec(page, kv_map(hi, r, n_heads)) for hi in range(hps) for r in range(n_pg)]
        + [new_spec, new_spec],
        out_specs=qo_spec,
    )
    return pl.pallas_call(
        functools.partial(_moba_s_attn_kernel, n_pg=n_pg, hps=hps),
        grid_spec=grid_spec,
        out_shape=jax.ShapeDtypeStruct((n_b, n_heads, 8, HEAD_DIM), F32),
        compiler_params=_params("parallel", "arbitrary"),
        name="moba_s_attn",
    )(pt_flat, ix_flat, q4, *([cache_fm] * (2 * hps * n_pg)), k_new, v_new)


def _layer_norm_silu(y, g, b):
    yc = y - jnp.mean(y, axis=-1, keepdims=True)
    yn = yc * lax.rsqrt(jnp.mean(yc * yc, axis=-1, keepdims=True) + LN_EPS) * g + b
    return yn * _sigmoid(yn)


def _conf_kernel(x_ref, g_ref, w1_ref, wdw_ref, bdw_ref, lg_ref, lb_ref, w2_ref, o_ref, st_ref, ubuf_ref, sh_ref,
                 *, hist):
    tm, d = x_ref.shape
    width = wdw_ref.shape[0]

    @pl.when(pl.program_id(1) == 0)
    def _():
        ubuf_ref[0:hist, :] = jnp.zeros((hist, d), F32)

    x = x_ref[...]
    xn = _rms(x, g_ref[...]).astype(BF16)
    u = _dot(xn, w1_ref[:, 0:d]) * _sigmoid(_dot(xn, w1_ref[:, d:2 * d]))
    ubuf_ref[hist:hist + tm, :] = u
    base = hist - (width - 1)
    rows = ubuf_ref.shape[0]
    y = bdw_ref[...]
    for r in range(8):
        taps = [k for k in range(width) if (base + k) % 8 == r]
        if not taps:
            continue
        src = ubuf_ref
        if r:
            sh_ref[...] = pltpu.roll(ubuf_ref[...], rows - r, 0)
            src = sh_ref
        for k in taps:
            y = y + src[base + k - r:base + k - r + tm, :] * wdw_ref[k:k + 1, :]
    z = _layer_norm_silu(y, lg_ref[...], lb_ref[...])
    o_ref[...] = x + _dot(z.astype(BF16), w2_ref[...])
    tail = ubuf_ref[tm:tm + hist, :]
    st_ref[...] = tail
    ubuf_ref[0:hist, :] = tail


def conf_prompt(x, g, w1, wdw, bdw, lg, lb, w2, *, n_b, t, tm):
    d = x.shape[1]
    nt = t // tm
    hist = 32
    row = lambda b, i: (b * nt + i, 0)
    const = lambda b, i: (0, 0)
    return pl.pallas_call(
        functools.partial(_conf_kernel, hist=hist),
        grid=(n_b, nt),
        in_specs=[pl.BlockSpec((tm, d), row), pl.BlockSpec((1, d), const),
                  pl.BlockSpec((d, 2 * d), const, pipeline_mode=pl.Buffered(1)),
                  pl.BlockSpec(wdw.shape, const), pl.BlockSpec((1, d), const), pl.BlockSpec((1, d), const),
                  pl.BlockSpec((1, d), const), pl.BlockSpec((d, d), const, pipeline_mode=pl.Buffered(1))],
        out_specs=[pl.BlockSpec((tm, d), row), pl.BlockSpec((None, hist, d), lambda b, i: (b, 0, 0))],
        out_shape=[jax.ShapeDtypeStruct((n_b * t, d), F32), jax.ShapeDtypeStruct((n_b, hist, d), F32)],
        scratch_shapes=[pltpu.VMEM((hist + tm, d), F32)] * 2,
        compiler_params=_params("parallel", "arbitrary"),
        name="conf_prompt",
    )(x, g, w1, wdw, bdw, lg, lb, w2)


def _conf_s_kernel(x_ref, g_ref, w1_ref, wdw_ref, bdw_ref, lg_ref, lb_ref, w2_ref, past_ref, o_ref, u_ref):
    d = x_ref.shape[1]
    width = wdw_ref.shape[0]
    x = x_ref[...]
    xn = _rms(x, g_ref[...]).astype(BF16)
    u = _dot(xn, w1_ref[:, 0:d]) * _sigmoid(_dot(xn, w1_ref[:, d:2 * d]))
    y = bdw_ref[...] + past_ref[0] * wdw_ref[0:1, :]
    for k in range(1, width - 1):
        y = y + past_ref[k] * wdw_ref[k:k + 1, :]
    y = y + u * wdw_ref[width - 1:width, :]
    z = _layer_norm_silu(y, lg_ref[...], lb_ref[...])
    u_ref[...] = u
    o_ref[...] = x + _dot(z.astype(BF16), w2_ref[...])


def conf_sample(x, g, w1, wdw, bdw, lg, lb, w2, past_t):
    m, d = x.shape
    full = lambda a: pl.BlockSpec(a.shape, lambda i: (0,) * a.ndim)
    args = (x, g, w1, wdw, bdw, lg, lb, w2, past_t)
    return pl.pallas_call(
        _conf_s_kernel,
        grid=(1,),
        in_specs=[full(a) for a in args],
        out_specs=[pl.BlockSpec((m, d), lambda i: (0, 0))] * 2,
        out_shape=[jax.ShapeDtypeStruct((m, d), F32)] * 2,
        compiler_params=_params("arbitrary"),
        name="conf_sample",
    )(*args)


def _rope_tables(pos):
    half = HEAD_DIM // 2
    inv_freq = ROPE_THETA ** (-jnp.arange(half, dtype=F32) / half)
    ang = pos.astype(F32)[:, None] * inv_freq[None, :]
    cos = jnp.cos(ang)
    sin = jnp.sin(ang)
    cos = jnp.concatenate([cos, cos], axis=-1)
    sin = jnp.concatenate([-sin, sin], axis=-1)
    return jnp.tile(cos, (1, LANES // HEAD_DIM)), jnp.tile(sin, (1, LANES // HEAD_DIM))


def _overlap(n_cmp_rows, n_cols):
    i = jnp.arange(n_cmp_rows, dtype=jnp.int32)[:, None]
    j = jnp.arange(n_cols, dtype=jnp.int32)[None, :]
    start = i * CMP_STRIDE
    hit = (start <= j * SEL_BLOCK + (SEL_BLOCK - 1)) & (start + (CMP_BLOCK - 1) >= j * SEL_BLOCK)
    return hit.astype(BF16)


def _nsa_layer(hp, hs, g, cache_kv, cache_win, pt_flat, n_pages, w, *, n_b, t, n_s):
    w_in, pe_k, w1_k, w2_k, pe_v, w1_v, w2_v, w_out = w
    d = hp.shape[1]
    past_len = n_pages * PAGE_SIZE
    w_in_p = jnp.pad(w_in, ((0, 0), (0, 2688 - w_in.shape[1]))).astype(BF16)
    w_out_b = w_out.astype(BF16)
    half = CMP_BLOCK * HEAD_DIM // 2

    def cmp_weights(pe, w1, w2):
        return (pe[:CMP_STRIDE].reshape(1, half), pe[CMP_STRIDE:].reshape(1, half),
                w1[:half].astype(BF16), w1[half:].astype(BF16), w2.astype(BF16))

    wk = cmp_weights(pe_k, w1_k, w2_k)
    wv = cmp_weights(pe_v, w1_v, w2_v) + (w2_v.T.astype(BF16),)

    cos_p, sin_p = _rope_tables(jnp.arange(t, dtype=jnp.int32))
    cos_pt, sin_pt = cos_p[:, :HEAD_DIM].T, sin_p[:, :HEAD_DIM].T
    kv0, kv1, n_gate = 1024, 2560, NSA_KV_HEADS * NSA_GROUP * 3
    wq_t = w_in[:, :kv0].T.astype(BF16)
    wkv_t = w_in[:, kv0:kv0 + 1024].T.astype(BF16)
    wv_t = w_in[:, kv0 + 1280:kv1].T.astype(BF16)
    wg_t = jnp.pad(w_in[:, kv1:kv1 + n_gate].T, ((0, LANES - n_gate), (0, 0))).astype(BF16)
    w_rm = jnp.concatenate([w_in[:, kv0:kv0 + 768], w_in[:, kv0 + 1024:kv1]], axis=1).astype(BF16)
    tks = 256
    qt, rows_t, cmp_rows, kk, vt, win, gt = nsa_in_t(hp, g, w_rm, wq_t, wkv_t, wv_t, wg_t,
                                                     cos_p, sin_p, cos_pt, sin_pt, tm=tks, n_b=n_b)
    npg_p = t // PAGE_SIZE
    cend_p = jnp.arange(npg_p * 8, dtype=jnp.int32) * CMP_STRIDE + (CMP_BLOCK - 1)
    kc, _, vct = nsa_compress(cmp_rows.reshape(n_b * npg_p, PAGE_SIZE, 512), jnp.arange(n_b * npg_p, dtype=jnp.int32),
                              n_b, npg_p, wk, wv, *_rope_tables(cend_p), feature_major=False)
    nblk = -(-(t // SEL_BLOCK) // 16) * 16
    tile = jnp.arange(t // tks, dtype=jnp.int32)[:, None, None]
    key = jnp.arange(tks, dtype=jnp.int32)[None, :, None]
    blk = jnp.arange(LANES, dtype=jnp.int32)[None, None, :]
    emt = (blk == (tile * tks + key) // SEL_BLOCK).astype(BF16)
    ot = nsa_attn(qt, gt, kc, vct, kk, vt, _overlap(npg_p * 8, nblk).T, emt, n_b=n_b, t=t, tq=128)
    hp = mm_res_t(ot, w_out.T.astype(BF16), hp, tm=512)
    kv_p = jnp.transpose(rows_t.reshape(n_b, 4, NSA_KV_HEADS, HEAD_DIM, t), (0, 4, 1, 2, 3))
    keep = min(WINDOW, t)
    win_p = win.reshape(n_b, t, 2, NSA_KV_HEADS, HEAD_DIM)[:, t - keep:]

    cos_s, sin_s = _rope_tables(jnp.full((n_s,), past_len, jnp.int32))
    q_s, rows_s, kva_s, win_s, gates_s = nsa_in(hs, g, w_in_p, cos_s, sin_s, tm=n_s)
    cend_s = jnp.arange(n_pages * 8, dtype=jnp.int32) * CMP_STRIDE + (CMP_BLOCK - 1)
    cache_fm = jnp.transpose(cache_kv, (0, 2, 3, 4, 1)).reshape(cache_kv.shape[0], 1024, PAGE_SIZE)
    kc_s, vc_s, _ = nsa_compress(cache_fm, pt_flat, n_s, n_pages, wk, wv, *_rope_tables(cend_s), feature_major=True)
    n_sel = -(-(past_len + 1) // SEL_BLOCK)
    n_pick = min(SEL_TOPN, n_sel) - 1
    q4 = q_s.reshape(n_s, NSA_KV_HEADS, NSA_GROUP, HEAD_DIM)
    eye = jnp.eye(NSA_KV_HEADS, dtype=bool)[None, :, None, :, None]
    qbd = jnp.where(eye, q4[:, :, :, None, :], jnp.zeros((), BF16)).reshape(n_s, 16, 256)
    gm = (jnp.arange(8)[:, None] == jnp.arange(16)[None, :] // NSA_GROUP).astype(BF16)
    n_blk_pad = -(-n_sel // LANES) * LANES
    oc16, idx = nsa_s_sel(qbd, kc_s, vc_s, _overlap(n_pages * 8, n_blk_pad), gm, pos=past_len, n_pick=n_pick)
    oc5 = oc16.reshape(n_s, NSA_KV_HEADS, NSA_GROUP, NSA_KV_HEADS, HEAD_DIM)
    o_c = jnp.sum(jnp.where(eye, oc5, 0.0), axis=3).reshape(n_s, d)
    ix_flat = idx[:, :NSA_KV_HEADS, :n_pick].reshape(-1)
    q8 = jnp.pad(q4, ((0, 0), (0, 0), (0, 8 - NSA_GROUP), (0, 0)))
    win_fm = jnp.transpose(cache_win, (0, 2, 3, 4, 1)).reshape(n_s, 512, cache_win.shape[1])
    os_p, ow_p = nsa_s_attn(pt_flat, ix_flat, q8, cache_fm, kva_s.reshape(n_s, 16, 1, HEAD_DIM), win_fm,
                            n_pages=n_pages, n_pick=n_pick)
    o_s = os_p[:, :, :NSA_GROUP].reshape(n_s, d)
    o_w = ow_p[:, :, :NSA_GROUP].reshape(n_s, d)
    g3 = jnp.repeat(gates_s[:, :48].reshape(n_s, 16, 3), HEAD_DIM, axis=1)
    hs = nsa_out_s(o_c, o_s, o_w, g3[:, :, 0], g3[:, :, 1], g3[:, :, 2], w_out_b, hs)
    kv_s = rows_s.reshape(n_s, 1, 4, NSA_KV_HEADS, HEAD_DIM)
    win_new = win_s.reshape(n_s, 1, 2, NSA_KV_HEADS, HEAD_DIM)
    win_all = jnp.concatenate([cache_win, win_new], axis=1)
    win_s_out = win_all[:, win_all.shape[1] - cache_win.shape[1]:]
    return hp, hs, kv_p, kv_s, win_p, win_s_out


def _sconv_layer(hp, hs, g, state, w, *, n_b, t):
    w_in, w_conv, w_out = w
    w_in_b = w_in.astype(BF16)
    w_out_b = w_out.astype(BF16)
    hp, st = sconv_prompt(hp, g, w_in_b, w_conv, w_out_b, n_b=n_b, t=t, tm=256)
    st_p = st[:, 8 - (w_conv.shape[0] - 1):]
    hs, pre = sconv_sample(hs, g, w_in_b, w_conv, w_out_b, state[:, 0], state[:, 1])
    st_s = jnp.concatenate([state[:, 1:], pre[:, None, :]], axis=1)
    return hp, hs, st_p, st_s


def _moba_layer(hp, hs, g, cache_kv, pt_flat, n_pages, w, *, n_b, t, n_s):
    w_qkv, w_out = w
    d = hp.shape[1]
    n_heads = d // HEAD_DIM
    past_len = n_pages * PAGE_SIZE
    w_qkv_b = w_qkv.astype(BF16)
    w_out_b = w_out.astype(BF16)

    cos_p, sin_p = _rope_tables(jnp.arange(t, dtype=jnp.int32))
    cos_pt, sin_pt = cos_p[:, :HEAD_DIM].T, sin_p[:, :HEAD_DIM].T
    qt, rows_t, kb, vt, km = moba_in_t(hp, g, w_qkv_b[:, d:2 * d], w_qkv_b[:, :2 * d].T, w_qkv_b[:, 2 * d:].T,
                                       cos_p, sin_p, cos_pt, sin_pt, n_b=n_b)
    nblk = t // MOBA_BLOCK
    kmean = jnp.pad(km.reshape(n_b, nblk, d), ((0, 0), (0, -(-nblk // 16) * 16 - nblk), (0, 0)))
    ot = moba_attn(qt, kb, vt, kmean, n_b=n_b, t=t, tq=512)
    hp = mm_res_t(ot, w_out_b.T, hp, tm=512)
    kv_p = jnp.transpose(rows_t.reshape(n_b, 2, n_heads, HEAD_DIM, t), (0, 4, 1, 2, 3))

    cos_s, sin_s = _rope_tables(jnp.full((n_s,), past_len, jnp.int32))
    q_s, rows_s, kb_s, vb_s, _ = moba_in(hs, g, w_qkv_b, cos_s, sin_s, tm=n_s)
    cache_fm = jnp.transpose(cache_kv, (0, 2, 3, 4, 1)).reshape(cache_kv.shape[0], 2 * d, PAGE_SIZE)
    n_top = min(MOBA_TOPK, -(-(past_len + 1) // MOBA_BLOCK))
    qh = q_s.reshape(n_s, n_heads, 1, HEAD_DIM)
    eye = jnp.eye(n_heads, dtype=bool)[None, :, :, None]
    qbd = jnp.where(eye, qh, jnp.zeros((), BF16)).reshape(n_s, n_heads, d)
    qbd = jnp.pad(qbd, ((0, 0), (0, LANES - n_heads), (0, 0)))
    idx = moba_s_select(cache_fm, pt_flat, qbd, n_pages=n_pages, n_top=n_top)
    ix_flat = idx[:, :n_heads, :n_top].reshape(-1)
    q8 = jnp.pad(qh, ((0, 0), (0, 0), (0, 7), (0, 0)))
    o_p = moba_s_attn(pt_flat, ix_flat, q8, cache_fm, kb_s.reshape(n_s, n_heads, 1, HEAD_DIM),
                      vb_s.reshape(n_s, n_heads, 1, HEAD_DIM), n_pages=n_pages, n_top=n_top, n_heads=n_heads)
    o_s = o_p[:, :, 0].reshape(n_s, d).astype(BF16)
    hs = mm_res(o_s, w_out_b, hs, tm=n_s)
    kv_s = rows_s.reshape(n_s, 1, 2, n_heads, HEAD_DIM)
    return hp, hs, kv_p, kv_s


def _conf_layer(hp, hs, g, state, w, *, n_b, t):
    w_pw1, w_dw, b_dw, ln_g, ln_b, w_pw2 = w
    d = hp.shape[1]
    r = lambda a: a.reshape(1, d)
    args = (w_pw1.astype(BF16), w_dw, r(b_dw), r(ln_g), r(ln_b), w_pw2.astype(BF16))
    hp, st = conf_prompt(hp, g, *args, n_b=n_b, t=t, tm=256)
    st_p = st[:, st.shape[1] - (w_dw.shape[0] - 1):]
    hs, u = conf_sample(hs, g, *args, jnp.transpose(state, (1, 0, 2)))
    st_s = jnp.concatenate([state[:, 1:], u[:, None, :]], axis=1)
    return hp, hs, st_p, st_s


def kernel(x_prompt, x_sample, cache_nsa_kv, cache_nsa_win, state_sconv, cache_moba_kv, state_conformer,
           page_table, norm_mix, norm_ffn, norm_final, ffn_w_up, ffn_w_down,
           nsa_w_in, nsa_pe_k, nsa_w1_k, nsa_w2_k, nsa_pe_v, nsa_w1_v, nsa_w2_v, nsa_w_out,
           sconv_w_in, sconv_w_conv, sconv_w_out, moba_w_qkv, moba_w_out,
           conf_w_pw1, conf_w_dw, conf_b_dw, conf_ln_g, conf_ln_b, conf_w_pw2):
    n_b, t, d = x_prompt.shape
    n_s = x_sample.shape[0]
    depth = norm_mix.shape[0]
    n_pages = page_table.shape[1]
    pt_flat = page_table.reshape(-1).astype(jnp.int32)
    hp = x_prompt.reshape(n_b * t, d)
    hs = x_sample.reshape(n_s, d)
    outs = {k: [] for k in ("nsa_kv_p", "nsa_kv_s", "nsa_win_p", "nsa_win_s", "sconv_p", "sconv_s",
                            "moba_p", "moba_s", "conf_p", "conf_s")}
    for i in range(depth):
        kind, j = i % 4, i // 4
        g = norm_mix[i].reshape(1, d)
        if kind == 0:
            w = (nsa_w_in[j], nsa_pe_k[j], nsa_w1_k[j], nsa_w2_k[j], nsa_pe_v[j], nsa_w1_v[j], nsa_w2_v[j],
                 nsa_w_out[j])
            hp, hs, kv_p, kv_s, win_p, win_s = _nsa_layer(hp, hs, g, cache_nsa_kv[j], cache_nsa_win[j], pt_flat,
                                                          n_pages, w, n_b=n_b, t=t, n_s=n_s)
            outs["nsa_kv_p"].append(kv_p)
            outs["nsa_kv_s"].append(kv_s)
            outs["nsa_win_p"].append(win_p)
            outs["nsa_win_s"].append(win_s)
        elif kind == 1:
            hp, hs, st_p, st_s = _sconv_layer(hp, hs, g, state_sconv[j],
                                              (sconv_w_in[j], sconv_w_conv[j], sconv_w_out[j]), n_b=n_b, t=t)
            outs["sconv_p"].append(st_p)
            outs["sconv_s"].append(st_s)
        elif kind == 2:
            hp, hs, kv_p, kv_s = _moba_layer(hp, hs, g, cache_moba_kv[j], pt_flat, n_pages,
                                             (moba_w_qkv[j], moba_w_out[j]), n_b=n_b, t=t, n_s=n_s)
            outs["moba_p"].append(kv_p)
            outs["moba_s"].append(kv_s)
        else:
            w = (conf_w_pw1[j], conf_w_dw[j], conf_b_dw[j], conf_ln_g[j], conf_ln_b[j], conf_w_pw2[j])
            hp, hs, st_p, st_s = _conf_layer(hp, hs, g, state_conformer[j], w, n_b=n_b, t=t)
            outs["conf_p"].append(st_p)
            outs["conf_s"].append(st_s)
        gf = norm_ffn[i].reshape(1, d)
        wu = ffn_w_up[i].astype(BF16)
        wd = ffn_w_down[i].astype(BF16)
        final = i == depth - 1
        gfin = norm_final.reshape(1, d)
        hp = ffn(hp, gf, wu, wd, gfin, tm=512, final=final)
        hs = ffn(hs, gf, wu, wd, gfin, tm=n_s, final=final)
    return (hp.reshape(n_b, t, d), hs.reshape(n_s, 1, d),
            jnp.stack(outs["nsa_kv_p"]), jnp.stack(outs["nsa_kv_s"]),
            jnp.stack(outs["nsa_win_p"]), jnp.stack(outs["nsa_win_s"]),
            jnp.stack(outs["sconv_p"]), jnp.stack(outs["sconv_s"]),
            jnp.stack(outs["moba_p"]), jnp.stack(outs["moba_s"]),
            jnp.stack(outs["conf_p"]), jnp.stack(outs["conf_s"]))
```
